```python
import math
import jax
import jax.numpy as jnp
from jax import lax
import numpy as np

D_MODEL = 1024
BATCH = 4
SEQ = 4096
DEPTH = 2
DEC_BATCH = 128
DEC_SEQ = 4
PAST_LEN = 2048
PAGE_SIZE = 128

PLE_DIM = 256
GLA_HEADS = 4
GLA_DK = 64
GLA_DV = 128
GLA_RANK = 16
GLA_GATE_NORM = 16.0
MLSTM_HEADS = 4
MLSTM_DQK = 64
MLSTM_DV = 128
MLSTM_FGATE_BIAS = 3.0
DIFF_HEADS = 4
DIFF_DQK = 64
DIFF_DV = 2 * DIFF_DQK
ROT_DIM = DIFF_DQK // 4
ROPE_THETA = 500000.0
Q_BLOCK = 128
CHUNK = 64
D_FF = -(-8 * D_MODEL // (3 * 256)) * 256
EPS = 1e-6
N_BRANCH = 3
GLA_VW = GLA_HEADS * GLA_DV
MLSTM_VW = MLSTM_HEADS * MLSTM_DV
DIFF_VW = DIFF_HEADS * DIFF_DV
MIX_V = GLA_VW + MLSTM_VW + DIFF_VW
IN_SPLITS = (GLA_HEADS * GLA_DK, GLA_HEADS * GLA_DK, GLA_VW, GLA_VW, GLA_RANK,
             MLSTM_HEADS * MLSTM_DQK, MLSTM_HEADS * MLSTM_DQK, MLSTM_VW, MLSTM_VW, 2 * MLSTM_HEADS,
             2 * DIFF_HEADS * DIFF_DQK, 2 * DIFF_HEADS * DIFF_DQK, DIFF_VW)
D_IN = sum(IN_SPLITS)

kernel_name = 'hybrid_gla_mlstm_diffattn_decode_step'


def _rmsnorm(x, g):
    x32 = x.astype(jnp.float32)
    y = x32 * lax.rsqrt(jnp.mean(x32 * x32, axis=-1, keepdims=True) + EPS)
    return (y * g.astype(jnp.float32)).astype(x.dtype)


def _chunk_len(L):
    return CHUNK if L % CHUNK == 0 else L


def _to_chunks(t, c):
    B, L, H, d = t.shape
    return t.reshape(B, L // c, c, H, d).transpose(1, 0, 3, 2, 4)


def _gate_chunks(t, c):
    B, L, H = t.shape
    return t.reshape(B, L // c, c, H).transpose(1, 0, 3, 2)


def _from_chunks(o):
    n, B, H, c, d = o.shape
    return o.transpose(1, 0, 3, 2, 4).reshape(B, n * c, H, d)


def _gla_chunked(q, k, v, log_a, s0):
    c = _chunk_len(q.shape[1])
    f32 = jnp.float32
    qc, kc, vc, gc = (_to_chunks(t.astype(f32), c) for t in (q, k, v, log_a))
    mask = jnp.tril(jnp.ones((c, c), dtype=bool))

    def step(S, inp):
        qi, ki, vi, gi = inp
        b = jnp.cumsum(gi, axis=-2)
        qg = qi * jnp.exp(b)
        kg = ki * jnp.exp(-b)
        A = jnp.where(mask, jnp.einsum('bhtd,bhsd->bhts', qg, kg), 0.0)
        o = jnp.einsum('bhtd,bhdv->bhtv', qg, S) + jnp.einsum('bhts,bhsv->bhtv', A, vi)
        b_last = b[..., -1:, :]
        kd = ki * jnp.exp(b_last - b)
        S = jnp.exp(b_last)[..., 0, :, None] * S + jnp.einsum('bhsd,bhsv->bhdv', kd, vi)
        return S, o

    S, o = lax.scan(step, s0.astype(f32), (qc, kc, vc, gc))
    return _from_chunks(o).astype(q.dtype), S


def _mlstm_chunked(q, k, v, i_pre, log_f, c0, n0, m0):
    c = _chunk_len(q.shape[1])
    f32 = jnp.float32
    qc, kc, vc = (_to_chunks(t.astype(f32), c) for t in (q, k, v))
    ic, fc = (_gate_chunks(t.astype(f32), c) for t in (i_pre, log_f))
    mask = jnp.tril(jnp.ones((c, c), dtype=bool))

    def step(carry, inp):
        C, nv, m = carry
        qi, ki, vi, ii, fi = inp
        F = jnp.cumsum(fi, axis=-1)
        logD = jnp.where(mask, F[..., :, None] - F[..., None, :] + ii[..., None, :], -jnp.inf)
        m_inter = m[..., None] + F
        m_row = jnp.maximum(m_inter, jnp.max(logD, axis=-1))
        D = jnp.exp(logD - m_row[..., None])
        w_inter = jnp.exp(m_inter - m_row)
        s = jnp.einsum('bhtd,bhsd->bhts', qi, ki) * D
        num = w_inter[..., None] * jnp.einsum('bhtd,bhdv->bhtv', qi, C) + jnp.einsum('bhts,bhsv->bhtv', s, vi)
        den = w_inter * jnp.einsum('bhtd,bhd->bht', qi, nv) + jnp.sum(s, axis=-1)
        h = num / jnp.maximum(jnp.abs(den), jnp.exp(-m_row))[..., None]
        F_last = F[..., -1]
        log_w = F_last[..., None] - F + ii
        m_new = jnp.maximum(m + F_last, jnp.max(log_w, axis=-1))
        wk = jnp.exp(log_w - m_new[..., None])
        decay = jnp.exp(m + F_last - m_new)
        C = decay[..., None, None] * C + jnp.einsum('bhs,bhsd,bhsv->bhdv', wk, ki, vi)
        nv = decay[..., None] * nv + jnp.einsum('bhs,bhsd->bhd', wk, ki)
        return (C, nv, m_new), h

    (C, nv, m), h = lax.scan(step, (c0.astype(f32), n0.astype(f32), m0.astype(f32)), (qc, kc, vc, ic, fc))
    return _from_chunks(h).astype(q.dtype), C, nv, m


def _rope_partial(x, pos):
    half = ROT_DIM // 2
    inv_freq = ROPE_THETA ** (-jnp.arange(half, dtype=jnp.float32) * 2.0 / ROT_DIM)
    ang = pos.astype(jnp.float32)[:, None] * inv_freq[None, :]
    cos = jnp.cos(ang)[None, :, None, :]
    sin = jnp.sin(ang)[None, :, None, :]
    x32 = x.astype(jnp.float32)
    x1 = x32[..., :half]
    x2 = x32[..., half:ROT_DIM]
    out = jnp.concatenate([x1 * cos - x2 * sin, x1 * sin + x2 * cos, x32[..., ROT_DIM:]], axis=-1)
    return out.astype(x.dtype)


def _diff_attention(q, k, v, q_pos, k_pos, lam):
    B, Lq, H2, d = q.shape
    H = H2 // 2
    Lk = k.shape[1]
    scale = d ** -0.5
    v32 = v.astype(jnp.float32)

    def block(qb, qp):
        s = jnp.einsum('bqhd,bkhd->bhqk', qb, k).astype(jnp.float32) * scale
        s = jnp.where(k_pos[None, :] <= qp[:, None], s, -jnp.inf)
        a = jax.nn.softmax(s, axis=-1).reshape(B, H, 2, qb.shape[1], Lk)
        w = a[:, :, 0] - lam * a[:, :, 1]
        return jnp.einsum('bhqk,bkhv->bqhv', w, v32)

    if Lq > Q_BLOCK and Lq % Q_BLOCK == 0:
        nb = Lq // Q_BLOCK
        qb = q.reshape(B, nb, Q_BLOCK, H2, d).transpose(1, 0, 2, 3, 4)
        qp = q_pos.reshape(nb, Q_BLOCK)
        out = lax.map(lambda a: block(a[0], a[1]), (qb, qp))
        out = out.transpose(1, 0, 2, 3, 4).reshape(B, Lq, H, v.shape[-1])
    else:
        out = block(q, q_pos)
    return out.astype(q.dtype)


def _layer(h, p_l, pos, layer_idx, lw, gla_s0, c0, n0, m0, past_k, past_v, past_pos):
    (g_mix, w_in, w_gla_gk, b_gla_gk, g_gla_norm, b_mlstm_if, g_mlstm_norm, diff_lambda, g_diff_norm,
     w_branch, w_gate, w_out, g_ffn, w_ffn_gate, w_ffn_up, w_ffn_down, g_ple, w_ple_gate, w_ple_proj) = lw
    B, L, D = h.shape
    xn = _rmsnorm(h, g_mix)
    proj = jnp.einsum('bld,de->ble', xn, w_in)
    offsets = np.cumsum(IN_SPLITS)[:-1].tolist()
    (gq, gk, gv, gr, glr, mq, mk, mv, mo, mif, dq, dk, dv) = jnp.split(proj, offsets, axis=-1)

    log_a = jax.nn.log_sigmoid((jnp.einsum('blr,re->ble', glr, w_gla_gk) + b_gla_gk).astype(jnp.float32)) / GLA_GATE_NORM
    o_a, gla_s = _gla_chunked(gq.reshape(B, L, GLA_HEADS, GLA_DK) * GLA_DK ** -0.5,
                              gk.reshape(B, L, GLA_HEADS, GLA_DK),
                              gv.reshape(B, L, GLA_HEADS, GLA_DV),
                              log_a.reshape(B, L, GLA_HEADS, GLA_DK), gla_s0)
    o_a = (_rmsnorm(o_a.astype(h.dtype), g_gla_norm) * jax.nn.silu(gr.reshape(B, L, GLA_HEADS, GLA_DV))).reshape(B, L, GLA_VW)

    g_if = mif.astype(jnp.float32) + b_mlstm_if.astype(jnp.float32)
    h_m, c_new, n_new, m_new = _mlstm_chunked(mq.reshape(B, L, MLSTM_HEADS, MLSTM_DQK),
                                              mk.reshape(B, L, MLSTM_HEADS, MLSTM_DQK) * MLSTM_DQK ** -0.5,
                                              mv.reshape(B, L, MLSTM_HEADS, MLSTM_DV),
                                              g_if[..., :MLSTM_HEADS], jax.nn.log_sigmoid(g_if[..., MLSTM_HEADS:]),
                                              c0, n0, m0)
    o_m = (jax.nn.sigmoid(mo.reshape(B, L, MLSTM_HEADS, MLSTM_DV)) * _rmsnorm(h_m.astype(h.dtype), g_mlstm_norm)).reshape(B, L, MLSTM_VW)

    q_d = _rope_partial(dq.reshape(B, L, 2 * DIFF_HEADS, DIFF_DQK), pos)
    k_d = _rope_partial(dk.reshape(B, L, 2 * DIFF_HEADS, DIFF_DQK), pos)
    v_d = dv.reshape(B, L, DIFF_HEADS, DIFF_DV)
    if past_k is None:
        k_all, v_all, k_pos = k_d, v_d, pos
    else:
        k_all = jnp.concatenate([past_k.astype(k_d.dtype), k_d], axis=1)
        v_all = jnp.concatenate([past_v.astype(v_d.dtype), v_d], axis=1)
        k_pos = jnp.concatenate([past_pos, pos], axis=0)
    lam32 = diff_lambda.astype(jnp.float32)
    lam_init = 0.8 - 0.6 * math.exp(-0.3 * layer_idx)
    lam = jnp.exp(jnp.sum(lam32[0] * lam32[1])) - jnp.exp(jnp.sum(lam32[2] * lam32[3])) + lam_init
    attn = _diff_attention(q_d, k_all, v_all, pos, k_pos, lam)
    o_d = (_rmsnorm(attn, g_diff_norm) * (1.0 - lam_init)).reshape(B, L, DIFF_VW)

    w_a = w_branch[:GLA_VW]
    w_m = w_branch[GLA_VW:GLA_VW + MLSTM_VW]
    w_d = w_branch[GLA_VW + MLSTM_VW:]
    gates = jax.nn.sigmoid(jnp.einsum('bld,de->ble', xn, w_gate).astype(jnp.float32)).astype(h.dtype).reshape(B, L, N_BRANCH, D)
    merged = (gates[:, :, 0] * jnp.einsum('blv,vd->bld', o_a, w_a)
              + gates[:, :, 1] * jnp.einsum('blv,vd->bld', o_m, w_m)
              + gates[:, :, 2] * jnp.einsum('blv,vd->bld', o_d, w_d))
    h = h + jnp.einsum('bld,de->ble', merged, w_out)

    xf = _rmsnorm(h, g_ffn)
    hid = jax.nn.silu(jnp.einsum('bld,df->blf', xf, w_ffn_gate)) * jnp.einsum('bld,df->blf', xf, w_ffn_up)
    h = h + jnp.einsum('blf,fd->bld', hid, w_ffn_down)

    ple_gate = jax.nn.sigmoid(jnp.einsum('bld,de->ble', _rmsnorm(h, g_ple), w_ple_gate))
    h = h + ple_gate * jnp.einsum('blp,pd->bld', p_l, w_ple_proj)
    return h, (k_d, v_d, gla_s, c_new, n_new, m_new)


def setup_inputs(seed: int = 0) -> dict:
    key = jax.random.key(seed)
    ks = iter(jax.random.split(key, 40))
    f32 = jnp.float32

    def nrm(shape, scale):
        return jax.random.normal(next(ks), shape, f32) * scale

    n_pages = PAST_LEN // PAGE_SIZE
    n_used = DEC_BATCH * n_pages
    n_pool = (n_used * 5 + 3) // 4
    page_table = jax.random.permutation(next(ks), n_pool)[:n_used].reshape(DEC_BATCH, n_pages).astype(jnp.int32)
    return {
        'x_prompt': nrm((BATCH, SEQ, D_MODEL), 1.0),
        'x_sample': nrm((DEC_BATCH, DEC_SEQ, D_MODEL), 1.0),
        'cache_k': nrm((DEPTH, n_pool, PAGE_SIZE, 2 * DIFF_HEADS, DIFF_DQK), 1.0),
        'cache_v': nrm((DEPTH, n_pool, PAGE_SIZE, DIFF_HEADS, DIFF_DV), 1.0),
        'state_gla': nrm((DEPTH, DEC_BATCH, GLA_HEADS, GLA_DK, GLA_DV), 1.0),
        'state_mlstm_c': nrm((DEPTH, DEC_BATCH, MLSTM_HEADS, MLSTM_DQK, MLSTM_DV), 1.0),
        'state_mlstm_n': nrm((DEPTH, DEC_BATCH, MLSTM_HEADS, MLSTM_DQK), 1.0),
        'state_mlstm_m': nrm((DEPTH, DEC_BATCH, MLSTM_HEADS), 1.0),
        'page_table': page_table,
        'p_prompt': nrm((DEPTH, BATCH, SEQ, PLE_DIM), 1.0),
        'p_sample': nrm((DEPTH, DEC_BATCH, DEC_SEQ, PLE_DIM), 1.0),
        'g_mix': 1.0 + nrm((DEPTH, D_MODEL), 0.05),
        'w_in': nrm((DEPTH, D_MODEL, D_IN), D_MODEL ** -0.5),
        'w_gla_gk': nrm((DEPTH, GLA_RANK, GLA_HEADS * GLA_DK), GLA_RANK ** -0.5),
        'b_gla_gk': nrm((DEPTH, GLA_HEADS * GLA_DK), 0.1),
        'g_gla_norm': 1.0 + nrm((DEPTH, GLA_DV), 0.05),
        'b_mlstm_if': jnp.concatenate([nrm((DEPTH, MLSTM_HEADS), 0.1),
                                       MLSTM_FGATE_BIAS + nrm((DEPTH, MLSTM_HEADS), 0.1)], axis=-1),
        'g_mlstm_norm': 1.0 + nrm((DEPTH, MLSTM_DV), 0.05),
        'diff_lambda': nrm((DEPTH, 4, DIFF_DQK), 0.1),
        'g_diff_norm': 1.0 + nrm((DEPTH, DIFF_DV), 0.05),
        'w_branch': nrm((DEPTH, MIX_V, D_MODEL), GLA_VW ** -0.5),
        'w_gate': nrm((DEPTH, D_MODEL, N_BRANCH * D_MODEL), D_MODEL ** -0.5),
        'w_out': nrm((DEPTH, D_MODEL, D_MODEL), D_MODEL ** -0.5),
        'g_ffn': 1.0 + nrm((DEPTH, D_MODEL), 0.05),
        'w_ffn_gate': nrm((DEPTH, D_MODEL, D_FF), D_MODEL ** -0.5),
        'w_ffn_up': nrm((DEPTH, D_MODEL, D_FF), D_MODEL ** -0.5),
        'w_ffn_down': nrm((DEPTH, D_FF, D_MODEL), D_FF ** -0.5),
        'g_ple': 1.0 + nrm((DEPTH, D_MODEL), 0.05),
        'w_ple_gate': nrm((DEPTH, D_MODEL, D_MODEL), D_MODEL ** -0.5),
        'w_ple_proj': nrm((DEPTH, PLE_DIM, D_MODEL), PLE_DIM ** -0.5),
        'g_final': 1.0 + nrm((D_MODEL,), 0.05),
    }


def reference(x_prompt, x_sample, cache_k, cache_v, state_gla, state_mlstm_c, state_mlstm_n, state_mlstm_m,
              page_table, p_prompt, p_sample, g_mix, w_in, w_gla_gk, b_gla_gk, g_gla_norm, b_mlstm_if,
              g_mlstm_norm, diff_lambda, g_diff_norm, w_branch, w_gate, w_out, g_ffn, w_ffn_gate, w_ffn_up,
              w_ffn_down, g_ple, w_ple_gate, w_ple_proj, g_final):
    f32 = jnp.float32
    bp, lp, _ = x_prompt.shape
    bs, ls, _ = x_sample.shape
    n_pages = page_table.shape[1]
    past_len = n_pages * cache_k.shape[2]
    pos_p = jnp.arange(lp, dtype=jnp.int32)
    pos_s = past_len + jnp.arange(ls, dtype=jnp.int32)
    past_pos = jnp.arange(past_len, dtype=jnp.int32)
    gla0 = jnp.zeros((bp, GLA_HEADS, GLA_DK, GLA_DV), f32)
    c0 = jnp.zeros((bp, MLSTM_HEADS, MLSTM_DQK, MLSTM_DV), f32)
    n0 = jnp.zeros((bp, MLSTM_HEADS, MLSTM_DQK), f32)
    m0 = jnp.zeros((bp, MLSTM_HEADS), f32)
    hp, hs = x_prompt, x_sample
    new_p, new_s = [], []
    for l in range(DEPTH):
        lw = (g_mix[l], w_in[l], w_gla_gk[l], b_gla_gk[l], g_gla_norm[l], b_mlstm_if[l], g_mlstm_norm[l],
              diff_lambda[l], g_diff_norm[l], w_branch[l], w_gate[l], w_out[l], g_ffn[l], w_ffn_gate[l],
              w_ffn_up[l], w_ffn_down[l], g_ple[l], w_ple_gate[l], w_ple_proj[l])
        hp, st_p = _layer(hp, p_prompt[l], pos_p, l, lw, gla0, c0, n0, m0, None, None, None)
        new_p.append(st_p)
        past_k = cache_k[l][page_table].reshape(bs, past_len, 2 * DIFF_HEADS, DIFF_DQK)
        past_v = cache_v[l][page_table].reshape(bs, past_len, DIFF_HEADS, DIFF_DV)
        hs, st_s = _layer(hs, p_sample[l], pos_s, l, lw, state_gla[l], state_mlstm_c[l], state_mlstm_n[l],
                          state_mlstm_m[l], past_k, past_v, past_pos)
        new_s.append(st_s)
    y_prompt = _rmsnorm(hp, g_final)
    y_sample = _rmsnorm(hs, g_final)
    k_prompt, v_prompt, gla_prompt, c_prompt, n_prompt, m_prompt = [jnp.stack(t) for t in zip(*new_p)]
    k_sample, v_sample, gla_sample, c_sample, n_sample, m_sample = [jnp.stack(t) for t in zip(*new_s)]
    return (y_prompt, y_sample, k_prompt, v_prompt, gla_prompt, c_prompt, n_prompt, m_prompt,
            k_sample, v_sample, gla_sample, c_sample, n_sample, m_sample)
```

```python
import functools
import math

import numpy as np
import jax
import jax.numpy as jnp
from jax import lax
from jax.experimental import pallas as pl
from jax.experimental.pallas import tpu as pltpu

f32 = jnp.float32
bf16 = jnp.bfloat16
HIGHEST = lax.Precision.HIGHEST

D_MODEL = 1024
PLE_DIM = 256
HEADS = 4
DK = 64
DV = 128
GLA_RANK = 16
GLA_GATE_NORM = 16.0
ROT_DIM = 16
ROPE_THETA = 500000.0
CHUNK = 64
D_FF = 2816
EPS = 1e-6
VW = HEADS * DV
QW = HEADS * DK

LANES = 128
SUBLANES = 8
VMEM_LIMIT = 48 * 1024 * 1024

TILE = 512
T_GATE = 0
T_GQK, T_GV, T_GR = 6, 7, 8
T_MQK, T_MV, T_MO = 9, 10, 11
T_DQ, T_DK, T_DV = 12, 13, 14
N_TILES = 15
P_MAIN = N_TILES * TILE
L_GLR = 0
L_MI = 16
L_MF = 20

TM = 512
TN_IN = 1536


def _cparams(*sem):
    return pltpu.CompilerParams(dimension_semantics=sem, vmem_limit_bytes=VMEM_LIMIT)


def _log_sigmoid(x):
    return jnp.minimum(x, 0.0) - jnp.log1p(jnp.exp(-jnp.abs(x)))


def _sigmoid(x):
    return 1.0 / (1.0 + jnp.exp(-x))


def _rms(x, g):
    return x * lax.rsqrt(jnp.mean(x * x, axis=-1, keepdims=True) + EPS) * g


def _dot(a, b):
    return jnp.dot(a.astype(bf16), b.astype(bf16), preferred_element_type=f32)


def _dot_nt(a, b):
    return lax.dot_general(a.astype(bf16), b.astype(bf16), (((1,), (1,)), ((), ())), preferred_element_type=f32)


def _dot_tn(a, b):
    return lax.dot_general(a.astype(bf16), b.astype(bf16), (((0,), (0,)), ((), ())), preferred_element_type=f32)


def _rope_tile(x, cos, sin_up, sin_dn):
    parts = []
    for c in range(x.shape[1] // LANES):
        xc = x[:, c * LANES:(c + 1) * LANES]
        parts.append(xc * cos + pltpu.roll(xc, LANES - ROT_DIM // 2, axis=1) * sin_up
                     + pltpu.roll(xc, ROT_DIM // 2, axis=1) * sin_dn)
    return jnp.concatenate(parts, axis=1)


def _inproj_kernel(x_ref, g_ref, w_ref, ws_ref, wgk_ref, bgk_ref, bif_ref, cos_ref, sup_ref, sdn_ref,
                   proj_ref, la_ref, gif_ref, xn_scr):
    j = pl.program_id(1)

    @pl.when(j == 0)
    def _():
        xn = _rms(x_ref[...], g_ref[...]).astype(bf16)
        xn_scr[...] = xn
        small = jnp.dot(xn, ws_ref[...], preferred_element_type=f32)
        z = jnp.dot(small, wgk_ref[...], preferred_element_type=f32, precision=HIGHEST) + bgk_ref[...]
        la_ref[...] = _log_sigmoid(z) * (1.0 / GLA_GATE_NORM)
        gi = small + bif_ref[...]
        lane = lax.broadcasted_iota(jnp.int32, gi.shape, 1)
        gif_ref[...] = jnp.where((lane >= L_MF) & (lane < L_MF + HEADS), _log_sigmoid(gi), gi)

    acc = jnp.dot(xn_scr[...], w_ref[...], preferred_element_type=f32)

    @pl.when(j < 2)
    def _():
        proj_ref[...] = _sigmoid(acc)

    @pl.when((j == 2) | (j == 3))
    def _():
        proj_ref[...] = acc

    @pl.when(j == 4)
    def _():
        cos, sup, sdn = cos_ref[...], sup_ref[...], sdn_ref[...]
        proj_ref[:, 0:TILE] = _rope_tile(acc[:, 0:TILE], cos, sup, sdn)
        proj_ref[:, TILE:2 * TILE] = _rope_tile(acc[:, TILE:2 * TILE], cos, sup, sdn)
        proj_ref[:, 2 * TILE:3 * TILE] = acc[:, 2 * TILE:3 * TILE]


def _inproj(h, g_mix, w_main, w_small, w_gk, b_gk, b_if, cos_t, sup_t, sdn_t, n_prompt_blocks, seq_blocks):
    T = h.shape[0]
    nt = T // TM
    nj = P_MAIN // TN_IN

    def tab_map(i, j):
        return (jnp.where(i < n_prompt_blocks, i % seq_blocks, seq_blocks), 0)

    return pl.pallas_call(
        _inproj_kernel,
        grid=(nt, nj),
        in_specs=[
            pl.BlockSpec((TM, D_MODEL), lambda i, j: (i, 0)),
            pl.BlockSpec((1, D_MODEL), lambda i, j: (0, 0)),
            pl.BlockSpec((D_MODEL, TN_IN), lambda i, j: (0, j)),
            pl.BlockSpec((D_MODEL, LANES), lambda i, j: (0, 0)),
            pl.BlockSpec((LANES, QW), lambda i, j: (0, 0)),
            pl.BlockSpec((1, QW), lambda i, j: (0, 0)),
            pl.BlockSpec((1, LANES), lambda i, j: (0, 0)),
            pl.BlockSpec((TM, LANES), tab_map),
            pl.BlockSpec((TM, LANES), tab_map),
            pl.BlockSpec((TM, LANES), tab_map),
        ],
        out_specs=[
            pl.BlockSpec((TM, TN_IN), lambda i, j: (i, j)),
            pl.BlockSpec((TM, QW), lambda i, j: (i, 0)),
            pl.BlockSpec((TM, LANES), lambda i, j: (i, 0)),
        ],
        out_shape=[
            jax.ShapeDtypeStruct((T, P_MAIN), f32),
            jax.ShapeDtypeStruct((T, QW), f32),
            jax.ShapeDtypeStruct((T, LANES), f32),
        ],
        scratch_shapes=[pltpu.VMEM((TM, D_MODEL), bf16)],
        compiler_params=_cparams("parallel", "arbitrary"),
        name="inproj",
    )(h, g_mix, w_main, w_small, w_gk, b_gk, b_if, cos_t, sup_t, sdn_t)


def _tril(n):
    r = lax.broadcasted_iota(jnp.int32, (n, n), 0)
    c = lax.broadcasted_iota(jnp.int32, (n, n), 1)
    return r >= c


def _gla_prompt_kernel(q_ref, k_ref, v_ref, la_ref, o_ref, s_out_ref, s_scr, *, n_chunks):
    blk = pl.program_id(1)

    @pl.when(blk == 0)
    def _():
        s_scr[...] = jnp.zeros_like(s_scr)

    ltri = _tril(CHUNK).astype(f32)
    r2 = lax.broadcasted_iota(jnp.int32, (2 * CHUNK, CHUNK), 0)
    c2 = lax.broadcasted_iota(jnp.int32, (2 * CHUNK, CHUNK), 1)
    tril2 = jnp.where(r2 >= CHUNK, r2 - CHUNK, r2) >= c2
    lane = lax.broadcasted_iota(jnp.int32, (CHUNK, LANES), 1)
    lo = (lane < DK).astype(f32)
    hi = 1.0 - lo

    def chunk(c, carry):
        rows = pl.ds(pl.multiple_of(c * CHUNK, CHUNK), CHUNK)
        g = la_ref[rows, :]
        b = jnp.dot(ltri, g, preferred_element_type=f32, precision=HIGHEST)
        b_last = b[CHUNK - 1:CHUNK, :]
        q = q_ref[rows, :]
        k = k_ref[rows, :]
        qg = q * jnp.exp(b) * (DK ** -0.5)
        kg = k * jnp.exp(-b)
        kd = k * jnp.exp(b_last - b)
        dec = jnp.exp(b_last)
        for p in range(HEADS // 2):
            ls = slice(p * LANES, (p + 1) * LANES)
            qg_p, kg_p, kd_p = qg[:, ls], kg[:, ls], kd[:, ls]
            qs = jnp.concatenate([qg_p * lo, qg_p * hi], axis=0)
            a = jnp.where(tril2, _dot_nt(qs, kg_p), 0.0)
            s_p = s_scr[p]
            inter = _dot(qs, s_p)
            v0 = v_ref[rows, pl.ds((2 * p) * DV, DV)]
            v1 = v_ref[rows, pl.ds((2 * p + 1) * DV, DV)]
            o_ref[rows, pl.ds((2 * p) * DV, DV)] = inter[:CHUNK] + _dot(a[:CHUNK], v0)
            o_ref[rows, pl.ds((2 * p + 1) * DV, DV)] = inter[CHUNK:] + _dot(a[CHUNK:], v1)
            dcol = jnp.broadcast_to(dec[:, ls], (LANES, LANES)).T
            s_scr[p] = dcol * s_p + _dot_tn(kd_p * lo, v0) + _dot_tn(kd_p * hi, v1)
        return carry

    lax.fori_loop(0, n_chunks, chunk, 0)

    @pl.when(blk == pl.num_programs(1) - 1)
    def _():
        s_out_ref[...] = s_scr[...]


def _gla_prompt(proj, la, batch, seq, rows_per_block):
    nb = seq // rows_per_block
    n_chunks = rows_per_block // CHUNK
    R = rows_per_block
    qw_blocks = TILE // QW
    return pl.pallas_call(
        functools.partial(_gla_prompt_kernel, n_chunks=n_chunks),
        grid=(batch, nb),
        in_specs=[
            pl.BlockSpec((R, QW), lambda b, i: (b * nb + i, T_GQK * qw_blocks)),
            pl.BlockSpec((R, QW), lambda b, i: (b * nb + i, T_GQK * qw_blocks + 1)),
            pl.BlockSpec((R, VW), lambda b, i: (b * nb + i, T_GV)),
            pl.BlockSpec((R, QW), lambda b, i: (b * nb + i, 0)),
        ],
        out_specs=[
            pl.BlockSpec((R, VW), lambda b, i: (b * nb + i, 0)),
            pl.BlockSpec((None, HEADS // 2, LANES, LANES), lambda b, i: (b, 0, 0, 0)),
        ],
        out_shape=[
            jax.ShapeDtypeStruct((batch * seq, VW), f32),
            jax.ShapeDtypeStruct((batch, HEADS // 2, LANES, LANES), f32),
        ],
        scratch_shapes=[pltpu.VMEM((HEADS // 2, LANES, LANES), f32)],
        compiler_params=_cparams("parallel", "arbitrary"),
        name="gla_prompt",
    )(proj, proj, proj, la)


def _mlstm_prompt_kernel(q_ref, k_ref, v_ref, gif_ref, h_ref, c_out_ref, n_out_ref, m_out_ref,
                         c_scr, n_scr, m_scr, *, n_chunks):
    blk = pl.program_id(1)

    @pl.when(blk == 0)
    def _():
        c_scr[...] = jnp.zeros_like(c_scr)
        n_scr[...] = jnp.zeros_like(n_scr)
        m_scr[...] = jnp.zeros_like(m_scr)

    tril = _tril(CHUNK)
    ltri = tril.astype(f32)

    def chunk(c, carry):
        rows = pl.ds(pl.multiple_of(c * CHUNK, CHUNK), CHUNK)
        g = gif_ref[rows, :]
        fc = jnp.dot(ltri, g, preferred_element_type=f32, precision=HIGHEST)
        g_t = g.T
        fc_t = fc.T
        q_all = q_ref[rows, :]
        k_all = k_ref[rows, :] * (DK ** -0.5)
        for h in range(HEADS):
            i_col = g[:, L_MI + h:L_MI + h + 1]
            f_col = fc[:, L_MF + h:L_MF + h + 1]
            i_row = g_t[L_MI + h:L_MI + h + 1, :]
            f_row = fc_t[L_MF + h:L_MF + h + 1, :]
            f_last = fc[CHUNK - 1:CHUNK, L_MF + h:L_MF + h + 1]
            m_prev = m_scr[h:h + 1, 0:1]
            n_row = n_scr[h:h + 1, :]
            c_h = c_scr[h]
            q = q_all[:, h * DK:(h + 1) * DK]
            k = k_all[:, h * DK:(h + 1) * DK]
            v = v_ref[rows, pl.ds(h * DV, DV)]

            log_d = jnp.where(tril, f_col - f_row + i_row, -jnp.inf)
            m_inter = m_prev + f_col
            m_row = jnp.maximum(m_inter, jnp.max(log_d, axis=1, keepdims=True))
            d = jnp.exp(log_d - m_row)
            w_inter = jnp.exp(m_inter - m_row)
            s = _dot_nt(q, k) * d
            num = w_inter * _dot(q, c_h) + _dot(s, v)
            den = w_inter * jnp.sum(q * n_row, axis=1, keepdims=True) + jnp.sum(s, axis=1, keepdims=True)
            h_ref[rows, pl.ds(h * DV, DV)] = num / jnp.maximum(jnp.abs(den), jnp.exp(-m_row))

            log_w = f_last - f_col + i_col
            m_new = jnp.maximum(m_prev + f_last, jnp.max(log_w, axis=0, keepdims=True))
            wk = jnp.exp(log_w - m_new) * k
            decay = jnp.exp(m_prev + f_last - m_new)
            c_scr[h] = decay * c_h + _dot_tn(wk, v)
            n_scr[h:h + 1, :] = decay * n_row + jnp.sum(wk, axis=0, keepdims=True)
            m_scr[h:h + 1, :] = jnp.broadcast_to(m_new, (1, LANES))
        return carry

    lax.fori_loop(0, n_chunks, chunk, 0)

    @pl.when(blk == pl.num_programs(1) - 1)
    def _():
        c_out_ref[...] = c_scr[...]
        n_out_ref[...] = n_scr[0:HEADS, :]
        m_out_ref[...] = m_scr[0:HEADS, :]


def _mlstm_prompt(proj, gif, batch, seq, rows_per_block):
    nb = seq // rows_per_block
    n_chunks = rows_per_block // CHUNK
    R = rows_per_block
    qw_blocks = TILE // QW
    return pl.pallas_call(
        functools.partial(_mlstm_prompt_kernel, n_chunks=n_chunks),
        grid=(batch, nb),
        in_specs=[
            pl.BlockSpec((R, QW), lambda b, i: (b * nb + i, T_MQK * qw_blocks)),
            pl.BlockSpec((R, QW), lambda b, i: (b * nb + i, T_MQK * qw_blocks + 1)),
            pl.BlockSpec((R, VW), lambda b, i: (b * nb + i, T_MV)),
            pl.BlockSpec((R, LANES), lambda b, i: (b * nb + i, 0)),
        ],
        out_specs=[
            pl.BlockSpec((R, VW), lambda b, i: (b * nb + i, 0)),
            pl.BlockSpec((None, HEADS, DK, DV), lambda b, i: (b, 0, 0, 0)),
            pl.BlockSpec((None, HEADS, DK), lambda b, i: (b, 0, 0)),
            pl.BlockSpec((None, HEADS, LANES), lambda b, i: (b, 0, 0)),
        ],
        out_shape=[
            jax.ShapeDtypeStruct((batch * seq, VW), f32),
            jax.ShapeDtypeStruct((batch, HEADS, DK, DV), f32),
            jax.ShapeDtypeStruct((batch, HEADS, DK), f32),
            jax.ShapeDtypeStruct((batch, HEADS, LANES), f32),
        ],
        scratch_shapes=[
            pltpu.VMEM((HEADS, DK, DV), f32),
            pltpu.VMEM((SUBLANES, DK), f32),
            pltpu.VMEM((SUBLANES, LANES), f32),
        ],
        compiler_params=_cparams("parallel", "arbitrary"),
        name="mlstm_prompt",
    )(proj, proj, proj, gif)


def _diff_lambda_value(lam_ref, lam_init):
    lam = lam_ref[...]
    s1 = jnp.sum(lam[0:1] * lam[1:2], axis=1, keepdims=True)
    s2 = jnp.sum(lam[2:3] * lam[3:4], axis=1, keepdims=True)
    return jnp.exp(s1) - jnp.exp(s2) + lam_init


def _attn_prompt_kernel(qi_ref, ki_ref, q_ref, k_ref, v_ref, lam_ref, o_ref, q2_scr, m_scr, l_scr, acc_scr,
                        *, tq, lam_init):
    p = pl.program_id(2)
    qi = qi_ref[p]
    ki = ki_ref[p]

    @pl.when(ki == 0)
    def _():
        q = q_ref[...] * (DK ** -0.5)
        lane = lax.broadcasted_iota(jnp.int32, q.shape, 1)
        q2_scr[0:tq, :] = jnp.where(lane < DK, q, 0.0).astype(bf16)
        q2_scr[tq:2 * tq, :] = jnp.where(lane >= DK, q, 0.0).astype(bf16)
        m_scr[...] = jnp.full_like(m_scr, -jnp.inf)
        l_scr[...] = jnp.zeros_like(l_scr)
        acc_scr[...] = jnp.zeros_like(acc_scr)

    def step(masked):
        s = _dot_nt(q2_scr[...], k_ref[...])
        if masked:
            r = lax.broadcasted_iota(jnp.int32, s.shape, 0)
            c = lax.broadcasted_iota(jnp.int32, s.shape, 1)
            r = jnp.where(r >= tq, r - tq, r)
            s = jnp.where(c <= r, s, -jnp.inf)
        m_prev = m_scr[...]
        m_new = jnp.maximum(m_prev, jnp.max(s, axis=1, keepdims=True))
        alpha = jnp.exp(m_prev - m_new)
        pr = jnp.exp(s - m_new)
        l_scr[...] = alpha * l_scr[...] + jnp.sum(pr, axis=1, keepdims=True)
        acc_scr[...] = alpha * acc_scr[...] + _dot(pr, v_ref[...])
        m_scr[...] = m_new

    @pl.when(ki < qi)
    def _():
        step(False)

    @pl.when(ki == qi)
    def _():
        step(True)
        o = acc_scr[...] / l_scr[...]
        lam = _diff_lambda_value(lam_ref, lam_init)
        o_ref[...] = o[0:tq] - lam * o[tq:2 * tq]


def _attn_prompt(proj, diff_lambda, batch, seq, lam_init, tq=512):
    nq = seq // tq
    pairs = [(q, k) for q in range(nq) for k in range(q + 1)]
    qi_tab = jnp.asarray(np.array([a for a, _ in pairs], np.int32))
    ki_tab = jnp.asarray(np.array([b for _, b in pairs], np.int32))
    lb = TILE // LANES
    grid_spec = pltpu.PrefetchScalarGridSpec(
        num_scalar_prefetch=2,
        grid=(batch, HEADS, len(pairs)),
        in_specs=[
            pl.BlockSpec((tq, LANES), lambda b, h, p, qt, kt: (b * nq + qt[p], T_DQ * lb + h)),
            pl.BlockSpec((tq, LANES), lambda b, h, p, qt, kt: (b * nq + kt[p], T_DK * lb + h)),
            pl.BlockSpec((tq, LANES), lambda b, h, p, qt, kt: (b * nq + kt[p], T_DV * lb + h)),
            pl.BlockSpec((4, DK), lambda b, h, p, qt, kt: (0, 0)),
        ],
        out_specs=pl.BlockSpec((tq, LANES), lambda b, h, p, qt, kt: (b * nq + qt[p], h)),
        scratch_shapes=[
            pltpu.VMEM((2 * tq, LANES), bf16),
            pltpu.VMEM((2 * tq, 1), f32),
            pltpu.VMEM((2 * tq, 1), f32),
            pltpu.VMEM((2 * tq, DV), f32),
        ],
    )
    return pl.pallas_call(
        functools.partial(_attn_prompt_kernel, tq=tq, lam_init=lam_init),
        grid_spec=grid_spec,
        out_shape=jax.ShapeDtypeStruct((batch * seq, VW), f32),
        compiler_params=_cparams("parallel", "parallel", "arbitrary"),
        name="attn_prompt",
    )(qi_tab, ki_tab, proj, proj, proj, diff_lambda)


N_NEW = SUBLANES
Q_ROWS = 2 * HEADS * 4


def _attn_sample_kernel(pt_ref, qbd_ref, kn_ref, vn_ref, lam_ref, ck_hbm, cv_hbm, o_ref, kbuf, vbuf, sem,
                        *, layer, n_pages, page, dec_seq, lam_init):
    b = pl.program_id(0)
    nb = pl.num_programs(0)
    past = n_pages * page
    slot = b % 2

    def copies(seq_idx, s):
        out = []
        for pg in range(n_pages):
            src = pt_ref[seq_idx, pg]
            rows = pl.ds(pg * page, page)
            out.append(pltpu.make_async_copy(ck_hbm.at[layer, src], kbuf.at[s, rows], sem.at[0, s]))
            out.append(pltpu.make_async_copy(cv_hbm.at[layer, src], vbuf.at[s, rows], sem.at[1, s]))
        return out

    @pl.when(b == 0)
    def _():
        tail = jnp.zeros((2, LANES, kbuf.shape[2]), f32)
        kbuf[:, pl.ds(past, LANES), :] = tail
        vbuf[:, pl.ds(past, LANES), :] = tail
        for cp in copies(0, 0):
            cp.start()

    @pl.when(b + 1 < nb)
    def _():
        for cp in copies(b + 1, 1 - slot):
            cp.start()

    for cp in copies(b, slot):
        cp.wait()

    kbuf[slot, pl.ds(past, N_NEW), :] = kn_ref[...]
    vbuf[slot, pl.ds(past, N_NEW), :] = vn_ref[...]

    s = _dot_nt(qbd_ref[...], kbuf[slot])
    r = lax.broadcasted_iota(jnp.int32, s.shape, 0)
    c = lax.broadcasted_iota(jnp.int32, s.shape, 1)
    step_of_row = r % dec_seq
    s = jnp.where(c - past <= step_of_row, s, -jnp.inf)
    m = jnp.max(s, axis=1, keepdims=True)
    pr = jnp.exp(s - m)
    l = jnp.sum(pr, axis=1, keepdims=True)
    o = _dot(pr, vbuf[slot]) / l
    lam = _diff_lambda_value(lam_ref, lam_init)
    for hv in range(HEADS):
        t = o[hv * SUBLANES:(hv + 1) * SUBLANES, hv * DV:(hv + 1) * DV]
        o_ref[:, hv * DV:(hv + 1) * DV] = t - lam * pltpu.roll(t, SUBLANES - dec_seq, axis=0)


def _attn_sample(page_table, qbd, k_new, v_new, diff_lambda, cache_k, cache_v, layer, lam_init, dec_seq):
    nseq, n_pages = page_table.shape
    page = cache_k.shape[2]
    width = cache_k.shape[3]
    rows = n_pages * page + LANES
    grid_spec = pltpu.PrefetchScalarGridSpec(
        num_scalar_prefetch=1,
        grid=(nseq,),
        in_specs=[
            pl.BlockSpec((None, Q_ROWS, width), lambda b, pt: (b, 0, 0)),
            pl.BlockSpec((None, N_NEW, width), lambda b, pt: (b, 0, 0)),
            pl.BlockSpec((None, N_NEW, width), lambda b, pt: (b, 0, 0)),
            pl.BlockSpec((4, DK), lambda b, pt: (0, 0)),
            pl.BlockSpec(memory_space=pl.ANY),
            pl.BlockSpec(memory_space=pl.ANY),
        ],
        out_specs=pl.BlockSpec((None, SUBLANES, VW), lambda b, pt: (b, 0, 0)),
        scratch_shapes=[
            pltpu.VMEM((2, rows, width), f32),
            pltpu.VMEM((2, rows, width), f32),
            pltpu.SemaphoreType.DMA((2, 2)),
        ],
    )
    return pl.pallas_call(
        functools.partial(_attn_sample_kernel, layer=layer, n_pages=n_pages, page=page, dec_seq=dec_seq,
                          lam_init=lam_init),
        grid_spec=grid_spec,
        out_shape=jax.ShapeDtypeStruct((nseq, SUBLANES, VW), f32),
        compiler_params=_cparams("arbitrary"),
        name="attn_sample",
    )(page_table, qbd, k_new, v_new, diff_lambda, cache_k, cache_v)


def _head_of_lane(shape):
    return lax.broadcasted_iota(jnp.int32, shape, 1) // DK


def _expand_heads(cols, base, head):
    out = cols[:, base + HEADS - 1:base + HEADS]
    for h in range(HEADS - 2, -1, -1):
        out = jnp.where(head == h, cols[:, base + h:base + h + 1], out)
    return out


def _sample_prep_kernel(gq_ref, gk_ref, la_ref, mq_ref, mk_ref, gif_ref, n0_ref, m0_ref,
                        ga_ref, gkk_ref, gqq_ref, ma_ref, mkk_ref, mqq_ref, n_out_ref, m_out_ref, *, nseq, dec_seq):
    head = _head_of_lane((nseq, QW))
    ga_ref[...] = jnp.exp(la_ref[...])
    gkk_ref[...] = gk_ref[...]
    gqq_ref[...] = gq_ref[...] * (DK ** -0.5)

    m = _expand_heads(m0_ref[...], 0, head)
    n = n0_ref[...]
    for t in range(dec_seq):
        rows = pl.ds(t * nseq, nseq)
        gates = gif_ref[rows, :]
        i_e = _expand_heads(gates, L_MI, head)
        f_e = _expand_heads(gates, L_MF, head)
        k = mk_ref[rows, :] * (DK ** -0.5)
        q = mq_ref[rows, :]
        m_new = jnp.maximum(f_e + m, i_e)
        fp = jnp.exp(f_e + m - m_new)
        ip = jnp.exp(i_e - m_new)
        n = fp * n + ip * k
        nq = n * q
        den = jnp.zeros_like(nq)
        for h in range(HEADS):
            den = jnp.where(head == h, jnp.sum(jnp.where(head == h, nq, 0.0), axis=1, keepdims=True), den)
        inv = 1.0 / jnp.maximum(jnp.abs(den), jnp.exp(-m_new))
        ma_ref[rows, :] = fp
        mkk_ref[rows, :] = ip * k
        mqq_ref[rows, :] = q * inv
        m = m_new
    n_out_ref[...] = n
    m_out_ref[...] = m


def _sample_prep(gq, gk, la, mq, mk, gif, n0, m0, nseq, dec_seq):
    rows = nseq * dec_seq
    tok = jax.ShapeDtypeStruct((rows, QW), f32)
    st = jax.ShapeDtypeStruct((nseq, QW), f32)
    return pl.pallas_call(
        functools.partial(_sample_prep_kernel, nseq=nseq, dec_seq=dec_seq),
        out_shape=[tok, tok, tok, tok, tok, tok, st, st],
        compiler_params=pltpu.CompilerParams(vmem_limit_bytes=VMEM_LIMIT),
        name="sample_prep",
    )(gq, gk, la, mq, mk, gif, n0, m0)


SEQ_PER_STEP = 8


def _sample_state_kernel(a_ref, k_ref, q_ref, v_ref, s_ref, o_ref, s_out_ref, *, dec_seq):
    for bb in range(SEQ_PER_STEP):
        for h in range(HEADS):
            s = s_ref[bb, h]
            for t in range(dec_seq):
                col = t * SEQ_PER_STEP + bb
                rows = slice(h * DK, (h + 1) * DK)
                a = a_ref[rows, col:col + 1]
                k = k_ref[rows, col:col + 1]
                q = q_ref[rows, col:col + 1]
                v = v_ref[col:col + 1, h * DV:(h + 1) * DV]
                s = a * s + k * v
                o_ref[col:col + 1, h * DV:(h + 1) * DV] = jnp.sum(q * s, axis=0, keepdims=True)
            s_out_ref[bb, h] = s


def _sample_state(a_t, k_t, q_t, v_b, state, dec_seq):
    nblk = a_t.shape[0]
    cols = dec_seq * SEQ_PER_STEP
    return pl.pallas_call(
        functools.partial(_sample_state_kernel, dec_seq=dec_seq),
        grid=(nblk,),
        in_specs=[
            pl.BlockSpec((None, QW, cols), lambda i: (i, 0, 0)),
            pl.BlockSpec((None, QW, cols), lambda i: (i, 0, 0)),
            pl.BlockSpec((None, QW, cols), lambda i: (i, 0, 0)),
            pl.BlockSpec((None, cols, VW), lambda i: (i, 0, 0)),
            pl.BlockSpec((SEQ_PER_STEP, HEADS, DK, DV), lambda i: (i, 0, 0, 0)),
        ],
        out_specs=[
            pl.BlockSpec((None, cols, VW), lambda i: (i, 0, 0)),
            pl.BlockSpec((SEQ_PER_STEP, HEADS, DK, DV), lambda i: (i, 0, 0, 0)),
        ],
        out_shape=[
            jax.ShapeDtypeStruct((nblk, cols, VW), f32),
            jax.ShapeDtypeStruct(state.shape, f32),
        ],
        compiler_params=_cparams("parallel"),
        name="sample_state",
    )(a_t, k_t, q_t, v_b, state)


def _head_rms(x, g):
    parts = []
    for h in range(HEADS):
        parts.append(_rms(x[:, h * DV:(h + 1) * DV], g))
    return jnp.concatenate(parts, axis=1)


def _merge_kernel(h_ref, oa_ref, om_ref, od_ref, gr_ref, mo_ref, g0_ref, g1_ref, g2_ref,
                  gg_ref, gm_ref, gd_ref, wa_ref, wm_ref, wd_ref, wo_ref, out_ref, *, lam_init):
    gr = gr_ref[...]
    a = _head_rms(oa_ref[...], gg_ref[...]) * (gr * _sigmoid(gr))
    m = _sigmoid(mo_ref[...]) * _head_rms(om_ref[...], gm_ref[...])
    d = _head_rms(od_ref[...], gd_ref[...]) * (1.0 - lam_init)
    merged = (g0_ref[...] * _dot(a, wa_ref[...]) + g1_ref[...] * _dot(m, wm_ref[...])
              + g2_ref[...] * _dot(d, wd_ref[...]))
    out_ref[...] = h_ref[...] + _dot(merged, wo_ref[...])


def _merge(h, o_a, o_m, o_d, proj, g_gla, g_ml, g_diff, w_a, w_m, w_d, w_out, lam_init):
    T = h.shape[0]
    row = lambda i: (i, 0)
    const = lambda i: (0, 0)
    gb = D_MODEL // TILE
    return pl.pallas_call(
        functools.partial(_merge_kernel, lam_init=lam_init),
        grid=(T // TM,),
        in_specs=[
            pl.BlockSpec((TM, D_MODEL), row),
            pl.BlockSpec((TM, VW), row),
            pl.BlockSpec((TM, VW), row),
            pl.BlockSpec((TM, VW), row),
            pl.BlockSpec((TM, VW), lambda i: (i, T_GR)),
            pl.BlockSpec((TM, VW), lambda i: (i, T_MO)),
            pl.BlockSpec((TM, D_MODEL), lambda i: (i, T_GATE // gb + 0)),
            pl.BlockSpec((TM, D_MODEL), lambda i: (i, T_GATE // gb + 1)),
            pl.BlockSpec((TM, D_MODEL), lambda i: (i, T_GATE // gb + 2)),
            pl.BlockSpec((1, DV), const),
            pl.BlockSpec((1, DV), const),
            pl.BlockSpec((1, DV), const),
            pl.BlockSpec((VW, D_MODEL), const),
            pl.BlockSpec((VW, D_MODEL), const),
            pl.BlockSpec((VW, D_MODEL), const),
            pl.BlockSpec((D_MODEL, D_MODEL), const),
        ],
        out_specs=pl.BlockSpec((TM, D_MODEL), row),
        out_shape=jax.ShapeDtypeStruct((T, D_MODEL), f32),
        compiler_params=_cparams("parallel"),
        name="merge",
    )(h, o_a, o_m, o_d, proj, proj, proj, proj, proj, g_gla, g_ml, g_diff, w_a, w_m, w_d, w_out)


def _ffn_kernel(h_ref, gf_ref, wg_ref, wu_ref, wd_ref, p_ref, gp_ref, wpg_ref, wpp_ref, gfin_ref,
                out_ref, xf_scr, acc_scr, *, final):
    j = pl.program_id(1)

    @pl.when(j == 0)
    def _():
        xf_scr[...] = _rms(h_ref[...], gf_ref[...]).astype(bf16)
        acc_scr[...] = h_ref[...]

    xf = xf_scr[...]
    gate = jnp.dot(xf, wg_ref[...], preferred_element_type=f32)
    up = jnp.dot(xf, wu_ref[...], preferred_element_type=f32)
    acc_scr[...] += _dot(gate * _sigmoid(gate) * up, wd_ref[...])

    @pl.when(j == pl.num_programs(1) - 1)
    def _():
        h2 = acc_scr[...]
        ple_gate = _sigmoid(_dot(_rms(h2, gp_ref[...]), wpg_ref[...]))
        h3 = h2 + ple_gate * _dot(p_ref[...], wpp_ref[...])
        out_ref[...] = _rms(h3, gfin_ref[...]) if final else h3


def _ffn(h, g_ffn, w_g, w_u, w_d, p, g_ple, w_pg, w_pp, g_final, final, tf=1408):
    T = h.shape[0]
    row = lambda i, j: (i, 0)
    const = lambda i, j: (0, 0)
    return pl.pallas_call(
        functools.partial(_ffn_kernel, final=final),
        grid=(T // TM, D_FF // tf),
        in_specs=[
            pl.BlockSpec((TM, D_MODEL), row),
            pl.BlockSpec((1, D_MODEL), const),
            pl.BlockSpec((D_MODEL, tf), lambda i, j: (0, j)),
            pl.BlockSpec((D_MODEL, tf), lambda i, j: (0, j)),
            pl.BlockSpec((tf, D_MODEL), lambda i, j: (j, 0)),
            pl.BlockSpec((TM, PLE_DIM), row),
            pl.BlockSpec((1, D_MODEL), const),
            pl.BlockSpec((D_MODEL, D_MODEL), const),
            pl.BlockSpec((PLE_DIM, D_MODEL), const),
            pl.BlockSpec((1, D_MODEL), const),
        ],
        out_specs=pl.BlockSpec((TM, D_MODEL), row),
        out_shape=jax.ShapeDtypeStruct((T, D_MODEL), f32),
        scratch_shapes=[pltpu.VMEM((TM, D_MODEL), bf16), pltpu.VMEM((TM, D_MODEL), f32)],
        compiler_params=_cparams("parallel", "arbitrary"),
        name="ffn",
    )(h, g_ffn, w_g, w_u, w_d, p, g_ple, w_pg, w_pp, g_final)


def _rope_tables(positions):
    half = ROT_DIM // 2
    inv_freq = ROPE_THETA ** (-jnp.arange(half, dtype=f32) * 2.0 / ROT_DIM)
    ang = positions.astype(f32)[:, None] * inv_freq[None, :]
    cos, sin = jnp.cos(ang), jnp.sin(ang)
    n = positions.shape[0]
    one = jnp.ones((n, DK - ROT_DIM), f32)
    zero = jnp.zeros((n, DK - ROT_DIM), f32)
    zh = jnp.zeros((n, half), f32)
    cos_h = jnp.concatenate([cos, cos, one], axis=1)
    up_h = jnp.concatenate([-sin, zh, zero], axis=1)
    dn_h = jnp.concatenate([zh, sin, zero], axis=1)
    rep = lambda a: jnp.concatenate([a] * (LANES // DK), axis=1)
    return rep(cos_h), rep(up_h), rep(dn_h)


def _prep_in_weights(w_in_l, w_gate_l):
    o = np.cumsum((0, QW, QW, VW, VW, GLA_RANK, QW, QW, VW, VW, 2 * HEADS, 2 * QW, 2 * QW, VW))
    seg = lambda i: w_in_l[:, int(o[i]):int(o[i + 1])]
    gq, gk, gv, gr, glr, mq, mk, mv, mo, mif, dq, dk, dv = (seg(i) for i in range(13))
    main = jnp.concatenate([w_gate_l, gq, gk, gv, gr, mq, mk, mv, mo, dq, dk, dv], axis=1).astype(bf16)
    pad = jnp.zeros((D_MODEL, LANES - GLA_RANK - 2 * HEADS), f32)
    small = jnp.concatenate([glr, mif, pad], axis=1).astype(bf16)
    return main, small


def kernel(x_prompt, x_sample, cache_k, cache_v, state_gla, state_mlstm_c, state_mlstm_n, state_mlstm_m, page_table, p_prompt, p_sample, g_mix, w_in, w_gla_gk, b_gla_gk, g_gla_norm, b_mlstm_if, g_mlstm_norm, diff_lambda, g_diff_norm, w_branch, w_gate, w_out, g_ffn, w_ffn_gate, w_ffn_up, w_ffn_down, g_ple, w_ple_gate, w_ple_proj, g_final):
    bp, lp, _ = x_prompt.shape
    bs, ls, _ = x_sample.shape
    depth = g_mix.shape[0]
    n_pages = page_table.shape[1]
    page = cache_k.shape[2]
    past = n_pages * page
    tp = bp * lp
    ts = bs * ls
    assert lp % TM == 0 and ts == TM and ls <= N_NEW and bs % SEQ_PER_STEP == 0

    def to_step_major(a):
        return jnp.swapaxes(a, 0, 1).reshape((ts,) + a.shape[2:])

    def from_step_major(a):
        return jnp.swapaxes(a.reshape((ls, bs) + a.shape[1:]), 0, 1)

    h = jnp.concatenate([x_prompt.reshape(tp, D_MODEL), to_step_major(x_sample)], axis=0)

    pos_rows = jnp.concatenate([jnp.arange(lp, dtype=jnp.int32),
                                past + jnp.repeat(jnp.arange(ls, dtype=jnp.int32), bs)])
    cos_t, sup_t, sdn_t = _rope_tables(pos_rows)

    ck = cache_k.reshape(cache_k.shape[0], cache_k.shape[1], page, 2 * QW)
    cv = cache_v.reshape(cache_v.shape[0], cache_v.shape[1], page, VW)
    nblk = bs // SEQ_PER_STEP

    def to_cols(a):
        return a.reshape(ls, nblk, SEQ_PER_STEP, QW).transpose(1, 3, 0, 2).reshape(nblk, QW, ls * SEQ_PER_STEP)

    def to_blocks(a):
        return a.reshape(ls, nblk, SEQ_PER_STEP, VW).transpose(1, 0, 2, 3).reshape(nblk, ls * SEQ_PER_STEP, VW)

    def from_blocks(a):
        return a.reshape(nblk, ls, SEQ_PER_STEP, VW).transpose(1, 0, 2, 3).reshape(ts, VW)

    eye_heads = jnp.eye(2 * HEADS, dtype=f32)
    outs_p, outs_s = [], []
    y = None
    for l in range(depth):
        lam_init = 0.8 - 0.6 * math.exp(-0.3 * l)
        w_main, w_small = _prep_in_weights(w_in[l], w_gate[l])
        w_gk = jnp.concatenate([w_gla_gk[l], jnp.zeros((LANES - GLA_RANK, QW), f32)], axis=0)
        b_if = jnp.zeros((1, LANES), f32).at[0, L_MI:L_MI + 2 * HEADS].set(b_mlstm_if[l])
        proj, la, gif = _inproj(h, g_mix[l][None], w_main, w_small, w_gk, b_gla_gk[l][None], b_if,
                                cos_t, sup_t, sdn_t, tp // TM, lp // TM)

        oa_p, gla_p = _gla_prompt(proj, la, bp, lp, TM)
        om_p, c_p, n_p, m_p = _mlstm_prompt(proj, gif, bp, lp, TM)
        od_p = _attn_prompt(proj, diff_lambda[l], bp, lp, lam_init)

        ps = proj[tp:]
        col = lambda t, lo=0, w=TILE: ps[:, t * TILE + lo:t * TILE + lo + w]
        ga, gk_, gq_, ma, mk_, mq_, n_s, m_s = _sample_prep(
            col(T_GQK, 0, QW), col(T_GQK, QW, QW), la[tp:], col(T_MQK, 0, QW), col(T_MQK, QW, QW), gif[tp:],
            state_mlstm_n[l].reshape(bs, QW), state_mlstm_m[l], bs, ls)
        oa_s, gla_s = _sample_state(to_cols(ga), to_cols(gk_), to_cols(gq_), to_blocks(col(T_GV)), state_gla[l], ls)
        om_s, c_s = _sample_state(to_cols(ma), to_cols(mk_), to_cols(mq_), to_blocks(col(T_MV)),
                                  state_mlstm_c[l], ls)
        q_s = from_step_major(col(T_DQ)).reshape(bs, ls, 2 * HEADS, DK)
        qbd = (q_s.transpose(0, 2, 1, 3)[:, :, :, None, :] * (DK ** -0.5)
               * eye_heads[None, :, None, :, None]).reshape(bs, Q_ROWS, 2 * QW).astype(bf16)
        k_s = from_step_major(col(T_DK))
        v_s = from_step_major(col(T_DV))
        pad_new = lambda a: jnp.pad(a, ((0, 0), (0, N_NEW - ls), (0, 0)))
        od_s = _attn_sample(page_table, qbd, pad_new(k_s), pad_new(v_s), diff_lambda[l], ck, cv, l, lam_init, ls)
        od_s = to_step_major(od_s[:, :ls])

        o_a = jnp.concatenate([oa_p, from_blocks(oa_s)], axis=0)
        o_m = jnp.concatenate([om_p, from_blocks(om_s)], axis=0)
        o_d = jnp.concatenate([od_p, od_s], axis=0)

        wb = w_branch[l].astype(bf16)
        h = _merge(h, o_a, o_m, o_d, proj, g_gla_norm[l][None], g_mlstm_norm[l][None], g_diff_norm[l][None],
                   wb[:VW], wb[VW:2 * VW], wb[2 * VW:], w_out[l].astype(bf16), lam_init)
        p_l = jnp.concatenate([p_prompt[l].reshape(tp, PLE_DIM), to_step_major(p_sample[l])], axis=0)
        h = _ffn(h, g_ffn[l][None], w_ffn_gate[l].astype(bf16), w_ffn_up[l].astype(bf16),
                 w_ffn_down[l].astype(bf16), p_l, g_ple[l][None], w_ple_gate[l].astype(bf16),
                 w_ple_proj[l].astype(bf16), g_final[None], l == depth - 1)

        pp = proj[:tp]
        outs_p.append((pp[:, T_DK * TILE:(T_DK + 1) * TILE].reshape(bp, lp, 2 * HEADS, DK),
                       pp[:, T_DV * TILE:(T_DV + 1) * TILE].reshape(bp, lp, HEADS, DV),
                       gla_p.reshape(bp, HEADS, DK, DV), c_p, n_p, m_p[:, :, 0]))
        outs_s.append((k_s.reshape(bs, ls, 2 * HEADS, DK), v_s.reshape(bs, ls, HEADS, DV),
                       gla_s, c_s, n_s.reshape(bs, HEADS, DK), m_s[:, ::DK]))

    y_prompt = h[:tp].reshape(bp, lp, D_MODEL)
    y_sample = from_step_major(h[tp:])
    stack = lambda items: [jnp.stack(t) for t in zip(*items)]
    return tuple([y_prompt, y_sample] + stack(outs_p) + stack(outs_s))
```

```python
import functools
import math

import numpy as np
import jax
import jax.numpy as jnp
from jax import lax
from jax.experimental import pallas as pl
from jax.experimental.pallas import tpu as pltpu

f32 = jnp.float32
bf16 = jnp.bfloat16
HIGHEST = lax.Precision.HIGHEST

D_MODEL = 1024
PLE_DIM = 256
HEADS = 4
DK = 64
DV = 128
GLA_RANK = 16
GLA_GATE_NORM = 16.0
ROT_DIM = 16
ROPE_THETA = 500000.0
CHUNK = 64
D_FF = 2816
EPS = 1e-6
LOG2E = 1.4426950408889634
VW = HEADS * DV
QW = HEADS * DK

LANES = 128
SUBLANES = 8
VMEM_LIMIT = 48 * 1024 * 1024

TILE = 512
T_GATE = 0
T_GQK, T_GV, T_GR = 6, 7, 8
T_MQK, T_MV, T_MO = 9, 10, 11
T_DQ, T_DK, T_DV = 12, 13, 14
N_TILES = 15
P_MAIN = N_TILES * TILE
L_GLR = 0
L_MI = 16
L_MF = 20

TM = 512
TN_IN = 1536


def _cparams(*sem):
    return pltpu.CompilerParams(dimension_semantics=sem, vmem_limit_bytes=VMEM_LIMIT)


def _log_sigmoid(x):
    return jnp.minimum(x, 0.0) - jnp.log1p(jnp.exp(-jnp.abs(x)))


def _sigmoid(x):
    return 1.0 / (1.0 + jnp.exp(-x))


def _rms(x, g):
    return x * lax.rsqrt(jnp.mean(x * x, axis=-1, keepdims=True) + EPS) * g


def _dot(a, b):
    return jnp.dot(a.astype(bf16), b.astype(bf16), preferred_element_type=f32)


def _dot_nt(a, b):
    return lax.dot_general(a.astype(bf16), b.astype(bf16), (((1,), (1,)), ((), ())), preferred_element_type=f32)


def _dot_tn(a, b):
    return lax.dot_general(a.astype(bf16), b.astype(bf16), (((0,), (0,)), ((), ())), preferred_element_type=f32)


def _rope_tile(x, cos, sin_up, sin_dn):
    parts = []
    for c in range(x.shape[1] // LANES):
        xc = x[:, c * LANES:(c + 1) * LANES]
        parts.append(xc * cos + pltpu.roll(xc, LANES - ROT_DIM // 2, axis=1) * sin_up
                     + pltpu.roll(xc, ROT_DIM // 2, axis=1) * sin_dn)
    return jnp.concatenate(parts, axis=1)


def _row_specs(width, n_prompt_blocks, sample_block, rank):
    if rank == 1:
        return (pl.BlockSpec((TM, width), lambda i: (jnp.minimum(i, n_prompt_blocks - 1), 0)),
                pl.BlockSpec((TM, width), lambda i: (sample_block, 0)))
    return (pl.BlockSpec((TM, width), lambda i, j: (jnp.minimum(i, n_prompt_blocks - 1), 0)),
            pl.BlockSpec((TM, width), lambda i, j: (sample_block, 0)))


def _pick_rows(n_prompt_blocks, p_ref, s_ref):
    return jnp.where(pl.program_id(0) >= n_prompt_blocks, s_ref[...], p_ref[...])


def _inproj_kernel(xp_ref, xs_ref, g_ref, w_ref, ws_ref, wgk_ref, bgk_ref, bif_ref, cos_ref, sup_ref, sdn_ref,
                   proj_ref, la_ref, gif_ref, xn_scr, *, n_prompt_blocks):
    j = pl.program_id(1)

    @pl.when(j == 0)
    def _():
        xn = _rms(_pick_rows(n_prompt_blocks, xp_ref, xs_ref), g_ref[...]).astype(bf16)
        xn_scr[...] = xn
        small = jnp.dot(xn, ws_ref[...], preferred_element_type=f32)
        z = jnp.dot(small, wgk_ref[...], preferred_element_type=f32, precision=HIGHEST) + bgk_ref[...]
        la_ref[...] = _log_sigmoid(z) * (1.0 / GLA_GATE_NORM)
        gi = small + bif_ref[...]
        lane = lax.broadcasted_iota(jnp.int32, gi.shape, 1)
        gif_ref[...] = jnp.where((lane >= L_MF) & (lane < L_MF + HEADS), _log_sigmoid(gi), gi)

    acc = jnp.dot(xn_scr[...], w_ref[...], preferred_element_type=f32)

    @pl.when(j < 2)
    def _():
        proj_ref[...] = _sigmoid(acc)

    @pl.when((j == 2) | (j == 3))
    def _():
        proj_ref[...] = acc

    @pl.when(j == 4)
    def _():
        cos, sup, sdn = cos_ref[...], sup_ref[...], sdn_ref[...]
        proj_ref[:, 0:TILE] = _rope_tile(acc[:, 0:TILE], cos, sup, sdn)
        proj_ref[:, TILE:2 * TILE] = _rope_tile(acc[:, TILE:2 * TILE], cos, sup, sdn)
        proj_ref[:, 2 * TILE:3 * TILE] = acc[:, 2 * TILE:3 * TILE]


def _inproj(h_p, h_s, sample_block, g_mix, w_main, w_small, w_gk, b_gk, b_if, cos_t, sup_t, sdn_t,
            n_prompt_blocks, seq_blocks):
    nt = n_prompt_blocks + 1
    T = nt * TM
    nj = P_MAIN // TN_IN

    def tab_map(i, j):
        return (jnp.where(i < n_prompt_blocks, i % seq_blocks, seq_blocks), 0)

    return pl.pallas_call(
        functools.partial(_inproj_kernel, n_prompt_blocks=n_prompt_blocks),
        grid=(nt, nj),
        in_specs=[
            *_row_specs(D_MODEL, n_prompt_blocks, sample_block, 2),
            pl.BlockSpec((1, D_MODEL), lambda i, j: (0, 0)),
            pl.BlockSpec((D_MODEL, TN_IN), lambda i, j: (0, j)),
            pl.BlockSpec((D_MODEL, LANES), lambda i, j: (0, 0)),
            pl.BlockSpec((LANES, QW), lambda i, j: (0, 0)),
            pl.BlockSpec((1, QW), lambda i, j: (0, 0)),
            pl.BlockSpec((1, LANES), lambda i, j: (0, 0)),
            pl.BlockSpec((TM, LANES), tab_map),
            pl.BlockSpec((TM, LANES), tab_map),
            pl.BlockSpec((TM, LANES), tab_map),
        ],
        out_specs=[
            pl.BlockSpec((TM, TN_IN), lambda i, j: (i, j)),
            pl.BlockSpec((TM, QW), lambda i, j: (i, 0)),
            pl.BlockSpec((TM, LANES), lambda i, j: (i, 0)),
        ],
        out_shape=[
            jax.ShapeDtypeStruct((T, P_MAIN), f32),
            jax.ShapeDtypeStruct((T, QW), f32),
            jax.ShapeDtypeStruct((T, LANES), f32),
        ],
        scratch_shapes=[pltpu.VMEM((TM, D_MODEL), bf16)],
        compiler_params=_cparams("parallel", "arbitrary"),
        name="inproj",
    )(h_p, h_s, g_mix, w_main, w_small, w_gk, b_gk, b_if, cos_t, sup_t, sdn_t)


def _tril(n):
    r = lax.broadcasted_iota(jnp.int32, (n, n), 0)
    c = lax.broadcasted_iota(jnp.int32, (n, n), 1)
    return r >= c


def _gla_prompt_kernel(q_ref, k_ref, v_ref, la_ref, o_ref, s_out_ref, s_scr, *, n_chunks):
    blk = pl.program_id(1)

    @pl.when(blk == 0)
    def _():
        s_scr[...] = jnp.zeros_like(s_scr)

    ltri = _tril(CHUNK).astype(f32)
    r2 = lax.broadcasted_iota(jnp.int32, (2 * CHUNK, CHUNK), 0)
    c2 = lax.broadcasted_iota(jnp.int32, (2 * CHUNK, CHUNK), 1)
    tril2 = jnp.where(r2 >= CHUNK, r2 - CHUNK, r2) >= c2
    lane = lax.broadcasted_iota(jnp.int32, (CHUNK, LANES), 1)
    lo = (lane < DK).astype(f32)
    hi = 1.0 - lo

    def chunk(c, carry):
        rows = pl.ds(pl.multiple_of(c * CHUNK, CHUNK), CHUNK)
        g = la_ref[rows, :]
        b = jnp.dot(ltri, g, preferred_element_type=f32, precision=HIGHEST)
        b_last = b[CHUNK - 1:CHUNK, :]
        q = q_ref[rows, :]
        k = k_ref[rows, :]
        qg = q * jnp.exp(b) * (DK ** -0.5)
        kg = k * jnp.exp(-b)
        kd = k * jnp.exp(b_last - b)
        dec = jnp.exp(b_last)
        for p in range(HEADS // 2):
            ls = slice(p * LANES, (p + 1) * LANES)
            qg_p, kg_p, kd_p = qg[:, ls], kg[:, ls], kd[:, ls]
            qs = jnp.concatenate([qg_p * lo, qg_p * hi], axis=0)
            a = jnp.where(tril2, _dot_nt(qs, kg_p), 0.0)
            s_p = s_scr[p]
            inter = _dot(qs, s_p)
            v0 = v_ref[rows, pl.ds((2 * p) * DV, DV)]
            v1 = v_ref[rows, pl.ds((2 * p + 1) * DV, DV)]
            o_ref[rows, pl.ds((2 * p) * DV, DV)] = inter[:CHUNK] + _dot(a[:CHUNK], v0)
            o_ref[rows, pl.ds((2 * p + 1) * DV, DV)] = inter[CHUNK:] + _dot(a[CHUNK:], v1)
            dcol = jnp.broadcast_to(dec[:, ls], (LANES, LANES)).T
            s_scr[p] = dcol * s_p + _dot_tn(kd_p * lo, v0) + _dot_tn(kd_p * hi, v1)
        return carry

    lax.fori_loop(0, n_chunks, chunk, 0)

    @pl.when(blk == pl.num_programs(1) - 1)
    def _():
        s_out_ref[...] = s_scr[...]


def _gla_prompt(proj, la, batch, seq, rows_per_block):
    nb = seq // rows_per_block
    n_chunks = rows_per_block // CHUNK
    R = rows_per_block
    qw_blocks = TILE // QW
    return pl.pallas_call(
        functools.partial(_gla_prompt_kernel, n_chunks=n_chunks),
        grid=(batch, nb),
        in_specs=[
            pl.BlockSpec((R, QW), lambda b, i: (b * nb + i, T_GQK * qw_blocks)),
            pl.BlockSpec((R, QW), lambda b, i: (b * nb + i, T_GQK * qw_blocks + 1)),
            pl.BlockSpec((R, VW), lambda b, i: (b * nb + i, T_GV)),
            pl.BlockSpec((R, QW), lambda b, i: (b * nb + i, 0)),
        ],
        out_specs=[
            pl.BlockSpec((R, VW), lambda b, i: (b * nb + i, 0)),
            pl.BlockSpec((None, HEADS // 2, LANES, LANES), lambda b, i: (b, 0, 0, 0)),
        ],
        out_shape=[
            jax.ShapeDtypeStruct((batch * seq, VW), f32),
            jax.ShapeDtypeStruct((batch, HEADS // 2, LANES, LANES), f32),
        ],
        scratch_shapes=[pltpu.VMEM((HEADS // 2, LANES, LANES), f32)],
        compiler_params=_cparams("parallel", "arbitrary"),
        name="gla_prompt",
    )(proj, proj, proj, la)


def _mlstm_prompt_kernel(q_ref, k_ref, v_ref, gif_ref, h_ref, c_out_ref, n_out_ref, m_out_ref,
                         c_scr, n_scr, m_scr, *, n_chunks):
    blk = pl.program_id(1)

    @pl.when(blk == 0)
    def _():
        c_scr[...] = jnp.zeros_like(c_scr)
        n_scr[...] = jnp.zeros_like(n_scr)
        m_scr[...] = jnp.zeros_like(m_scr)

    tril = _tril(CHUNK)
    ltri = tril.astype(f32)

    def chunk(c, carry):
        rows = pl.ds(pl.multiple_of(c * CHUNK, CHUNK), CHUNK)
        g = gif_ref[rows, :]
        fc = jnp.dot(ltri, g, preferred_element_type=f32, precision=HIGHEST)
        g_t = g.T
        fc_t = fc.T
        q_all = q_ref[rows, :]
        k_all = k_ref[rows, :] * (DK ** -0.5)
        for h in range(HEADS):
            i_col = g[:, L_MI + h:L_MI + h + 1]
            f_col = fc[:, L_MF + h:L_MF + h + 1]
            i_row = g_t[L_MI + h:L_MI + h + 1, :]
            f_row = fc_t[L_MF + h:L_MF + h + 1, :]
            f_last = fc[CHUNK - 1:CHUNK, L_MF + h:L_MF + h + 1]
            m_prev = m_scr[h:h + 1, 0:1]
            n_row = n_scr[h:h + 1, :]
            c_h = c_scr[h]
            q = q_all[:, h * DK:(h + 1) * DK]
            k = k_all[:, h * DK:(h + 1) * DK]
            v = v_ref[rows, pl.ds(h * DV, DV)]

            log_d = jnp.where(tril, f_col - f_row + i_row, -jnp.inf)
            m_inter = m_prev + f_col
            m_row = jnp.maximum(m_inter, jnp.max(log_d, axis=1, keepdims=True))
            d = jnp.exp(log_d - m_row)
            w_inter = jnp.exp(m_inter - m_row)
            s = _dot_nt(q, k) * d
            num = w_inter * _dot(q, c_h) + _dot(s, v)
            den = w_inter * jnp.sum(q * n_row, axis=1, keepdims=True) + jnp.sum(s, axis=1, keepdims=True)
            h_ref[rows, pl.ds(h * DV, DV)] = num / jnp.maximum(jnp.abs(den), jnp.exp(-m_row))

            log_w = f_last - f_col + i_col
            m_new = jnp.maximum(m_prev + f_last, jnp.max(log_w, axis=0, keepdims=True))
            wk = jnp.exp(log_w - m_new) * k
            decay = jnp.exp(m_prev + f_last - m_new)
            c_scr[h] = decay * c_h + _dot_tn(wk, v)
            n_scr[h:h + 1, :] = decay * n_row + jnp.sum(wk, axis=0, keepdims=True)
            m_scr[h:h + 1, :] = jnp.broadcast_to(m_new, (1, LANES))
        return carry

    lax.fori_loop(0, n_chunks, chunk, 0)

    @pl.when(blk == pl.num_programs(1) - 1)
    def _():
        c_out_ref[...] = c_scr[...]
        n_out_ref[...] = n_scr[0:HEADS, :]
        m_out_ref[...] = m_scr[0:HEADS, :]


def _mlstm_prompt(proj, gif, batch, seq, rows_per_block):
    nb = seq // rows_per_block
    n_chunks = rows_per_block // CHUNK
    R = rows_per_block
    qw_blocks = TILE // QW
    return pl.pallas_call(
        functools.partial(_mlstm_prompt_kernel, n_chunks=n_chunks),
        grid=(batch, nb),
        in_specs=[
            pl.BlockSpec((R, QW), lambda b, i: (b * nb + i, T_MQK * qw_blocks)),
            pl.BlockSpec((R, QW), lambda b, i: (b * nb + i, T_MQK * qw_blocks + 1)),
            pl.BlockSpec((R, VW), lambda b, i: (b * nb + i, T_MV)),
            pl.BlockSpec((R, LANES), lambda b, i: (b * nb + i, 0)),
        ],
        out_specs=[
            pl.BlockSpec((R, VW), lambda b, i: (b * nb + i, 0)),
            pl.BlockSpec((None, HEADS, DK, DV), lambda b, i: (b, 0, 0, 0)),
            pl.BlockSpec((None, HEADS, DK), lambda b, i: (b, 0, 0)),
            pl.BlockSpec((None, HEADS, LANES), lambda b, i: (b, 0, 0)),
        ],
        out_shape=[
            jax.ShapeDtypeStruct((batch * seq, VW), f32),
            jax.ShapeDtypeStruct((batch, HEADS, DK, DV), f32),
            jax.ShapeDtypeStruct((batch, HEADS, DK), f32),
            jax.ShapeDtypeStruct((batch, HEADS, LANES), f32),
        ],
        scratch_shapes=[
            pltpu.VMEM((HEADS, DK, DV), f32),
            pltpu.VMEM((SUBLANES, DK), f32),
            pltpu.VMEM((SUBLANES, LANES), f32),
        ],
        compiler_params=_cparams("parallel", "arbitrary"),
        name="mlstm_prompt",
    )(proj, proj, proj, gif)


def _diff_lambda_value(lam_ref, lam_init):
    lam = lam_ref[...]
    s1 = jnp.sum(lam[0:1] * lam[1:2], axis=1, keepdims=True)
    s2 = jnp.sum(lam[2:3] * lam[3:4], axis=1, keepdims=True)
    return jnp.exp(s1) - jnp.exp(s2) + lam_init


def _attn_prompt_kernel(qi_ref, ki_ref, q_ref, k_ref, v_ref, lam_ref, o_ref, q2_scr, m_scr, l_scr, acc_scr,
                        *, tq, lam_init):
    p = pl.program_id(2)
    qi = qi_ref[p]
    ki = ki_ref[p]

    @pl.when(ki == 0)
    def _():
        q = q_ref[...] * (DK ** -0.5 * LOG2E)
        lane = lax.broadcasted_iota(jnp.int32, q.shape, 1)
        q2_scr[0:tq, :] = jnp.where(lane < DK, q, 0.0).astype(bf16)
        q2_scr[tq:2 * tq, :] = jnp.where(lane >= DK, q, 0.0).astype(bf16)
        m_scr[...] = jnp.full_like(m_scr, -jnp.inf)
        l_scr[...] = jnp.zeros_like(l_scr)
        acc_scr[...] = jnp.zeros_like(acc_scr)

    def step(masked):
        st = _dot_nt(k_ref[...], q2_scr[...])
        if masked:
            r = lax.broadcasted_iota(jnp.int32, st.shape, 0)
            c = lax.broadcasted_iota(jnp.int32, st.shape, 1)
            c = jnp.where(c >= tq, c - tq, c)
            st = jnp.where(r <= c, st, -jnp.inf)
        m_prev = m_scr[...]
        m_new = jnp.maximum(m_prev, jnp.max(st, axis=0, keepdims=True))
        alpha = jnp.exp2(m_prev - m_new)
        pt = jnp.exp2(st - m_new)
        l_scr[...] = alpha * l_scr[...] + jnp.sum(pt, axis=0, keepdims=True)
        acc_scr[...] = alpha * acc_scr[...] + _dot_tn(v_ref[...], pt)
        m_scr[...] = m_new

    @pl.when(ki < qi)
    def _():
        step(False)

    @pl.when(ki == qi)
    def _():
        step(True)
        ot = acc_scr[...] / l_scr[...]
        lam = _diff_lambda_value(lam_ref, lam_init)
        o_ref[...] = (ot[:, 0:tq] - lam * ot[:, tq:2 * tq]).T


def _attn_prompt(proj, diff_lambda, batch, seq, lam_init, tq=512):
    nq = seq // tq
    pairs = [(q, k) for q in range(nq) for k in range(q + 1)]
    qi_tab = jnp.asarray(np.array([a for a, _ in pairs], np.int32))
    ki_tab = jnp.asarray(np.array([b for _, b in pairs], np.int32))
    lb = TILE // LANES
    grid_spec = pltpu.PrefetchScalarGridSpec(
        num_scalar_prefetch=2,
        grid=(batch, HEADS, len(pairs)),
        in_specs=[
            pl.BlockSpec((tq, LANES), lambda b, h, p, qt, kt: (b * nq + qt[p], T_DQ * lb + h)),
            pl.BlockSpec((tq, LANES), lambda b, h, p, qt, kt: (b * nq + kt[p], T_DK * lb + h)),
            pl.BlockSpec((tq, LANES), lambda b, h, p, qt, kt: (b * nq + kt[p], T_DV * lb + h)),
            pl.BlockSpec((4, DK), lambda b, h, p, qt, kt: (0, 0)),
        ],
        out_specs=pl.BlockSpec((tq, LANES), lambda b, h, p, qt, kt: (b * nq + qt[p], h)),
        scratch_shapes=[
            pltpu.VMEM((2 * tq, LANES), bf16),
            pltpu.VMEM((1, 2 * tq), f32),
            pltpu.VMEM((1, 2 * tq), f32),
            pltpu.VMEM((DV, 2 * tq), f32),
        ],
    )
    return pl.pallas_call(
        functools.partial(_attn_prompt_kernel, tq=tq, lam_init=lam_init),
        grid_spec=grid_spec,
        out_shape=jax.ShapeDtypeStruct((batch * seq, VW), f32),
        compiler_params=_cparams("parallel", "parallel", "arbitrary"),
        name="attn_prompt",
    )(qi_tab, ki_tab, proj, proj, proj, diff_lambda)


Q_ROWS = 2 * HEADS * 4


def _attn_sample_kernel(pt_ref, qbd_ref, kn_ref, vn_ref, lam_ref, ck_hbm, cv_hbm, o_ref, kbuf, vbuf, sem,
                        *, layer, n_pages, page, dec_seq, lam_init):
    b = pl.program_id(0)
    nb = pl.num_programs(0)
    past = n_pages * page
    slot = b % 2

    def copies(seq_idx, s):
        out = []
        for pg in range(n_pages):
            src = pt_ref[seq_idx, pg]
            out.append(pltpu.make_async_copy(ck_hbm.at[layer, src], kbuf.at[s, :, pl.ds(pg * page, page)],
                                             sem.at[0, s]))
            out.append(pltpu.make_async_copy(cv_hbm.at[layer, src],
                                             vbuf.at[s, pl.ds(pg * page * HEADS, page * HEADS), :], sem.at[1, s]))
        return out

    @pl.when(b == 0)
    def _():
        for cp in copies(0, 0):
            cp.start()

    @pl.when(b + 1 < nb)
    def _():
        for cp in copies(b + 1, 1 - slot):
            cp.start()

    for cp in copies(b, slot):
        cp.wait()

    kbuf[slot, :, pl.ds(past, LANES)] = kn_ref[...]
    vbuf[slot, pl.ds(past * HEADS, LANES * HEADS), :] = vn_ref[...]

    s = _dot(qbd_ref[...], kbuf[slot])
    r = lax.broadcasted_iota(jnp.int32, s.shape, 0)
    c = lax.broadcasted_iota(jnp.int32, s.shape, 1)
    s = jnp.where(c - past <= r % dec_seq, s, -jnp.inf)
    m = jnp.max(s, axis=1, keepdims=True)
    pr = jnp.exp(s - m)
    inv_l = 1.0 / jnp.sum(pr, axis=1, keepdims=True)
    lam = _diff_lambda_value(lam_ref, lam_init)
    for hv in range(HEADS):
        rows = slice(hv * SUBLANES, (hv + 1) * SUBLANES)
        v_h = vbuf[slot, pl.ds(hv, past + LANES, stride=HEADS), :]
        t = _dot(pr[rows], v_h) * inv_l[rows]
        o_ref[:, hv * DV:(hv + 1) * DV] = t - lam * pltpu.roll(t, SUBLANES - dec_seq, axis=0)


def _attn_sample(page_table, qbd, k_new_t, v_new, diff_lambda, cache_kt, cache_v2, layer, lam_init, dec_seq):
    nseq, n_pages = page_table.shape
    page = cache_kt.shape[3]
    cols = n_pages * page + LANES
    grid_spec = pltpu.PrefetchScalarGridSpec(
        num_scalar_prefetch=1,
        grid=(nseq,),
        in_specs=[
            pl.BlockSpec((None, Q_ROWS, 2 * QW), lambda b, pt: (b, 0, 0)),
            pl.BlockSpec((None, 2 * QW, LANES), lambda b, pt: (b, 0, 0)),
            pl.BlockSpec((None, LANES * HEADS, DV), lambda b, pt: (b, 0, 0)),
            pl.BlockSpec((4, DK), lambda b, pt: (0, 0)),
            pl.BlockSpec(memory_space=pl.ANY),
            pl.BlockSpec(memory_space=pl.ANY),
        ],
        out_specs=pl.BlockSpec((None, SUBLANES, VW), lambda b, pt: (b, 0, 0)),
        scratch_shapes=[
            pltpu.VMEM((2, 2 * QW, cols), f32),
            pltpu.VMEM((2, cols * HEADS, DV), f32),
            pltpu.SemaphoreType.DMA((2, 2)),
        ],
    )
    return pl.pallas_call(
        functools.partial(_attn_sample_kernel, layer=layer, n_pages=n_pages, page=page, dec_seq=dec_seq,
                          lam_init=lam_init),
        grid_spec=grid_spec,
        out_shape=jax.ShapeDtypeStruct((nseq, SUBLANES, VW), f32),
        compiler_params=_cparams("arbitrary"),
        name="attn_sample",
    )(page_table, qbd, k_new_t, v_new, diff_lambda, cache_kt, cache_v2)


def _head_of_lane(shape):
    return lax.broadcasted_iota(jnp.int32, shape, 1) // DK


def _expand_heads(cols, base, head):
    out = cols[:, base + HEADS - 1:base + HEADS]
    for h in range(HEADS - 2, -1, -1):
        out = jnp.where(head == h, cols[:, base + h:base + h + 1], out)
    return out


def _sample_prep_kernel(gq_ref, gk_ref, la_ref, mq_ref, mk_ref, gif_ref, n0_ref, m0_ref,
                        ga_ref, gkk_ref, gqq_ref, ma_ref, mkk_ref, mqq_ref, n_out_ref, m_out_ref, *, nseq, dec_seq):
    head = _head_of_lane((nseq, QW))
    ga_ref[...] = jnp.exp(la_ref[...])
    gkk_ref[...] = gk_ref[...]
    gqq_ref[...] = gq_ref[...] * (DK ** -0.5)

    m = _expand_heads(m0_ref[...], 0, head)
    n = n0_ref[...]
    for t in range(dec_seq):
        rows = pl.ds(t * nseq, nseq)
        gates = gif_ref[rows, :]
        i_e = _expand_heads(gates, L_MI, head)
        f_e = _expand_heads(gates, L_MF, head)
        k = mk_ref[rows, :] * (DK ** -0.5)
        q = mq_ref[rows, :]
        m_new = jnp.maximum(f_e + m, i_e)
        fp = jnp.exp(f_e + m - m_new)
        ip = jnp.exp(i_e - m_new)
        n = fp * n + ip * k
        nq = n * q
        den = jnp.zeros_like(nq)
        for h in range(HEADS):
            den = jnp.where(head == h, jnp.sum(jnp.where(head == h, nq, 0.0), axis=1, keepdims=True), den)
        inv = 1.0 / jnp.maximum(jnp.abs(den), jnp.exp(-m_new))
        ma_ref[rows, :] = fp
        mkk_ref[rows, :] = ip * k
        mqq_ref[rows, :] = q * inv
        m = m_new
    n_out_ref[...] = n
    m_out_ref[...] = m


def _sample_prep(gq, gk, la, mq, mk, gif, n0, m0, nseq, dec_seq):
    rows = nseq * dec_seq
    tok = jax.ShapeDtypeStruct((rows, QW), f32)
    st = jax.ShapeDtypeStruct((nseq, QW), f32)
    return pl.pallas_call(
        functools.partial(_sample_prep_kernel, nseq=nseq, dec_seq=dec_seq),
        out_shape=[tok, tok, tok, tok, tok, tok, st, st],
        compiler_params=pltpu.CompilerParams(vmem_limit_bytes=VMEM_LIMIT),
        name="sample_prep",
    )(gq, gk, la, mq, mk, gif, n0, m0)


SEQ_PER_STEP = 8


def _sample_state_kernel(a_ref, k_ref, q_ref, v_ref, s_ref, o_ref, s_out_ref, *, dec_seq):
    for bb in range(SEQ_PER_STEP):
        for h in range(HEADS):
            s = s_ref[bb, h]
            for t in range(dec_seq):
                col = t * SEQ_PER_STEP + bb
                rows = slice(h * DK, (h + 1) * DK)
                a = a_ref[rows, col:col + 1]
                k = k_ref[rows, col:col + 1]
                q = q_ref[rows, col:col + 1]
                v = v_ref[col:col + 1, h * DV:(h + 1) * DV]
                s = a * s + k * v
                o_ref[col:col + 1, h * DV:(h + 1) * DV] = jnp.sum(q * s, axis=0, keepdims=True)
            s_out_ref[bb, h] = s


def _sample_state(a_t, k_t, q_t, v_b, state, dec_seq):
    nblk = a_t.shape[0]
    cols = dec_seq * SEQ_PER_STEP
    return pl.pallas_call(
        functools.partial(_sample_state_kernel, dec_seq=dec_seq),
        grid=(nblk,),
        in_specs=[
            pl.BlockSpec((None, QW, cols), lambda i: (i, 0, 0)),
            pl.BlockSpec((None, QW, cols), lambda i: (i, 0, 0)),
            pl.BlockSpec((None, QW, cols), lambda i: (i, 0, 0)),
            pl.BlockSpec((None, cols, VW), lambda i: (i, 0, 0)),
            pl.BlockSpec((SEQ_PER_STEP, HEADS, DK, DV), lambda i: (i, 0, 0, 0)),
        ],
        out_specs=[
            pl.BlockSpec((None, cols, VW), lambda i: (i, 0, 0)),
            pl.BlockSpec((SEQ_PER_STEP, HEADS, DK, DV), lambda i: (i, 0, 0, 0)),
        ],
        out_shape=[
            jax.ShapeDtypeStruct((nblk, cols, VW), f32),
            jax.ShapeDtypeStruct(state.shape, f32),
        ],
        compiler_params=_cparams("parallel"),
        name="sample_state",
    )(a_t, k_t, q_t, v_b, state)


def _head_rms(x, g):
    parts = []
    for h in range(HEADS):
        parts.append(_rms(x[:, h * DV:(h + 1) * DV], g))
    return jnp.concatenate(parts, axis=1)


def _merge_kernel(hp_ref, hs_ref, oap_ref, oas_ref, omp_ref, oms_ref, odp_ref, ods_ref, gr_ref, mo_ref,
                  g0_ref, g1_ref, g2_ref, gg_ref, gm_ref, gd_ref, wa_ref, wm_ref, wd_ref, wo_ref, out_ref,
                  *, lam_init, n_prompt_blocks):
    pick = functools.partial(_pick_rows, n_prompt_blocks)
    gr = gr_ref[...]
    a = _head_rms(pick(oap_ref, oas_ref), gg_ref[...]) * (gr * _sigmoid(gr))
    m = _sigmoid(mo_ref[...]) * _head_rms(pick(omp_ref, oms_ref), gm_ref[...])
    d = _head_rms(pick(odp_ref, ods_ref), gd_ref[...]) * (1.0 - lam_init)
    merged = (g0_ref[...] * _dot(a, wa_ref[...]) + g1_ref[...] * _dot(m, wm_ref[...])
              + g2_ref[...] * _dot(d, wd_ref[...]))
    out_ref[...] = pick(hp_ref, hs_ref) + _dot(merged, wo_ref[...])


def _merge(h_p, h_s, sample_block, o_a, o_m, o_d, proj, g_gla, g_ml, g_diff, w_a, w_m, w_d, w_out, lam_init,
           n_prompt_blocks):
    T = (n_prompt_blocks + 1) * TM
    row = lambda i: (i, 0)
    const = lambda i: (0, 0)
    gb = D_MODEL // TILE
    return pl.pallas_call(
        functools.partial(_merge_kernel, lam_init=lam_init, n_prompt_blocks=n_prompt_blocks),
        grid=(T // TM,),
        in_specs=[
            *_row_specs(D_MODEL, n_prompt_blocks, sample_block, 1),
            *_row_specs(VW, n_prompt_blocks, 0, 1),
            *_row_specs(VW, n_prompt_blocks, 0, 1),
            *_row_specs(VW, n_prompt_blocks, 0, 1),
            pl.BlockSpec((TM, VW), lambda i: (i, T_GR)),
            pl.BlockSpec((TM, VW), lambda i: (i, T_MO)),
            pl.BlockSpec((TM, D_MODEL), lambda i: (i, T_GATE // gb + 0)),
            pl.BlockSpec((TM, D_MODEL), lambda i: (i, T_GATE // gb + 1)),
            pl.BlockSpec((TM, D_MODEL), lambda i: (i, T_GATE // gb + 2)),
            pl.BlockSpec((1, DV), const),
            pl.BlockSpec((1, DV), const),
            pl.BlockSpec((1, DV), const),
            pl.BlockSpec((VW, D_MODEL), const),
            pl.BlockSpec((VW, D_MODEL), const),
            pl.BlockSpec((VW, D_MODEL), const),
            pl.BlockSpec((D_MODEL, D_MODEL), const),
        ],
        out_specs=pl.BlockSpec((TM, D_MODEL), row),
        out_shape=jax.ShapeDtypeStruct((T, D_MODEL), f32),
        compiler_params=_cparams("parallel"),
        name="merge",
    )(h_p, h_s, *o_a, *o_m, *o_d, proj, proj, proj, proj, proj, g_gla, g_ml, g_diff, w_a, w_m, w_d, w_out)


def _ffn_kernel(h_ref, gf_ref, wg_ref, wu_ref, wd_ref, pp_ref, ps_ref, gp_ref, wpg_ref, wpp_ref, gfin_ref,
                out_ref, xf_scr, acc_scr, *, final, n_prompt_blocks):
    j = pl.program_id(1)

    @pl.when(j == 0)
    def _():
        xf_scr[...] = _rms(h_ref[...], gf_ref[...]).astype(bf16)
        acc_scr[...] = h_ref[...]

    xf = xf_scr[...]
    gate = jnp.dot(xf, wg_ref[...], preferred_element_type=f32)
    up = jnp.dot(xf, wu_ref[...], preferred_element_type=f32)
    acc_scr[...] += _dot(gate * _sigmoid(gate) * up, wd_ref[...])

    @pl.when(j == pl.num_programs(1) - 1)
    def _():
        h2 = acc_scr[...]
        ple_gate = _sigmoid(_dot(_rms(h2, gp_ref[...]), wpg_ref[...]))
        h3 = h2 + ple_gate * _dot(_pick_rows(n_prompt_blocks, pp_ref, ps_ref), wpp_ref[...])
        out_ref[...] = _rms(h3, gfin_ref[...]) if final else h3


def _ffn(h, g_ffn, w_g, w_u, w_d, p_p, p_s, g_ple, w_pg, w_pp, g_final, final, n_prompt_blocks, tf=1408):
    T = h.shape[0]
    row = lambda i, j: (i, 0)
    const = lambda i, j: (0, 0)
    return pl.pallas_call(
        functools.partial(_ffn_kernel, final=final, n_prompt_blocks=n_prompt_blocks),
        grid=(T // TM, D_FF // tf),
        in_specs=[
            pl.BlockSpec((TM, D_MODEL), row),
            pl.BlockSpec((1, D_MODEL), const),
            pl.BlockSpec((D_MODEL, tf), lambda i, j: (0, j)),
            pl.BlockSpec((D_MODEL, tf), lambda i, j: (0, j)),
            pl.BlockSpec((tf, D_MODEL), lambda i, j: (j, 0)),
            *_row_specs(PLE_DIM, n_prompt_blocks, 0, 2),
            pl.BlockSpec((1, D_MODEL), const),
            pl.BlockSpec((D_MODEL, D_MODEL), const),
            pl.BlockSpec((PLE_DIM, D_MODEL), const),
            pl.BlockSpec((1, D_MODEL), const),
        ],
        out_specs=pl.BlockSpec((TM, D_MODEL), row),
        out_shape=jax.ShapeDtypeStruct((T, D_MODEL), f32),
        scratch_shapes=[pltpu.VMEM((TM, D_MODEL), bf16), pltpu.VMEM((TM, D_MODEL), f32)],
        compiler_params=_cparams("parallel", "arbitrary"),
        name="ffn",
    )(h, g_ffn, w_g, w_u, w_d, p_p, p_s, g_ple, w_pg, w_pp, g_final)


def _rope_tables(positions):
    half = ROT_DIM // 2
    inv_freq = ROPE_THETA ** (-jnp.arange(half, dtype=f32) * 2.0 / ROT_DIM)
    ang = positions.astype(f32)[:, None] * inv_freq[None, :]
    cos, sin = jnp.cos(ang), jnp.sin(ang)
    n = positions.shape[0]
    one = jnp.ones((n, DK - ROT_DIM), f32)
    zero = jnp.zeros((n, DK - ROT_DIM), f32)
    zh = jnp.zeros((n, half), f32)
    cos_h = jnp.concatenate([cos, cos, one], axis=1)
    up_h = jnp.concatenate([-sin, zh, zero], axis=1)
    dn_h = jnp.concatenate([zh, sin, zero], axis=1)
    rep = lambda a: jnp.concatenate([a] * (LANES // DK), axis=1)
    return rep(cos_h), rep(up_h), rep(dn_h)


def _prep_in_weights(w_in_l, w_gate_l):
    o = np.cumsum((0, QW, QW, VW, VW, GLA_RANK, QW, QW, VW, VW, 2 * HEADS, 2 * QW, 2 * QW, VW))
    seg = lambda i: w_in_l[:, int(o[i]):int(o[i + 1])]
    gq, gk, gv, gr, glr, mq, mk, mv, mo, mif, dq, dk, dv = (seg(i) for i in range(13))
    main = jnp.concatenate([w_gate_l, gq, gk, gv, gr, mq, mk, mv, mo, dq, dk, dv], axis=1).astype(bf16)
    pad = jnp.zeros((D_MODEL, LANES - GLA_RANK - 2 * HEADS), f32)
    small = jnp.concatenate([glr, mif, pad], axis=1).astype(bf16)
    return main, small


def kernel(x_prompt, x_sample, cache_k, cache_v, state_gla, state_mlstm_c, state_mlstm_n, state_mlstm_m, page_table, p_prompt, p_sample, g_mix, w_in, w_gla_gk, b_gla_gk, g_gla_norm, b_mlstm_if, g_mlstm_norm, diff_lambda, g_diff_norm, w_branch, w_gate, w_out, g_ffn, w_ffn_gate, w_ffn_up, w_ffn_down, g_ple, w_ple_gate, w_ple_proj, g_final):
    bp, lp, _ = x_prompt.shape
    bs, ls, _ = x_sample.shape
    depth = g_mix.shape[0]
    n_pages = page_table.shape[1]
    page = cache_k.shape[2]
    past = n_pages * page
    tp = bp * lp
    ts = bs * ls
    assert lp % TM == 0 and ts == TM and ls * 2 * HEADS == Q_ROWS and bs % SEQ_PER_STEP == 0 and page == LANES
    npb = tp // TM

    def to_step_major(a):
        return jnp.swapaxes(a, 0, 1).reshape((ts,) + a.shape[2:])

    def from_step_major(a):
        return jnp.swapaxes(a.reshape((ls, bs) + a.shape[1:]), 0, 1)

    h_p, h_s, h_s_block = x_prompt.reshape(tp, D_MODEL), to_step_major(x_sample), 0

    pos_rows = jnp.concatenate([jnp.arange(lp, dtype=jnp.int32),
                                past + jnp.repeat(jnp.arange(ls, dtype=jnp.int32), bs)])
    cos_t, sup_t, sdn_t = _rope_tables(pos_rows)

    ck = jnp.transpose(cache_k, (0, 1, 3, 4, 2)).reshape(cache_k.shape[0], cache_k.shape[1], 2 * QW, page)
    cv = cache_v.reshape(cache_v.shape[0], cache_v.shape[1], page * HEADS, DV)
    nblk = bs // SEQ_PER_STEP

    def to_cols(a):
        return a.reshape(ls, nblk, SEQ_PER_STEP, QW).transpose(1, 3, 0, 2).reshape(nblk, QW, ls * SEQ_PER_STEP)

    def to_blocks(a):
        return a.reshape(ls, nblk, SEQ_PER_STEP, VW).transpose(1, 0, 2, 3).reshape(nblk, ls * SEQ_PER_STEP, VW)

    def from_blocks(a):
        return a.reshape(nblk, ls, SEQ_PER_STEP, VW).transpose(1, 0, 2, 3).reshape(ts, VW)

    eye_heads = jnp.eye(2 * HEADS, dtype=f32)
    outs_p, outs_s = [], []
    y = None
    for l in range(depth):
        lam_init = 0.8 - 0.6 * math.exp(-0.3 * l)
        w_main, w_small = _prep_in_weights(w_in[l], w_gate[l])
        w_gk = jnp.concatenate([w_gla_gk[l], jnp.zeros((LANES - GLA_RANK, QW), f32)], axis=0)
        b_if = jnp.zeros((1, LANES), f32).at[0, L_MI:L_MI + 2 * HEADS].set(b_mlstm_if[l])
        proj, la, gif = _inproj(h_p, h_s, h_s_block, g_mix[l][None], w_main, w_small, w_gk, b_gla_gk[l][None],
                                b_if, cos_t, sup_t, sdn_t, npb, lp // TM)

        oa_p, gla_p = _gla_prompt(proj, la, bp, lp, TM)
        om_p, c_p, n_p, m_p = _mlstm_prompt(proj, gif, bp, lp, TM)
        od_p = _attn_prompt(proj, diff_lambda[l], bp, lp, lam_init)

        ps = proj[tp:]
        col = lambda t, lo=0, w=TILE: ps[:, t * TILE + lo:t * TILE + lo + w]
        ga, gk_, gq_, ma, mk_, mq_, n_s, m_s = _sample_prep(
            col(T_GQK, 0, QW), col(T_GQK, QW, QW), la[tp:], col(T_MQK, 0, QW), col(T_MQK, QW, QW), gif[tp:],
            state_mlstm_n[l].reshape(bs, QW), state_mlstm_m[l], bs, ls)
        oa_s, gla_s = _sample_state(to_cols(ga), to_cols(gk_), to_cols(gq_), to_blocks(col(T_GV)), state_gla[l], ls)
        om_s, c_s = _sample_state(to_cols(ma), to_cols(mk_), to_cols(mq_), to_blocks(col(T_MV)),
                                  state_mlstm_c[l], ls)
        q_s = from_step_major(col(T_DQ)).reshape(bs, ls, 2 * HEADS, DK)
        qbd = (q_s.transpose(0, 2, 1, 3)[:, :, :, None, :] * (DK ** -0.5)
               * eye_heads[None, :, None, :, None]).reshape(bs, Q_ROWS, 2 * QW).astype(bf16)
        k_s = from_step_major(col(T_DK))
        v_s = from_step_major(col(T_DV))
        k_new_t = jnp.pad(jnp.swapaxes(k_s, 1, 2), ((0, 0), (0, 0), (0, LANES - ls)))
        v_new = jnp.pad(v_s.reshape(bs, ls * HEADS, DV), ((0, 0), (0, (LANES - ls) * HEADS), (0, 0)))
        od_s = _attn_sample(page_table, qbd, k_new_t, v_new, diff_lambda[l], ck, cv, l, lam_init, ls)
        od_s = to_step_major(od_s[:, :ls])

        wb = w_branch[l].astype(bf16)
        h = _merge(h_p, h_s, h_s_block, (oa_p, from_blocks(oa_s)), (om_p, from_blocks(om_s)), (od_p, od_s), proj,
                   g_gla_norm[l][None], g_mlstm_norm[l][None], g_diff_norm[l][None],
                   wb[:VW], wb[VW:2 * VW], wb[2 * VW:], w_out[l].astype(bf16), lam_init, npb)
        h = _ffn(h, g_ffn[l][None], w_ffn_gate[l].astype(bf16), w_ffn_up[l].astype(bf16),
                 w_ffn_down[l].astype(bf16), p_prompt[l].reshape(tp, PLE_DIM), to_step_major(p_sample[l]),
                 g_ple[l][None], w_ple_gate[l].astype(bf16), w_ple_proj[l].astype(bf16), g_final[None],
                 l == depth - 1, npb)
        h_p, h_s, h_s_block = h, h, npb

        pp = proj[:tp]
        outs_p.append((pp[:, T_DK * TILE:(T_DK + 1) * TILE].reshape(bp, lp, 2 * HEADS, DK),
                       pp[:, T_DV * TILE:(T_DV + 1) * TILE].reshape(bp, lp, HEADS, DV),
                       gla_p.reshape(bp, HEADS, DK, DV), c_p, n_p, m_p[:, :, 0]))
        outs_s.append((k_s.reshape(bs, ls, 2 * HEADS, DK), v_s.reshape(bs, ls, HEADS, DV),
                       gla_s, c_s, n_s.reshape(bs, HEADS, DK), m_s[:, ::DK]))

    y_prompt = h[:tp].reshape(bp, lp, D_MODEL)
    y_sample = from_step_major(h[tp:])
    stack = lambda items: [jnp.stack(t) for t in zip(*items)]
    return tuple([y_prompt, y_sample] + stack(outs_p) + stack(outs_s))
```

```python
import functools
import math

import numpy as np
import jax
import jax.numpy as jnp
from jax import lax
from jax.experimental import pallas as pl
from jax.experimental.pallas import tpu as pltpu

f32 = jnp.float32
bf16 = jnp.bfloat16
HIGHEST = lax.Precision.HIGHEST

D_MODEL = 1024
PLE_DIM = 256
HEADS = 4
DK = 64
DV = 128
GLA_RANK = 16
GLA_GATE_NORM = 16.0
ROT_DIM = 16
ROPE_THETA = 500000.0
CHUNK = 64
D_FF = 2816
EPS = 1e-6
LOG2E = 1.4426950408889634
VW = HEADS * DV
QW = HEADS * DK

LANES = 128
SUBLANES = 8
VMEM_LIMIT = 48 * 1024 * 1024

TILE = 512
GATE_W = 3 * D_MODEL
T_GQK, T_GV, T_GR = 0, 1, 2
T_MQK, T_MV, T_MO = 3, 4, 5
T_DQ, T_DK, T_DV = 6, 7, 8
N_TILES = 9
P_MIX = N_TILES * TILE
P_MAIN = GATE_W + P_MIX
L_GLR = 0
L_MI = 16
L_MF = 20

TM = 512
TN_IN = 1536


def _cparams(*sem):
    return pltpu.CompilerParams(dimension_semantics=sem, vmem_limit_bytes=VMEM_LIMIT)


def _log_sigmoid(x):
    return jnp.minimum(x, 0.0) - jnp.log1p(jnp.exp(-jnp.abs(x)))


def _sigmoid(x):
    return 1.0 / (1.0 + jnp.exp(-x))


def _rms(x, g):
    return x * lax.rsqrt(jnp.mean(x * x, axis=-1, keepdims=True) + EPS) * g


def _dot(a, b):
    return jnp.dot(a.astype(bf16), b.astype(bf16), preferred_element_type=f32)


def _dot_nt(a, b):
    return lax.dot_general(a.astype(bf16), b.astype(bf16), (((1,), (1,)), ((), ())), preferred_element_type=f32)


def _dot_tn(a, b):
    return lax.dot_general(a.astype(bf16), b.astype(bf16), (((0,), (0,)), ((), ())), preferred_element_type=f32)


def _rope_tile(x, cos, sin_up, sin_dn):
    parts = []
    for c in range(x.shape[1] // LANES):
        xc = x[:, c * LANES:(c + 1) * LANES]
        parts.append(xc * cos + pltpu.roll(xc, LANES - ROT_DIM // 2, axis=1) * sin_up
                     + pltpu.roll(xc, ROT_DIM // 2, axis=1) * sin_dn)
    return jnp.concatenate(parts, axis=1)


def _row_specs(width, n_prompt_blocks, sample_block, rank):
    if rank == 1:
        return (pl.BlockSpec((TM, width), lambda i: (jnp.minimum(i, n_prompt_blocks - 1), 0)),
                pl.BlockSpec((TM, width), lambda i: (sample_block, 0)))
    return (pl.BlockSpec((TM, width), lambda i, j: (jnp.minimum(i, n_prompt_blocks - 1), 0)),
            pl.BlockSpec((TM, width), lambda i, j: (sample_block, 0)))


def _pick_rows(n_prompt_blocks, p_ref, s_ref):
    return jnp.where(pl.program_id(0) >= n_prompt_blocks, s_ref[...], p_ref[...])


def _inproj_kernel(xp_ref, xs_ref, g_ref, w_ref, ws_ref, wgk_ref, bgk_ref, bif_ref, cos_ref, sup_ref, sdn_ref,
                   gates_ref, proj_ref, la_ref, gif_ref, xn_scr, *, n_prompt_blocks):
    j = pl.program_id(1)

    @pl.when(j == 0)
    def _():
        xn = _rms(_pick_rows(n_prompt_blocks, xp_ref, xs_ref), g_ref[...]).astype(bf16)
        xn_scr[...] = xn
        small = jnp.dot(xn, ws_ref[...], preferred_element_type=f32)
        z = jnp.dot(small, wgk_ref[...], preferred_element_type=f32, precision=HIGHEST) + bgk_ref[...]
        la_ref[...] = _log_sigmoid(z) * (1.0 / GLA_GATE_NORM)
        gi = small + bif_ref[...]
        lane = lax.broadcasted_iota(jnp.int32, gi.shape, 1)
        gif_ref[...] = jnp.where((lane >= L_MF) & (lane < L_MF + HEADS), _log_sigmoid(gi), gi)

    acc = jnp.dot(xn_scr[...], w_ref[j], preferred_element_type=f32)

    @pl.when(j < 2)
    def _():
        gates_ref[...] = _sigmoid(acc).astype(bf16)

    @pl.when((j == 2) | (j == 3))
    def _():
        proj_ref[...] = acc

    @pl.when(j == 4)
    def _():
        cos, sup, sdn = cos_ref[...], sup_ref[...], sdn_ref[...]
        proj_ref[:, 0:TILE] = _rope_tile(acc[:, 0:TILE], cos, sup, sdn)
        proj_ref[:, TILE:2 * TILE] = _rope_tile(acc[:, TILE:2 * TILE], cos, sup, sdn)
        proj_ref[:, 2 * TILE:3 * TILE] = acc[:, 2 * TILE:3 * TILE]


def _inproj(h_p, h_s, sample_block, g_mix, w_main, w_small, w_gk, b_gk, b_if, cos_t, sup_t, sdn_t,
            n_prompt_blocks, seq_blocks):
    nt = n_prompt_blocks + 1
    T = nt * TM
    nj = P_MAIN // TN_IN
    n_gate_steps = GATE_W // TN_IN

    def tab_map(i, j):
        return (jnp.where(i < n_prompt_blocks, i % seq_blocks, seq_blocks), 0)

    return pl.pallas_call(
        functools.partial(_inproj_kernel, n_prompt_blocks=n_prompt_blocks),
        grid=(nt, nj),
        in_specs=[
            *_row_specs(D_MODEL, n_prompt_blocks, sample_block, 2),
            pl.BlockSpec((1, D_MODEL), lambda i, j: (0, 0)),
            pl.BlockSpec((nj, D_MODEL, TN_IN), lambda i, j: (0, 0, 0), pipeline_mode=pl.Buffered(1)),
            pl.BlockSpec((D_MODEL, LANES), lambda i, j: (0, 0)),
            pl.BlockSpec((LANES, QW), lambda i, j: (0, 0)),
            pl.BlockSpec((1, QW), lambda i, j: (0, 0)),
            pl.BlockSpec((1, LANES), lambda i, j: (0, 0)),
            pl.BlockSpec((TM, LANES), tab_map),
            pl.BlockSpec((TM, LANES), tab_map),
            pl.BlockSpec((TM, LANES), tab_map),
        ],
        out_specs=[
            pl.BlockSpec((TM, TN_IN), lambda i, j: (i, jnp.minimum(j, n_gate_steps - 1))),
            pl.BlockSpec((TM, TN_IN), lambda i, j: (i, jnp.maximum(j - n_gate_steps, 0))),
            pl.BlockSpec((TM, QW), lambda i, j: (i, 0)),
            pl.BlockSpec((TM, LANES), lambda i, j: (i, 0)),
        ],
        out_shape=[
            jax.ShapeDtypeStruct((T, GATE_W), bf16),
            jax.ShapeDtypeStruct((T, P_MIX), f32),
            jax.ShapeDtypeStruct((T, QW), f32),
            jax.ShapeDtypeStruct((T, LANES), f32),
        ],
        scratch_shapes=[pltpu.VMEM((TM, D_MODEL), bf16)],
        compiler_params=_cparams("parallel", "arbitrary"),
        name="inproj",
    )(h_p, h_s, g_mix, w_main, w_small, w_gk, b_gk, b_if, cos_t, sup_t, sdn_t)


def _tril(n):
    r = lax.broadcasted_iota(jnp.int32, (n, n), 0)
    c = lax.broadcasted_iota(jnp.int32, (n, n), 1)
    return r >= c


def _gla_prompt_kernel(*refs, n_chunks, batch):
    q_refs, k_refs, v_refs, la_refs = (refs[i * batch:(i + 1) * batch] for i in range(4))
    o_ref, s_out_ref, s_scr = refs[4 * batch:]
    blk = pl.program_id(0)

    @pl.when(blk == 0)
    def _():
        s_scr[...] = jnp.zeros_like(s_scr)

    ltri = _tril(CHUNK).astype(f32)
    r2 = lax.broadcasted_iota(jnp.int32, (2 * CHUNK, CHUNK), 0)
    c2 = lax.broadcasted_iota(jnp.int32, (2 * CHUNK, CHUNK), 1)
    tril2 = jnp.where(r2 >= CHUNK, r2 - CHUNK, r2) >= c2
    lane = lax.broadcasted_iota(jnp.int32, (CHUNK, LANES), 1)
    lo = (lane < DK).astype(f32)
    hi = 1.0 - lo

    def chunk(c, carry):
        rows = pl.ds(pl.multiple_of(c * CHUNK, CHUNK), CHUNK)
        for s in range(batch):
            g = la_refs[s][rows, :]
            b = jnp.dot(ltri, g, preferred_element_type=f32, precision=HIGHEST)
            b_last = b[CHUNK - 1:CHUNK, :]
            q = q_refs[s][rows, :]
            k = k_refs[s][rows, :]
            qg = q * jnp.exp(b) * (DK ** -0.5)
            kg = k * jnp.exp(-b)
            kd = k * jnp.exp(b_last - b)
            dec = jnp.exp(b_last)
            for p in range(HEADS // 2):
                ls = slice(p * LANES, (p + 1) * LANES)
                qg_p, kg_p, kd_p = qg[:, ls], kg[:, ls], kd[:, ls]
                qs = jnp.concatenate([qg_p * lo, qg_p * hi], axis=0)
                a = jnp.where(tril2, _dot_nt(qs, kg_p), 0.0)
                s_p = s_scr[s, p]
                inter = _dot(qs, s_p)
                v0 = v_refs[s][rows, pl.ds((2 * p) * DV, DV)]
                v1 = v_refs[s][rows, pl.ds((2 * p + 1) * DV, DV)]
                o_ref[s, rows, pl.ds((2 * p) * DV, DV)] = inter[:CHUNK] + _dot(a[:CHUNK], v0)
                o_ref[s, rows, pl.ds((2 * p + 1) * DV, DV)] = inter[CHUNK:] + _dot(a[CHUNK:], v1)
                dcol = jnp.broadcast_to(dec[:, ls], (LANES, LANES)).T
                s_scr[s, p] = dcol * s_p + _dot_tn(kd_p * lo, v0) + _dot_tn(kd_p * hi, v1)
        return carry

    lax.fori_loop(0, n_chunks, chunk, 0)

    @pl.when(blk == pl.num_programs(0) - 1)
    def _():
        s_out_ref[...] = s_scr[...]


def _stream_specs(width, batch, nb, col):
    return [pl.BlockSpec((TM, width), functools.partial(lambda s, i: (s * nb + i, col), s)) for s in range(batch)]


def _gla_prompt(proj, la, batch, seq):
    nb = seq // TM
    qw_blocks = TILE // QW
    return pl.pallas_call(
        functools.partial(_gla_prompt_kernel, n_chunks=TM // CHUNK, batch=batch),
        grid=(nb,),
        in_specs=[
            *_stream_specs(QW, batch, nb, T_GQK * qw_blocks),
            *_stream_specs(QW, batch, nb, T_GQK * qw_blocks + 1),
            *_stream_specs(VW, batch, nb, T_GV),
            *_stream_specs(QW, batch, nb, 0),
        ],
        out_specs=[
            pl.BlockSpec((batch, TM, VW), lambda i: (0, i, 0)),
            pl.BlockSpec((batch, HEADS // 2, LANES, LANES), lambda i: (0, 0, 0, 0)),
        ],
        out_shape=[
            jax.ShapeDtypeStruct((batch, seq, VW), f32),
            jax.ShapeDtypeStruct((batch, HEADS // 2, LANES, LANES), f32),
        ],
        scratch_shapes=[pltpu.VMEM((batch, HEADS // 2, LANES, LANES), f32)],
        compiler_params=_cparams("arbitrary"),
        name="gla_prompt",
    )(*([proj] * (3 * batch)), *([la] * batch))


def _mlstm_prompt_kernel(*refs, n_chunks, batch):
    q_refs, k_refs, v_refs, gif_refs = (refs[i * batch:(i + 1) * batch] for i in range(4))
    h_ref, c_out_ref, n_out_ref, m_out_ref, ct_scr, n_scr, m_scr = refs[4 * batch:]
    blk = pl.program_id(0)

    @pl.when(blk == 0)
    def _():
        ct_scr[...] = jnp.zeros_like(ct_scr)
        n_scr[...] = jnp.zeros_like(n_scr)
        m_scr[...] = jnp.zeros_like(m_scr)

    ltri = _tril(CHUNK).astype(f32)
    r_i = lax.broadcasted_iota(jnp.int32, (CHUNK, CHUNK), 0)
    c_i = lax.broadcasted_iota(jnp.int32, (CHUNK, CHUNK), 1)
    upper = r_i <= c_i

    def chunk(c, carry):
        rows = pl.ds(pl.multiple_of(c * CHUNK, CHUNK), CHUNK)
        for sq in range(batch):
            g = gif_refs[sq][rows, :]
            fc = jnp.dot(ltri, g, preferred_element_type=f32, precision=HIGHEST)
            g_t = g.T
            fc_t = fc.T
            q_all = q_refs[sq][rows, :]
            k_all = k_refs[sq][rows, :] * (DK ** -0.5)
            n_all = n_scr[sq]
            for h in range(HEADS):
                i_col = g[:, L_MI + h:L_MI + h + 1]
                f_col = fc[:, L_MF + h:L_MF + h + 1]
                i_row = g_t[L_MI + h:L_MI + h + 1, :]
                f_row = fc_t[L_MF + h:L_MF + h + 1, :]
                f_last = fc[CHUNK - 1:CHUNK, L_MF + h:L_MF + h + 1]
                m_prev = m_scr[sq, h:h + 1, 0:1]
                c_t = ct_scr[sq, h]
                q = q_all[:, h * DK:(h + 1) * DK]
                k = k_all[:, h * DK:(h + 1) * DK]
                v_t = v_refs[sq][rows, pl.ds(h * DV, DV)].T

                log_d = jnp.where(upper, f_row + (i_col - f_col), -jnp.inf)
                m_inter = m_prev + f_row
                m_row = jnp.maximum(m_inter, jnp.max(log_d, axis=0, keepdims=True))
                w_inter = jnp.exp(m_inter - m_row)
                sc = _dot_nt(k, q) * jnp.exp(log_d - m_row)
                num = w_inter * _dot_nt(c_t, q) + _dot(v_t, sc)
                qn = _dot_nt(n_all, q)[h:h + 1, :]
                den = w_inter * qn + jnp.sum(sc, axis=0, keepdims=True)
                h_t = num / jnp.maximum(jnp.abs(den), jnp.exp(-m_row))
                h_ref[sq, rows, pl.ds(h * DV, DV)] = h_t.T

                log_w = f_last - f_col + i_col
                m_new = jnp.maximum(m_prev + f_last, jnp.max(log_w, axis=0, keepdims=True))
                wk = jnp.exp(log_w - m_new) * k
                decay = jnp.exp(m_prev + f_last - m_new)
                ct_scr[sq, h] = decay * c_t + _dot(v_t, wk)
                n_scr[sq, h:h + 1, :] = decay * n_all[h:h + 1, :] + jnp.sum(wk, axis=0, keepdims=True)
                m_scr[sq, h:h + 1, :] = jnp.broadcast_to(m_new, (1, LANES))
        return carry

    lax.fori_loop(0, n_chunks, chunk, 0)

    @pl.when(blk == pl.num_programs(0) - 1)
    def _():
        for sq in range(batch):
            for h in range(HEADS):
                c_out_ref[sq, h] = ct_scr[sq, h].T
        n_out_ref[...] = n_scr[:, 0:HEADS, :]
        m_out_ref[...] = m_scr[:, 0:HEADS, :]


def _mlstm_prompt(proj, gif, batch, seq):
    nb = seq // TM
    qw_blocks = TILE // QW
    return pl.pallas_call(
        functools.partial(_mlstm_prompt_kernel, n_chunks=TM // CHUNK, batch=batch),
        grid=(nb,),
        in_specs=[
            *_stream_specs(QW, batch, nb, T_MQK * qw_blocks),
            *_stream_specs(QW, batch, nb, T_MQK * qw_blocks + 1),
            *_stream_specs(VW, batch, nb, T_MV),
            *_stream_specs(LANES, batch, nb, 0),
        ],
        out_specs=[
            pl.BlockSpec((batch, TM, VW), lambda i: (0, i, 0)),
            pl.BlockSpec((batch, HEADS, DK, DV), lambda i: (0, 0, 0, 0)),
            pl.BlockSpec((batch, HEADS, DK), lambda i: (0, 0, 0)),
            pl.BlockSpec((batch, HEADS, LANES), lambda i: (0, 0, 0)),
        ],
        out_shape=[
            jax.ShapeDtypeStruct((batch, seq, VW), f32),
            jax.ShapeDtypeStruct((batch, HEADS, DK, DV), f32),
            jax.ShapeDtypeStruct((batch, HEADS, DK), f32),
            jax.ShapeDtypeStruct((batch, HEADS, LANES), f32),
        ],
        scratch_shapes=[
            pltpu.VMEM((batch, HEADS, DV, DK), f32),
            pltpu.VMEM((batch, SUBLANES, DK), f32),
            pltpu.VMEM((batch, SUBLANES, LANES), f32),
        ],
        compiler_params=_cparams("arbitrary"),
        name="mlstm_prompt",
    )(*([proj] * (3 * batch)), *([gif] * batch))


def _diff_lambda_value(lam_ref, lam_init):
    lam = lam_ref[...]
    s1 = jnp.sum(lam[0:1] * lam[1:2], axis=1, keepdims=True)
    s2 = jnp.sum(lam[2:3] * lam[3:4], axis=1, keepdims=True)
    return jnp.exp(s1) - jnp.exp(s2) + lam_init


def _attn_prompt_kernel(qi_ref, ki_ref, q_ref, k_ref, v_ref, lam_ref, o_ref, q2_scr, m_scr, l_scr, acc_scr,
                        *, tq, lam_init):
    p = pl.program_id(1)
    qi = qi_ref[p]
    ki = ki_ref[p]
    pair = lambda j: slice(j * LANES, (j + 1) * LANES)

    @pl.when(ki == 0)
    def _():
        for j in range(HEADS):
            q = q_ref[:, pair(j)] * (DK ** -0.5 * LOG2E)
            lane = lax.broadcasted_iota(jnp.int32, q.shape, 1)
            q2_scr[j, 0:tq, :] = jnp.where(lane < DK, q, 0.0).astype(bf16)
            q2_scr[j, tq:2 * tq, :] = jnp.where(lane >= DK, q, 0.0).astype(bf16)
        m_scr[...] = jnp.full_like(m_scr, -jnp.inf)
        l_scr[...] = jnp.zeros_like(l_scr)
        acc_scr[...] = jnp.zeros_like(acc_scr)

    def step(masked):
        sts = [_dot_nt(k_ref[:, pair(j)], q2_scr[j]) for j in range(HEADS)]
        pts, alphas = [], []
        for j in range(HEADS):
            st = sts[j]
            if masked:
                r = lax.broadcasted_iota(jnp.int32, st.shape, 0)
                c = lax.broadcasted_iota(jnp.int32, st.shape, 1)
                c = jnp.where(c >= tq, c - tq, c)
                st = jnp.where(r <= c, st, -jnp.inf)
            m_prev = m_scr[j]
            m_new = jnp.maximum(m_prev, jnp.max(st, axis=0, keepdims=True))
            alpha = jnp.exp2(m_prev - m_new)
            pt = jnp.exp2(st - m_new)
            l_scr[j] = alpha * l_scr[j] + jnp.sum(pt, axis=0, keepdims=True)
            m_scr[j] = m_new
            pts.append(pt.astype(bf16))
            alphas.append(alpha)
        for j in range(HEADS):
            acc_scr[j] = alphas[j] * acc_scr[j] + _dot_tn(v_ref[:, pair(j)], pts[j])

    @pl.when(ki < qi)
    def _():
        step(False)

    @pl.when(ki == qi)
    def _():
        step(True)
        lam = _diff_lambda_value(lam_ref, lam_init)
        for j in range(HEADS):
            ot = acc_scr[j] / l_scr[j]
            o_ref[:, pair(j)] = (ot[:, 0:tq] - lam * ot[:, tq:2 * tq]).T


def _attn_prompt(proj, diff_lambda, batch, seq, lam_init, tq=512):
    nq = seq // tq
    pairs = [(q, k) for q in range(nq) for k in range(q + 1)]
    qi_tab = jnp.asarray(np.array([a for a, _ in pairs], np.int32))
    ki_tab = jnp.asarray(np.array([b for _, b in pairs], np.int32))
    grid_spec = pltpu.PrefetchScalarGridSpec(
        num_scalar_prefetch=2,
        grid=(batch, len(pairs)),
        in_specs=[
            pl.BlockSpec((tq, TILE), lambda b, p, qt, kt: (b * nq + qt[p], T_DQ)),
            pl.BlockSpec((tq, TILE), lambda b, p, qt, kt: (b * nq + kt[p], T_DK)),
            pl.BlockSpec((tq, TILE), lambda b, p, qt, kt: (b * nq + kt[p], T_DV)),
            pl.BlockSpec((4, DK), lambda b, p, qt, kt: (0, 0)),
        ],
        out_specs=pl.BlockSpec((tq, VW), lambda b, p, qt, kt: (b * nq + qt[p], 0)),
        scratch_shapes=[
            pltpu.VMEM((HEADS, 2 * tq, LANES), bf16),
            pltpu.VMEM((HEADS, 1, 2 * tq), f32),
            pltpu.VMEM((HEADS, 1, 2 * tq), f32),
            pltpu.VMEM((HEADS, DV, 2 * tq), f32),
        ],
    )
    return pl.pallas_call(
        functools.partial(_attn_prompt_kernel, tq=tq, lam_init=lam_init),
        grid_spec=grid_spec,
        out_shape=jax.ShapeDtypeStruct((batch * seq, VW), f32),
        compiler_params=_cparams("parallel", "arbitrary"),
        name="attn_prompt",
    )(qi_tab, ki_tab, proj, proj, proj, diff_lambda)


Q_ROWS = 2 * HEADS * 4


def _attn_sample_kernel(pt_ref, qbd_ref, kn_ref, vn_ref, lam_ref, ck_hbm, cv_hbm, o_ref, kbuf, vbuf, sem,
                        *, layer, n_pages, page, dec_seq, lam_init):
    b = pl.program_id(0)
    nb = pl.num_programs(0)
    past = n_pages * page
    slot = b % 2

    def copies(seq_idx, s):
        out = []
        for pg in range(n_pages):
            src = pt_ref[seq_idx, pg]
            out.append(pltpu.make_async_copy(ck_hbm.at[layer, src], kbuf.at[s, :, pl.ds(pg * page, page)],
                                             sem.at[0, s]))
            out.append(pltpu.make_async_copy(cv_hbm.at[layer, src],
                                             vbuf.at[s, pl.ds(pg * page * HEADS, page * HEADS), :], sem.at[1, s]))
        return out

    @pl.when(b == 0)
    def _():
        for cp in copies(0, 0):
            cp.start()

    @pl.when(b + 1 < nb)
    def _():
        for cp in copies(b + 1, 1 - slot):
            cp.start()

    for cp in copies(b, slot):
        cp.wait()

    kbuf[slot, :, pl.ds(past, LANES)] = kn_ref[...]
    vbuf[slot, pl.ds(past * HEADS, LANES * HEADS), :] = vn_ref[...]

    s = _dot(qbd_ref[...], kbuf[slot])
    r = lax.broadcasted_iota(jnp.int32, s.shape, 0)
    c = lax.broadcasted_iota(jnp.int32, s.shape, 1)
    s = jnp.where(c - past <= r % dec_seq, s, -jnp.inf)
    m = jnp.max(s, axis=1, keepdims=True)
    pr = jnp.exp(s - m)
    inv_l = 1.0 / jnp.sum(pr, axis=1, keepdims=True)
    lam = _diff_lambda_value(lam_ref, lam_init)
    for hv in range(HEADS):
        rows = slice(hv * SUBLANES, (hv + 1) * SUBLANES)
        v_h = vbuf[slot, pl.ds(hv, past + LANES, stride=HEADS), :]
        t = _dot(pr[rows], v_h) * inv_l[rows]
        o_ref[:, hv * DV:(hv + 1) * DV] = t - lam * pltpu.roll(t, SUBLANES - dec_seq, axis=0)


def _attn_sample(page_table, qbd, k_new_t, v_new, diff_lambda, cache_kt, cache_v2, layer, lam_init, dec_seq):
    nseq, n_pages = page_table.shape
    page = cache_kt.shape[3]
    cols = n_pages * page + LANES
    grid_spec = pltpu.PrefetchScalarGridSpec(
        num_scalar_prefetch=1,
        grid=(nseq,),
        in_specs=[
            pl.BlockSpec((None, Q_ROWS, 2 * QW), lambda b, pt: (b, 0, 0)),
            pl.BlockSpec((None, 2 * QW, LANES), lambda b, pt: (b, 0, 0)),
            pl.BlockSpec((None, LANES * HEADS, DV), lambda b, pt: (b, 0, 0)),
            pl.BlockSpec((4, DK), lambda b, pt: (0, 0)),
            pl.BlockSpec(memory_space=pl.ANY),
            pl.BlockSpec(memory_space=pl.ANY),
        ],
        out_specs=pl.BlockSpec((None, SUBLANES, VW), lambda b, pt: (b, 0, 0)),
        scratch_shapes=[
            pltpu.VMEM((2, 2 * QW, cols), f32),
            pltpu.VMEM((2, cols * HEADS, DV), f32),
            pltpu.SemaphoreType.DMA((2, 2)),
        ],
    )
    return pl.pallas_call(
        functools.partial(_attn_sample_kernel, layer=layer, n_pages=n_pages, page=page, dec_seq=dec_seq,
                          lam_init=lam_init),
        grid_spec=grid_spec,
        out_shape=jax.ShapeDtypeStruct((nseq, SUBLANES, VW), f32),
        compiler_params=_cparams("arbitrary"),
        name="attn_sample",
    )(page_table, qbd, k_new_t, v_new, diff_lambda, cache_kt, cache_v2)


def _head_of_lane(shape):
    return lax.broadcasted_iota(jnp.int32, shape, 1) // DK


def _expand_heads(cols, base, head):
    out = cols[:, base + HEADS - 1:base + HEADS]
    for h in range(HEADS - 2, -1, -1):
        out = jnp.where(head == h, cols[:, base + h:base + h + 1], out)
    return out


def _sample_prep_kernel(gq_ref, gk_ref, la_ref, mq_ref, mk_ref, gif_ref, n0_ref, m0_ref,
                        ga_ref, gkk_ref, gqq_ref, ma_ref, mkk_ref, mqq_ref, n_out_ref, m_out_ref, *, nseq, dec_seq):
    head = _head_of_lane((nseq, QW))
    ga_ref[...] = jnp.exp(la_ref[...])
    gkk_ref[...] = gk_ref[...]
    gqq_ref[...] = gq_ref[...] * (DK ** -0.5)

    m = _expand_heads(m0_ref[...], 0, head)
    n = n0_ref[...]
    for t in range(dec_seq):
        rows = pl.ds(t * nseq, nseq)
        gates = gif_ref[rows, :]
        i_e = _expand_heads(gates, L_MI, head)
        f_e = _expand_heads(gates, L_MF, head)
        k = mk_ref[rows, :] * (DK ** -0.5)
        q = mq_ref[rows, :]
        m_new = jnp.maximum(f_e + m, i_e)
        fp = jnp.exp(f_e + m - m_new)
        ip = jnp.exp(i_e - m_new)
        n = fp * n + ip * k
        nq = n * q
        den = jnp.zeros_like(nq)
        for h in range(HEADS):
            den = jnp.where(head == h, jnp.sum(jnp.where(head == h, nq, 0.0), axis=1, keepdims=True), den)
        inv = 1.0 / jnp.maximum(jnp.abs(den), jnp.exp(-m_new))
        ma_ref[rows, :] = fp
        mkk_ref[rows, :] = ip * k
        mqq_ref[rows, :] = q * inv
        m = m_new
    n_out_ref[...] = n
    m_out_ref[...] = m


def _sample_prep(gq, gk, la, mq, mk, gif, n0, m0, nseq, dec_seq):
    rows = nseq * dec_seq
    tok = jax.ShapeDtypeStruct((rows, QW), f32)
    st = jax.ShapeDtypeStruct((nseq, QW), f32)
    return pl.pallas_call(
        functools.partial(_sample_prep_kernel, nseq=nseq, dec_seq=dec_seq),
        out_shape=[tok, tok, tok, tok, tok, tok, st, st],
        compiler_params=pltpu.CompilerParams(vmem_limit_bytes=VMEM_LIMIT),
        name="sample_prep",
    )(gq, gk, la, mq, mk, gif, n0, m0)


SEQ_PER_STEP = 8


def _sample_state_kernel(a_ref, k_ref, q_ref, v_ref, s_ref, o_ref, s_out_ref, *, dec_seq):
    for bb in range(SEQ_PER_STEP):
        for h in range(HEADS):
            s = s_ref[bb, h]
            for t in range(dec_seq):
                col = t * SEQ_PER_STEP + bb
                rows = slice(h * DK, (h + 1) * DK)
                a = a_ref[rows, col:col + 1]
                k = k_ref[rows, col:col + 1]
                q = q_ref[rows, col:col + 1]
                v = v_ref[col:col + 1, h * DV:(h + 1) * DV]
                s = a * s + k * v
                o_ref[col:col + 1, h * DV:(h + 1) * DV] = jnp.sum(q * s, axis=0, keepdims=True)
            s_out_ref[bb, h] = s


def _sample_state(a_t, k_t, q_t, v_b, state, dec_seq):
    nblk = a_t.shape[0]
    cols = dec_seq * SEQ_PER_STEP
    return pl.pallas_call(
        functools.partial(_sample_state_kernel, dec_seq=dec_seq),
        grid=(nblk,),
        in_specs=[
            pl.BlockSpec((None, QW, cols), lambda i: (i, 0, 0)),
            pl.BlockSpec((None, QW, cols), lambda i: (i, 0, 0)),
            pl.BlockSpec((None, QW, cols), lambda i: (i, 0, 0)),
            pl.BlockSpec((None, cols, VW), lambda i: (i, 0, 0)),
            pl.BlockSpec((SEQ_PER_STEP, HEADS, DK, DV), lambda i: (i, 0, 0, 0)),
        ],
        out_specs=[
            pl.BlockSpec((None, cols, VW), lambda i: (i, 0, 0)),
            pl.BlockSpec((SEQ_PER_STEP, HEADS, DK, DV), lambda i: (i, 0, 0, 0)),
        ],
        out_shape=[
            jax.ShapeDtypeStruct((nblk, cols, VW), f32),
            jax.ShapeDtypeStruct(state.shape, f32),
        ],
        compiler_params=_cparams("parallel"),
        name="sample_state",
    )(a_t, k_t, q_t, v_b, state)


def _head_rms(x, g):
    parts = []
    for h in range(HEADS):
        parts.append(_rms(x[:, h * DV:(h + 1) * DV], g))
    return jnp.concatenate(parts, axis=1)


def _merge_kernel(hp_ref, hs_ref, oap_ref, oas_ref, omp_ref, oms_ref, odp_ref, ods_ref, gr_ref, mo_ref,
                  g0_ref, g1_ref, g2_ref, gg_ref, gm_ref, gd_ref, wa_ref, wm_ref, wd_ref, wo_ref, out_ref,
                  *, lam_init, n_prompt_blocks):
    pick = functools.partial(_pick_rows, n_prompt_blocks)
    gr = gr_ref[...]
    a = _head_rms(pick(oap_ref, oas_ref), gg_ref[...]) * (gr * _sigmoid(gr))
    m = _sigmoid(mo_ref[...]) * _head_rms(pick(omp_ref, oms_ref), gm_ref[...])
    d = _head_rms(pick(odp_ref, ods_ref), gd_ref[...]) * (1.0 - lam_init)
    merged = (g0_ref[...] * _dot(a, wa_ref[...]) + g1_ref[...] * _dot(m, wm_ref[...])
              + g2_ref[...] * _dot(d, wd_ref[...]))
    out_ref[...] = pick(hp_ref, hs_ref) + _dot(merged, wo_ref[...])


def _merge(h_p, h_s, sample_block, o_a, o_m, o_d, proj, gates, g_gla, g_ml, g_diff, w_a, w_m, w_d, w_out, lam_init,
           n_prompt_blocks):
    T = (n_prompt_blocks + 1) * TM
    row = lambda i: (i, 0)
    const = lambda i: (0, 0)
    return pl.pallas_call(
        functools.partial(_merge_kernel, lam_init=lam_init, n_prompt_blocks=n_prompt_blocks),
        grid=(T // TM,),
        in_specs=[
            *_row_specs(D_MODEL, n_prompt_blocks, sample_block, 1),
            *_row_specs(VW, n_prompt_blocks, 0, 1),
            *_row_specs(VW, n_prompt_blocks, 0, 1),
            *_row_specs(VW, n_prompt_blocks, 0, 1),
            pl.BlockSpec((TM, VW), lambda i: (i, T_GR)),
            pl.BlockSpec((TM, VW), lambda i: (i, T_MO)),
            pl.BlockSpec((TM, D_MODEL), lambda i: (i, 0)),
            pl.BlockSpec((TM, D_MODEL), lambda i: (i, 1)),
            pl.BlockSpec((TM, D_MODEL), lambda i: (i, 2)),
            pl.BlockSpec((1, DV), const),
            pl.BlockSpec((1, DV), const),
            pl.BlockSpec((1, DV), const),
            pl.BlockSpec((VW, D_MODEL), const),
            pl.BlockSpec((VW, D_MODEL), const),
            pl.BlockSpec((VW, D_MODEL), const),
            pl.BlockSpec((D_MODEL, D_MODEL), const),
        ],
        out_specs=pl.BlockSpec((TM, D_MODEL), row),
        out_shape=jax.ShapeDtypeStruct((T, D_MODEL), f32),
        compiler_params=_cparams("parallel"),
        name="merge",
    )(h_p, h_s, *o_a, *o_m, *o_d, proj, proj, gates, gates, gates, g_gla, g_ml, g_diff, w_a, w_m, w_d, w_out)


FF_SLABS = ((0, 768), (768, 768), (1536, 768), (2304, 512))


def _ffn_kernel(h_ref, gf_ref, wg_ref, wu_ref, wd_ref, pp_ref, ps_ref, gp_ref, wpg_ref, wpp_ref, gfin_ref,
                out_ref, *, final, n_prompt_blocks):
    h = h_ref[...]
    xf = _rms(h, gf_ref[...]).astype(bf16)
    h2 = h
    for lo, width in FF_SLABS:
        gate = jnp.dot(xf, wg_ref[:, lo:lo + width], preferred_element_type=f32)
        up = jnp.dot(xf, wu_ref[:, lo:lo + width], preferred_element_type=f32)
        h2 = h2 + _dot(gate * _sigmoid(gate) * up, wd_ref[lo:lo + width, :])
    ple_gate = _sigmoid(_dot(_rms(h2, gp_ref[...]), wpg_ref[...]))
    h3 = h2 + ple_gate * _dot(_pick_rows(n_prompt_blocks, pp_ref, ps_ref), wpp_ref[...])
    out_ref[...] = _rms(h3, gfin_ref[...]) if final else h3


def _ffn(h, g_ffn, w_g, w_u, w_d, p_p, p_s, g_ple, w_pg, w_pp, g_final, final, n_prompt_blocks):
    T = h.shape[0]
    row = lambda i: (i, 0)
    const = lambda i: (0, 0)
    resident = lambda shape: pl.BlockSpec(shape, const, pipeline_mode=pl.Buffered(1))
    assert FF_SLABS[-1][0] + FF_SLABS[-1][1] == D_FF
    return pl.pallas_call(
        functools.partial(_ffn_kernel, final=final, n_prompt_blocks=n_prompt_blocks),
        grid=(T // TM,),
        in_specs=[
            pl.BlockSpec((TM, D_MODEL), row),
            pl.BlockSpec((1, D_MODEL), const),
            resident((D_MODEL, D_FF)),
            resident((D_MODEL, D_FF)),
            resident((D_FF, D_MODEL)),
            *_row_specs(PLE_DIM, n_prompt_blocks, 0, 1),
            pl.BlockSpec((1, D_MODEL), const),
            resident((D_MODEL, D_MODEL)),
            resident((PLE_DIM, D_MODEL)),
            pl.BlockSpec((1, D_MODEL), const),
        ],
        out_specs=pl.BlockSpec((TM, D_MODEL), row),
        out_shape=jax.ShapeDtypeStruct((T, D_MODEL), f32),
        compiler_params=_cparams("parallel"),
        name="ffn",
    )(h, g_ffn, w_g, w_u, w_d, p_p, p_s, g_ple, w_pg, w_pp, g_final)


def _rope_tables(positions):
    half = ROT_DIM // 2
    inv_freq = ROPE_THETA ** (-jnp.arange(half, dtype=f32) * 2.0 / ROT_DIM)
    ang = positions.astype(f32)[:, None] * inv_freq[None, :]
    cos, sin = jnp.cos(ang), jnp.sin(ang)
    n = positions.shape[0]
    one = jnp.ones((n, DK - ROT_DIM), f32)
    zero = jnp.zeros((n, DK - ROT_DIM), f32)
    zh = jnp.zeros((n, half), f32)
    cos_h = jnp.concatenate([cos, cos, one], axis=1)
    up_h = jnp.concatenate([-sin, zh, zero], axis=1)
    dn_h = jnp.concatenate([zh, sin, zero], axis=1)
    rep = lambda a: jnp.concatenate([a] * (LANES // DK), axis=1)
    return rep(cos_h), rep(up_h), rep(dn_h)


def _prep_in_weights(w_in_l, w_gate_l):
    o = np.cumsum((0, QW, QW, VW, VW, GLA_RANK, QW, QW, VW, VW, 2 * HEADS, 2 * QW, 2 * QW, VW))
    seg = lambda i: w_in_l[:, int(o[i]):int(o[i + 1])]
    gq, gk, gv, gr, glr, mq, mk, mv, mo, mif, dq, dk, dv = (seg(i) for i in range(13))
    main = jnp.concatenate([w_gate_l, gq, gk, gv, gr, mq, mk, mv, mo, dq, dk, dv], axis=1).astype(bf16)
    main = main.reshape(D_MODEL, P_MAIN // TN_IN, TN_IN).transpose(1, 0, 2)
    pad = jnp.zeros((D_MODEL, LANES - GLA_RANK - 2 * HEADS), f32)
    small = jnp.concatenate([glr, mif, pad], axis=1).astype(bf16)
    return main, small


def kernel(x_prompt, x_sample, cache_k, cache_v, state_gla, state_mlstm_c, state_mlstm_n, state_mlstm_m, page_table, p_prompt, p_sample, g_mix, w_in, w_gla_gk, b_gla_gk, g_gla_norm, b_mlstm_if, g_mlstm_norm, diff_lambda, g_diff_norm, w_branch, w_gate, w_out, g_ffn, w_ffn_gate, w_ffn_up, w_ffn_down, g_ple, w_ple_gate, w_ple_proj, g_final):
    bp, lp, _ = x_prompt.shape
    bs, ls, _ = x_sample.shape
    depth = g_mix.shape[0]
    n_pages = page_table.shape[1]
    page = cache_k.shape[2]
    past = n_pages * page
    tp = bp * lp
    ts = bs * ls
    assert lp % TM == 0 and ts == TM and ls * 2 * HEADS == Q_ROWS and bs % SEQ_PER_STEP == 0 and page == LANES
    npb = tp // TM

    def to_step_major(a):
        return jnp.swapaxes(a, 0, 1).reshape((ts,) + a.shape[2:])

    def from_step_major(a):
        return jnp.swapaxes(a.reshape((ls, bs) + a.shape[1:]), 0, 1)

    h_p, h_s, h_s_block = x_prompt.reshape(tp, D_MODEL), to_step_major(x_sample), 0

    pos_rows = jnp.concatenate([jnp.arange(lp, dtype=jnp.int32),
                                past + jnp.repeat(jnp.arange(ls, dtype=jnp.int32), bs)])
    cos_t, sup_t, sdn_t = _rope_tables(pos_rows)

    ck = jnp.transpose(cache_k, (0, 1, 3, 4, 2)).reshape(cache_k.shape[0], cache_k.shape[1], 2 * QW, page)
    cv = cache_v.reshape(cache_v.shape[0], cache_v.shape[1], page * HEADS, DV)
    nblk = bs // SEQ_PER_STEP

    def to_cols(a):
        return a.reshape(ls, nblk, SEQ_PER_STEP, QW).transpose(1, 3, 0, 2).reshape(nblk, QW, ls * SEQ_PER_STEP)

    def to_blocks(a):
        return a.reshape(ls, nblk, SEQ_PER_STEP, VW).transpose(1, 0, 2, 3).reshape(nblk, ls * SEQ_PER_STEP, VW)

    def from_blocks(a):
        return a.reshape(nblk, ls, SEQ_PER_STEP, VW).transpose(1, 0, 2, 3).reshape(ts, VW)

    eye_heads = jnp.eye(2 * HEADS, dtype=f32)
    outs_p, outs_s = [], []
    y = None
    for l in range(depth):
        lam_init = 0.8 - 0.6 * math.exp(-0.3 * l)
        w_main, w_small = _prep_in_weights(w_in[l], w_gate[l])
        w_gk = jnp.concatenate([w_gla_gk[l], jnp.zeros((LANES - GLA_RANK, QW), f32)], axis=0)
        b_if = jnp.zeros((1, LANES), f32).at[0, L_MI:L_MI + 2 * HEADS].set(b_mlstm_if[l])
        gates, proj, la, gif = _inproj(h_p, h_s, h_s_block, g_mix[l][None], w_main, w_small, w_gk,
                                       b_gla_gk[l][None], b_if, cos_t, sup_t, sdn_t, npb, lp // TM)

        oa_p, gla_p = _gla_prompt(proj, la, bp, lp)
        om_p, c_p, n_p, m_p = _mlstm_prompt(proj, gif, bp, lp)
        oa_p, om_p = oa_p.reshape(tp, VW), om_p.reshape(tp, VW)
        od_p = _attn_prompt(proj, diff_lambda[l], bp, lp, lam_init)

        ps = proj[tp:]
        col = lambda t, lo=0, w=TILE: ps[:, t * TILE + lo:t * TILE + lo + w]
        ga, gk_, gq_, ma, mk_, mq_, n_s, m_s = _sample_prep(
            col(T_GQK, 0, QW), col(T_GQK, QW, QW), la[tp:], col(T_MQK, 0, QW), col(T_MQK, QW, QW), gif[tp:],
            state_mlstm_n[l].reshape(bs, QW), state_mlstm_m[l], bs, ls)
        oa_s, gla_s = _sample_state(to_cols(ga), to_cols(gk_), to_cols(gq_), to_blocks(col(T_GV)), state_gla[l], ls)
        om_s, c_s = _sample_state(to_cols(ma), to_cols(mk_), to_cols(mq_), to_blocks(col(T_MV)),
                                  state_mlstm_c[l], ls)
        q_s = from_step_major(col(T_DQ)).reshape(bs, ls, 2 * HEADS, DK)
        qbd = (q_s.transpose(0, 2, 1, 3)[:, :, :, None, :] * (DK ** -0.5)
               * eye_heads[None, :, None, :, None]).reshape(bs, Q_ROWS, 2 * QW).astype(bf16)
        k_s = from_step_major(col(T_DK))
        v_s = from_step_major(col(T_DV))
        k_new_t = jnp.pad(jnp.swapaxes(k_s, 1, 2), ((0, 0), (0, 0), (0, LANES - ls)))
        v_new = jnp.pad(v_s.reshape(bs, ls * HEADS, DV), ((0, 0), (0, (LANES - ls) * HEADS), (0, 0)))
        od_s = _attn_sample(page_table, qbd, k_new_t, v_new, diff_lambda[l], ck, cv, l, lam_init, ls)
        od_s = to_step_major(od_s[:, :ls])

        wb = w_branch[l].astype(bf16)
        h = _merge(h_p, h_s, h_s_block, (oa_p, from_blocks(oa_s)), (om_p, from_blocks(om_s)), (od_p, od_s), proj,
                   gates, g_gla_norm[l][None], g_mlstm_norm[l][None], g_diff_norm[l][None],
                   wb[:VW], wb[VW:2 * VW], wb[2 * VW:], w_out[l].astype(bf16), lam_init, npb)
        h = _ffn(h, g_ffn[l][None], w_ffn_gate[l].astype(bf16), w_ffn_up[l].astype(bf16),
                 w_ffn_down[l].astype(bf16), p_prompt[l].reshape(tp, PLE_DIM), to_step_major(p_sample[l]),
                 g_ple[l][None], w_ple_gate[l].astype(bf16), w_ple_proj[l].astype(bf16), g_final[None],
                 l == depth - 1, npb)
        h_p, h_s, h_s_block = h, h, npb

        pp = proj[:tp]
        outs_p.append((pp[:, T_DK * TILE:(T_DK + 1) * TILE].reshape(bp, lp, 2 * HEADS, DK),
                       pp[:, T_DV * TILE:(T_DV + 1) * TILE].reshape(bp, lp, HEADS, DV),
                       gla_p.reshape(bp, HEADS, DK, DV), c_p, n_p, m_p[:, :, 0]))
        outs_s.append((k_s.reshape(bs, ls, 2 * HEADS, DK), v_s.reshape(bs, ls, HEADS, DV),
                       gla_s, c_s, n_s.reshape(bs, HEADS, DK), m_s[:, ::DK]))

    y_prompt = h[:tp].reshape(bp, lp, D_MODEL)
    y_sample = from_step_major(h[tp:])
    stack = lambda items: [jnp.stack(t) for t in zip(*items)]
    return tuple([y_prompt, y_sample] + stack(outs_p) + stack(outs_s))
```

```python
import functools
import math

import numpy as np
import jax
import jax.numpy as jnp
from jax import lax
from jax.experimental import pallas as pl
from jax.experimental.pallas import tpu as pltpu

f32 = jnp.float32
bf16 = jnp.bfloat16
HIGHEST = lax.Precision.HIGHEST

D_MODEL = 1024
PLE_DIM = 256
HEADS = 4
DK = 64
DV = 128
GLA_RANK = 16
GLA_GATE_NORM = 16.0
ROT_DIM = 16
ROPE_THETA = 500000.0
CHUNK = 64
D_FF = 2816
EPS = 1e-6
LOG2E = 1.4426950408889634
VW = HEADS * DV
QW = HEADS * DK

LANES = 128
SUBLANES = 8
VMEM_LIMIT = 48 * 1024 * 1024

TILE = 512
GATE_W = 3 * D_MODEL
T_GQK, T_GV, T_GR = 0, 1, 2
T_MQK, T_MV, T_MO = 3, 4, 5
T_DQ, T_DK, T_DV = 6, 7, 8
N_TILES = 9
P_MIX = N_TILES * TILE
P_MAIN = GATE_W + P_MIX
L_GLR = 0
L_MI = 16
L_MF = 20

TM = 512


def _cparams(*sem):
    return pltpu.CompilerParams(dimension_semantics=sem, vmem_limit_bytes=VMEM_LIMIT)


def _log_sigmoid(x):
    return jnp.minimum(x, 0.0) - jnp.log1p(jnp.exp(-jnp.abs(x)))


def _sigmoid(x):
    return 0.5 * jnp.tanh(0.5 * x) + 0.5


def _rms(x, g):
    return x * lax.rsqrt(jnp.mean(x * x, axis=-1, keepdims=True) + EPS) * g


def _dot(a, b):
    return jnp.dot(a.astype(bf16), b.astype(bf16), preferred_element_type=f32)


def _dot_nt(a, b):
    return lax.dot_general(a.astype(bf16), b.astype(bf16), (((1,), (1,)), ((), ())), preferred_element_type=f32)


def _dot_tn(a, b):
    return lax.dot_general(a.astype(bf16), b.astype(bf16), (((0,), (0,)), ((), ())), preferred_element_type=f32)


def _rope_tile(x, cos, sin_up, sin_dn):
    parts = []
    for c in range(x.shape[1] // LANES):
        xc = x[:, c * LANES:(c + 1) * LANES]
        parts.append(xc * cos + pltpu.roll(xc, LANES - ROT_DIM // 2, axis=1) * sin_up
                     + pltpu.roll(xc, ROT_DIM // 2, axis=1) * sin_dn)
    return jnp.concatenate(parts, axis=1)


def _row_specs(width, n_prompt_blocks, sample_block, tm=None, n_sample_blocks=1):
    tm = TM if tm is None else tm
    return (pl.BlockSpec((tm, width), lambda i: (jnp.minimum(i, n_prompt_blocks - 1), 0)),
            pl.BlockSpec((tm, width),
                         lambda i: (sample_block + jnp.clip(i - n_prompt_blocks, 0, n_sample_blocks - 1), 0)))


def _pick_rows(n_prompt_blocks, p_ref, s_ref):
    return jnp.where(pl.program_id(0) >= n_prompt_blocks, s_ref[...], p_ref[...])


TM_IN = 256


def _inproj_kernel(xp_ref, xs_ref, g_ref, w_ref, ws_ref, wgk_ref, bgk_ref, bif_ref, cos_ref, sup_ref, sdn_ref,
                   gates_ref, proj_ref, kt_ref, kts_ref, la_ref, gif_ref, *, n_prompt_blocks, dec_batch):
    i = pl.program_id(0)
    xn = _rms(_pick_rows(n_prompt_blocks, xp_ref, xs_ref), g_ref[...]).astype(bf16)
    small = jnp.dot(xn, ws_ref[...], preferred_element_type=f32)
    z = jnp.dot(small, wgk_ref[...], preferred_element_type=f32, precision=HIGHEST) + bgk_ref[...]
    la_ref[...] = _log_sigmoid(z) * (1.0 / GLA_GATE_NORM)
    gi = small + bif_ref[...]
    lane = lax.broadcasted_iota(jnp.int32, gi.shape, 1)
    gif_ref[...] = jnp.where((lane >= L_MF) & (lane < L_MF + HEADS), _log_sigmoid(gi), gi)

    for c in range(GATE_W // TILE):
        cols = slice(c * TILE, (c + 1) * TILE)
        gates_ref[:, cols] = _sigmoid(jnp.dot(xn, w_ref[:, cols], preferred_element_type=f32)).astype(bf16)

    cos, sup, sdn = cos_ref[...], sup_ref[...], sdn_ref[...]
    for t in range(N_TILES):
        acc = jnp.dot(xn, w_ref[:, GATE_W + t * TILE:GATE_W + (t + 1) * TILE], preferred_element_type=f32)
        if t in (T_DQ, T_DK):
            acc = _rope_tile(acc, cos, sup, sdn)
        proj_ref[:, t * TILE:(t + 1) * TILE] = acc
        if t == T_DK:
            k_t = acc.T

    @pl.when(i < n_prompt_blocks)
    def _():
        kt_ref[...] = k_t

    @pl.when(i >= n_prompt_blocks)
    def _():
        for s in range(k_t.shape[1] // dec_batch):
            kts_ref[s] = k_t[:, s * dec_batch:(s + 1) * dec_batch]


def _inproj(h_p, h_s, sample_block, g_mix, w_main, w_small, w_gk, b_gk, b_if, cos_t, sup_t, sdn_t,
            batch, seq, dec_batch, dec_seq):
    tm = TM_IN
    seq_blocks = seq // tm
    npb = batch * seq_blocks
    steps_per_block = tm // dec_batch
    nsb = dec_seq // steps_per_block
    nt = npb + nsb
    T = nt * tm
    const = lambda i: (0, 0)
    resident = lambda shape: pl.BlockSpec(shape, const, pipeline_mode=pl.Buffered(1))

    def tab_map(i):
        return (jnp.where(i < npb, i % seq_blocks, seq_blocks + i - npb), 0)

    def kt_map(i):
        j = jnp.minimum(i, npb - 1)
        return (j // seq_blocks, 0, j % seq_blocks)

    return pl.pallas_call(
        functools.partial(_inproj_kernel, n_prompt_blocks=npb, dec_batch=dec_batch),
        grid=(nt,),
        in_specs=[
            *_row_specs(D_MODEL, npb, sample_block, tm, nsb),
            pl.BlockSpec((1, D_MODEL), const),
            resident((D_MODEL, P_MAIN)),
            resident((D_MODEL, LANES)),
            resident((LANES, QW)),
            pl.BlockSpec((1, QW), const),
            pl.BlockSpec((1, LANES), const),
            pl.BlockSpec((tm, LANES), tab_map),
            pl.BlockSpec((tm, LANES), tab_map),
            pl.BlockSpec((tm, LANES), tab_map),
        ],
        out_specs=[
            pl.BlockSpec((tm, GATE_W), lambda i: (i, 0)),
            pl.BlockSpec((tm, P_MIX), lambda i: (i, 0)),
            pl.BlockSpec((None, TILE, tm), kt_map),
            pl.BlockSpec((steps_per_block, TILE, dec_batch), lambda i: (jnp.maximum(i - npb, 0), 0, 0)),
            pl.BlockSpec((tm, QW), lambda i: (i, 0)),
            pl.BlockSpec((tm, LANES), lambda i: (i, 0)),
        ],
        out_shape=[
            jax.ShapeDtypeStruct((T, GATE_W), bf16),
            jax.ShapeDtypeStruct((T, P_MIX), f32),
            jax.ShapeDtypeStruct((batch, TILE, seq), f32),
            jax.ShapeDtypeStruct((dec_seq, TILE, dec_batch), f32),
            jax.ShapeDtypeStruct((T, QW), f32),
            jax.ShapeDtypeStruct((T, LANES), f32),
        ],
        compiler_params=_cparams("arbitrary"),
        name="inproj",
    )(h_p, h_s, g_mix, w_main, w_small, w_gk, b_gk, b_if, cos_t, sup_t, sdn_t)


def _tril(n):
    r = lax.broadcasted_iota(jnp.int32, (n, n), 0)
    c = lax.broadcasted_iota(jnp.int32, (n, n), 1)
    return r >= c


def _gla_prompt_kernel(*refs, n_chunks, batch):
    q_refs, k_refs, v_refs, la_refs = (refs[i * batch:(i + 1) * batch] for i in range(4))
    o_ref, s_out_ref, s_scr = refs[4 * batch:]
    blk = pl.program_id(0)

    @pl.when(blk == 0)
    def _():
        s_scr[...] = jnp.zeros_like(s_scr)

    ltri = _tril(CHUNK).astype(f32)
    r2 = lax.broadcasted_iota(jnp.int32, (2 * CHUNK, CHUNK), 0)
    c2 = lax.broadcasted_iota(jnp.int32, (2 * CHUNK, CHUNK), 1)
    tril2 = jnp.where(r2 >= CHUNK, r2 - CHUNK, r2) >= c2
    lane = lax.broadcasted_iota(jnp.int32, (CHUNK, LANES), 1)
    lo = (lane < DK).astype(f32)
    hi = 1.0 - lo

    def chunk(c, carry):
        rows = pl.ds(pl.multiple_of(c * CHUNK, CHUNK), CHUNK)
        for s in range(batch):
            g = la_refs[s][rows, :]
            b = jnp.dot(ltri, g, preferred_element_type=f32, precision=HIGHEST)
            b_last = b[CHUNK - 1:CHUNK, :]
            q = q_refs[s][rows, :]
            k = k_refs[s][rows, :]
            qg = q * jnp.exp(b) * (DK ** -0.5)
            kg = k * jnp.exp(-b)
            kd = k * jnp.exp(b_last - b)
            dec = jnp.exp(b_last)
            for p in range(HEADS // 2):
                ls = slice(p * LANES, (p + 1) * LANES)
                qg_p, kg_p, kd_p = qg[:, ls], kg[:, ls], kd[:, ls]
                qs = jnp.concatenate([qg_p * lo, qg_p * hi], axis=0)
                a = jnp.where(tril2, _dot_nt(qs, kg_p), 0.0)
                s_p = s_scr[s, p]
                inter = _dot(qs, s_p)
                v0 = v_refs[s][rows, pl.ds((2 * p) * DV, DV)]
                v1 = v_refs[s][rows, pl.ds((2 * p + 1) * DV, DV)]
                o_ref[s, rows, pl.ds((2 * p) * DV, DV)] = inter[:CHUNK] + _dot(a[:CHUNK], v0)
                o_ref[s, rows, pl.ds((2 * p + 1) * DV, DV)] = inter[CHUNK:] + _dot(a[CHUNK:], v1)
                dcol = jnp.broadcast_to(dec[:, ls], (LANES, LANES)).T
                s_scr[s, p] = dcol * s_p + _dot_tn(kd_p * lo, v0) + _dot_tn(kd_p * hi, v1)
        return carry

    lax.fori_loop(0, n_chunks, chunk, 0)

    @pl.when(blk == pl.num_programs(0) - 1)
    def _():
        s_out_ref[...] = s_scr[...]


def _stream_specs(width, batch, nb, col):
    return [pl.BlockSpec((TM, width), functools.partial(lambda s, i: (s * nb + i, col), s)) for s in range(batch)]


def _gla_prompt(proj, la, batch, seq):
    nb = seq // TM
    qw_blocks = TILE // QW
    return pl.pallas_call(
        functools.partial(_gla_prompt_kernel, n_chunks=TM // CHUNK, batch=batch),
        grid=(nb,),
        in_specs=[
            *_stream_specs(QW, batch, nb, T_GQK * qw_blocks),
            *_stream_specs(QW, batch, nb, T_GQK * qw_blocks + 1),
            *_stream_specs(VW, batch, nb, T_GV),
            *_stream_specs(QW, batch, nb, 0),
        ],
        out_specs=[
            pl.BlockSpec((batch, TM, VW), lambda i: (0, i, 0)),
            pl.BlockSpec((batch, HEADS // 2, LANES, LANES), lambda i: (0, 0, 0, 0)),
        ],
        out_shape=[
            jax.ShapeDtypeStruct((batch, seq, VW), f32),
            jax.ShapeDtypeStruct((batch, HEADS // 2, LANES, LANES), f32),
        ],
        scratch_shapes=[pltpu.VMEM((batch, HEADS // 2, LANES, LANES), f32)],
        compiler_params=_cparams("arbitrary"),
        name="gla_prompt",
    )(*([proj] * (3 * batch)), *([la] * batch))


def _mlstm_prompt_kernel(*refs, n_chunks, batch):
    q_refs, k_refs, v_refs, gif_refs = (refs[i * batch:(i + 1) * batch] for i in range(4))
    h_ref, c_out_ref, n_out_ref, m_out_ref, ct_scr, n_scr, m_scr = refs[4 * batch:]
    blk = pl.program_id(0)

    @pl.when(blk == 0)
    def _():
        ct_scr[...] = jnp.zeros_like(ct_scr)
        n_scr[...] = jnp.zeros_like(n_scr)
        m_scr[...] = jnp.zeros_like(m_scr)

    ltri = _tril(CHUNK).astype(f32)
    r_i = lax.broadcasted_iota(jnp.int32, (CHUNK, CHUNK), 0)
    c_i = lax.broadcasted_iota(jnp.int32, (CHUNK, CHUNK), 1)
    upper = r_i <= c_i

    def chunk(c, carry):
        rows = pl.ds(pl.multiple_of(c * CHUNK, CHUNK), CHUNK)
        for sq in range(batch):
            g = gif_refs[sq][rows, :]
            fc = jnp.dot(ltri, g, preferred_element_type=f32, precision=HIGHEST)
            g_t = g.T
            fc_t = fc.T
            q_all = q_refs[sq][rows, :]
            k_all = k_refs[sq][rows, :] * (DK ** -0.5)
            n_all = n_scr[sq]
            for h in range(HEADS):
                i_col = g[:, L_MI + h:L_MI + h + 1]
                f_col = fc[:, L_MF + h:L_MF + h + 1]
                i_row = g_t[L_MI + h:L_MI + h + 1, :]
                f_row = fc_t[L_MF + h:L_MF + h + 1, :]
                f_last = fc[CHUNK - 1:CHUNK, L_MF + h:L_MF + h + 1]
                m_prev = m_scr[sq, h:h + 1, 0:1]
                c_t = ct_scr[sq, h]
                q = q_all[:, h * DK:(h + 1) * DK]
                k = k_all[:, h * DK:(h + 1) * DK]
                v_t = v_refs[sq][rows, pl.ds(h * DV, DV)].T

                log_d = jnp.where(upper, f_row + (i_col - f_col), -jnp.inf)
                m_inter = m_prev + f_row
                m_row = jnp.maximum(m_inter, jnp.max(log_d, axis=0, keepdims=True))
                w_inter = jnp.exp(m_inter - m_row)
                sc = _dot_nt(k, q) * jnp.exp(log_d - m_row)
                num = w_inter * _dot_nt(c_t, q) + _dot(v_t, sc)
                qn = _dot_nt(n_all, q)[h:h + 1, :]
                den = w_inter * qn + jnp.sum(sc, axis=0, keepdims=True)
                h_t = num / jnp.maximum(jnp.abs(den), jnp.exp(-m_row))
                h_ref[sq, rows, pl.ds(h * DV, DV)] = h_t.T

                log_w = f_last - f_col + i_col
                m_new = jnp.maximum(m_prev + f_last, jnp.max(log_w, axis=0, keepdims=True))
                wk = jnp.exp(log_w - m_new) * k
                decay = jnp.exp(m_prev + f_last - m_new)
                ct_scr[sq, h] = decay * c_t + _dot(v_t, wk)
                n_scr[sq, h:h + 1, :] = decay * n_all[h:h + 1, :] + jnp.sum(wk, axis=0, keepdims=True)
                m_scr[sq, h:h + 1, :] = jnp.broadcast_to(m_new, (1, LANES))
        return carry

    lax.fori_loop(0, n_chunks, chunk, 0)

    @pl.when(blk == pl.num_programs(0) - 1)
    def _():
        for sq in range(batch):
            for h in range(HEADS):
                c_out_ref[sq, h] = ct_scr[sq, h].T
        n_out_ref[...] = n_scr[:, 0:HEADS, :]
        m_out_ref[...] = m_scr[:, 0:HEADS, :]


def _mlstm_prompt(proj, gif, batch, seq):
    nb = seq // TM
    qw_blocks = TILE // QW
    return pl.pallas_call(
        functools.partial(_mlstm_prompt_kernel, n_chunks=TM // CHUNK, batch=batch),
        grid=(nb,),
        in_specs=[
            *_stream_specs(QW, batch, nb, T_MQK * qw_blocks),
            *_stream_specs(QW, batch, nb, T_MQK * qw_blocks + 1),
            *_stream_specs(VW, batch, nb, T_MV),
            *_stream_specs(LANES, batch, nb, 0),
        ],
        out_specs=[
            pl.BlockSpec((batch, TM, VW), lambda i: (0, i, 0)),
            pl.BlockSpec((batch, HEADS, DK, DV), lambda i: (0, 0, 0, 0)),
            pl.BlockSpec((batch, HEADS, DK), lambda i: (0, 0, 0)),
            pl.BlockSpec((batch, HEADS, LANES), lambda i: (0, 0, 0)),
        ],
        out_shape=[
            jax.ShapeDtypeStruct((batch, seq, VW), f32),
            jax.ShapeDtypeStruct((batch, HEADS, DK, DV), f32),
            jax.ShapeDtypeStruct((batch, HEADS, DK), f32),
            jax.ShapeDtypeStruct((batch, HEADS, LANES), f32),
        ],
        scratch_shapes=[
            pltpu.VMEM((batch, HEADS, DV, DK), f32),
            pltpu.VMEM((batch, SUBLANES, DK), f32),
            pltpu.VMEM((batch, SUBLANES, LANES), f32),
        ],
        compiler_params=_cparams("arbitrary"),
        name="mlstm_prompt",
    )(*([proj] * (3 * batch)), *([gif] * batch))


def _diff_lambda_value(lam_ref, lam_init):
    lam = lam_ref[...]
    s1 = jnp.sum(lam[0:1] * lam[1:2], axis=1, keepdims=True)
    s2 = jnp.sum(lam[2:3] * lam[3:4], axis=1, keepdims=True)
    return jnp.exp(s1) - jnp.exp(s2) + lam_init


def _attn_prompt_kernel(qi_ref, ki_ref, q_ref, k_ref, v_ref, lam_ref, o_ref, q2_scr, m_scr, l_scr, acc_scr,
                        *, tq, lam_init):
    p = pl.program_id(1)
    qi = qi_ref[p]
    ki = ki_ref[p]
    pair = lambda j: slice(j * LANES, (j + 1) * LANES)

    @pl.when(ki == 0)
    def _():
        for j in range(HEADS):
            q = q_ref[:, pair(j)] * (DK ** -0.5 * LOG2E)
            lane = lax.broadcasted_iota(jnp.int32, q.shape, 1)
            q2_scr[j, 0:tq, :] = jnp.where(lane < DK, q, 0.0).astype(bf16)
            q2_scr[j, tq:2 * tq, :] = jnp.where(lane >= DK, q, 0.0).astype(bf16)
        m_scr[...] = jnp.full_like(m_scr, -jnp.inf)
        l_scr[...] = jnp.zeros_like(l_scr)
        acc_scr[...] = jnp.zeros_like(acc_scr)

    def step(masked):
        sts = [_dot_nt(k_ref[:, pair(j)], q2_scr[j]) for j in range(HEADS)]
        pts, alphas = [], []
        for j in range(HEADS):
            st = sts[j]
            if masked:
                r = lax.broadcasted_iota(jnp.int32, st.shape, 0)
                c = lax.broadcasted_iota(jnp.int32, st.shape, 1)
                c = jnp.where(c >= tq, c - tq, c)
                st = jnp.where(r <= c, st, -jnp.inf)
            m_prev = m_scr[j]
            m_new = jnp.maximum(m_prev, jnp.max(st, axis=0, keepdims=True))
            alpha = jnp.exp2(m_prev - m_new)
            pt = jnp.exp2(st - m_new)
            l_scr[j] = alpha * l_scr[j] + jnp.sum(pt, axis=0, keepdims=True)
            m_scr[j] = m_new
            pts.append(pt.astype(bf16))
            alphas.append(alpha)
        for j in range(HEADS):
            acc_scr[j] = alphas[j] * acc_scr[j] + _dot_tn(v_ref[:, pair(j)], pts[j])

    @pl.when(ki < qi)
    def _():
        step(False)

    @pl.when(ki == qi)
    def _():
        step(True)
        lam = _diff_lambda_value(lam_ref, lam_init)
        for j in range(HEADS):
            ot = acc_scr[j] / l_scr[j]
            o_ref[:, pair(j)] = (ot[:, 0:tq] - lam * ot[:, tq:2 * tq]).T


def _attn_prompt(proj, diff_lambda, batch, seq, lam_init, tq=512):
    nq = seq // tq
    pairs = [(q, k) for q in range(nq) for k in range(q + 1)]
    qi_tab = jnp.asarray(np.array([a for a, _ in pairs], np.int32))
    ki_tab = jnp.asarray(np.array([b for _, b in pairs], np.int32))
    grid_spec = pltpu.PrefetchScalarGridSpec(
        num_scalar_prefetch=2,
        grid=(batch, len(pairs)),
        in_specs=[
            pl.BlockSpec((tq, TILE), lambda b, p, qt, kt: (b * nq + qt[p], T_DQ)),
            pl.BlockSpec((tq, TILE), lambda b, p, qt, kt: (b * nq + kt[p], T_DK)),
            pl.BlockSpec((tq, TILE), lambda b, p, qt, kt: (b * nq + kt[p], T_DV)),
            pl.BlockSpec((4, DK), lambda b, p, qt, kt: (0, 0)),
        ],
        out_specs=pl.BlockSpec((tq, VW), lambda b, p, qt, kt: (b * nq + qt[p], 0)),
        scratch_shapes=[
            pltpu.VMEM((HEADS, 2 * tq, LANES), bf16),
            pltpu.VMEM((HEADS, 1, 2 * tq), f32),
            pltpu.VMEM((HEADS, 1, 2 * tq), f32),
            pltpu.VMEM((HEADS, DV, 2 * tq), f32),
        ],
    )
    return pl.pallas_call(
        functools.partial(_attn_prompt_kernel, tq=tq, lam_init=lam_init),
        grid_spec=grid_spec,
        out_shape=jax.ShapeDtypeStruct((batch * seq, VW), f32),
        compiler_params=_cparams("parallel", "arbitrary"),
        name="attn_prompt",
    )(qi_tab, ki_tab, proj, proj, proj, diff_lambda)


Q_ROWS = 2 * HEADS * 4


def _attn_sample_kernel(pt_ref, qbd_ref, kn_ref, vn_ref, lam_ref, ck_hbm, cv_hbm, o_ref, kbuf, vbuf, sem,
                        *, layer, n_pages, page, dec_seq, lam_init):
    b = pl.program_id(0)
    nb = pl.num_programs(0)
    past = n_pages * page
    slot = b % 2

    def copies(seq_idx, s):
        out = []
        for pg in range(n_pages):
            src = pt_ref[seq_idx, pg]
            out.append(pltpu.make_async_copy(ck_hbm.at[layer, src], kbuf.at[s, :, pl.ds(pg * page, page)],
                                             sem.at[0, s]))
            out.append(pltpu.make_async_copy(cv_hbm.at[layer, src],
                                             vbuf.at[s, pl.ds(pg * page * HEADS, page * HEADS), :], sem.at[1, s]))
        return out

    @pl.when(b == 0)
    def _():
        for cp in copies(0, 0):
            cp.start()

    @pl.when(b + 1 < nb)
    def _():
        for cp in copies(b + 1, 1 - slot):
            cp.start()

    for cp in copies(b, slot):
        cp.wait()

    kbuf[slot, :, pl.ds(past, LANES)] = kn_ref[...]
    vbuf[slot, pl.ds(past * HEADS, LANES * HEADS), :] = vn_ref[...]

    s = _dot(qbd_ref[...], kbuf[slot])
    r = lax.broadcasted_iota(jnp.int32, s.shape, 0)
    c = lax.broadcasted_iota(jnp.int32, s.shape, 1)
    s = jnp.where(c - past <= r % dec_seq, s, -jnp.inf)
    m = jnp.max(s, axis=1, keepdims=True)
    pr = jnp.exp(s - m)
    inv_l = 1.0 / jnp.sum(pr, axis=1, keepdims=True)
    lam = _diff_lambda_value(lam_ref, lam_init)
    for hv in range(HEADS):
        rows = slice(hv * SUBLANES, (hv + 1) * SUBLANES)
        v_h = vbuf[slot, pl.ds(hv, past + LANES, stride=HEADS), :]
        t = _dot(pr[rows], v_h) * inv_l[rows]
        o_ref[:, hv * DV:(hv + 1) * DV] = t - lam * pltpu.roll(t, SUBLANES - dec_seq, axis=0)


def _attn_sample(page_table, qbd, k_new_t, v_new, diff_lambda, cache_kt, cache_v2, layer, lam_init, dec_seq):
    nseq, n_pages = page_table.shape
    page = cache_kt.shape[3]
    cols = n_pages * page + LANES
    grid_spec = pltpu.PrefetchScalarGridSpec(
        num_scalar_prefetch=1,
        grid=(nseq,),
        in_specs=[
            pl.BlockSpec((None, Q_ROWS, 2 * QW), lambda b, pt: (b, 0, 0)),
            pl.BlockSpec((None, 2 * QW, LANES), lambda b, pt: (b, 0, 0)),
            pl.BlockSpec((None, LANES * HEADS, DV), lambda b, pt: (b, 0, 0)),
            pl.BlockSpec((4, DK), lambda b, pt: (0, 0)),
            pl.BlockSpec(memory_space=pl.ANY),
            pl.BlockSpec(memory_space=pl.ANY),
        ],
        out_specs=pl.BlockSpec((None, SUBLANES, VW), lambda b, pt: (b, 0, 0)),
        scratch_shapes=[
            pltpu.VMEM((2, 2 * QW, cols), f32),
            pltpu.VMEM((2, cols * HEADS, DV), f32),
            pltpu.SemaphoreType.DMA((2, 2)),
        ],
    )
    return pl.pallas_call(
        functools.partial(_attn_sample_kernel, layer=layer, n_pages=n_pages, page=page, dec_seq=dec_seq,
                          lam_init=lam_init),
        grid_spec=grid_spec,
        out_shape=jax.ShapeDtypeStruct((nseq, SUBLANES, VW), f32),
        compiler_params=_cparams("arbitrary"),
        name="attn_sample",
    )(page_table, qbd, k_new_t, v_new, diff_lambda, cache_kt, cache_v2)


def _head_of_lane(shape):
    return lax.broadcasted_iota(jnp.int32, shape, 1) // DK


def _expand_heads(cols, base, head):
    out = cols[:, base + HEADS - 1:base + HEADS]
    for h in range(HEADS - 2, -1, -1):
        out = jnp.where(head == h, cols[:, base + h:base + h + 1], out)
    return out


def _sample_prep_kernel(gq_ref, gk_ref, la_ref, mq_ref, mk_ref, gif_ref, n0_ref, m0_ref,
                        ga_ref, gkk_ref, gqq_ref, ma_ref, mkk_ref, mqq_ref, n_out_ref, m_out_ref, *, nseq, dec_seq):
    head = _head_of_lane((nseq, QW))
    ga_ref[...] = jnp.exp(la_ref[...])
    gkk_ref[...] = gk_ref[...]
    gqq_ref[...] = gq_ref[...] * (DK ** -0.5)

    m = _expand_heads(m0_ref[...], 0, head)
    n = n0_ref[...]
    for t in range(dec_seq):
        rows = pl.ds(t * nseq, nseq)
        gates = gif_ref[rows, :]
        i_e = _expand_heads(gates, L_MI, head)
        f_e = _expand_heads(gates, L_MF, head)
        k = mk_ref[rows, :] * (DK ** -0.5)
        q = mq_ref[rows, :]
        m_new = jnp.maximum(f_e + m, i_e)
        fp = jnp.exp(f_e + m - m_new)
        ip = jnp.exp(i_e - m_new)
        n = fp * n + ip * k
        nq = n * q
        den = jnp.zeros_like(nq)
        for h in range(HEADS):
            den = jnp.where(head == h, jnp.sum(jnp.where(head == h, nq, 0.0), axis=1, keepdims=True), den)
        inv = 1.0 / jnp.maximum(jnp.abs(den), jnp.exp(-m_new))
        ma_ref[rows, :] = fp
        mkk_ref[rows, :] = ip * k
        mqq_ref[rows, :] = q * inv
        m = m_new
    n_out_ref[...] = n
    m_out_ref[...] = m


def _sample_prep(gq, gk, la, mq, mk, gif, n0, m0, nseq, dec_seq):
    rows = nseq * dec_seq
    tok = jax.ShapeDtypeStruct((rows, QW), f32)
    st = jax.ShapeDtypeStruct((nseq, QW), f32)
    return pl.pallas_call(
        functools.partial(_sample_prep_kernel, nseq=nseq, dec_seq=dec_seq),
        out_shape=[tok, tok, tok, tok, tok, tok, st, st],
        compiler_params=pltpu.CompilerParams(vmem_limit_bytes=VMEM_LIMIT),
        name="sample_prep",
    )(gq, gk, la, mq, mk, gif, n0, m0)


SEQ_PER_STEP = 8


def _sample_state_kernel(a_ref, k_ref, q_ref, v_ref, s_ref, o_ref, s_out_ref, *, dec_seq):
    for bb in range(SEQ_PER_STEP):
        for h in range(HEADS):
            s = s_ref[bb, h]
            for t in range(dec_seq):
                col = t * SEQ_PER_STEP + bb
                rows = slice(h * DK, (h + 1) * DK)
                a = a_ref[rows, col:col + 1]
                k = k_ref[rows, col:col + 1]
                q = q_ref[rows, col:col + 1]
                v = v_ref[col:col + 1, h * DV:(h + 1) * DV]
                s = a * s + k * v
                o_ref[col:col + 1, h * DV:(h + 1) * DV] = jnp.sum(q * s, axis=0, keepdims=True)
            s_out_ref[bb, h] = s


def _sample_state(a_t, k_t, q_t, v_b, state, dec_seq):
    nblk = a_t.shape[0]
    cols = dec_seq * SEQ_PER_STEP
    return pl.pallas_call(
        functools.partial(_sample_state_kernel, dec_seq=dec_seq),
        grid=(nblk,),
        in_specs=[
            pl.BlockSpec((None, QW, cols), lambda i: (i, 0, 0)),
            pl.BlockSpec((None, QW, cols), lambda i: (i, 0, 0)),
            pl.BlockSpec((None, QW, cols), lambda i: (i, 0, 0)),
            pl.BlockSpec((None, cols, VW), lambda i: (i, 0, 0)),
            pl.BlockSpec((SEQ_PER_STEP, HEADS, DK, DV), lambda i: (i, 0, 0, 0)),
        ],
        out_specs=[
            pl.BlockSpec((None, cols, VW), lambda i: (i, 0, 0)),
            pl.BlockSpec((SEQ_PER_STEP, HEADS, DK, DV), lambda i: (i, 0, 0, 0)),
        ],
        out_shape=[
            jax.ShapeDtypeStruct((nblk, cols, VW), f32),
            jax.ShapeDtypeStruct(state.shape, f32),
        ],
        compiler_params=_cparams("parallel"),
        name="sample_state",
    )(a_t, k_t, q_t, v_b, state)


def _head_rms(x, g):
    parts = []
    for h in range(HEADS):
        parts.append(_rms(x[:, h * DV:(h + 1) * DV], g))
    return jnp.concatenate(parts, axis=1)


def _merge_kernel(hp_ref, hs_ref, oap_ref, oas_ref, omp_ref, oms_ref, odp_ref, ods_ref, gr_ref, mo_ref,
                  g0_ref, g1_ref, g2_ref, gg_ref, gm_ref, gd_ref, wa_ref, wm_ref, wd_ref, wo_ref, out_ref,
                  *, lam_init, n_prompt_blocks):
    pick = functools.partial(_pick_rows, n_prompt_blocks)
    gr = gr_ref[...]
    a = _head_rms(pick(oap_ref, oas_ref), gg_ref[...]) * (gr * _sigmoid(gr))
    m = _sigmoid(mo_ref[...]) * _head_rms(pick(omp_ref, oms_ref), gm_ref[...])
    d = _head_rms(pick(odp_ref, ods_ref), gd_ref[...]) * (1.0 - lam_init)
    merged = (g0_ref[...] * _dot(a, wa_ref[...]) + g1_ref[...] * _dot(m, wm_ref[...])
              + g2_ref[...] * _dot(d, wd_ref[...]))
    out_ref[...] = pick(hp_ref, hs_ref) + _dot(merged, wo_ref[...])


def _merge(h_p, h_s, sample_block, o_a, o_m, o_d, proj, gates, g_gla, g_ml, g_diff, w_a, w_m, w_d, w_out, lam_init,
           n_prompt_blocks):
    T = (n_prompt_blocks + 1) * TM
    row = lambda i: (i, 0)
    const = lambda i: (0, 0)
    return pl.pallas_call(
        functools.partial(_merge_kernel, lam_init=lam_init, n_prompt_blocks=n_prompt_blocks),
        grid=(T // TM,),
        in_specs=[
            *_row_specs(D_MODEL, n_prompt_blocks, sample_block),
            *_row_specs(VW, n_prompt_blocks, 0),
            *_row_specs(VW, n_prompt_blocks, 0),
            *_row_specs(VW, n_prompt_blocks, 0),
            pl.BlockSpec((TM, VW), lambda i: (i, T_GR)),
            pl.BlockSpec((TM, VW), lambda i: (i, T_MO)),
            pl.BlockSpec((TM, D_MODEL), lambda i: (i, 0)),
            pl.BlockSpec((TM, D_MODEL), lambda i: (i, 1)),
            pl.BlockSpec((TM, D_MODEL), lambda i: (i, 2)),
            pl.BlockSpec((1, DV), const),
            pl.BlockSpec((1, DV), const),
            pl.BlockSpec((1, DV), const),
            pl.BlockSpec((VW, D_MODEL), const),
            pl.BlockSpec((VW, D_MODEL), const),
            pl.BlockSpec((VW, D_MODEL), const),
            pl.BlockSpec((D_MODEL, D_MODEL), const),
        ],
        out_specs=pl.BlockSpec((TM, D_MODEL), row),
        out_shape=jax.ShapeDtypeStruct((T, D_MODEL), f32),
        compiler_params=_cparams("parallel"),
        name="merge",
    )(h_p, h_s, *o_a, *o_m, *o_d, proj, proj, gates, gates, gates, g_gla, g_ml, g_diff, w_a, w_m, w_d, w_out)


FF_SLABS = ((0, 768), (768, 768), (1536, 768), (2304, 512))


def _ffn_kernel(h_ref, gf_ref, wg_ref, wu_ref, wd_ref, pp_ref, ps_ref, gp_ref, wpg_ref, wpp_ref, gfin_ref,
                out_ref, *, final, n_prompt_blocks):
    h = h_ref[...]
    xf = _rms(h, gf_ref[...]).astype(bf16)
    h2 = h
    for lo, width in FF_SLABS:
        gate = jnp.dot(xf, wg_ref[:, lo:lo + width], preferred_element_type=f32)
        up = jnp.dot(xf, wu_ref[:, lo:lo + width], preferred_element_type=f32)
        h2 = h2 + _dot(gate * _sigmoid(gate) * up, wd_ref[lo:lo + width, :])
    ple_gate = _sigmoid(_dot(_rms(h2, gp_ref[...]), wpg_ref[...]))
    h3 = h2 + ple_gate * _dot(_pick_rows(n_prompt_blocks, pp_ref, ps_ref), wpp_ref[...])
    out_ref[...] = _rms(h3, gfin_ref[...]) if final else h3


def _ffn(h, g_ffn, w_g, w_u, w_d, p_p, p_s, g_ple, w_pg, w_pp, g_final, final, n_prompt_blocks):
    T = h.shape[0]
    row = lambda i: (i, 0)
    const = lambda i: (0, 0)
    resident = lambda shape: pl.BlockSpec(shape, const, pipeline_mode=pl.Buffered(1))
    assert FF_SLABS[-1][0] + FF_SLABS[-1][1] == D_FF
    return pl.pallas_call(
        functools.partial(_ffn_kernel, final=final, n_prompt_blocks=n_prompt_blocks),
        grid=(T // TM,),
        in_specs=[
            pl.BlockSpec((TM, D_MODEL), row),
            pl.BlockSpec((1, D_MODEL), const),
            resident((D_MODEL, D_FF)),
            resident((D_MODEL, D_FF)),
            resident((D_FF, D_MODEL)),
            *_row_specs(PLE_DIM, n_prompt_blocks, 0),
            pl.BlockSpec((1, D_MODEL), const),
            resident((D_MODEL, D_MODEL)),
            resident((PLE_DIM, D_MODEL)),
            pl.BlockSpec((1, D_MODEL), const),
        ],
        out_specs=pl.BlockSpec((TM, D_MODEL), row),
        out_shape=jax.ShapeDtypeStruct((T, D_MODEL), f32),
        compiler_params=_cparams("parallel"),
        name="ffn",
    )(h, g_ffn, w_g, w_u, w_d, p_p, p_s, g_ple, w_pg, w_pp, g_final)


def _rope_tables(positions):
    half = ROT_DIM // 2
    inv_freq = ROPE_THETA ** (-jnp.arange(half, dtype=f32) * 2.0 / ROT_DIM)
    ang = positions.astype(f32)[:, None] * inv_freq[None, :]
    cos, sin = jnp.cos(ang), jnp.sin(ang)
    n = positions.shape[0]
    one = jnp.ones((n, DK - ROT_DIM), f32)
    zero = jnp.zeros((n, DK - ROT_DIM), f32)
    zh = jnp.zeros((n, half), f32)
    cos_h = jnp.concatenate([cos, cos, one], axis=1)
    up_h = jnp.concatenate([-sin, zh, zero], axis=1)
    dn_h = jnp.concatenate([zh, sin, zero], axis=1)
    rep = lambda a: jnp.concatenate([a] * (LANES // DK), axis=1)
    return rep(cos_h), rep(up_h), rep(dn_h)


def _prep_in_weights(w_in_l, w_gate_l):
    o = np.cumsum((0, QW, QW, VW, VW, GLA_RANK, QW, QW, VW, VW, 2 * HEADS, 2 * QW, 2 * QW, VW))
    seg = lambda i: w_in_l[:, int(o[i]):int(o[i + 1])]
    gq, gk, gv, gr, glr, mq, mk, mv, mo, mif, dq, dk, dv = (seg(i) for i in range(13))
    main = jnp.concatenate([w_gate_l, gq, gk, gv, gr, mq, mk, mv, mo, dq, dk, dv], axis=1).astype(bf16)
    pad = jnp.zeros((D_MODEL, LANES - GLA_RANK - 2 * HEADS), f32)
    small = jnp.concatenate([glr, mif, pad], axis=1).astype(bf16)
    return main, small


def kernel(x_prompt, x_sample, cache_k, cache_v, state_gla, state_mlstm_c, state_mlstm_n, state_mlstm_m, page_table, p_prompt, p_sample, g_mix, w_in, w_gla_gk, b_gla_gk, g_gla_norm, b_mlstm_if, g_mlstm_norm, diff_lambda, g_diff_norm, w_branch, w_gate, w_out, g_ffn, w_ffn_gate, w_ffn_up, w_ffn_down, g_ple, w_ple_gate, w_ple_proj, g_final):
    bp, lp, _ = x_prompt.shape
    bs, ls, _ = x_sample.shape
    depth = g_mix.shape[0]
    n_pages = page_table.shape[1]
    page = cache_k.shape[2]
    past = n_pages * page
    tp = bp * lp
    ts = bs * ls
    assert lp % TM == 0 and ts == TM and ls * 2 * HEADS == Q_ROWS and bs % SEQ_PER_STEP == 0 and page == LANES
    npb = tp // TM

    def to_step_major(a):
        return jnp.swapaxes(a, 0, 1).reshape((ts,) + a.shape[2:])

    def from_step_major(a):
        return jnp.swapaxes(a.reshape((ls, bs) + a.shape[1:]), 0, 1)

    h_p, h_s, h_s_row0 = x_prompt.reshape(tp, D_MODEL), to_step_major(x_sample), 0

    pos_rows = jnp.concatenate([jnp.arange(lp, dtype=jnp.int32),
                                past + jnp.repeat(jnp.arange(ls, dtype=jnp.int32), bs)])
    cos_t, sup_t, sdn_t = _rope_tables(pos_rows)

    ck = jnp.transpose(cache_k, (0, 1, 3, 4, 2)).reshape(cache_k.shape[0], cache_k.shape[1], 2 * QW, page)
    cv = cache_v.reshape(cache_v.shape[0], cache_v.shape[1], page * HEADS, DV)
    nblk = bs // SEQ_PER_STEP

    def to_cols(a):
        return a.reshape(ls, nblk, SEQ_PER_STEP, QW).transpose(1, 3, 0, 2).reshape(nblk, QW, ls * SEQ_PER_STEP)

    def to_blocks(a):
        return a.reshape(ls, nblk, SEQ_PER_STEP, VW).transpose(1, 0, 2, 3).reshape(nblk, ls * SEQ_PER_STEP, VW)

    def from_blocks(a):
        return a.reshape(nblk, ls, SEQ_PER_STEP, VW).transpose(1, 0, 2, 3).reshape(ts, VW)

    eye_heads = jnp.eye(2 * HEADS, dtype=f32)
    outs_p, outs_s = [], []
    y = None
    for l in range(depth):
        lam_init = 0.8 - 0.6 * math.exp(-0.3 * l)
        w_main, w_small = _prep_in_weights(w_in[l], w_gate[l])
        w_gk = jnp.concatenate([w_gla_gk[l], jnp.zeros((LANES - GLA_RANK, QW), f32)], axis=0)
        b_if = jnp.zeros((1, LANES), f32).at[0, L_MI:L_MI + 2 * HEADS].set(b_mlstm_if[l])
        gates, proj, k_t, k_ts, la, gif = _inproj(h_p, h_s, h_s_row0 // TM_IN, g_mix[l][None], w_main, w_small, w_gk,
                                                  b_gla_gk[l][None], b_if, cos_t, sup_t, sdn_t, bp, lp, bs, ls)

        oa_p, gla_p = _gla_prompt(proj, la, bp, lp)
        om_p, c_p, n_p, m_p = _mlstm_prompt(proj, gif, bp, lp)
        oa_p, om_p = oa_p.reshape(tp, VW), om_p.reshape(tp, VW)
        od_p = _attn_prompt(proj, diff_lambda[l], bp, lp, lam_init)

        ps = proj[tp:]
        col = lambda t, lo=0, w=TILE: ps[:, t * TILE + lo:t * TILE + lo + w]
        ga, gk_, gq_, ma, mk_, mq_, n_s, m_s = _sample_prep(
            col(T_GQK, 0, QW), col(T_GQK, QW, QW), la[tp:], col(T_MQK, 0, QW), col(T_MQK, QW, QW), gif[tp:],
            state_mlstm_n[l].reshape(bs, QW), state_mlstm_m[l], bs, ls)
        oa_s, gla_s = _sample_state(to_cols(ga), to_cols(gk_), to_cols(gq_), to_blocks(col(T_GV)), state_gla[l], ls)
        om_s, c_s = _sample_state(to_cols(ma), to_cols(mk_), to_cols(mq_), to_blocks(col(T_MV)),
                                  state_mlstm_c[l], ls)
        q_s = from_step_major(col(T_DQ)).reshape(bs, ls, 2 * HEADS, DK)
        qbd = (q_s.transpose(0, 2, 1, 3)[:, :, :, None, :] * (DK ** -0.5)
               * eye_heads[None, :, None, :, None]).reshape(bs, Q_ROWS, 2 * QW).astype(bf16)
        v_s = from_step_major(col(T_DV))
        k_new_t = jnp.pad(jnp.transpose(k_ts, (2, 1, 0)), ((0, 0), (0, 0), (0, LANES - ls)))
        v_new = jnp.pad(v_s.reshape(bs, ls * HEADS, DV), ((0, 0), (0, (LANES - ls) * HEADS), (0, 0)))
        od_s = _attn_sample(page_table, qbd, k_new_t, v_new, diff_lambda[l], ck, cv, l, lam_init, ls)
        od_s = to_step_major(od_s[:, :ls])

        wb = w_branch[l].astype(bf16)
        h = _merge(h_p, h_s, h_s_row0 // TM, (oa_p, from_blocks(oa_s)), (om_p, from_blocks(om_s)), (od_p, od_s), proj,
                   gates, g_gla_norm[l][None], g_mlstm_norm[l][None], g_diff_norm[l][None],
                   wb[:VW], wb[VW:2 * VW], wb[2 * VW:], w_out[l].astype(bf16), lam_init, npb)
        h = _ffn(h, g_ffn[l][None], w_ffn_gate[l].astype(bf16), w_ffn_up[l].astype(bf16),
                 w_ffn_down[l].astype(bf16), p_prompt[l].reshape(tp, PLE_DIM), to_step_major(p_sample[l]),
                 g_ple[l][None], w_ple_gate[l].astype(bf16), w_ple_proj[l].astype(bf16), g_final[None],
                 l == depth - 1, npb)
        h_p, h_s, h_s_row0 = h, h, tp

        k_p = k_t.reshape(bp, 2 * HEADS, DK, lp).transpose(0, 3, 1, 2)
        k_new = k_ts.reshape(ls, 2 * HEADS, DK, bs).transpose(3, 0, 1, 2)
        outs_p.append((k_p, proj[:tp, T_DV * TILE:(T_DV + 1) * TILE].reshape(bp, lp, HEADS, DV),
                       gla_p.reshape(bp, HEADS, DK, DV), c_p, n_p, m_p[:, :, 0]))
        outs_s.append((k_new, v_s.reshape(bs, ls, HEADS, DV),
                       gla_s, c_s, n_s.reshape(bs, HEADS, DK), m_s[:, ::DK]))

    y_prompt = h[:tp].reshape(bp, lp, D_MODEL)
    y_sample = from_step_major(h[tp:])
    stack = lambda items: [jnp.stack(t) for t in zip(*items)]
    return tuple([y_prompt, y_sample] + stack(outs_p) + stack(outs_s))
```

```python
import functools
import math

import numpy as np
import jax
import jax.numpy as jnp
from jax import lax
from jax.experimental import pallas as pl
from jax.experimental.pallas import tpu as pltpu

f32 = jnp.float32
bf16 = jnp.bfloat16
HIGHEST = lax.Precision.HIGHEST

D_MODEL = 1024
PLE_DIM = 256
HEADS = 4
DK = 64
DV = 128
GLA_RANK = 16
GLA_GATE_NORM = 16.0
ROT_DIM = 16
ROPE_THETA = 500000.0
CHUNK = 64
D_FF = 2816
EPS = 1e-6
LOG2E = 1.4426950408889634
VW = HEADS * DV
QW = HEADS * DK

LANES = 128
SUBLANES = 8
VMEM_LIMIT = 48 * 1024 * 1024

TILE = 512
GATE_W = 3 * D_MODEL
T_GQK, T_GV, T_GR = 0, 1, 2
T_MQK, T_MV, T_MO = 3, 4, 5
T_DQ, T_DK, T_DV = 6, 7, 8
N_TILES = 9
P_MIX = N_TILES * TILE
P_MAIN = GATE_W + P_MIX
L_GLR = 0
L_MI = 16
L_MF = 20

TM = 512


def _cparams(*sem):
    return pltpu.CompilerParams(dimension_semantics=sem, vmem_limit_bytes=VMEM_LIMIT)


def _log_sigmoid(x):
    return jnp.minimum(x, 0.0) - jnp.log1p(jnp.exp(-jnp.abs(x)))


def _sigmoid(x):
    return 0.5 * jnp.tanh(0.5 * x) + 0.5


def _rms(x, g):
    return x * lax.rsqrt(jnp.mean(x * x, axis=-1, keepdims=True) + EPS) * g


def _dot(a, b):
    return jnp.dot(a.astype(bf16), b.astype(bf16), preferred_element_type=f32)


def _dot_nt(a, b):
    return lax.dot_general(a.astype(bf16), b.astype(bf16), (((1,), (1,)), ((), ())), preferred_element_type=f32)


def _dot_tn(a, b):
    return lax.dot_general(a.astype(bf16), b.astype(bf16), (((0,), (0,)), ((), ())), preferred_element_type=f32)


def _rope_tile(x, cos, sin_up, sin_dn):
    parts = []
    for c in range(x.shape[1] // LANES):
        xc = x[:, c * LANES:(c + 1) * LANES]
        parts.append(xc * cos + pltpu.roll(xc, LANES - ROT_DIM // 2, axis=1) * sin_up
                     + pltpu.roll(xc, ROT_DIM // 2, axis=1) * sin_dn)
    return jnp.concatenate(parts, axis=1)


def _row_specs(width, n_prompt_blocks, sample_block, tm=None, n_sample_blocks=1, prompt_block0=0):
    tm = TM if tm is None else tm
    return (pl.BlockSpec((tm, width), lambda i: (prompt_block0 + jnp.minimum(i, n_prompt_blocks - 1), 0)),
            pl.BlockSpec((tm, width),
                         lambda i: (sample_block + jnp.clip(i - n_prompt_blocks, 0, n_sample_blocks - 1), 0)))


def _pick_rows(n_prompt_blocks, p_ref, s_ref):
    return jnp.where(pl.program_id(0) >= n_prompt_blocks, s_ref[...], p_ref[...])


TM_IN = 256


def _inproj_kernel(xp_ref, xs_ref, g_ref, w_ref, ws_ref, wgk_ref, bgk_ref, bif_ref, cos_ref, sup_ref, sdn_ref,
                   gates_ref, proj_ref, kt_ref, kts_ref, la_ref, gif_ref, *, n_prompt_blocks, dec_batch):
    i = pl.program_id(0)
    xn = _rms(_pick_rows(n_prompt_blocks, xp_ref, xs_ref), g_ref[...]).astype(bf16)
    small = jnp.dot(xn, ws_ref[...], preferred_element_type=f32)
    z = jnp.dot(small, wgk_ref[...], preferred_element_type=f32, precision=HIGHEST) + bgk_ref[...]
    la_ref[...] = _log_sigmoid(z) * (1.0 / GLA_GATE_NORM)
    gi = small + bif_ref[...]
    lane = lax.broadcasted_iota(jnp.int32, gi.shape, 1)
    gif_ref[...] = jnp.where((lane >= L_MF) & (lane < L_MF + HEADS), _log_sigmoid(gi), gi)

    for c in range(GATE_W // TILE):
        cols = slice(c * TILE, (c + 1) * TILE)
        gates_ref[:, cols] = _sigmoid(jnp.dot(xn, w_ref[:, cols], preferred_element_type=f32)).astype(bf16)

    cos, sup, sdn = cos_ref[...], sup_ref[...], sdn_ref[...]
    for t in range(N_TILES):
        acc = jnp.dot(xn, w_ref[:, GATE_W + t * TILE:GATE_W + (t + 1) * TILE], preferred_element_type=f32)
        if t in (T_DQ, T_DK):
            acc = _rope_tile(acc, cos, sup, sdn)
        proj_ref[:, t * TILE:(t + 1) * TILE] = acc
        if t == T_DK:
            k_t = acc.T

    @pl.when(i < n_prompt_blocks)
    def _():
        kt_ref[...] = k_t

    @pl.when(i >= n_prompt_blocks)
    def _():
        for s in range(k_t.shape[1] // dec_batch):
            kts_ref[s] = k_t[:, s * dec_batch:(s + 1) * dec_batch]


def _inproj(h_p, h_s, sample_block, g_mix, w_main, w_small, w_gk, b_gk, b_if, cos_t, sup_t, sdn_t,
            batch, seq, dec_batch, dec_seq):
    tm = TM_IN
    seq_blocks = seq // tm
    npb = batch * seq_blocks
    steps_per_block = tm // dec_batch
    nsb = dec_seq // steps_per_block
    nt = npb + nsb
    T = nt * tm
    const = lambda i: (0, 0)
    resident = lambda shape: pl.BlockSpec(shape, const, pipeline_mode=pl.Buffered(1))

    def tab_map(i):
        return (jnp.where(i < npb, i % seq_blocks, seq_blocks + i - npb), 0)

    def kt_map(i):
        j = jnp.minimum(i, npb - 1)
        return (j // seq_blocks, 0, j % seq_blocks)

    return pl.pallas_call(
        functools.partial(_inproj_kernel, n_prompt_blocks=npb, dec_batch=dec_batch),
        grid=(nt,),
        in_specs=[
            *_row_specs(D_MODEL, npb, sample_block, tm, nsb),
            pl.BlockSpec((1, D_MODEL), const),
            resident((D_MODEL, P_MAIN)),
            resident((D_MODEL, LANES)),
            resident((LANES, QW)),
            pl.BlockSpec((1, QW), const),
            pl.BlockSpec((1, LANES), const),
            pl.BlockSpec((tm, LANES), tab_map),
            pl.BlockSpec((tm, LANES), tab_map),
            pl.BlockSpec((tm, LANES), tab_map),
        ],
        out_specs=[
            pl.BlockSpec((tm, GATE_W), lambda i: (i, 0)),
            pl.BlockSpec((tm, P_MIX), lambda i: (i, 0)),
            pl.BlockSpec((None, TILE, tm), kt_map),
            pl.BlockSpec((steps_per_block, TILE, dec_batch), lambda i: (jnp.maximum(i - npb, 0), 0, 0)),
            pl.BlockSpec((tm, QW), lambda i: (i, 0)),
            pl.BlockSpec((tm, LANES), lambda i: (i, 0)),
        ],
        out_shape=[
            jax.ShapeDtypeStruct((T, GATE_W), bf16),
            jax.ShapeDtypeStruct((T, P_MIX), f32),
            jax.ShapeDtypeStruct((batch, TILE, seq), f32),
            jax.ShapeDtypeStruct((dec_seq, TILE, dec_batch), f32),
            jax.ShapeDtypeStruct((T, QW), f32),
            jax.ShapeDtypeStruct((T, LANES), f32),
        ],
        compiler_params=_cparams("arbitrary"),
        name="inproj",
    )(h_p, h_s, g_mix, w_main, w_small, w_gk, b_gk, b_if, cos_t, sup_t, sdn_t)


def _tril(n):
    r = lax.broadcasted_iota(jnp.int32, (n, n), 0)
    c = lax.broadcasted_iota(jnp.int32, (n, n), 1)
    return r >= c


def _gla_prompt_kernel(*refs, n_chunks, batch):
    q_refs, k_refs, v_refs, la_refs = (refs[i * batch:(i + 1) * batch] for i in range(4))
    o_ref, s_out_ref, s_scr = refs[4 * batch:]
    blk = pl.program_id(0)

    @pl.when(blk == 0)
    def _():
        s_scr[...] = jnp.zeros_like(s_scr)

    ltri = _tril(CHUNK).astype(f32)
    r2 = lax.broadcasted_iota(jnp.int32, (2 * CHUNK, CHUNK), 0)
    c2 = lax.broadcasted_iota(jnp.int32, (2 * CHUNK, CHUNK), 1)
    tril2 = jnp.where(r2 >= CHUNK, r2 - CHUNK, r2) >= c2
    lane = lax.broadcasted_iota(jnp.int32, (CHUNK, LANES), 1)
    lo = (lane < DK).astype(f32)
    hi = 1.0 - lo

    def chunk(c, carry):
        rows = pl.ds(pl.multiple_of(c * CHUNK, CHUNK), CHUNK)
        for s in range(batch):
            g = la_refs[s][rows, :]
            b = jnp.dot(ltri, g, preferred_element_type=f32, precision=HIGHEST)
            b_last = b[CHUNK - 1:CHUNK, :]
            q = q_refs[s][rows, :]
            k = k_refs[s][rows, :]
            qg = q * jnp.exp(b) * (DK ** -0.5)
            kg = k * jnp.exp(-b)
            kd = k * jnp.exp(b_last - b)
            dec = jnp.exp(b_last)
            for p in range(HEADS // 2):
                ls = slice(p * LANES, (p + 1) * LANES)
                qg_p, kg_p, kd_p = qg[:, ls], kg[:, ls], kd[:, ls]
                qs = jnp.concatenate([qg_p * lo, qg_p * hi], axis=0)
                a = jnp.where(tril2, _dot_nt(qs, kg_p), 0.0)
                s_p = s_scr[s, p]
                inter = _dot(qs, s_p)
                v0 = v_refs[s][rows, pl.ds((2 * p) * DV, DV)]
                v1 = v_refs[s][rows, pl.ds((2 * p + 1) * DV, DV)]
                o_ref[s, rows, pl.ds((2 * p) * DV, DV)] = inter[:CHUNK] + _dot(a[:CHUNK], v0)
                o_ref[s, rows, pl.ds((2 * p + 1) * DV, DV)] = inter[CHUNK:] + _dot(a[CHUNK:], v1)
                dcol = jnp.broadcast_to(dec[:, ls], (LANES, LANES)).T
                s_scr[s, p] = dcol * s_p + _dot_tn(kd_p * lo, v0) + _dot_tn(kd_p * hi, v1)
        return carry

    lax.fori_loop(0, n_chunks, chunk, 0)

    @pl.when(blk == pl.num_programs(0) - 1)
    def _():
        s_out_ref[...] = s_scr[...]


def _stream_specs(width, batch, nb, col):
    return [pl.BlockSpec((TM, width), functools.partial(lambda s, i: (s * nb + i, col), s)) for s in range(batch)]


def _gla_prompt(proj, la, batch, seq):
    nb = seq // TM
    qw_blocks = TILE // QW
    return pl.pallas_call(
        functools.partial(_gla_prompt_kernel, n_chunks=TM // CHUNK, batch=batch),
        grid=(nb,),
        in_specs=[
            *_stream_specs(QW, batch, nb, T_GQK * qw_blocks),
            *_stream_specs(QW, batch, nb, T_GQK * qw_blocks + 1),
            *_stream_specs(VW, batch, nb, T_GV),
            *_stream_specs(QW, batch, nb, 0),
        ],
        out_specs=[
            pl.BlockSpec((batch, TM, VW), lambda i: (0, i, 0)),
            pl.BlockSpec((batch, HEADS // 2, LANES, LANES), lambda i: (0, 0, 0, 0)),
        ],
        out_shape=[
            jax.ShapeDtypeStruct((batch, seq, VW), f32),
            jax.ShapeDtypeStruct((batch, HEADS // 2, LANES, LANES), f32),
        ],
        scratch_shapes=[pltpu.VMEM((batch, HEADS // 2, LANES, LANES), f32)],
        compiler_params=_cparams("arbitrary"),
        name="gla_prompt",
    )(*([proj] * (3 * batch)), *([la] * batch))


def _mlstm_prompt_kernel(*refs, n_chunks, batch):
    q_refs, k_refs, v_refs, gif_refs = (refs[i * batch:(i + 1) * batch] for i in range(4))
    h_ref, c_out_ref, n_out_ref, m_out_ref, ct_scr, n_scr, m_scr = refs[4 * batch:]
    blk = pl.program_id(0)

    @pl.when(blk == 0)
    def _():
        ct_scr[...] = jnp.zeros_like(ct_scr)
        n_scr[...] = jnp.zeros_like(n_scr)
        m_scr[...] = jnp.zeros_like(m_scr)

    ltri = _tril(CHUNK).astype(f32)
    r_i = lax.broadcasted_iota(jnp.int32, (CHUNK, CHUNK), 0)
    c_i = lax.broadcasted_iota(jnp.int32, (CHUNK, CHUNK), 1)
    upper = r_i <= c_i

    def chunk(c, carry):
        rows = pl.ds(pl.multiple_of(c * CHUNK, CHUNK), CHUNK)
        for sq in range(batch):
            g = gif_refs[sq][rows, :]
            fc = jnp.dot(ltri, g, preferred_element_type=f32, precision=HIGHEST)
            g_t = g.T
            fc_t = fc.T
            q_all = q_refs[sq][rows, :]
            k_all = k_refs[sq][rows, :] * (DK ** -0.5)
            n_all = n_scr[sq]
            for h in range(HEADS):
                i_col = g[:, L_MI + h:L_MI + h + 1]
                f_col = fc[:, L_MF + h:L_MF + h + 1]
                i_row = g_t[L_MI + h:L_MI + h + 1, :]
                f_row = fc_t[L_MF + h:L_MF + h + 1, :]
                f_last = fc[CHUNK - 1:CHUNK, L_MF + h:L_MF + h + 1]
                m_prev = m_scr[sq, h:h + 1, 0:1]
                c_t = ct_scr[sq, h]
                q = q_all[:, h * DK:(h + 1) * DK]
                k = k_all[:, h * DK:(h + 1) * DK]
                v_t = v_refs[sq][rows, pl.ds(h * DV, DV)].T

                log_d = jnp.where(upper, f_row + (i_col - f_col), -jnp.inf)
                m_inter = m_prev + f_row
                m_row = jnp.maximum(m_inter, jnp.max(log_d, axis=0, keepdims=True))
                w_inter = jnp.exp(m_inter - m_row)
                sc = _dot_nt(k, q) * jnp.exp(log_d - m_row)
                num = w_inter * _dot_nt(c_t, q) + _dot(v_t, sc)
                qn = _dot_nt(n_all, q)[h:h + 1, :]
                den = w_inter * qn + jnp.sum(sc, axis=0, keepdims=True)
                h_t = num / jnp.maximum(jnp.abs(den), jnp.exp(-m_row))
                h_ref[sq, rows, pl.ds(h * DV, DV)] = h_t.T

                log_w = f_last - f_col + i_col
                m_new = jnp.maximum(m_prev + f_last, jnp.max(log_w, axis=0, keepdims=True))
                wk = jnp.exp(log_w - m_new) * k
                decay = jnp.exp(m_prev + f_last - m_new)
                ct_scr[sq, h] = decay * c_t + _dot(v_t, wk)
                n_scr[sq, h:h + 1, :] = decay * n_all[h:h + 1, :] + jnp.sum(wk, axis=0, keepdims=True)
                m_scr[sq, h:h + 1, :] = jnp.broadcast_to(m_new, (1, LANES))
        return carry

    lax.fori_loop(0, n_chunks, chunk, 0)

    @pl.when(blk == pl.num_programs(0) - 1)
    def _():
        for sq in range(batch):
            for h in range(HEADS):
                c_out_ref[sq, h] = ct_scr[sq, h].T
        n_out_ref[...] = n_scr[:, 0:HEADS, :]
        m_out_ref[...] = m_scr[:, 0:HEADS, :]


def _mlstm_prompt(proj, gif, batch, seq):
    nb = seq // TM
    qw_blocks = TILE // QW
    return pl.pallas_call(
        functools.partial(_mlstm_prompt_kernel, n_chunks=TM // CHUNK, batch=batch),
        grid=(nb,),
        in_specs=[
            *_stream_specs(QW, batch, nb, T_MQK * qw_blocks),
            *_stream_specs(QW, batch, nb, T_MQK * qw_blocks + 1),
            *_stream_specs(VW, batch, nb, T_MV),
            *_stream_specs(LANES, batch, nb, 0),
        ],
        out_specs=[
            pl.BlockSpec((batch, TM, VW), lambda i: (0, i, 0)),
            pl.BlockSpec((batch, HEADS, DK, DV), lambda i: (0, 0, 0, 0)),
            pl.BlockSpec((batch, HEADS, DK), lambda i: (0, 0, 0)),
            pl.BlockSpec((batch, HEADS, LANES), lambda i: (0, 0, 0)),
        ],
        out_shape=[
            jax.ShapeDtypeStruct((batch, seq, VW), f32),
            jax.ShapeDtypeStruct((batch, HEADS, DK, DV), f32),
            jax.ShapeDtypeStruct((batch, HEADS, DK), f32),
            jax.ShapeDtypeStruct((batch, HEADS, LANES), f32),
        ],
        scratch_shapes=[
            pltpu.VMEM((batch, HEADS, DV, DK), f32),
            pltpu.VMEM((batch, SUBLANES, DK), f32),
            pltpu.VMEM((batch, SUBLANES, LANES), f32),
        ],
        compiler_params=_cparams("arbitrary"),
        name="mlstm_prompt",
    )(*([proj] * (3 * batch)), *([gif] * batch))


def _diff_lambda_value(lam_ref, lam_init):
    lam = lam_ref[...]
    s1 = jnp.sum(lam[0:1] * lam[1:2], axis=1, keepdims=True)
    s2 = jnp.sum(lam[2:3] * lam[3:4], axis=1, keepdims=True)
    return jnp.exp(s1) - jnp.exp(s2) + lam_init


def _attn_prompt_kernel(qi_ref, ki_ref, q_ref, k_ref, v_ref, lam_ref, o_ref, q2_scr, m_scr, l_scr, acc_scr,
                        *, tq, lam_init):
    p = pl.program_id(1)
    qi = qi_ref[p]
    ki = ki_ref[p]
    pair = lambda j: slice(j * LANES, (j + 1) * LANES)

    @pl.when(ki == 0)
    def _():
        for j in range(HEADS):
            q = q_ref[:, pair(j)] * (DK ** -0.5 * LOG2E)
            lane = lax.broadcasted_iota(jnp.int32, q.shape, 1)
            q2_scr[j, 0:tq, :] = jnp.where(lane < DK, q, 0.0).astype(bf16)
            q2_scr[j, tq:2 * tq, :] = jnp.where(lane >= DK, q, 0.0).astype(bf16)
        m_scr[...] = jnp.full_like(m_scr, -jnp.inf)
        l_scr[...] = jnp.zeros_like(l_scr)
        acc_scr[...] = jnp.zeros_like(acc_scr)

    def step(masked):
        sts = [_dot_nt(k_ref[:, pair(j)], q2_scr[j]) for j in range(HEADS)]
        pts, alphas = [], []
        for j in range(HEADS):
            st = sts[j]
            if masked:
                r = lax.broadcasted_iota(jnp.int32, st.shape, 0)
                c = lax.broadcasted_iota(jnp.int32, st.shape, 1)
                c = jnp.where(c >= tq, c - tq, c)
                st = jnp.where(r <= c, st, -jnp.inf)
            m_prev = m_scr[j]
            m_new = jnp.maximum(m_prev, jnp.max(st, axis=0, keepdims=True))
            alpha = jnp.exp2(m_prev - m_new)
            pt = jnp.exp2(st - m_new)
            l_scr[j] = alpha * l_scr[j] + jnp.sum(pt, axis=0, keepdims=True)
            m_scr[j] = m_new
            pts.append(pt.astype(bf16))
            alphas.append(alpha)
        for j in range(HEADS):
            acc_scr[j] = alphas[j] * acc_scr[j] + _dot_tn(v_ref[:, pair(j)], pts[j])

    @pl.when(ki < qi)
    def _():
        step(False)

    @pl.when(ki == qi)
    def _():
        step(True)
        lam = _diff_lambda_value(lam_ref, lam_init)
        for j in range(HEADS):
            ot = acc_scr[j] / l_scr[j]
            o_ref[:, pair(j)] = (ot[:, 0:tq] - lam * ot[:, tq:2 * tq]).T


def _attn_prompt(proj, diff_lambda, batch, seq, lam_init, tq=512):
    nq = seq // tq
    pairs = [(q, k) for q in range(nq) for k in range(q + 1)]
    qi_tab = jnp.asarray(np.array([a for a, _ in pairs], np.int32))
    ki_tab = jnp.asarray(np.array([b for _, b in pairs], np.int32))
    grid_spec = pltpu.PrefetchScalarGridSpec(
        num_scalar_prefetch=2,
        grid=(batch, len(pairs)),
        in_specs=[
            pl.BlockSpec((tq, TILE), lambda b, p, qt, kt: (b * nq + qt[p], T_DQ)),
            pl.BlockSpec((tq, TILE), lambda b, p, qt, kt: (b * nq + kt[p], T_DK)),
            pl.BlockSpec((tq, TILE), lambda b, p, qt, kt: (b * nq + kt[p], T_DV)),
            pl.BlockSpec((4, DK), lambda b, p, qt, kt: (0, 0)),
        ],
        out_specs=pl.BlockSpec((tq, VW), lambda b, p, qt, kt: (b * nq + qt[p], 0)),
        scratch_shapes=[
            pltpu.VMEM((HEADS, 2 * tq, LANES), bf16),
            pltpu.VMEM((HEADS, 1, 2 * tq), f32),
            pltpu.VMEM((HEADS, 1, 2 * tq), f32),
            pltpu.VMEM((HEADS, DV, 2 * tq), f32),
        ],
    )
    return pl.pallas_call(
        functools.partial(_attn_prompt_kernel, tq=tq, lam_init=lam_init),
        grid_spec=grid_spec,
        out_shape=jax.ShapeDtypeStruct((batch * seq, VW), f32),
        compiler_params=_cparams("parallel", "arbitrary"),
        name="attn_prompt",
    )(qi_tab, ki_tab, proj, proj, proj, diff_lambda)


def _attn_sample_kernel(pt_ref, q_ref, kts_ref, v_ref, lam_ref, ck_hbm, cv_hbm, o_ref, kbuf, vbuf, sem,
                        *, layer, n_pages, page, dec_seq, nseq, lam_init):
    b = pl.program_id(0)
    nb = pl.num_programs(0)
    past = n_pages * page
    slot = b % 2
    n_qh = 2 * HEADS

    def copies(seq_idx, s):
        out = []
        for pg in range(n_pages):
            src = pt_ref[seq_idx, pg]
            out.append(pltpu.make_async_copy(ck_hbm.at[layer, src], kbuf.at[s, :, pl.ds(pg * page, page)],
                                             sem.at[0, s]))
            out.append(pltpu.make_async_copy(cv_hbm.at[layer, src],
                                             vbuf.at[s, pl.ds(pg * page * HEADS, page * HEADS), :], sem.at[1, s]))
        return out

    @pl.when(b == 0)
    def _():
        kbuf[:, :, pl.ds(past, LANES)] = jnp.zeros((2, kbuf.shape[1], LANES), f32)
        vbuf[:, pl.ds(past * HEADS, LANES * HEADS), :] = jnp.zeros((2, LANES * HEADS, DV), f32)
        o_ref[...] = jnp.zeros_like(o_ref)
        for cp in copies(0, 0):
            cp.start()

    @pl.when(b + 1 < nb)
    def _():
        for cp in copies(b + 1, 1 - slot):
            cp.start()

    for cp in copies(b, slot):
        cp.wait()

    shift = lax.rem(nseq - b, nseq)
    b8 = (b // SUBLANES) * SUBLANES
    groups = [pl.ds(pl.multiple_of(t * nseq + b8, SUBLANES), SUBLANES) for t in range(dec_seq)]
    mine = lax.broadcasted_iota(jnp.int32, (SUBLANES, 2 * QW), 0) == b - b8
    pick = lambda ref, t: jnp.sum(jnp.where(mine, ref[groups[t], :], 0.0), axis=0, keepdims=True)
    head_of_lane = lax.broadcasted_iota(jnp.int32, (n_qh, 2 * QW), 1) // DK
    head_of_row = lax.broadcasted_iota(jnp.int32, (n_qh, 2 * QW), 0)
    q_rows = []
    for t in range(dec_seq):
        kbuf[slot, :, pl.ds(past + t, 1)] = pltpu.roll(kts_ref[t], shift, axis=1)[:, 0:1]
        v_row = pick(v_ref, t)
        for h in range(HEADS):
            vbuf[slot, pl.ds((past + t) * HEADS + h, 1), :] = v_row[:, h * DV:(h + 1) * DV]
        q = jnp.broadcast_to(pick(q_ref, t) * (DK ** -0.5), (n_qh, 2 * QW))
        q_rows.append(jnp.where(head_of_lane == head_of_row, q, 0.0))
    qbd = jnp.concatenate(q_rows, axis=0)

    s = _dot(qbd, kbuf[slot])
    r = lax.broadcasted_iota(jnp.int32, s.shape, 0)
    c = lax.broadcasted_iota(jnp.int32, s.shape, 1)
    s = jnp.where(c - past <= r // n_qh, s, -jnp.inf)
    m = jnp.max(s, axis=1, keepdims=True)
    pr = jnp.exp(s - m)
    inv_l = 1.0 / jnp.sum(pr, axis=1, keepdims=True)
    lam = _diff_lambda_value(lam_ref, lam_init)
    mine_v = mine[:, 0:DV]
    for hv in range(HEADS):
        v_h = vbuf[slot, pl.ds(hv, past + LANES, stride=HEADS), :]
        o_h = _dot(pr, v_h) * inv_l
        for t in range(dec_seq):
            r1 = t * n_qh + 2 * hv
            d = o_h[r1:r1 + 1] - lam * o_h[r1 + 1:r1 + 2]
            lanes = slice(hv * DV, (hv + 1) * DV)
            o_ref[groups[t], lanes] = jnp.where(mine_v, d, o_ref[groups[t], lanes])


def _attn_sample(page_table, proj, k_ts, sample_block, diff_lambda, cache_kt, cache_v2, layer, lam_init):
    nseq, n_pages = page_table.shape
    dec_seq = k_ts.shape[0]
    rows = dec_seq * nseq
    page = cache_kt.shape[3]
    cols = n_pages * page + LANES
    grid_spec = pltpu.PrefetchScalarGridSpec(
        num_scalar_prefetch=1,
        grid=(nseq,),
        in_specs=[
            pl.BlockSpec((rows, TILE), lambda b, pt: (sample_block, T_DQ)),
            pl.BlockSpec((dec_seq, TILE, nseq), lambda b, pt: (0, 0, 0)),
            pl.BlockSpec((rows, TILE), lambda b, pt: (sample_block, T_DV)),
            pl.BlockSpec((4, DK), lambda b, pt: (0, 0)),
            pl.BlockSpec(memory_space=pl.ANY),
            pl.BlockSpec(memory_space=pl.ANY),
        ],
        out_specs=pl.BlockSpec((rows, VW), lambda b, pt: (0, 0)),
        scratch_shapes=[
            pltpu.VMEM((2, 2 * QW, cols), f32),
            pltpu.VMEM((2, cols * HEADS, DV), f32),
            pltpu.SemaphoreType.DMA((2, 2)),
        ],
    )
    return pl.pallas_call(
        functools.partial(_attn_sample_kernel, layer=layer, n_pages=n_pages, page=page, dec_seq=dec_seq,
                          nseq=nseq, lam_init=lam_init),
        grid_spec=grid_spec,
        out_shape=jax.ShapeDtypeStruct((rows, VW), f32),
        compiler_params=_cparams("arbitrary"),
        name="attn_sample",
    )(page_table, proj, k_ts, proj, diff_lambda, cache_kt, cache_v2)


def _head_of_lane(shape):
    return lax.broadcasted_iota(jnp.int32, shape, 1) // DK


def _expand_heads(cols, base, head):
    out = cols[:, base + HEADS - 1:base + HEADS]
    for h in range(HEADS - 2, -1, -1):
        out = jnp.where(head == h, cols[:, base + h:base + h + 1], out)
    return out


def _sample_prep_kernel(gq_ref, gk_ref, la_ref, mq_ref, mk_ref, gif_ref, n0_ref, m0_ref,
                        ga_ref, gkk_ref, gqq_ref, ma_ref, mkk_ref, mqq_ref, n_out_ref, m_out_ref, *, nseq, dec_seq):
    head = _head_of_lane((nseq, QW))
    m = _expand_heads(m0_ref[...], 0, head)
    n = n0_ref[...]
    for t in range(dec_seq):
        rows = pl.ds(t * nseq, nseq)
        ga_ref[t] = jnp.exp(la_ref[rows, :]).T
        gkk_ref[t] = gk_ref[rows, :].T
        gqq_ref[t] = (gq_ref[rows, :] * (DK ** -0.5)).T
        gates = gif_ref[rows, :]
        i_e = _expand_heads(gates, L_MI, head)
        f_e = _expand_heads(gates, L_MF, head)
        k = mk_ref[rows, :] * (DK ** -0.5)
        q = mq_ref[rows, :]
        m_new = jnp.maximum(f_e + m, i_e)
        fp = jnp.exp(f_e + m - m_new)
        ip = jnp.exp(i_e - m_new)
        n = fp * n + ip * k
        nq = n * q
        den = jnp.zeros_like(nq)
        for h in range(HEADS):
            den = jnp.where(head == h, jnp.sum(jnp.where(head == h, nq, 0.0), axis=1, keepdims=True), den)
        inv = 1.0 / jnp.maximum(jnp.abs(den), jnp.exp(-m_new))
        ma_ref[t] = fp.T
        mkk_ref[t] = (ip * k).T
        mqq_ref[t] = (q * inv).T
        m = m_new
    n_out_ref[...] = n
    m_out_ref[...] = m


def _sample_prep(proj, la, gif, state_n, state_m, layer, sample_block, nseq, dec_seq):
    rows = nseq * dec_seq
    qw_blocks = TILE // QW
    tok = jax.ShapeDtypeStruct((dec_seq, QW, nseq), f32)
    st = jax.ShapeDtypeStruct((nseq, QW), f32)
    tok_spec = pl.BlockSpec((dec_seq, QW, nseq), lambda i: (0, 0, 0))
    st_spec = pl.BlockSpec((nseq, QW), lambda i: (0, 0))
    col = lambda c: pl.BlockSpec((rows, QW), lambda i: (sample_block, c))
    return pl.pallas_call(
        functools.partial(_sample_prep_kernel, nseq=nseq, dec_seq=dec_seq),
        grid=(1,),
        in_specs=[
            col(T_GQK * qw_blocks), col(T_GQK * qw_blocks + 1), col(0),
            col(T_MQK * qw_blocks), col(T_MQK * qw_blocks + 1),
            pl.BlockSpec((rows, LANES), lambda i: (sample_block, 0)),
            pl.BlockSpec((None, nseq, QW), lambda i: (layer, 0, 0)),
            pl.BlockSpec((None, nseq, HEADS), lambda i: (layer, 0, 0)),
        ],
        out_specs=[tok_spec] * 6 + [st_spec] * 2,
        out_shape=[tok] * 6 + [st] * 2,
        compiler_params=_cparams("arbitrary"),
        name="sample_prep",
    )(proj, proj, la, proj, proj, gif, state_n, state_m)


SEQ_PER_STEP = 8


def _sample_state_kernel(a_ref, k_ref, q_ref, v_ref, s_ref, o_ref, s_out_ref, *, dec_seq, nseq):
    blk = pl.program_id(0)
    seq0 = blk * SEQ_PER_STEP
    shift = lax.rem(nseq - seq0, nseq)
    a_t = [pltpu.roll(a_ref[t], shift, axis=1) for t in range(dec_seq)]
    k_t = [pltpu.roll(k_ref[t], shift, axis=1) for t in range(dec_seq)]
    q_t = [pltpu.roll(q_ref[t], shift, axis=1) for t in range(dec_seq)]
    toks = [pl.ds(pl.multiple_of(t * nseq + seq0, SEQ_PER_STEP), SEQ_PER_STEP) for t in range(dec_seq)]
    v_t = [v_ref[toks[t], :] for t in range(dec_seq)]
    row = lax.broadcasted_iota(jnp.int32, (SEQ_PER_STEP, DV), 0)
    outs = [[jnp.zeros((SEQ_PER_STEP, DV), f32) for _ in range(HEADS)] for _ in range(dec_seq)]
    for bb in range(SEQ_PER_STEP):
        for h in range(HEADS):
            s = s_ref[bb, h]
            rows = slice(h * DK, (h + 1) * DK)
            for t in range(dec_seq):
                a = a_t[t][rows, bb:bb + 1]
                k = k_t[t][rows, bb:bb + 1]
                q = q_t[t][rows, bb:bb + 1]
                v = v_t[t][bb:bb + 1, h * DV:(h + 1) * DV]
                s = a * s + k * v
                outs[t][h] = jnp.where(row == bb, jnp.sum(q * s, axis=0, keepdims=True), outs[t][h])
            s_out_ref[bb, h] = s
    for t in range(dec_seq):
        for h in range(HEADS):
            o_ref[toks[t], h * DV:(h + 1) * DV] = outs[t][h]


def _sample_state(a_t, k_t, q_t, proj, v_tile, sample_block, state, layer, dec_seq):
    nseq = a_t.shape[2]
    rows = dec_seq * nseq
    full = pl.BlockSpec((dec_seq, QW, nseq), lambda i: (0, 0, 0))
    return pl.pallas_call(
        functools.partial(_sample_state_kernel, dec_seq=dec_seq, nseq=nseq),
        grid=(nseq // SEQ_PER_STEP,),
        in_specs=[
            full, full, full,
            pl.BlockSpec((rows, VW), lambda i: (sample_block, v_tile)),
            pl.BlockSpec((None, SEQ_PER_STEP, HEADS, DK, DV), lambda i: (layer, i, 0, 0, 0)),
        ],
        out_specs=[
            pl.BlockSpec((rows, VW), lambda i: (0, 0)),
            pl.BlockSpec((SEQ_PER_STEP, HEADS, DK, DV), lambda i: (i, 0, 0, 0)),
        ],
        out_shape=[
            jax.ShapeDtypeStruct((rows, VW), f32),
            jax.ShapeDtypeStruct(state.shape[1:], f32),
        ],
        compiler_params=_cparams("arbitrary"),
        name="sample_state",
    )(a_t, k_t, q_t, proj, state)


def _head_rms(x, g):
    parts = []
    for h in range(HEADS):
        parts.append(_rms(x[:, h * DV:(h + 1) * DV], g))
    return jnp.concatenate(parts, axis=1)


def _merge_kernel(hp_ref, hs_ref, oap_ref, oas_ref, omp_ref, oms_ref, odp_ref, ods_ref, gr_ref, mo_ref,
                  g0_ref, g1_ref, g2_ref, gg_ref, gm_ref, gd_ref, wa_ref, wm_ref, wd_ref, wo_ref, out_ref,
                  *, lam_init, n_prompt_blocks):
    pick = functools.partial(_pick_rows, n_prompt_blocks)
    gr = gr_ref[...]
    a = _head_rms(pick(oap_ref, oas_ref), gg_ref[...]) * (gr * _sigmoid(gr))
    m = _sigmoid(mo_ref[...]) * _head_rms(pick(omp_ref, oms_ref), gm_ref[...])
    d = _head_rms(pick(odp_ref, ods_ref), gd_ref[...]) * (1.0 - lam_init)
    merged = (g0_ref[...] * _dot(a, wa_ref[...]) + g1_ref[...] * _dot(m, wm_ref[...])
              + g2_ref[...] * _dot(d, wd_ref[...]))
    out_ref[...] = pick(hp_ref, hs_ref) + _dot(merged, wo_ref[...])


def _merge(h_p, h_s, sample_block, o_a, o_m, o_d, proj, gates, g_gla, g_ml, g_diff, w_a, w_m, w_d, w_out, lam_init,
           n_prompt_blocks):
    T = (n_prompt_blocks + 1) * TM
    row = lambda i: (i, 0)
    const = lambda i: (0, 0)
    return pl.pallas_call(
        functools.partial(_merge_kernel, lam_init=lam_init, n_prompt_blocks=n_prompt_blocks),
        grid=(T // TM,),
        in_specs=[
            *_row_specs(D_MODEL, n_prompt_blocks, sample_block),
            *_row_specs(VW, n_prompt_blocks, 0),
            *_row_specs(VW, n_prompt_blocks, 0),
            *_row_specs(VW, n_prompt_blocks, 0),
            pl.BlockSpec((TM, VW), lambda i: (i, T_GR)),
            pl.BlockSpec((TM, VW), lambda i: (i, T_MO)),
            pl.BlockSpec((TM, D_MODEL), lambda i: (i, 0)),
            pl.BlockSpec((TM, D_MODEL), lambda i: (i, 1)),
            pl.BlockSpec((TM, D_MODEL), lambda i: (i, 2)),
            pl.BlockSpec((1, DV), const),
            pl.BlockSpec((1, DV), const),
            pl.BlockSpec((1, DV), const),
            pl.BlockSpec((VW, D_MODEL), const),
            pl.BlockSpec((VW, D_MODEL), const),
            pl.BlockSpec((VW, D_MODEL), const),
            pl.BlockSpec((D_MODEL, D_MODEL), const),
        ],
        out_specs=pl.BlockSpec((TM, D_MODEL), row),
        out_shape=jax.ShapeDtypeStruct((T, D_MODEL), f32),
        compiler_params=_cparams("parallel"),
        name="merge",
    )(h_p, h_s, *o_a, *o_m, *o_d, proj, proj, gates, gates, gates, g_gla, g_ml, g_diff, w_a, w_m, w_d, w_out)


FF_SLABS = ((0, 768), (768, 768), (1536, 768), (2304, 512))


def _ffn_kernel(h_ref, gf_ref, wg_ref, wu_ref, wd_ref, pp_ref, ps_ref, gp_ref, wpg_ref, wpp_ref, gfin_ref,
                *out_refs, final, n_prompt_blocks):
    h = h_ref[...]
    xf = _rms(h, gf_ref[...]).astype(bf16)
    h2 = h
    for lo, width in FF_SLABS:
        gate = jnp.dot(xf, wg_ref[:, lo:lo + width], preferred_element_type=f32)
        up = jnp.dot(xf, wu_ref[:, lo:lo + width], preferred_element_type=f32)
        h2 = h2 + _dot(gate * _sigmoid(gate) * up, wd_ref[lo:lo + width, :])
    ple_gate = _sigmoid(_dot(_rms(h2, gp_ref[...]), wpg_ref[...]))
    h3 = h2 + ple_gate * _dot(_pick_rows(n_prompt_blocks, pp_ref, ps_ref), wpp_ref[...])
    if not final:
        out_refs[0][...] = h3
        return
    y = _rms(h3, gfin_ref[...])
    yp_ref, ys_ref = out_refs
    i = pl.program_id(0)

    @pl.when(i < n_prompt_blocks)
    def _():
        yp_ref[...] = y

    @pl.when(i >= n_prompt_blocks)
    def _():
        ys_ref[...] = y


def _ffn(h, g_ffn, w_g, w_u, w_d, p_p, p_s, p_block0, g_ple, w_pg, w_pp, g_final, final, n_prompt_blocks):
    T = h.shape[0]
    row = lambda i: (i, 0)
    const = lambda i: (0, 0)
    resident = lambda shape: pl.BlockSpec(shape, const, pipeline_mode=pl.Buffered(1))
    assert FF_SLABS[-1][0] + FF_SLABS[-1][1] == D_FF
    if final:
        out_specs = [pl.BlockSpec((TM, D_MODEL), lambda i: (jnp.minimum(i, n_prompt_blocks - 1), 0)),
                     pl.BlockSpec((TM, D_MODEL), const)]
        out_shape = [jax.ShapeDtypeStruct((n_prompt_blocks * TM, D_MODEL), f32),
                     jax.ShapeDtypeStruct((TM, D_MODEL), f32)]
    else:
        out_specs = pl.BlockSpec((TM, D_MODEL), row)
        out_shape = jax.ShapeDtypeStruct((T, D_MODEL), f32)
    return pl.pallas_call(
        functools.partial(_ffn_kernel, final=final, n_prompt_blocks=n_prompt_blocks),
        grid=(T // TM,),
        in_specs=[
            pl.BlockSpec((TM, D_MODEL), row),
            pl.BlockSpec((1, D_MODEL), const),
            resident((D_MODEL, D_FF)),
            resident((D_MODEL, D_FF)),
            resident((D_FF, D_MODEL)),
            *_row_specs(PLE_DIM, n_prompt_blocks, 0, prompt_block0=p_block0),
            pl.BlockSpec((1, D_MODEL), const),
            resident((D_MODEL, D_MODEL)),
            resident((PLE_DIM, D_MODEL)),
            pl.BlockSpec((1, D_MODEL), const),
        ],
        out_specs=out_specs,
        out_shape=out_shape,
        compiler_params=_cparams("arbitrary"),
        name="ffn",
    )(h, g_ffn, w_g, w_u, w_d, p_p, p_s, g_ple, w_pg, w_pp, g_final)


def _rope_tables(positions):
    half = ROT_DIM // 2
    inv_freq = ROPE_THETA ** (-jnp.arange(half, dtype=f32) * 2.0 / ROT_DIM)
    ang = positions.astype(f32)[:, None] * inv_freq[None, :]
    cos, sin = jnp.cos(ang), jnp.sin(ang)
    n = positions.shape[0]
    one = jnp.ones((n, DK - ROT_DIM), f32)
    zero = jnp.zeros((n, DK - ROT_DIM), f32)
    zh = jnp.zeros((n, half), f32)
    cos_h = jnp.concatenate([cos, cos, one], axis=1)
    up_h = jnp.concatenate([-sin, zh, zero], axis=1)
    dn_h = jnp.concatenate([zh, sin, zero], axis=1)
    rep = lambda a: jnp.concatenate([a] * (LANES // DK), axis=1)
    return rep(cos_h), rep(up_h), rep(dn_h)


def _prep_in_weights(w_in_l, w_gate_l):
    o = np.cumsum((0, QW, QW, VW, VW, GLA_RANK, QW, QW, VW, VW, 2 * HEADS, 2 * QW, 2 * QW, VW))
    seg = lambda i: w_in_l[:, int(o[i]):int(o[i + 1])]
    gq, gk, gv, gr, glr, mq, mk, mv, mo, mif, dq, dk, dv = (seg(i) for i in range(13))
    main = jnp.concatenate([w_gate_l, gq, gk, gv, gr, mq, mk, mv, mo, dq, dk, dv], axis=1).astype(bf16)
    pad = jnp.zeros((D_MODEL, LANES - GLA_RANK - 2 * HEADS), f32)
    small = jnp.concatenate([glr, mif, pad], axis=1).astype(bf16)
    return main, small


def kernel(x_prompt, x_sample, cache_k, cache_v, state_gla, state_mlstm_c, state_mlstm_n, state_mlstm_m, page_table, p_prompt, p_sample, g_mix, w_in, w_gla_gk, b_gla_gk, g_gla_norm, b_mlstm_if, g_mlstm_norm, diff_lambda, g_diff_norm, w_branch, w_gate, w_out, g_ffn, w_ffn_gate, w_ffn_up, w_ffn_down, g_ple, w_ple_gate, w_ple_proj, g_final):
    bp, lp, _ = x_prompt.shape
    bs, ls, _ = x_sample.shape
    depth = g_mix.shape[0]
    n_pages = page_table.shape[1]
    page = cache_k.shape[2]
    past = n_pages * page
    tp = bp * lp
    ts = bs * ls
    assert lp % TM == 0 and ts == TM and bs == LANES and page == LANES and TM_IN % bs == 0
    npb = tp // TM

    def to_step_major(a):
        return jnp.swapaxes(a, 0, 1).reshape((ts,) + a.shape[2:])

    def from_step_major(a):
        return jnp.swapaxes(a.reshape((ls, bs) + a.shape[1:]), 0, 1)

    h_p, h_s, h_s_row0 = x_prompt.reshape(tp, D_MODEL), to_step_major(x_sample), 0

    pos_rows = jnp.concatenate([jnp.arange(lp, dtype=jnp.int32),
                                past + jnp.repeat(jnp.arange(ls, dtype=jnp.int32), bs)])
    cos_t, sup_t, sdn_t = _rope_tables(pos_rows)

    ck = jnp.transpose(cache_k, (0, 1, 3, 4, 2)).reshape(cache_k.shape[0], cache_k.shape[1], 2 * QW, page)
    cv = cache_v.reshape(cache_v.shape[0], cache_v.shape[1], page * HEADS, DV)
    state_n = state_mlstm_n.reshape(depth, bs, QW)
    p_rows = p_prompt.reshape(depth * tp, PLE_DIM)
    outs_p, outs_s = [], []
    for l in range(depth):
        lam_init = 0.8 - 0.6 * math.exp(-0.3 * l)
        w_main, w_small = _prep_in_weights(w_in[l], w_gate[l])
        w_gk = jnp.concatenate([w_gla_gk[l], jnp.zeros((LANES - GLA_RANK, QW), f32)], axis=0)
        b_if = jnp.zeros((1, LANES), f32).at[0, L_MI:L_MI + 2 * HEADS].set(b_mlstm_if[l])
        gates, proj, k_t, k_ts, la, gif = _inproj(h_p, h_s, h_s_row0 // TM_IN, g_mix[l][None], w_main, w_small, w_gk,
                                                  b_gla_gk[l][None], b_if, cos_t, sup_t, sdn_t, bp, lp, bs, ls)

        oa_p, gla_p = _gla_prompt(proj, la, bp, lp)
        om_p, c_p, n_p, m_p = _mlstm_prompt(proj, gif, bp, lp)
        oa_p, om_p = oa_p.reshape(tp, VW), om_p.reshape(tp, VW)
        od_p = _attn_prompt(proj, diff_lambda[l], bp, lp, lam_init)

        ga, gk_, gq_, ma, mk_, mq_, n_s, m_s = _sample_prep(proj, la, gif, state_n, state_mlstm_m, l, npb, bs, ls)
        oa_s, gla_s = _sample_state(ga, gk_, gq_, proj, T_GV, npb, state_gla, l, ls)
        om_s, c_s = _sample_state(ma, mk_, mq_, proj, T_MV, npb, state_mlstm_c, l, ls)
        od_s = _attn_sample(page_table, proj, k_ts, npb, diff_lambda[l], ck, cv, l, lam_init)
        v_s = from_step_major(proj[tp:, T_DV * TILE:(T_DV + 1) * TILE])

        wb = w_branch[l].astype(bf16)
        h = _merge(h_p, h_s, h_s_row0 // TM, (oa_p, oa_s), (om_p, om_s), (od_p, od_s), proj,
                   gates, g_gla_norm[l][None], g_mlstm_norm[l][None], g_diff_norm[l][None],
                   wb[:VW], wb[VW:2 * VW], wb[2 * VW:], w_out[l].astype(bf16), lam_init, npb)
        h = _ffn(h, g_ffn[l][None], w_ffn_gate[l].astype(bf16), w_ffn_up[l].astype(bf16),
                 w_ffn_down[l].astype(bf16), p_rows, to_step_major(p_sample[l]), l * npb,
                 g_ple[l][None], w_ple_gate[l].astype(bf16), w_ple_proj[l].astype(bf16), g_final[None],
                 l == depth - 1, npb)
        h_p, h_s, h_s_row0 = h, h, tp

        k_p = k_t.reshape(bp, 2 * HEADS, DK, lp).transpose(0, 3, 1, 2)
        k_new = k_ts.reshape(ls, 2 * HEADS, DK, bs).transpose(3, 0, 1, 2)
        outs_p.append((k_p, proj[:tp, T_DV * TILE:(T_DV + 1) * TILE].reshape(bp, lp, HEADS, DV),
                       gla_p.reshape(bp, HEADS, DK, DV), c_p, n_p, m_p[:, :, 0]))
        outs_s.append((k_new, v_s.reshape(bs, ls, HEADS, DV),
                       gla_s, c_s, n_s.reshape(bs, HEADS, DK), m_s[:, ::DK]))

    y_prompt = h[0].reshape(bp, lp, D_MODEL)
    y_sample = from_step_major(h[1])
    stack = lambda items: [jnp.stack(t) for t in zip(*items)]
    return tuple([y_prompt, y_sample] + stack(outs_p) + stack(outs_s))
```

```python
import functools
import math

import numpy as np
import jax
import jax.numpy as jnp
from jax import lax
from jax.experimental import pallas as pl
from jax.experimental.pallas import tpu as pltpu

f32 = jnp.float32
bf16 = jnp.bfloat16
HIGHEST = lax.Precision.HIGHEST

D_MODEL = 1024
PLE_DIM = 256
HEADS = 4
DK = 64
DV = 128
GLA_RANK = 16
GLA_GATE_NORM = 16.0
ROT_DIM = 16
ROPE_THETA = 500000.0
CHUNK = 64
D_FF = 2816
EPS = 1e-6
LOG2E = 1.4426950408889634
VW = HEADS * DV
QW = HEADS * DK

LANES = 128
SUBLANES = 8
VMEM_LIMIT = 48 * 1024 * 1024

TILE = 512
GATE_W = 3 * D_MODEL
T_GQK, T_GV, T_GR = 0, 1, 2
T_MQK, T_MV, T_MO = 3, 4, 5
T_DQ, T_DK, T_DV = 6, 7, 8
N_TILES = 9
P_MIX = N_TILES * TILE
P_MAIN = GATE_W + P_MIX
L_GLR = 0
L_MI = 16
L_MF = 20

TM = 512


def _cparams(*sem):
    return pltpu.CompilerParams(dimension_semantics=sem, vmem_limit_bytes=VMEM_LIMIT)


def _log_sigmoid(x):
    return jnp.minimum(x, 0.0) - jnp.log1p(jnp.exp(-jnp.abs(x)))


def _sigmoid(x):
    return 0.5 * jnp.tanh(0.5 * x) + 0.5


def _rms(x, g):
    return x * lax.rsqrt(jnp.mean(x * x, axis=-1, keepdims=True) + EPS) * g


def _dot(a, b):
    return jnp.dot(a.astype(bf16), b.astype(bf16), preferred_element_type=f32)


def _dot_nt(a, b):
    return lax.dot_general(a.astype(bf16), b.astype(bf16), (((1,), (1,)), ((), ())), preferred_element_type=f32)


def _dot_tn(a, b):
    return lax.dot_general(a.astype(bf16), b.astype(bf16), (((0,), (0,)), ((), ())), preferred_element_type=f32)


def _rope_tile(x, cos, sin_up, sin_dn):
    parts = []
    for c in range(x.shape[1] // LANES):
        xc = x[:, c * LANES:(c + 1) * LANES]
        parts.append(xc * cos + pltpu.roll(xc, LANES - ROT_DIM // 2, axis=1) * sin_up
                     + pltpu.roll(xc, ROT_DIM // 2, axis=1) * sin_dn)
    return jnp.concatenate(parts, axis=1)


def _row_specs(width, n_prompt_blocks, sample_block, tm=None, n_sample_blocks=1, prompt_block0=0):
    tm = TM if tm is None else tm
    return (pl.BlockSpec((tm, width), lambda i: (prompt_block0 + jnp.minimum(i, n_prompt_blocks - 1), 0)),
            pl.BlockSpec((tm, width),
                         lambda i: (sample_block + jnp.clip(i - n_prompt_blocks, 0, n_sample_blocks - 1), 0)))


def _pick_rows(n_prompt_blocks, p_ref, s_ref):
    return jnp.where(pl.program_id(0) >= n_prompt_blocks, s_ref[...], p_ref[...])


TM_IN = 256


def _inproj_kernel(xp_ref, xs_ref, g_ref, w_ref, ws_ref, wgk_ref, bgk_ref, bif_ref, cos_ref, sup_ref, sdn_ref,
                   gates_ref, proj_ref, kt_ref, vp_ref, kts_ref, la_ref, gif_ref, *, n_prompt_blocks, dec_batch):
    i = pl.program_id(0)
    xn = _rms(_pick_rows(n_prompt_blocks, xp_ref, xs_ref), g_ref[...]).astype(bf16)
    small = jnp.dot(xn, ws_ref[...], preferred_element_type=f32)
    z = jnp.dot(small, wgk_ref[...], preferred_element_type=f32, precision=HIGHEST) + bgk_ref[...]
    la_ref[...] = _log_sigmoid(z) * (1.0 / GLA_GATE_NORM)
    gi = small + bif_ref[...]
    lane = lax.broadcasted_iota(jnp.int32, gi.shape, 1)
    gif_ref[...] = jnp.where((lane >= L_MF) & (lane < L_MF + HEADS), _log_sigmoid(gi), gi)

    for c in range(GATE_W // TILE):
        cols = slice(c * TILE, (c + 1) * TILE)
        gates_ref[:, cols] = _sigmoid(jnp.dot(xn, w_ref[:, cols], preferred_element_type=f32)).astype(bf16)

    cos, sup, sdn = cos_ref[...], sup_ref[...], sdn_ref[...]
    for t in range(N_TILES):
        acc = jnp.dot(xn, w_ref[:, GATE_W + t * TILE:GATE_W + (t + 1) * TILE], preferred_element_type=f32)
        if t in (T_DQ, T_DK):
            acc = _rope_tile(acc, cos, sup, sdn)
        proj_ref[:, t * TILE:(t + 1) * TILE] = acc
        if t == T_DK:
            k_t = acc.T
        if t == T_DV:
            v_rows = acc

    @pl.when(i < n_prompt_blocks)
    def _():
        kt_ref[...] = k_t
        for h in range(HEADS):
            vp_ref[pl.ds(h, v_rows.shape[0], stride=HEADS), :] = v_rows[:, h * DV:(h + 1) * DV]

    @pl.when(i >= n_prompt_blocks)
    def _():
        for s in range(k_t.shape[1] // dec_batch):
            kts_ref[s] = k_t[:, s * dec_batch:(s + 1) * dec_batch]


def _inproj(h_p, h_s, sample_block, g_mix, w_main, w_small, w_gk, b_gk, b_if, cos_t, sup_t, sdn_t,
            batch, seq, dec_batch, dec_seq):
    tm = TM_IN
    seq_blocks = seq // tm
    npb = batch * seq_blocks
    steps_per_block = tm // dec_batch
    nsb = dec_seq // steps_per_block
    nt = npb + nsb
    T = nt * tm
    const = lambda i: (0, 0)
    resident = lambda shape: pl.BlockSpec(shape, const, pipeline_mode=pl.Buffered(1))

    def tab_map(i):
        return (jnp.where(i < npb, i % seq_blocks, seq_blocks + i - npb), 0)

    def kt_map(i):
        j = jnp.minimum(i, npb - 1)
        return (j // seq_blocks, 0, j % seq_blocks)

    return pl.pallas_call(
        functools.partial(_inproj_kernel, n_prompt_blocks=npb, dec_batch=dec_batch),
        grid=(nt,),
        in_specs=[
            *_row_specs(D_MODEL, npb, sample_block, tm, nsb),
            pl.BlockSpec((1, D_MODEL), const),
            resident((D_MODEL, P_MAIN)),
            resident((D_MODEL, LANES)),
            resident((LANES, QW)),
            pl.BlockSpec((1, QW), const),
            pl.BlockSpec((1, LANES), const),
            pl.BlockSpec((tm, LANES), tab_map),
            pl.BlockSpec((tm, LANES), tab_map),
            pl.BlockSpec((tm, LANES), tab_map),
        ],
        out_specs=[
            pl.BlockSpec((tm, GATE_W), lambda i: (i, 0)),
            pl.BlockSpec((tm, P_MIX), lambda i: (i, 0)),
            pl.BlockSpec((None, TILE, tm), kt_map),
            pl.BlockSpec((tm * HEADS, DV), lambda i: (jnp.minimum(i, npb - 1), 0)),
            pl.BlockSpec((steps_per_block, TILE, dec_batch), lambda i: (jnp.maximum(i - npb, 0), 0, 0)),
            pl.BlockSpec((tm, QW), lambda i: (i, 0)),
            pl.BlockSpec((tm, LANES), lambda i: (i, 0)),
        ],
        out_shape=[
            jax.ShapeDtypeStruct((T, GATE_W), bf16),
            jax.ShapeDtypeStruct((T, P_MIX), f32),
            jax.ShapeDtypeStruct((batch, TILE, seq), f32),
            jax.ShapeDtypeStruct((batch * seq * HEADS, DV), f32),
            jax.ShapeDtypeStruct((dec_seq, TILE, dec_batch), f32),
            jax.ShapeDtypeStruct((T, QW), f32),
            jax.ShapeDtypeStruct((T, LANES), f32),
        ],
        compiler_params=_cparams("arbitrary"),
        name="inproj",
    )(h_p, h_s, g_mix, w_main, w_small, w_gk, b_gk, b_if, cos_t, sup_t, sdn_t)


def _tril(n):
    r = lax.broadcasted_iota(jnp.int32, (n, n), 0)
    c = lax.broadcasted_iota(jnp.int32, (n, n), 1)
    return r >= c


def _gla_prompt_kernel(*refs, n_chunks, batch):
    q_refs, k_refs, v_refs, la_refs = (refs[i * batch:(i + 1) * batch] for i in range(4))
    o_ref, s_out_ref, s_scr = refs[4 * batch:]
    blk = pl.program_id(0)

    @pl.when(blk == 0)
    def _():
        s_scr[...] = jnp.zeros_like(s_scr)

    ltri = _tril(CHUNK).astype(f32)
    r2 = lax.broadcasted_iota(jnp.int32, (2 * CHUNK, CHUNK), 0)
    c2 = lax.broadcasted_iota(jnp.int32, (2 * CHUNK, CHUNK), 1)
    tril2 = jnp.where(r2 >= CHUNK, r2 - CHUNK, r2) >= c2
    lane = lax.broadcasted_iota(jnp.int32, (CHUNK, LANES), 1)
    lo = (lane < DK).astype(f32)
    hi = 1.0 - lo

    def chunk(c, carry):
        rows = pl.ds(pl.multiple_of(c * CHUNK, CHUNK), CHUNK)
        for s in range(batch):
            g = la_refs[s][rows, :]
            b = jnp.dot(ltri, g, preferred_element_type=f32, precision=HIGHEST)
            b_last = b[CHUNK - 1:CHUNK, :]
            q = q_refs[s][rows, :]
            k = k_refs[s][rows, :]
            qg = q * jnp.exp(b) * (DK ** -0.5)
            kg = k * jnp.exp(-b)
            kd = k * jnp.exp(b_last - b)
            dec = jnp.exp(b_last)
            for p in range(HEADS // 2):
                ls = slice(p * LANES, (p + 1) * LANES)
                qg_p, kg_p, kd_p = qg[:, ls], kg[:, ls], kd[:, ls]
                qs = jnp.concatenate([qg_p * lo, qg_p * hi], axis=0)
                a = jnp.where(tril2, _dot_nt(qs, kg_p), 0.0)
                s_p = s_scr[s, p]
                inter = _dot(qs, s_p)
                v0 = v_refs[s][rows, pl.ds((2 * p) * DV, DV)]
                v1 = v_refs[s][rows, pl.ds((2 * p + 1) * DV, DV)]
                o_ref[s, rows, pl.ds((2 * p) * DV, DV)] = inter[:CHUNK] + _dot(a[:CHUNK], v0)
                o_ref[s, rows, pl.ds((2 * p + 1) * DV, DV)] = inter[CHUNK:] + _dot(a[CHUNK:], v1)
                dcol = jnp.broadcast_to(dec[:, ls], (LANES, LANES)).T
                s_scr[s, p] = dcol * s_p + _dot_tn(kd_p * lo, v0) + _dot_tn(kd_p * hi, v1)
        return carry

    lax.fori_loop(0, n_chunks, chunk, 0)

    @pl.when(blk == pl.num_programs(0) - 1)
    def _():
        s_out_ref[...] = s_scr[...]


def _stream_specs(width, batch, nb, col):
    return [pl.BlockSpec((TM, width), functools.partial(lambda s, i: (s * nb + i, col), s)) for s in range(batch)]


def _gla_prompt(proj, la, batch, seq):
    nb = seq // TM
    qw_blocks = TILE // QW
    return pl.pallas_call(
        functools.partial(_gla_prompt_kernel, n_chunks=TM // CHUNK, batch=batch),
        grid=(nb,),
        in_specs=[
            *_stream_specs(QW, batch, nb, T_GQK * qw_blocks),
            *_stream_specs(QW, batch, nb, T_GQK * qw_blocks + 1),
            *_stream_specs(VW, batch, nb, T_GV),
            *_stream_specs(QW, batch, nb, 0),
        ],
        out_specs=[
            pl.BlockSpec((batch, TM, VW), lambda i: (0, i, 0)),
            pl.BlockSpec((batch, HEADS // 2, LANES, LANES), lambda i: (0, 0, 0, 0)),
        ],
        out_shape=[
            jax.ShapeDtypeStruct((batch, seq, VW), f32),
            jax.ShapeDtypeStruct((batch, HEADS // 2, LANES, LANES), f32),
        ],
        scratch_shapes=[pltpu.VMEM((batch, HEADS // 2, LANES, LANES), f32)],
        compiler_params=_cparams("arbitrary"),
        name="gla_prompt",
    )(*([proj] * (3 * batch)), *([la] * batch))


def _mlstm_prompt_kernel(*refs, n_chunks, batch):
    q_refs, k_refs, v_refs, gif_refs = (refs[i * batch:(i + 1) * batch] for i in range(4))
    h_ref, c_out_ref, n_out_ref, m_out_ref, ct_scr, n_scr, m_scr = refs[4 * batch:]
    blk = pl.program_id(0)

    @pl.when(blk == 0)
    def _():
        ct_scr[...] = jnp.zeros_like(ct_scr)
        n_scr[...] = jnp.zeros_like(n_scr)
        m_scr[...] = jnp.zeros_like(m_scr)

    ltri = _tril(CHUNK).astype(f32)
    r_i = lax.broadcasted_iota(jnp.int32, (CHUNK, CHUNK), 0)
    c_i = lax.broadcasted_iota(jnp.int32, (CHUNK, CHUNK), 1)
    upper = r_i <= c_i

    def chunk(c, carry):
        rows = pl.ds(pl.multiple_of(c * CHUNK, CHUNK), CHUNK)
        for sq in range(batch):
            g = gif_refs[sq][rows, :]
            fc = jnp.dot(ltri, g, preferred_element_type=f32, precision=HIGHEST)
            g_t = g.T
            fc_t = fc.T
            q_all = q_refs[sq][rows, :]
            k_all = k_refs[sq][rows, :] * (DK ** -0.5)
            n_all = n_scr[sq]
            for h in range(HEADS):
                i_col = g[:, L_MI + h:L_MI + h + 1]
                f_col = fc[:, L_MF + h:L_MF + h + 1]
                i_row = g_t[L_MI + h:L_MI + h + 1, :]
                f_row = fc_t[L_MF + h:L_MF + h + 1, :]
                f_last = fc[CHUNK - 1:CHUNK, L_MF + h:L_MF + h + 1]
                m_prev = m_scr[sq, h:h + 1, 0:1]
                c_t = ct_scr[sq, h]
                q = q_all[:, h * DK:(h + 1) * DK]
                k = k_all[:, h * DK:(h + 1) * DK]
                v_t = v_refs[sq][rows, pl.ds(h * DV, DV)].T

                log_d = jnp.where(upper, f_row + (i_col - f_col), -jnp.inf)
                m_inter = m_prev + f_row
                m_row = jnp.maximum(m_inter, jnp.max(log_d, axis=0, keepdims=True))
                w_inter = jnp.exp(m_inter - m_row)
                sc = _dot_nt(k, q) * jnp.exp(log_d - m_row)
                num = w_inter * _dot_nt(c_t, q) + _dot(v_t, sc)
                qn = _dot_nt(n_all, q)[h:h + 1, :]
                den = w_inter * qn + jnp.sum(sc, axis=0, keepdims=True)
                h_t = num / jnp.maximum(jnp.abs(den), jnp.exp(-m_row))
                h_ref[sq, rows, pl.ds(h * DV, DV)] = h_t.T

                log_w = f_last - f_col + i_col
                m_new = jnp.maximum(m_prev + f_last, jnp.max(log_w, axis=0, keepdims=True))
                wk = jnp.exp(log_w - m_new) * k
                decay = jnp.exp(m_prev + f_last - m_new)
                ct_scr[sq, h] = decay * c_t + _dot(v_t, wk)
                n_scr[sq, h:h + 1, :] = decay * n_all[h:h + 1, :] + jnp.sum(wk, axis=0, keepdims=True)
                m_scr[sq, h:h + 1, :] = jnp.broadcast_to(m_new, (1, LANES))
        return carry

    lax.fori_loop(0, n_chunks, chunk, 0)

    @pl.when(blk == pl.num_programs(0) - 1)
    def _():
        for sq in range(batch):
            for h in range(HEADS):
                c_out_ref[sq, h] = ct_scr[sq, h].T
        n_out_ref[...] = n_scr[:, 0:HEADS, :]
        m_out_ref[...] = m_scr[:, 0:HEADS, :]


def _mlstm_prompt(proj, gif, batch, seq):
    nb = seq // TM
    qw_blocks = TILE // QW
    return pl.pallas_call(
        functools.partial(_mlstm_prompt_kernel, n_chunks=TM // CHUNK, batch=batch),
        grid=(nb,),
        in_specs=[
            *_stream_specs(QW, batch, nb, T_MQK * qw_blocks),
            *_stream_specs(QW, batch, nb, T_MQK * qw_blocks + 1),
            *_stream_specs(VW, batch, nb, T_MV),
            *_stream_specs(LANES, batch, nb, 0),
        ],
        out_specs=[
            pl.BlockSpec((batch, TM, VW), lambda i: (0, i, 0)),
            pl.BlockSpec((batch, HEADS, DK, DV), lambda i: (0, 0, 0, 0)),
            pl.BlockSpec((batch, HEADS, DK), lambda i: (0, 0, 0)),
            pl.BlockSpec((batch, HEADS, LANES), lambda i: (0, 0, 0)),
        ],
        out_shape=[
            jax.ShapeDtypeStruct((batch, seq, VW), f32),
            jax.ShapeDtypeStruct((batch, HEADS, DK, DV), f32),
            jax.ShapeDtypeStruct((batch, HEADS, DK), f32),
            jax.ShapeDtypeStruct((batch, HEADS, LANES), f32),
        ],
        scratch_shapes=[
            pltpu.VMEM((batch, HEADS, DV, DK), f32),
            pltpu.VMEM((batch, SUBLANES, DK), f32),
            pltpu.VMEM((batch, SUBLANES, LANES), f32),
        ],
        compiler_params=_cparams("arbitrary"),
        name="mlstm_prompt",
    )(*([proj] * (3 * batch)), *([gif] * batch))


def _diff_lambda_value(lam_ref, lam_init):
    lam = lam_ref[...]
    s1 = jnp.sum(lam[0:1] * lam[1:2], axis=1, keepdims=True)
    s2 = jnp.sum(lam[2:3] * lam[3:4], axis=1, keepdims=True)
    return jnp.exp(s1) - jnp.exp(s2) + lam_init


def _attn_prompt_step(p, qi_ref, ki_ref, q_ref, k_ref, v_ref, lam_ref, o_ref, q2_scr, m_scr, l_scr, acc_scr,
                      *, tq, lam_init):
    qi = qi_ref[p]
    ki = ki_ref[p]
    pair = lambda j: slice(j * LANES, (j + 1) * LANES)

    @pl.when(ki == 0)
    def _():
        for j in range(HEADS):
            q = q_ref[:, pair(j)] * (DK ** -0.5 * LOG2E)
            lane = lax.broadcasted_iota(jnp.int32, q.shape, 1)
            q2_scr[j, 0:tq, :] = jnp.where(lane < DK, q, 0.0).astype(bf16)
            q2_scr[j, tq:2 * tq, :] = jnp.where(lane >= DK, q, 0.0).astype(bf16)
        m_scr[...] = jnp.full_like(m_scr, -jnp.inf)
        l_scr[...] = jnp.zeros_like(l_scr)
        acc_scr[...] = jnp.zeros_like(acc_scr)

    def step(masked):
        sts = [_dot_nt(k_ref[:, pair(j)], q2_scr[j]) for j in range(HEADS)]
        pts, alphas = [], []
        for j in range(HEADS):
            st = sts[j]
            if masked:
                r = lax.broadcasted_iota(jnp.int32, st.shape, 0)
                c = lax.broadcasted_iota(jnp.int32, st.shape, 1)
                c = jnp.where(c >= tq, c - tq, c)
                st = jnp.where(r <= c, st, -jnp.inf)
            m_prev = m_scr[j]
            m_new = jnp.maximum(m_prev, jnp.max(st, axis=0, keepdims=True))
            alpha = jnp.exp2(m_prev - m_new)
            pt = jnp.exp2(st - m_new)
            l_scr[j] = alpha * l_scr[j] + jnp.sum(pt, axis=0, keepdims=True)
            m_scr[j] = m_new
            pts.append(pt.astype(bf16))
            alphas.append(alpha)
        for j in range(HEADS):
            acc_scr[j] = alphas[j] * acc_scr[j] + _dot_tn(v_ref[:, pair(j)], pts[j])

    @pl.when(ki < qi)
    def _():
        step(False)

    @pl.when(ki == qi)
    def _():
        step(True)
        lam = _diff_lambda_value(lam_ref, lam_init)
        for j in range(HEADS):
            ot = acc_scr[j] / l_scr[j]
            o_ref[:, pair(j)] = (ot[:, 0:tq] - lam * ot[:, tq:2 * tq]).T


def _attn_sample_step(b, pt_ref, q_ref, kts_ref, v_ref, lam_ref, ck_hbm, cv_hbm, o_ref, kbuf, vbuf, sem,
                      *, layer, n_pages, page, dec_seq, nseq, lam_init):
    nb = nseq
    past = n_pages * page
    slot = b % 2
    n_qh = 2 * HEADS

    def copies(seq_idx, s):
        out = []
        for pg in range(n_pages):
            src = pt_ref[seq_idx, pg]
            out.append(pltpu.make_async_copy(ck_hbm.at[layer, src], kbuf.at[s, :, pl.ds(pg * page, page)],
                                             sem.at[0, s]))
            out.append(pltpu.make_async_copy(cv_hbm.at[layer, src],
                                             vbuf.at[s, pl.ds(pg * page * HEADS, page * HEADS), :], sem.at[1, s]))
        return out

    @pl.when(b == 0)
    def _():
        kbuf[:, :, pl.ds(past, LANES)] = jnp.zeros((2, kbuf.shape[1], LANES), f32)
        vbuf[:, pl.ds(past * HEADS, LANES * HEADS), :] = jnp.zeros((2, LANES * HEADS, DV), f32)
        o_ref[...] = jnp.zeros_like(o_ref)
        for cp in copies(0, 0):
            cp.start()

    @pl.when(b + 1 < nb)
    def _():
        for cp in copies(b + 1, 1 - slot):
            cp.start()

    for cp in copies(b, slot):
        cp.wait()

    shift = lax.rem(nseq - b, nseq)
    b8 = (b // SUBLANES) * SUBLANES
    groups = [pl.ds(pl.multiple_of(t * nseq + b8, SUBLANES), SUBLANES) for t in range(dec_seq)]
    mine = lax.broadcasted_iota(jnp.int32, (SUBLANES, 2 * QW), 0) == b - b8
    pick = lambda ref, t: jnp.sum(jnp.where(mine, ref[groups[t], :], 0.0), axis=0, keepdims=True)
    head_of_lane = lax.broadcasted_iota(jnp.int32, (n_qh, 2 * QW), 1) // DK
    head_of_row = lax.broadcasted_iota(jnp.int32, (n_qh, 2 * QW), 0)
    q_rows = []
    for t in range(dec_seq):
        kbuf[slot, :, pl.ds(past + t, 1)] = pltpu.roll(kts_ref[t], shift, axis=1)[:, 0:1]
        v_row = pick(v_ref, t)
        for h in range(HEADS):
            vbuf[slot, pl.ds((past + t) * HEADS + h, 1), :] = v_row[:, h * DV:(h + 1) * DV]
        q = jnp.broadcast_to(pick(q_ref, t) * (DK ** -0.5), (n_qh, 2 * QW))
        q_rows.append(jnp.where(head_of_lane == head_of_row, q, 0.0))
    qbd = jnp.concatenate(q_rows, axis=0)

    s = _dot(qbd, kbuf[slot])
    r = lax.broadcasted_iota(jnp.int32, s.shape, 0)
    c = lax.broadcasted_iota(jnp.int32, s.shape, 1)
    s = jnp.where(c - past <= r // n_qh, s, -jnp.inf)
    m = jnp.max(s, axis=1, keepdims=True)
    pr = jnp.exp(s - m)
    inv_l = 1.0 / jnp.sum(pr, axis=1, keepdims=True)
    lam = _diff_lambda_value(lam_ref, lam_init)
    mine_v = mine[:, 0:DV]
    for hv in range(HEADS):
        v_h = vbuf[slot, pl.ds(hv, past + LANES, stride=HEADS), :]
        o_h = _dot(pr, v_h) * inv_l
        for t in range(dec_seq):
            r1 = t * n_qh + 2 * hv
            d = o_h[r1:r1 + 1] - lam * o_h[r1 + 1:r1 + 2]
            lanes = slice(hv * DV, (hv + 1) * DV)
            o_ref[groups[t], lanes] = jnp.where(mine_v, d, o_ref[groups[t], lanes])


def _attn_kernel(qi_ref, ki_ref, pt_ref, q_ref, k_ref, v_ref, lam_ref, qs_ref, kts_ref, vs_ref, ck_hbm, cv_hbm,
                 o_ref, os_ref, q2_scr, m_scr, l_scr, acc_scr, kbuf, vbuf, sem, *, prompt, sample, nseq, n_pairs):
    p = pl.program_id(1)

    @pl.when(p < n_pairs)
    def _():
        _attn_prompt_step(p, qi_ref, ki_ref, q_ref, k_ref, v_ref, lam_ref, o_ref,
                          q2_scr, m_scr, l_scr, acc_scr, **prompt)

    step = pl.program_id(0) * pl.num_programs(1) + p

    @pl.when(step < nseq)
    def _():
        _attn_sample_step(step, pt_ref, qs_ref, kts_ref, vs_ref, lam_ref, ck_hbm, cv_hbm, os_ref, kbuf, vbuf, sem,
                          nseq=nseq, **sample)


def _attention(proj, k_ts, page_table, diff_lambda, cache_kt, cache_v2, layer, lam_init, batch, seq, sample_block,
               tq=512):
    nq = seq // tq
    pairs = [(q, k) for q in range(nq) for k in range(q + 1)]
    qi_tab = jnp.asarray(np.array([a for a, _ in pairs], np.int32))
    ki_tab = jnp.asarray(np.array([b for _, b in pairs], np.int32))
    nseq, n_pages = page_table.shape
    n_pairs = len(pairs)
    steps = max(n_pairs, -(-nseq // batch))
    last = n_pairs - 1
    dec_seq = k_ts.shape[0]
    rows = dec_seq * nseq
    page = cache_kt.shape[3]
    cols = n_pages * page + LANES
    grid_spec = pltpu.PrefetchScalarGridSpec(
        num_scalar_prefetch=3,
        grid=(batch, steps),
        in_specs=[
            pl.BlockSpec((tq, TILE), lambda b, p, qt, kt, pt: (b * nq + qt[jnp.minimum(p, last)], T_DQ)),
            pl.BlockSpec((tq, TILE), lambda b, p, qt, kt, pt: (b * nq + kt[jnp.minimum(p, last)], T_DK)),
            pl.BlockSpec((tq, TILE), lambda b, p, qt, kt, pt: (b * nq + kt[jnp.minimum(p, last)], T_DV)),
            pl.BlockSpec((4, DK), lambda b, p, qt, kt, pt: (0, 0)),
            pl.BlockSpec((rows, TILE), lambda b, p, qt, kt, pt: (sample_block, T_DQ)),
            pl.BlockSpec((dec_seq, TILE, nseq), lambda b, p, qt, kt, pt: (0, 0, 0)),
            pl.BlockSpec((rows, TILE), lambda b, p, qt, kt, pt: (sample_block, T_DV)),
            pl.BlockSpec(memory_space=pl.ANY),
            pl.BlockSpec(memory_space=pl.ANY),
        ],
        out_specs=[
            pl.BlockSpec((tq, VW), lambda b, p, qt, kt, pt: (b * nq + qt[jnp.minimum(p, last)], 0)),
            pl.BlockSpec((rows, VW), lambda b, p, qt, kt, pt: (0, 0)),
        ],
        scratch_shapes=[
            pltpu.VMEM((HEADS, 2 * tq, LANES), bf16),
            pltpu.VMEM((HEADS, 1, 2 * tq), f32),
            pltpu.VMEM((HEADS, 1, 2 * tq), f32),
            pltpu.VMEM((HEADS, DV, 2 * tq), f32),
            pltpu.VMEM((2, 2 * QW, cols), f32),
            pltpu.VMEM((2, cols * HEADS, DV), f32),
            pltpu.SemaphoreType.DMA((2, 2)),
        ],
    )
    return pl.pallas_call(
        functools.partial(_attn_kernel, nseq=nseq, n_pairs=n_pairs, prompt=dict(tq=tq, lam_init=lam_init),
                          sample=dict(layer=layer, n_pages=n_pages, page=page, dec_seq=dec_seq, lam_init=lam_init)),
        grid_spec=grid_spec,
        out_shape=[jax.ShapeDtypeStruct((batch * seq, VW), f32), jax.ShapeDtypeStruct((rows, VW), f32)],
        compiler_params=_cparams("arbitrary", "arbitrary"),
        name="attention",
    )(qi_tab, ki_tab, page_table, proj, proj, proj, diff_lambda, proj, k_ts, proj, cache_kt, cache_v2)


def _head_of_lane(shape):
    return lax.broadcasted_iota(jnp.int32, shape, 1) // DK


def _expand_heads(cols, base, head):
    out = cols[:, base + HEADS - 1:base + HEADS]
    for h in range(HEADS - 2, -1, -1):
        out = jnp.where(head == h, cols[:, base + h:base + h + 1], out)
    return out


def _sample_prep_kernel(gq_ref, gk_ref, la_ref, mq_ref, mk_ref, gif_ref, n0_ref, m0_ref,
                        ga_ref, gkk_ref, gqq_ref, ma_ref, mkk_ref, mqq_ref, n_out_ref, m_out_ref, *, nseq, dec_seq):
    head = _head_of_lane((nseq, QW))
    m = _expand_heads(m0_ref[...], 0, head)
    n = n0_ref[...]
    for t in range(dec_seq):
        rows = pl.ds(t * nseq, nseq)
        ga_ref[t] = jnp.exp(la_ref[rows, :]).T
        gkk_ref[t] = gk_ref[rows, :].T
        gqq_ref[t] = (gq_ref[rows, :] * (DK ** -0.5)).T
        gates = gif_ref[rows, :]
        i_e = _expand_heads(gates, L_MI, head)
        f_e = _expand_heads(gates, L_MF, head)
        k = mk_ref[rows, :] * (DK ** -0.5)
        q = mq_ref[rows, :]
        m_new = jnp.maximum(f_e + m, i_e)
        fp = jnp.exp(f_e + m - m_new)
        ip = jnp.exp(i_e - m_new)
        n = fp * n + ip * k
        nq = n * q
        den = jnp.zeros_like(nq)
        for h in range(HEADS):
            den = jnp.where(head == h, jnp.sum(jnp.where(head == h, nq, 0.0), axis=1, keepdims=True), den)
        inv = 1.0 / jnp.maximum(jnp.abs(den), jnp.exp(-m_new))
        ma_ref[t] = fp.T
        mkk_ref[t] = (ip * k).T
        mqq_ref[t] = (q * inv).T
        m = m_new
    n_out_ref[...] = n
    m_out_ref[...] = m


def _sample_prep(proj, la, gif, state_n, state_m, layer, sample_block, nseq, dec_seq):
    rows = nseq * dec_seq
    qw_blocks = TILE // QW
    tok = jax.ShapeDtypeStruct((dec_seq, QW, nseq), f32)
    st = jax.ShapeDtypeStruct((nseq, QW), f32)
    tok_spec = pl.BlockSpec((dec_seq, QW, nseq), lambda i: (0, 0, 0))
    st_spec = pl.BlockSpec((nseq, QW), lambda i: (0, 0))
    col = lambda c: pl.BlockSpec((rows, QW), lambda i: (sample_block, c))
    return pl.pallas_call(
        functools.partial(_sample_prep_kernel, nseq=nseq, dec_seq=dec_seq),
        grid=(1,),
        in_specs=[
            col(T_GQK * qw_blocks), col(T_GQK * qw_blocks + 1), col(0),
            col(T_MQK * qw_blocks), col(T_MQK * qw_blocks + 1),
            pl.BlockSpec((rows, LANES), lambda i: (sample_block, 0)),
            pl.BlockSpec((None, nseq, QW), lambda i: (layer, 0, 0)),
            pl.BlockSpec((None, nseq, HEADS), lambda i: (layer, 0, 0)),
        ],
        out_specs=[tok_spec] * 6 + [st_spec] * 2,
        out_shape=[tok] * 6 + [st] * 2,
        compiler_params=_cparams("arbitrary"),
        name="sample_prep",
    )(proj, proj, la, proj, proj, gif, state_n, state_m)


SEQ_PER_STEP = 8


def _sample_state_kernel(a_ref, k_ref, q_ref, v_ref, s_ref, o_ref, s_out_ref, *, dec_seq, nseq):
    blk = pl.program_id(0)
    seq0 = blk * SEQ_PER_STEP
    shift = lax.rem(nseq - seq0, nseq)
    a_t = [pltpu.roll(a_ref[t], shift, axis=1) for t in range(dec_seq)]
    k_t = [pltpu.roll(k_ref[t], shift, axis=1) for t in range(dec_seq)]
    q_t = [pltpu.roll(q_ref[t], shift, axis=1) for t in range(dec_seq)]
    toks = [pl.ds(pl.multiple_of(t * nseq + seq0, SEQ_PER_STEP), SEQ_PER_STEP) for t in range(dec_seq)]
    v_t = [v_ref[toks[t], :] for t in range(dec_seq)]
    row = lax.broadcasted_iota(jnp.int32, (SEQ_PER_STEP, DV), 0)
    outs = [[jnp.zeros((SEQ_PER_STEP, DV), f32) for _ in range(HEADS)] for _ in range(dec_seq)]
    for bb in range(SEQ_PER_STEP):
        for h in range(HEADS):
            s = s_ref[bb, h]
            rows = slice(h * DK, (h + 1) * DK)
            for t in range(dec_seq):
                a = a_t[t][rows, bb:bb + 1]
                k = k_t[t][rows, bb:bb + 1]
                q = q_t[t][rows, bb:bb + 1]
                v = v_t[t][bb:bb + 1, h * DV:(h + 1) * DV]
                s = a * s + k * v
                outs[t][h] = jnp.where(row == bb, jnp.sum(q * s, axis=0, keepdims=True), outs[t][h])
            s_out_ref[bb, h] = s
    for t in range(dec_seq):
        for h in range(HEADS):
            o_ref[toks[t], h * DV:(h + 1) * DV] = outs[t][h]


def _sample_state(a_t, k_t, q_t, proj, v_tile, sample_block, state, layer, dec_seq):
    nseq = a_t.shape[2]
    rows = dec_seq * nseq
    full = pl.BlockSpec((dec_seq, QW, nseq), lambda i: (0, 0, 0))
    return pl.pallas_call(
        functools.partial(_sample_state_kernel, dec_seq=dec_seq, nseq=nseq),
        grid=(nseq // SEQ_PER_STEP,),
        in_specs=[
            full, full, full,
            pl.BlockSpec((rows, VW), lambda i: (sample_block, v_tile)),
            pl.BlockSpec((None, SEQ_PER_STEP, HEADS, DK, DV), lambda i: (layer, i, 0, 0, 0)),
        ],
        out_specs=[
            pl.BlockSpec((rows, VW), lambda i: (0, 0)),
            pl.BlockSpec((SEQ_PER_STEP, HEADS, DK, DV), lambda i: (i, 0, 0, 0)),
        ],
        out_shape=[
            jax.ShapeDtypeStruct((rows, VW), f32),
            jax.ShapeDtypeStruct(state.shape[1:], f32),
        ],
        compiler_params=_cparams("arbitrary"),
        name="sample_state",
    )(a_t, k_t, q_t, proj, state)


def _head_rms(x, g):
    parts = []
    for h in range(HEADS):
        parts.append(_rms(x[:, h * DV:(h + 1) * DV], g))
    return jnp.concatenate(parts, axis=1)


def _merge_kernel(hp_ref, hs_ref, oap_ref, oas_ref, omp_ref, oms_ref, odp_ref, ods_ref, gr_ref, mo_ref,
                  g0_ref, g1_ref, g2_ref, gg_ref, gm_ref, gd_ref, wa_ref, wm_ref, wd_ref, wo_ref, out_ref,
                  *, lam_init, n_prompt_blocks):
    pick = functools.partial(_pick_rows, n_prompt_blocks)
    gr = gr_ref[...]
    a = _head_rms(pick(oap_ref, oas_ref), gg_ref[...]) * (gr * _sigmoid(gr))
    m = _sigmoid(mo_ref[...]) * _head_rms(pick(omp_ref, oms_ref), gm_ref[...])
    d = _head_rms(pick(odp_ref, ods_ref), gd_ref[...]) * (1.0 - lam_init)
    merged = (g0_ref[...] * _dot(a, wa_ref[...]) + g1_ref[...] * _dot(m, wm_ref[...])
              + g2_ref[...] * _dot(d, wd_ref[...]))
    out_ref[...] = pick(hp_ref, hs_ref) + _dot(merged, wo_ref[...])


def _merge(h_p, h_s, sample_block, o_a, o_m, o_d, proj, gates, g_gla, g_ml, g_diff, w_a, w_m, w_d, w_out, lam_init,
           n_prompt_blocks):
    T = (n_prompt_blocks + 1) * TM
    row = lambda i: (i, 0)
    const = lambda i: (0, 0)
    return pl.pallas_call(
        functools.partial(_merge_kernel, lam_init=lam_init, n_prompt_blocks=n_prompt_blocks),
        grid=(T // TM,),
        in_specs=[
            *_row_specs(D_MODEL, n_prompt_blocks, sample_block),
            *_row_specs(VW, n_prompt_blocks, 0),
            *_row_specs(VW, n_prompt_blocks, 0),
            *_row_specs(VW, n_prompt_blocks, 0),
            pl.BlockSpec((TM, VW), lambda i: (i, T_GR)),
            pl.BlockSpec((TM, VW), lambda i: (i, T_MO)),
            pl.BlockSpec((TM, D_MODEL), lambda i: (i, 0)),
            pl.BlockSpec((TM, D_MODEL), lambda i: (i, 1)),
            pl.BlockSpec((TM, D_MODEL), lambda i: (i, 2)),
            pl.BlockSpec((1, DV), const),
            pl.BlockSpec((1, DV), const),
            pl.BlockSpec((1, DV), const),
            pl.BlockSpec((VW, D_MODEL), const),
            pl.BlockSpec((VW, D_MODEL), const),
            pl.BlockSpec((VW, D_MODEL), const),
            pl.BlockSpec((D_MODEL, D_MODEL), const),
        ],
        out_specs=pl.BlockSpec((TM, D_MODEL), row),
        out_shape=jax.ShapeDtypeStruct((T, D_MODEL), f32),
        compiler_params=_cparams("parallel"),
        name="merge",
    )(h_p, h_s, *o_a, *o_m, *o_d, proj, proj, gates, gates, gates, g_gla, g_ml, g_diff, w_a, w_m, w_d, w_out)


FF_SLABS = ((0, 768), (768, 768), (1536, 768), (2304, 512))


def _ffn_kernel(h_ref, gf_ref, wg_ref, wu_ref, wd_ref, pp_ref, ps_ref, gp_ref, wpg_ref, wpp_ref, gfin_ref,
                *out_refs, final, n_prompt_blocks):
    h = h_ref[...]
    xf = _rms(h, gf_ref[...]).astype(bf16)
    h2 = h
    for lo, width in FF_SLABS:
        gate = jnp.dot(xf, wg_ref[:, lo:lo + width], preferred_element_type=f32)
        up = jnp.dot(xf, wu_ref[:, lo:lo + width], preferred_element_type=f32)
        h2 = h2 + _dot(gate * _sigmoid(gate) * up, wd_ref[lo:lo + width, :])
    ple_gate = _sigmoid(_dot(_rms(h2, gp_ref[...]), wpg_ref[...]))
    h3 = h2 + ple_gate * _dot(_pick_rows(n_prompt_blocks, pp_ref, ps_ref), wpp_ref[...])
    if not final:
        out_refs[0][...] = h3
        return
    y = _rms(h3, gfin_ref[...])
    yp_ref, ys_ref = out_refs
    i = pl.program_id(0)

    @pl.when(i < n_prompt_blocks)
    def _():
        yp_ref[...] = y

    @pl.when(i >= n_prompt_blocks)
    def _():
        ys_ref[...] = y


def _ffn(h, g_ffn, w_g, w_u, w_d, p_p, p_s, p_block0, g_ple, w_pg, w_pp, g_final, final, n_prompt_blocks):
    T = h.shape[0]
    row = lambda i: (i, 0)
    const = lambda i: (0, 0)
    resident = lambda shape: pl.BlockSpec(shape, const, pipeline_mode=pl.Buffered(1))
    assert FF_SLABS[-1][0] + FF_SLABS[-1][1] == D_FF
    if final:
        out_specs = [pl.BlockSpec((TM, D_MODEL), lambda i: (jnp.minimum(i, n_prompt_blocks - 1), 0)),
                     pl.BlockSpec((TM, D_MODEL), const)]
        out_shape = [jax.ShapeDtypeStruct((n_prompt_blocks * TM, D_MODEL), f32),
                     jax.ShapeDtypeStruct((TM, D_MODEL), f32)]
    else:
        out_specs = pl.BlockSpec((TM, D_MODEL), row)
        out_shape = jax.ShapeDtypeStruct((T, D_MODEL), f32)
    return pl.pallas_call(
        functools.partial(_ffn_kernel, final=final, n_prompt_blocks=n_prompt_blocks),
        grid=(T // TM,),
        in_specs=[
            pl.BlockSpec((TM, D_MODEL), row),
            pl.BlockSpec((1, D_MODEL), const),
            resident((D_MODEL, D_FF)),
            resident((D_MODEL, D_FF)),
            resident((D_FF, D_MODEL)),
            *_row_specs(PLE_DIM, n_prompt_blocks, 0, prompt_block0=p_block0),
            pl.BlockSpec((1, D_MODEL), const),
            resident((D_MODEL, D_MODEL)),
            resident((PLE_DIM, D_MODEL)),
            pl.BlockSpec((1, D_MODEL), const),
        ],
        out_specs=out_specs,
        out_shape=out_shape,
        compiler_params=_cparams("arbitrary"),
        name="ffn",
    )(h, g_ffn, w_g, w_u, w_d, p_p, p_s, g_ple, w_pg, w_pp, g_final)


def _rope_tables(positions):
    half = ROT_DIM // 2
    inv_freq = ROPE_THETA ** (-jnp.arange(half, dtype=f32) * 2.0 / ROT_DIM)
    ang = positions.astype(f32)[:, None] * inv_freq[None, :]
    cos, sin = jnp.cos(ang), jnp.sin(ang)
    n = positions.shape[0]
    one = jnp.ones((n, DK - ROT_DIM), f32)
    zero = jnp.zeros((n, DK - ROT_DIM), f32)
    zh = jnp.zeros((n, half), f32)
    cos_h = jnp.concatenate([cos, cos, one], axis=1)
    up_h = jnp.concatenate([-sin, zh, zero], axis=1)
    dn_h = jnp.concatenate([zh, sin, zero], axis=1)
    rep = lambda a: jnp.concatenate([a] * (LANES // DK), axis=1)
    return rep(cos_h), rep(up_h), rep(dn_h)


def _prep_in_weights(w_in_l, w_gate_l):
    o = np.cumsum((0, QW, QW, VW, VW, GLA_RANK, QW, QW, VW, VW, 2 * HEADS, 2 * QW, 2 * QW, VW))
    seg = lambda i: w_in_l[:, int(o[i]):int(o[i + 1])]
    gq, gk, gv, gr, glr, mq, mk, mv, mo, mif, dq, dk, dv = (seg(i) for i in range(13))
    main = jnp.concatenate([w_gate_l, gq, gk, gv, gr, mq, mk, mv, mo, dq, dk, dv], axis=1).astype(bf16)
    pad = jnp.zeros((D_MODEL, LANES - GLA_RANK - 2 * HEADS), f32)
    small = jnp.concatenate([glr, mif, pad], axis=1).astype(bf16)
    return main, small


def kernel(x_prompt, x_sample, cache_k, cache_v, state_gla, state_mlstm_c, state_mlstm_n, state_mlstm_m, page_table, p_prompt, p_sample, g_mix, w_in, w_gla_gk, b_gla_gk, g_gla_norm, b_mlstm_if, g_mlstm_norm, diff_lambda, g_diff_norm, w_branch, w_gate, w_out, g_ffn, w_ffn_gate, w_ffn_up, w_ffn_down, g_ple, w_ple_gate, w_ple_proj, g_final):
    bp, lp, _ = x_prompt.shape
    bs, ls, _ = x_sample.shape
    depth = g_mix.shape[0]
    n_pages = page_table.shape[1]
    page = cache_k.shape[2]
    past = n_pages * page
    tp = bp * lp
    ts = bs * ls
    assert lp % TM == 0 and ts == TM and bs == LANES and page == LANES and TM_IN % bs == 0
    npb = tp // TM

    def to_step_major(a):
        return jnp.swapaxes(a, 0, 1).reshape((ts,) + a.shape[2:])

    def from_step_major(a):
        return jnp.swapaxes(a.reshape((ls, bs) + a.shape[1:]), 0, 1)

    h_p, h_s, h_s_row0 = x_prompt.reshape(tp, D_MODEL), to_step_major(x_sample), 0

    pos_rows = jnp.concatenate([jnp.arange(lp, dtype=jnp.int32),
                                past + jnp.repeat(jnp.arange(ls, dtype=jnp.int32), bs)])
    cos_t, sup_t, sdn_t = _rope_tables(pos_rows)

    ck = jnp.transpose(cache_k, (0, 1, 3, 4, 2)).reshape(cache_k.shape[0], cache_k.shape[1], 2 * QW, page)
    cv = cache_v.reshape(cache_v.shape[0], cache_v.shape[1], page * HEADS, DV)
    state_n = state_mlstm_n.reshape(depth, bs, QW)
    p_rows = p_prompt.reshape(depth * tp, PLE_DIM)
    outs_p, outs_s = [], []
    for l in range(depth):
        lam_init = 0.8 - 0.6 * math.exp(-0.3 * l)
        w_main, w_small = _prep_in_weights(w_in[l], w_gate[l])
        w_gk = jnp.concatenate([w_gla_gk[l], jnp.zeros((LANES - GLA_RANK, QW), f32)], axis=0)
        b_if = jnp.zeros((1, LANES), f32).at[0, L_MI:L_MI + 2 * HEADS].set(b_mlstm_if[l])
        gates, proj, k_t, v_p, k_ts, la, gif = _inproj(h_p, h_s, h_s_row0 // TM_IN, g_mix[l][None], w_main, w_small, w_gk,
                                                  b_gla_gk[l][None], b_if, cos_t, sup_t, sdn_t, bp, lp, bs, ls)

        oa_p, gla_p = _gla_prompt(proj, la, bp, lp)
        om_p, c_p, n_p, m_p = _mlstm_prompt(proj, gif, bp, lp)
        oa_p, om_p = oa_p.reshape(tp, VW), om_p.reshape(tp, VW)
        od_p, od_s = _attention(proj, k_ts, page_table, diff_lambda[l], ck, cv, l, lam_init, bp, lp, npb)

        ga, gk_, gq_, ma, mk_, mq_, n_s, m_s = _sample_prep(proj, la, gif, state_n, state_mlstm_m, l, npb, bs, ls)
        oa_s, gla_s = _sample_state(ga, gk_, gq_, proj, T_GV, npb, state_gla, l, ls)
        om_s, c_s = _sample_state(ma, mk_, mq_, proj, T_MV, npb, state_mlstm_c, l, ls)
        v_s = from_step_major(proj[tp:, T_DV * TILE:(T_DV + 1) * TILE])

        wb = w_branch[l].astype(bf16)
        h = _merge(h_p, h_s, h_s_row0 // TM, (oa_p, oa_s), (om_p, om_s), (od_p, od_s), proj,
                   gates, g_gla_norm[l][None], g_mlstm_norm[l][None], g_diff_norm[l][None],
                   wb[:VW], wb[VW:2 * VW], wb[2 * VW:], w_out[l].astype(bf16), lam_init, npb)
        h = _ffn(h, g_ffn[l][None], w_ffn_gate[l].astype(bf16), w_ffn_up[l].astype(bf16),
                 w_ffn_down[l].astype(bf16), p_rows, to_step_major(p_sample[l]), l * npb,
                 g_ple[l][None], w_ple_gate[l].astype(bf16), w_ple_proj[l].astype(bf16), g_final[None],
                 l == depth - 1, npb)
        h_p, h_s, h_s_row0 = h, h, tp

        k_p = k_t.reshape(bp, 2 * HEADS, DK, lp).transpose(0, 3, 1, 2)
        k_new = k_ts.reshape(ls, 2 * HEADS, DK, bs).transpose(3, 0, 1, 2)
        outs_p.append((k_p, v_p.reshape(bp, lp, HEADS, DV),
                       gla_p.reshape(bp, HEADS, DK, DV), c_p, n_p, m_p[:, :, 0]))
        outs_s.append((k_new, v_s.reshape(bs, ls, HEADS, DV),
                       gla_s, c_s, n_s.reshape(bs, HEADS, DK), m_s[:, ::DK]))

    y_prompt = h[0].reshape(bp, lp, D_MODEL)
    y_sample = from_step_major(h[1])
    stack = lambda items: [jnp.stack(t) for t in zip(*items)]
    return tuple([y_prompt, y_sample] + stack(outs_p) + stack(outs_s))
```

```python
import functools
import math

import numpy as np
import jax
import jax.numpy as jnp
from jax import lax
from jax.experimental import pallas as pl
from jax.experimental.pallas import tpu as pltpu

f32 = jnp.float32
bf16 = jnp.bfloat16
HIGHEST = lax.Precision.HIGHEST

D_MODEL = 1024
PLE_DIM = 256
HEADS = 4
DK = 64
DV = 128
GLA_RANK = 16
GLA_GATE_NORM = 16.0
ROT_DIM = 16
ROPE_THETA = 500000.0
CHUNK = 64
D_FF = 2816
EPS = 1e-6
LOG2E = 1.4426950408889634
VW = HEADS * DV
QW = HEADS * DK

LANES = 128
SUBLANES = 8
VMEM_LIMIT = 48 * 1024 * 1024

TILE = 512
GATE_W = 3 * D_MODEL
T_GQK, T_GV, T_GR = 0, 1, 2
T_MQK, T_MV, T_MO = 3, 4, 5
T_DQ, T_DK, T_DV = 6, 7, 8
N_TILES = 9
P_MIX = N_TILES * TILE
P_MAIN = GATE_W + P_MIX
L_GLR = 0
L_MI = 16
L_MF = 20

TM = 512


def _cparams(*sem):
    return pltpu.CompilerParams(dimension_semantics=sem, vmem_limit_bytes=VMEM_LIMIT)


def _log_sigmoid(x):
    return jnp.minimum(x, 0.0) - jnp.log1p(jnp.exp(-jnp.abs(x)))


def _sigmoid(x):
    return 0.5 * jnp.tanh(0.5 * x) + 0.5


def _rms(x, g):
    return x * lax.rsqrt(jnp.mean(x * x, axis=-1, keepdims=True) + EPS) * g


def _dot(a, b):
    return jnp.dot(a.astype(bf16), b.astype(bf16), preferred_element_type=f32)


def _dot_nt(a, b):
    return lax.dot_general(a.astype(bf16), b.astype(bf16), (((1,), (1,)), ((), ())), preferred_element_type=f32)


def _dot_tn(a, b):
    return lax.dot_general(a.astype(bf16), b.astype(bf16), (((0,), (0,)), ((), ())), preferred_element_type=f32)


def _rope_tile(x, cos, sin_up, sin_dn):
    parts = []
    for c in range(x.shape[1] // LANES):
        xc = x[:, c * LANES:(c + 1) * LANES]
        parts.append(xc * cos + pltpu.roll(xc, LANES - ROT_DIM // 2, axis=1) * sin_up
                     + pltpu.roll(xc, ROT_DIM // 2, axis=1) * sin_dn)
    return jnp.concatenate(parts, axis=1)


def _row_specs(width, n_prompt_blocks, sample_block, tm=None, n_sample_blocks=1, prompt_block0=0):
    tm = TM if tm is None else tm
    return (pl.BlockSpec((tm, width), lambda i: (prompt_block0 + jnp.minimum(i, n_prompt_blocks - 1), 0)),
            pl.BlockSpec((tm, width),
                         lambda i: (sample_block + jnp.clip(i - n_prompt_blocks, 0, n_sample_blocks - 1), 0)))


def _pick_rows(n_prompt_blocks, p_ref, s_ref):
    return jnp.where(pl.program_id(0) >= n_prompt_blocks, s_ref[...], p_ref[...])


TM_IN = 256


def _inproj_kernel(xp_ref, xs_ref, g_ref, w_ref, ws_ref, wgk_ref, bgk_ref, bif_ref, cos_ref, sup_ref, sdn_ref,
                   gates_ref, proj_ref, kt_ref, vp_ref, kts_ref, la_ref, gif_ref, *, n_prompt_blocks, dec_batch):
    i = pl.program_id(0)
    xn = _rms(_pick_rows(n_prompt_blocks, xp_ref, xs_ref), g_ref[...]).astype(bf16)
    small = jnp.dot(xn, ws_ref[...], preferred_element_type=f32)
    z = jnp.dot(small, wgk_ref[...], preferred_element_type=f32, precision=HIGHEST) + bgk_ref[...]
    la_ref[...] = _log_sigmoid(z) * (1.0 / GLA_GATE_NORM)
    gi = small + bif_ref[...]
    lane = lax.broadcasted_iota(jnp.int32, gi.shape, 1)
    gif_ref[...] = jnp.where((lane >= L_MF) & (lane < L_MF + HEADS), _log_sigmoid(gi), gi)

    for c in range(GATE_W // TILE):
        cols = slice(c * TILE, (c + 1) * TILE)
        gates_ref[:, cols] = _sigmoid(jnp.dot(xn, w_ref[:, cols], preferred_element_type=f32)).astype(bf16)

    cos, sup, sdn = cos_ref[...], sup_ref[...], sdn_ref[...]
    for t in range(N_TILES):
        acc = jnp.dot(xn, w_ref[:, GATE_W + t * TILE:GATE_W + (t + 1) * TILE], preferred_element_type=f32)
        if t in (T_DQ, T_DK):
            acc = _rope_tile(acc, cos, sup, sdn)
        proj_ref[:, t * TILE:(t + 1) * TILE] = acc
        if t == T_DK:
            k_t = acc.T
        if t == T_DV:
            v_rows = acc

    @pl.when(i < n_prompt_blocks)
    def _():
        kt_ref[...] = k_t
        for h in range(HEADS):
            vp_ref[pl.ds(h, v_rows.shape[0], stride=HEADS), :] = v_rows[:, h * DV:(h + 1) * DV]

    @pl.when(i >= n_prompt_blocks)
    def _():
        for s in range(k_t.shape[1] // dec_batch):
            kts_ref[s] = k_t[:, s * dec_batch:(s + 1) * dec_batch]


def _inproj(h_p, h_s, sample_block, g_mix, w_main, w_small, w_gk, b_gk, b_if, cos_t, sup_t, sdn_t,
            batch, seq, dec_batch, dec_seq):
    tm = TM_IN
    seq_blocks = seq // tm
    npb = batch * seq_blocks
    steps_per_block = tm // dec_batch
    nsb = dec_seq // steps_per_block
    nt = npb + nsb
    T = nt * tm
    const = lambda i: (0, 0)
    resident = lambda shape: pl.BlockSpec(shape, const, pipeline_mode=pl.Buffered(1))

    def tab_map(i):
        return (jnp.where(i < npb, i % seq_blocks, seq_blocks + i - npb), 0)

    def kt_map(i):
        j = jnp.minimum(i, npb - 1)
        return (j // seq_blocks, 0, j % seq_blocks)

    return pl.pallas_call(
        functools.partial(_inproj_kernel, n_prompt_blocks=npb, dec_batch=dec_batch),
        grid=(nt,),
        in_specs=[
            *_row_specs(D_MODEL, npb, sample_block, tm, nsb),
            pl.BlockSpec((1, D_MODEL), const),
            resident((D_MODEL, P_MAIN)),
            resident((D_MODEL, LANES)),
            resident((LANES, QW)),
            pl.BlockSpec((1, QW), const),
            pl.BlockSpec((1, LANES), const),
            pl.BlockSpec((tm, LANES), tab_map),
            pl.BlockSpec((tm, LANES), tab_map),
            pl.BlockSpec((tm, LANES), tab_map),
        ],
        out_specs=[
            pl.BlockSpec((tm, GATE_W), lambda i: (i, 0)),
            pl.BlockSpec((tm, P_MIX), lambda i: (i, 0)),
            pl.BlockSpec((None, TILE, tm), kt_map),
            pl.BlockSpec((tm * HEADS, DV), lambda i: (jnp.minimum(i, npb - 1), 0)),
            pl.BlockSpec((steps_per_block, TILE, dec_batch), lambda i: (jnp.maximum(i - npb, 0), 0, 0)),
            pl.BlockSpec((tm, QW), lambda i: (i, 0)),
            pl.BlockSpec((tm, LANES), lambda i: (i, 0)),
        ],
        out_shape=[
            jax.ShapeDtypeStruct((T, GATE_W), bf16),
            jax.ShapeDtypeStruct((T, P_MIX), f32),
            jax.ShapeDtypeStruct((batch, TILE, seq), f32),
            jax.ShapeDtypeStruct((batch * seq * HEADS, DV), f32),
            jax.ShapeDtypeStruct((dec_seq, TILE, dec_batch), f32),
            jax.ShapeDtypeStruct((T, QW), f32),
            jax.ShapeDtypeStruct((T, LANES), f32),
        ],
        compiler_params=_cparams("arbitrary"),
        name="inproj",
    )(h_p, h_s, g_mix, w_main, w_small, w_gk, b_gk, b_if, cos_t, sup_t, sdn_t)


def _tril(n):
    r = lax.broadcasted_iota(jnp.int32, (n, n), 0)
    c = lax.broadcasted_iota(jnp.int32, (n, n), 1)
    return r >= c


def _gla_prompt_kernel(*refs, n_chunks, batch):
    q_refs, k_refs, v_refs, la_refs = (refs[i * batch:(i + 1) * batch] for i in range(4))
    o_ref, s_out_ref, s_scr = refs[4 * batch:]
    blk = pl.program_id(0)

    @pl.when(blk == 0)
    def _():
        s_scr[...] = jnp.zeros_like(s_scr)

    ltri = _tril(CHUNK).astype(f32)
    r2 = lax.broadcasted_iota(jnp.int32, (2 * CHUNK, CHUNK), 0)
    c2 = lax.broadcasted_iota(jnp.int32, (2 * CHUNK, CHUNK), 1)
    tril2 = jnp.where(r2 >= CHUNK, r2 - CHUNK, r2) >= c2
    lane = lax.broadcasted_iota(jnp.int32, (CHUNK, LANES), 1)
    lo = (lane < DK).astype(f32)
    hi = 1.0 - lo

    def chunk(c, carry):
        rows = pl.ds(pl.multiple_of(c * CHUNK, CHUNK), CHUNK)
        for s in range(batch):
            g = la_refs[s][rows, :]
            b = jnp.dot(ltri, g, preferred_element_type=f32, precision=HIGHEST)
            b_last = b[CHUNK - 1:CHUNK, :]
            q = q_refs[s][rows, :]
            k = k_refs[s][rows, :]
            qg = q * jnp.exp(b) * (DK ** -0.5)
            kg = k * jnp.exp(-b)
            kd = k * jnp.exp(b_last - b)
            dec = jnp.exp(b_last)
            for p in range(HEADS // 2):
                ls = slice(p * LANES, (p + 1) * LANES)
                qg_p, kg_p, kd_p = qg[:, ls], kg[:, ls], kd[:, ls]
                qs = jnp.concatenate([qg_p * lo, qg_p * hi], axis=0)
                a = jnp.where(tril2, _dot_nt(qs, kg_p), 0.0)
                s_p = s_scr[s, p]
                inter = _dot(qs, s_p)
                v0 = v_refs[s][rows, pl.ds((2 * p) * DV, DV)]
                v1 = v_refs[s][rows, pl.ds((2 * p + 1) * DV, DV)]
                o_ref[s, rows, pl.ds((2 * p) * DV, DV)] = inter[:CHUNK] + _dot(a[:CHUNK], v0)
                o_ref[s, rows, pl.ds((2 * p + 1) * DV, DV)] = inter[CHUNK:] + _dot(a[CHUNK:], v1)
                dcol = jnp.broadcast_to(dec[:, ls], (LANES, LANES)).T
                s_scr[s, p] = dcol * s_p + _dot_tn(kd_p * lo, v0) + _dot_tn(kd_p * hi, v1)
        return carry

    lax.fori_loop(0, n_chunks, chunk, 0)

    @pl.when(blk == pl.num_programs(0) - 1)
    def _():
        s_out_ref[...] = s_scr[...]


def _stream_specs(width, batch, nb, col):
    return [pl.BlockSpec((TM, width), functools.partial(lambda s, i: (s * nb + i, col), s)) for s in range(batch)]


def _gla_prompt(proj, la, batch, seq):
    nb = seq // TM
    qw_blocks = TILE // QW
    return pl.pallas_call(
        functools.partial(_gla_prompt_kernel, n_chunks=TM // CHUNK, batch=batch),
        grid=(nb,),
        in_specs=[
            *_stream_specs(QW, batch, nb, T_GQK * qw_blocks),
            *_stream_specs(QW, batch, nb, T_GQK * qw_blocks + 1),
            *_stream_specs(VW, batch, nb, T_GV),
            *_stream_specs(QW, batch, nb, 0),
        ],
        out_specs=[
            pl.BlockSpec((batch, TM, VW), lambda i: (0, i, 0)),
            pl.BlockSpec((batch, HEADS // 2, LANES, LANES), lambda i: (0, 0, 0, 0)),
        ],
        out_shape=[
            jax.ShapeDtypeStruct((batch, seq, VW), f32),
            jax.ShapeDtypeStruct((batch, HEADS // 2, LANES, LANES), f32),
        ],
        scratch_shapes=[pltpu.VMEM((batch, HEADS // 2, LANES, LANES), f32)],
        compiler_params=_cparams("arbitrary"),
        name="gla_prompt",
    )(*([proj] * (3 * batch)), *([la] * batch))


def _mlstm_prompt_kernel(*refs, n_chunks, batch):
    q_refs, k_refs, v_refs, gif_refs = (refs[i * batch:(i + 1) * batch] for i in range(4))
    h_ref, c_out_ref, n_out_ref, m_out_ref, ct_scr, n_scr, m_scr = refs[4 * batch:]
    blk = pl.program_id(0)

    @pl.when(blk == 0)
    def _():
        ct_scr[...] = jnp.zeros_like(ct_scr)
        n_scr[...] = jnp.zeros_like(n_scr)
        m_scr[...] = jnp.zeros_like(m_scr)

    ltri = _tril(CHUNK).astype(f32)
    r_i = lax.broadcasted_iota(jnp.int32, (CHUNK, CHUNK), 0)
    c_i = lax.broadcasted_iota(jnp.int32, (CHUNK, CHUNK), 1)
    upper = r_i <= c_i

    def chunk(c, carry):
        rows = pl.ds(pl.multiple_of(c * CHUNK, CHUNK), CHUNK)
        for sq in range(batch):
            g = gif_refs[sq][rows, :]
            fc = jnp.dot(ltri, g, preferred_element_type=f32, precision=HIGHEST)
            g_t = g.T
            fc_t = fc.T
            q_all = q_refs[sq][rows, :]
            k_all = k_refs[sq][rows, :] * (DK ** -0.5)
            n_all = n_scr[sq]
            for h in range(HEADS):
                i_col = g[:, L_MI + h:L_MI + h + 1]
                f_col = fc[:, L_MF + h:L_MF + h + 1]
                i_row = g_t[L_MI + h:L_MI + h + 1, :]
                f_row = fc_t[L_MF + h:L_MF + h + 1, :]
                f_last = fc[CHUNK - 1:CHUNK, L_MF + h:L_MF + h + 1]
                m_prev = m_scr[sq, h:h + 1, 0:1]
                c_t = ct_scr[sq, h]
                q = q_all[:, h * DK:(h + 1) * DK]
                k = k_all[:, h * DK:(h + 1) * DK]
                v_t = v_refs[sq][rows, pl.ds(h * DV, DV)].T

                log_d = jnp.where(upper, f_row + (i_col - f_col), -jnp.inf)
                m_inter = m_prev + f_row
                m_row = jnp.maximum(m_inter, jnp.max(log_d, axis=0, keepdims=True))
                w_inter = jnp.exp(m_inter - m_row)
                sc = _dot_nt(k, q) * jnp.exp(log_d - m_row)
                num = w_inter * _dot_nt(c_t, q) + _dot(v_t, sc)
                qn = _dot_nt(n_all, q)[h:h + 1, :]
                den = w_inter * qn + jnp.sum(sc, axis=0, keepdims=True)
                h_t = num / jnp.maximum(jnp.abs(den), jnp.exp(-m_row))
                h_ref[sq, rows, pl.ds(h * DV, DV)] = h_t.T

                log_w = f_last - f_col + i_col
                m_new = jnp.maximum(m_prev + f_last, jnp.max(log_w, axis=0, keepdims=True))
                wk = jnp.exp(log_w - m_new) * k
                decay = jnp.exp(m_prev + f_last - m_new)
                ct_scr[sq, h] = decay * c_t + _dot(v_t, wk)
                n_scr[sq, h:h + 1, :] = decay * n_all[h:h + 1, :] + jnp.sum(wk, axis=0, keepdims=True)
                m_scr[sq, h:h + 1, :] = jnp.broadcast_to(m_new, (1, LANES))
        return carry

    lax.fori_loop(0, n_chunks, chunk, 0)

    @pl.when(blk == pl.num_programs(0) - 1)
    def _():
        for sq in range(batch):
            for h in range(HEADS):
                c_out_ref[sq, h] = ct_scr[sq, h].T
        n_out_ref[...] = n_scr[:, 0:HEADS, :]
        m_out_ref[...] = m_scr[:, 0:HEADS, :]


def _mlstm_prompt(proj, gif, batch, seq):
    nb = seq // TM
    qw_blocks = TILE // QW
    return pl.pallas_call(
        functools.partial(_mlstm_prompt_kernel, n_chunks=TM // CHUNK, batch=batch),
        grid=(nb,),
        in_specs=[
            *_stream_specs(QW, batch, nb, T_MQK * qw_blocks),
            *_stream_specs(QW, batch, nb, T_MQK * qw_blocks + 1),
            *_stream_specs(VW, batch, nb, T_MV),
            *_stream_specs(LANES, batch, nb, 0),
        ],
        out_specs=[
            pl.BlockSpec((batch, TM, VW), lambda i: (0, i, 0)),
            pl.BlockSpec((batch, HEADS, DK, DV), lambda i: (0, 0, 0, 0)),
            pl.BlockSpec((batch, HEADS, DK), lambda i: (0, 0, 0)),
            pl.BlockSpec((batch, HEADS, LANES), lambda i: (0, 0, 0)),
        ],
        out_shape=[
            jax.ShapeDtypeStruct((batch, seq, VW), f32),
            jax.ShapeDtypeStruct((batch, HEADS, DK, DV), f32),
            jax.ShapeDtypeStruct((batch, HEADS, DK), f32),
            jax.ShapeDtypeStruct((batch, HEADS, LANES), f32),
        ],
        scratch_shapes=[
            pltpu.VMEM((batch, HEADS, DV, DK), f32),
            pltpu.VMEM((batch, SUBLANES, DK), f32),
            pltpu.VMEM((batch, SUBLANES, LANES), f32),
        ],
        compiler_params=_cparams("arbitrary"),
        name="mlstm_prompt",
    )(*([proj] * (3 * batch)), *([gif] * batch))


def _diff_lambda_value(lam_ref, lam_init):
    lam = lam_ref[...]
    s1 = jnp.sum(lam[0:1] * lam[1:2], axis=1, keepdims=True)
    s2 = jnp.sum(lam[2:3] * lam[3:4], axis=1, keepdims=True)
    return jnp.exp(s1) - jnp.exp(s2) + lam_init


def _attn_prompt_step(p, qi_ref, ki_ref, q_ref, k_ref, v_ref, lam_ref, o_ref, q2_scr, m_scr, l_scr, acc_scr,
                      *, tq, lam_init):
    qi = qi_ref[p]
    ki = ki_ref[p]
    pair = lambda j: slice(j * LANES, (j + 1) * LANES)

    @pl.when(ki == 0)
    def _():
        for j in range(HEADS):
            q = q_ref[:, pair(j)] * (DK ** -0.5 * LOG2E)
            lane = lax.broadcasted_iota(jnp.int32, q.shape, 1)
            q2_scr[j, 0:tq, :] = jnp.where(lane < DK, q, 0.0).astype(bf16)
            q2_scr[j, tq:2 * tq, :] = jnp.where(lane >= DK, q, 0.0).astype(bf16)
        m_scr[...] = jnp.full_like(m_scr, -jnp.inf)
        l_scr[...] = jnp.zeros_like(l_scr)
        acc_scr[...] = jnp.zeros_like(acc_scr)

    def step(masked):
        sts = [_dot_nt(k_ref[:, pair(j)], q2_scr[j]) for j in range(HEADS)]
        pts, alphas = [], []
        for j in range(HEADS):
            st = sts[j]
            if masked:
                r = lax.broadcasted_iota(jnp.int32, st.shape, 0)
                c = lax.broadcasted_iota(jnp.int32, st.shape, 1)
                c = jnp.where(c >= tq, c - tq, c)
                st = jnp.where(r <= c, st, -jnp.inf)
            m_prev = m_scr[j]
            m_new = jnp.maximum(m_prev, jnp.max(st, axis=0, keepdims=True))
            alpha = jnp.exp2(m_prev - m_new)
            pt = jnp.exp2(st - m_new)
            l_scr[j] = alpha * l_scr[j] + jnp.sum(pt, axis=0, keepdims=True)
            m_scr[j] = m_new
            pts.append(pt.astype(bf16))
            alphas.append(alpha)
        for j in range(HEADS):
            acc_scr[j] = alphas[j] * acc_scr[j] + _dot_tn(v_ref[:, pair(j)], pts[j])

    @pl.when(ki < qi)
    def _():
        step(False)

    @pl.when(ki == qi)
    def _():
        step(True)
        lam = _diff_lambda_value(lam_ref, lam_init)
        for j in range(HEADS):
            ot = acc_scr[j] / l_scr[j]
            o_ref[:, pair(j)] = (ot[:, 0:tq] - lam * ot[:, tq:2 * tq]).T


def _attn_sample_step(b, pt_ref, q_ref, kts_ref, v_ref, lam_ref, ck_hbm, cv_hbm, o_ref, kbuf, vbuf, sem,
                      *, layer, n_pages, page, dec_seq, nseq, lam_init):
    nb = nseq
    past = n_pages * page
    slot = b % 2
    n_qh = 2 * HEADS

    def copies(seq_idx, s):
        out = []
        for pg in range(n_pages):
            src = pt_ref[seq_idx, pg]
            out.append(pltpu.make_async_copy(ck_hbm.at[layer, src], kbuf.at[s, :, pl.ds(pg * page, page)],
                                             sem.at[0, s]))
            out.append(pltpu.make_async_copy(cv_hbm.at[layer, src],
                                             vbuf.at[s, pl.ds(pg * page * HEADS, page * HEADS), :], sem.at[1, s]))
        return out

    @pl.when(b == 0)
    def _():
        kbuf[:, :, pl.ds(past, LANES)] = jnp.zeros((2, kbuf.shape[1], LANES), f32)
        vbuf[:, pl.ds(past * HEADS, LANES * HEADS), :] = jnp.zeros((2, LANES * HEADS, DV), f32)
        o_ref[...] = jnp.zeros_like(o_ref)
        for cp in copies(0, 0):
            cp.start()

    @pl.when(b + 1 < nb)
    def _():
        for cp in copies(b + 1, 1 - slot):
            cp.start()

    for cp in copies(b, slot):
        cp.wait()

    shift = lax.rem(nseq - b, nseq)
    b8 = (b // SUBLANES) * SUBLANES
    groups = [pl.ds(pl.multiple_of(t * nseq + b8, SUBLANES), SUBLANES) for t in range(dec_seq)]
    mine = lax.broadcasted_iota(jnp.int32, (SUBLANES, 2 * QW), 0) == b - b8
    pick = lambda ref, t: jnp.sum(jnp.where(mine, ref[groups[t], :], 0.0), axis=0, keepdims=True)
    head_of_lane = lax.broadcasted_iota(jnp.int32, (n_qh, 2 * QW), 1) // DK
    head_of_row = lax.broadcasted_iota(jnp.int32, (n_qh, 2 * QW), 0)
    q_rows = []
    for t in range(dec_seq):
        kbuf[slot, :, pl.ds(past + t, 1)] = pltpu.roll(kts_ref[t], shift, axis=1)[:, 0:1]
        v_row = pick(v_ref, t)
        for h in range(HEADS):
            vbuf[slot, pl.ds((past + t) * HEADS + h, 1), :] = v_row[:, h * DV:(h + 1) * DV]
        q = jnp.broadcast_to(pick(q_ref, t) * (DK ** -0.5), (n_qh, 2 * QW))
        q_rows.append(jnp.where(head_of_lane == head_of_row, q, 0.0))
    qbd = jnp.concatenate(q_rows, axis=0)

    s = _dot(qbd, kbuf[slot])
    r = lax.broadcasted_iota(jnp.int32, s.shape, 0)
    c = lax.broadcasted_iota(jnp.int32, s.shape, 1)
    s = jnp.where(c - past <= r // n_qh, s, -jnp.inf)
    m = jnp.max(s, axis=1, keepdims=True)
    pr = jnp.exp(s - m)
    inv_l = 1.0 / jnp.sum(pr, axis=1, keepdims=True)
    lam = _diff_lambda_value(lam_ref, lam_init)
    mine_v = mine[:, 0:DV]
    for hv in range(HEADS):
        v_h = vbuf[slot, pl.ds(hv, past + LANES, stride=HEADS), :]
        o_h = _dot(pr, v_h) * inv_l
        for t in range(dec_seq):
            r1 = t * n_qh + 2 * hv
            d = o_h[r1:r1 + 1] - lam * o_h[r1 + 1:r1 + 2]
            lanes = slice(hv * DV, (hv + 1) * DV)
            o_ref[groups[t], lanes] = jnp.where(mine_v, d, o_ref[groups[t], lanes])


def _attn_kernel(qi_ref, ki_ref, pt_ref, q_ref, k_ref, v_ref, lam_ref, qs_ref, kts_ref, vs_ref, ck_hbm, cv_hbm,
                 o_ref, os_ref, q2_scr, m_scr, l_scr, acc_scr, kbuf, vbuf, sem, *, prompt, sample, nseq, n_pairs):
    p = pl.program_id(1)

    @pl.when(p < n_pairs)
    def _():
        _attn_prompt_step(p, qi_ref, ki_ref, q_ref, k_ref, v_ref, lam_ref, o_ref,
                          q2_scr, m_scr, l_scr, acc_scr, **prompt)

    step = pl.program_id(0) * pl.num_programs(1) + p

    @pl.when(step < nseq)
    def _():
        _attn_sample_step(step, pt_ref, qs_ref, kts_ref, vs_ref, lam_ref, ck_hbm, cv_hbm, os_ref, kbuf, vbuf, sem,
                          nseq=nseq, **sample)


def _attention(proj, k_ts, page_table, diff_lambda, cache_kt, cache_v2, layer, lam_init, batch, seq, sample_block,
               tq=512):
    nq = seq // tq
    pairs = [(q, k) for q in range(nq) for k in range(q + 1)]
    qi_tab = jnp.asarray(np.array([a for a, _ in pairs], np.int32))
    ki_tab = jnp.asarray(np.array([b for _, b in pairs], np.int32))
    nseq, n_pages = page_table.shape
    n_pairs = len(pairs)
    steps = max(n_pairs, -(-nseq // batch))
    last = n_pairs - 1
    dec_seq = k_ts.shape[0]
    rows = dec_seq * nseq
    page = cache_kt.shape[3]
    cols = n_pages * page + LANES
    grid_spec = pltpu.PrefetchScalarGridSpec(
        num_scalar_prefetch=3,
        grid=(batch, steps),
        in_specs=[
            pl.BlockSpec((tq, TILE), lambda b, p, qt, kt, pt: (b * nq + qt[jnp.minimum(p, last)], T_DQ)),
            pl.BlockSpec((tq, TILE), lambda b, p, qt, kt, pt: (b * nq + kt[jnp.minimum(p, last)], T_DK)),
            pl.BlockSpec((tq, TILE), lambda b, p, qt, kt, pt: (b * nq + kt[jnp.minimum(p, last)], T_DV)),
            pl.BlockSpec((4, DK), lambda b, p, qt, kt, pt: (0, 0)),
            pl.BlockSpec((rows, TILE), lambda b, p, qt, kt, pt: (sample_block, T_DQ)),
            pl.BlockSpec((dec_seq, TILE, nseq), lambda b, p, qt, kt, pt: (0, 0, 0)),
            pl.BlockSpec((rows, TILE), lambda b, p, qt, kt, pt: (sample_block, T_DV)),
            pl.BlockSpec(memory_space=pl.ANY),
            pl.BlockSpec(memory_space=pl.ANY),
        ],
        out_specs=[
            pl.BlockSpec((tq, VW), lambda b, p, qt, kt, pt: (b * nq + qt[jnp.minimum(p, last)], 0)),
            pl.BlockSpec((rows, VW), lambda b, p, qt, kt, pt: (0, 0)),
        ],
        scratch_shapes=[
            pltpu.VMEM((HEADS, 2 * tq, LANES), bf16),
            pltpu.VMEM((HEADS, 1, 2 * tq), f32),
            pltpu.VMEM((HEADS, 1, 2 * tq), f32),
            pltpu.VMEM((HEADS, DV, 2 * tq), f32),
            pltpu.VMEM((2, 2 * QW, cols), f32),
            pltpu.VMEM((2, cols * HEADS, DV), f32),
            pltpu.SemaphoreType.DMA((2, 2)),
        ],
    )
    return pl.pallas_call(
        functools.partial(_attn_kernel, nseq=nseq, n_pairs=n_pairs, prompt=dict(tq=tq, lam_init=lam_init),
                          sample=dict(layer=layer, n_pages=n_pages, page=page, dec_seq=dec_seq, lam_init=lam_init)),
        grid_spec=grid_spec,
        out_shape=[jax.ShapeDtypeStruct((batch * seq, VW), f32), jax.ShapeDtypeStruct((rows, VW), f32)],
        compiler_params=_cparams("arbitrary", "arbitrary"),
        name="attention",
    )(qi_tab, ki_tab, page_table, proj, proj, proj, diff_lambda, proj, k_ts, proj, cache_kt, cache_v2)


def _head_of_lane(shape):
    return lax.broadcasted_iota(jnp.int32, shape, 1) // DK


def _expand_heads(cols, base, head):
    out = cols[:, base + HEADS - 1:base + HEADS]
    for h in range(HEADS - 2, -1, -1):
        out = jnp.where(head == h, cols[:, base + h:base + h + 1], out)
    return out


def _sample_prep_kernel(gq_ref, gk_ref, la_ref, mq_ref, mk_ref, gif_ref, n0_ref, m0_ref,
                        ga_ref, gkk_ref, gqq_ref, ma_ref, mkk_ref, mqq_ref, n_out_ref, m_out_ref, *, nseq, dec_seq):
    head = _head_of_lane((nseq, QW))
    m = _expand_heads(m0_ref[...], 0, head)
    n = n0_ref[...]
    for t in range(dec_seq):
        rows = pl.ds(t * nseq, nseq)
        ga_ref[t] = jnp.exp(la_ref[rows, :]).T
        gkk_ref[t] = gk_ref[rows, :].T
        gqq_ref[t] = (gq_ref[rows, :] * (DK ** -0.5)).T
        gates = gif_ref[rows, :]
        i_e = _expand_heads(gates, L_MI, head)
        f_e = _expand_heads(gates, L_MF, head)
        k = mk_ref[rows, :] * (DK ** -0.5)
        q = mq_ref[rows, :]
        m_new = jnp.maximum(f_e + m, i_e)
        fp = jnp.exp(f_e + m - m_new)
        ip = jnp.exp(i_e - m_new)
        n = fp * n + ip * k
        nq = n * q
        den = jnp.zeros_like(nq)
        for h in range(HEADS):
            den = jnp.where(head == h, jnp.sum(jnp.where(head == h, nq, 0.0), axis=1, keepdims=True), den)
        inv = 1.0 / jnp.maximum(jnp.abs(den), jnp.exp(-m_new))
        ma_ref[t] = fp.T
        mkk_ref[t] = (ip * k).T
        mqq_ref[t] = (q * inv).T
        m = m_new
    n_out_ref[...] = n
    m_out_ref[...] = m


def _sample_prep(proj, la, gif, state_n, state_m, layer, sample_block, nseq, dec_seq):
    rows = nseq * dec_seq
    qw_blocks = TILE // QW
    tok = jax.ShapeDtypeStruct((dec_seq, QW, nseq), f32)
    st = jax.ShapeDtypeStruct((nseq, QW), f32)
    tok_spec = pl.BlockSpec((dec_seq, QW, nseq), lambda i: (0, 0, 0))
    st_spec = pl.BlockSpec((nseq, QW), lambda i: (0, 0))
    col = lambda c: pl.BlockSpec((rows, QW), lambda i: (sample_block, c))
    return pl.pallas_call(
        functools.partial(_sample_prep_kernel, nseq=nseq, dec_seq=dec_seq),
        grid=(1,),
        in_specs=[
            col(T_GQK * qw_blocks), col(T_GQK * qw_blocks + 1), col(0),
            col(T_MQK * qw_blocks), col(T_MQK * qw_blocks + 1),
            pl.BlockSpec((rows, LANES), lambda i: (sample_block, 0)),
            pl.BlockSpec((None, nseq, QW), lambda i: (layer, 0, 0)),
            pl.BlockSpec((None, nseq, HEADS), lambda i: (layer, 0, 0)),
        ],
        out_specs=[tok_spec] * 6 + [st_spec] * 2,
        out_shape=[tok] * 6 + [st] * 2,
        compiler_params=_cparams("arbitrary"),
        name="sample_prep",
    )(proj, proj, la, proj, proj, gif, state_n, state_m)


SEQ_PER_STEP = 8


def _sample_state_kernel(a_ref, k_ref, q_ref, v_ref, s_ref, o_ref, s_out_ref, *, dec_seq, nseq):
    blk = pl.program_id(0)
    seq0 = blk * SEQ_PER_STEP
    shift = lax.rem(nseq - seq0, nseq)
    a_t = [pltpu.roll(a_ref[t], shift, axis=1) for t in range(dec_seq)]
    k_t = [pltpu.roll(k_ref[t], shift, axis=1) for t in range(dec_seq)]
    q_t = [pltpu.roll(q_ref[t], shift, axis=1) for t in range(dec_seq)]
    toks = [pl.ds(pl.multiple_of(t * nseq + seq0, SEQ_PER_STEP), SEQ_PER_STEP) for t in range(dec_seq)]
    v_t = [v_ref[toks[t], :] for t in range(dec_seq)]
    row = lax.broadcasted_iota(jnp.int32, (SEQ_PER_STEP, DV), 0)
    outs = [[jnp.zeros((SEQ_PER_STEP, DV), f32) for _ in range(HEADS)] for _ in range(dec_seq)]
    for bb in range(SEQ_PER_STEP):
        for h in range(HEADS):
            s = s_ref[bb, h]
            rows = slice(h * DK, (h + 1) * DK)
            for t in range(dec_seq):
                a = a_t[t][rows, bb:bb + 1]
                k = k_t[t][rows, bb:bb + 1]
                q = q_t[t][rows, bb:bb + 1]
                v = v_t[t][bb:bb + 1, h * DV:(h + 1) * DV]
                s = a * s + k * v
                outs[t][h] = jnp.where(row == bb, jnp.sum(q * s, axis=0, keepdims=True), outs[t][h])
            s_out_ref[bb, h] = s
    for t in range(dec_seq):
        for h in range(HEADS):
            o_ref[toks[t], h * DV:(h + 1) * DV] = outs[t][h]


def _sample_state(a_t, k_t, q_t, proj, v_tile, sample_block, state, layer, dec_seq):
    nseq = a_t.shape[2]
    rows = dec_seq * nseq
    full = pl.BlockSpec((dec_seq, QW, nseq), lambda i: (0, 0, 0))
    return pl.pallas_call(
        functools.partial(_sample_state_kernel, dec_seq=dec_seq, nseq=nseq),
        grid=(nseq // SEQ_PER_STEP,),
        in_specs=[
            full, full, full,
            pl.BlockSpec((rows, VW), lambda i: (sample_block, v_tile)),
            pl.BlockSpec((None, SEQ_PER_STEP, HEADS, DK, DV), lambda i: (layer, i, 0, 0, 0)),
        ],
        out_specs=[
            pl.BlockSpec((rows, VW), lambda i: (0, 0)),
            pl.BlockSpec((SEQ_PER_STEP, HEADS, DK, DV), lambda i: (i, 0, 0, 0)),
        ],
        out_shape=[
            jax.ShapeDtypeStruct((rows, VW), f32),
            jax.ShapeDtypeStruct(state.shape[1:], f32),
        ],
        compiler_params=_cparams("arbitrary"),
        name="sample_state",
    )(a_t, k_t, q_t, proj, state)


def _head_rms(x, g):
    parts = []
    for h in range(HEADS):
        parts.append(_rms(x[:, h * DV:(h + 1) * DV], g))
    return jnp.concatenate(parts, axis=1)


FF_SLABS = ((0, 768), (768, 768), (1536, 768), (2304, 512))
TM_CH = 256


def _merge_ffn_kernel(hp_ref, hs_ref, oap_ref, oas_ref, omp_ref, oms_ref, odp_ref, ods_ref, gr_ref, mo_ref,
                      g0_ref, g1_ref, g2_ref, gg_ref, gm_ref, gd_ref, wa_ref, wm_ref, wd_ref, wo_ref,
                      gf_ref, wg_ref, wu_ref, wdn_ref, pp_ref, ps_ref, gp_ref, wpg_ref, wpp_ref, gfin_ref,
                      *out_refs, lam_init, final, n_prompt_blocks):
    pick = functools.partial(_pick_rows, n_prompt_blocks)
    gr = gr_ref[...]
    a = _head_rms(pick(oap_ref, oas_ref), gg_ref[...]) * (gr * _sigmoid(gr))
    m = _sigmoid(mo_ref[...]) * _head_rms(pick(omp_ref, oms_ref), gm_ref[...])
    d = _head_rms(pick(odp_ref, ods_ref), gd_ref[...]) * (1.0 - lam_init)
    merged = (g0_ref[...] * _dot(a, wa_ref[...]) + g1_ref[...] * _dot(m, wm_ref[...])
              + g2_ref[...] * _dot(d, wd_ref[...]))
    h = pick(hp_ref, hs_ref) + _dot(merged, wo_ref[...])

    xf = _rms(h, gf_ref[...]).astype(bf16)
    h2 = h
    for lo, width in FF_SLABS:
        gate = jnp.dot(xf, wg_ref[:, lo:lo + width], preferred_element_type=f32)
        up = jnp.dot(xf, wu_ref[:, lo:lo + width], preferred_element_type=f32)
        h2 = h2 + _dot(gate * _sigmoid(gate) * up, wdn_ref[lo:lo + width, :])
    ple_gate = _sigmoid(_dot(_rms(h2, gp_ref[...]), wpg_ref[...]))
    h3 = h2 + ple_gate * _dot(pick(pp_ref, ps_ref), wpp_ref[...])
    if not final:
        out_refs[0][...] = h3
        return
    y = _rms(h3, gfin_ref[...])
    yp_ref, ys_ref = out_refs
    i = pl.program_id(0)

    @pl.when(i < n_prompt_blocks)
    def _():
        yp_ref[...] = y

    @pl.when(i >= n_prompt_blocks)
    def _():
        ys_ref[...] = y


def _merge_ffn(h_p, h_s, h_s_row0, o_a, o_m, o_d, proj, gates, g_gla, g_ml, g_diff, w_a, w_m, w_d, w_out, lam_init,
               g_ffn, w_g, w_u, w_dn, p_p, p_s, p_row0, g_ple, w_pg, w_pp, g_final, final, n_prompt_rows, n_sample_rows):
    tm = TM_CH
    npb, nsb = n_prompt_rows // tm, n_sample_rows // tm
    T = n_prompt_rows + n_sample_rows
    row = lambda i: (i, 0)
    const = lambda i: (0, 0)
    resident = lambda shape: pl.BlockSpec(shape, const, pipeline_mode=pl.Buffered(1))
    rows2 = lambda width, sample_block=0, prompt_block0=0: _row_specs(width, npb, sample_block, tm, nsb, prompt_block0)
    assert FF_SLABS[-1][0] + FF_SLABS[-1][1] == D_FF
    if final:
        out_specs = [pl.BlockSpec((tm, D_MODEL), lambda i: (jnp.minimum(i, npb - 1), 0)),
                     pl.BlockSpec((tm, D_MODEL), lambda i: (jnp.clip(i - npb, 0, nsb - 1), 0))]
        out_shape = [jax.ShapeDtypeStruct((n_prompt_rows, D_MODEL), f32),
                     jax.ShapeDtypeStruct((n_sample_rows, D_MODEL), f32)]
    else:
        out_specs = pl.BlockSpec((tm, D_MODEL), row)
        out_shape = jax.ShapeDtypeStruct((T, D_MODEL), f32)
    return pl.pallas_call(
        functools.partial(_merge_ffn_kernel, lam_init=lam_init, final=final, n_prompt_blocks=npb),
        grid=(npb + nsb,),
        in_specs=[
            *rows2(D_MODEL, sample_block=h_s_row0 // tm),
            *rows2(VW), *rows2(VW), *rows2(VW),
            pl.BlockSpec((tm, VW), lambda i: (i, T_GR)),
            pl.BlockSpec((tm, VW), lambda i: (i, T_MO)),
            pl.BlockSpec((tm, D_MODEL), lambda i: (i, 0)),
            pl.BlockSpec((tm, D_MODEL), lambda i: (i, 1)),
            pl.BlockSpec((tm, D_MODEL), lambda i: (i, 2)),
            pl.BlockSpec((1, DV), const),
            pl.BlockSpec((1, DV), const),
            pl.BlockSpec((1, DV), const),
            resident((VW, D_MODEL)), resident((VW, D_MODEL)), resident((VW, D_MODEL)),
            resident((D_MODEL, D_MODEL)),
            pl.BlockSpec((1, D_MODEL), const),
            resident((D_MODEL, D_FF)), resident((D_MODEL, D_FF)), resident((D_FF, D_MODEL)),
            *rows2(PLE_DIM, prompt_block0=p_row0 // tm),
            pl.BlockSpec((1, D_MODEL), const),
            resident((D_MODEL, D_MODEL)),
            resident((PLE_DIM, D_MODEL)),
            pl.BlockSpec((1, D_MODEL), const),
        ],
        out_specs=out_specs,
        out_shape=out_shape,
        compiler_params=_cparams("arbitrary"),
        name="merge_ffn",
    )(h_p, h_s, *o_a, *o_m, *o_d, proj, proj, gates, gates, gates, g_gla, g_ml, g_diff, w_a, w_m, w_d, w_out,
      g_ffn, w_g, w_u, w_dn, p_p, p_s, g_ple, w_pg, w_pp, g_final)


def _rope_tables(positions):
    half = ROT_DIM // 2
    inv_freq = ROPE_THETA ** (-jnp.arange(half, dtype=f32) * 2.0 / ROT_DIM)
    ang = positions.astype(f32)[:, None] * inv_freq[None, :]
    cos, sin = jnp.cos(ang), jnp.sin(ang)
    n = positions.shape[0]
    one = jnp.ones((n, DK - ROT_DIM), f32)
    zero = jnp.zeros((n, DK - ROT_DIM), f32)
    zh = jnp.zeros((n, half), f32)
    cos_h = jnp.concatenate([cos, cos, one], axis=1)
    up_h = jnp.concatenate([-sin, zh, zero], axis=1)
    dn_h = jnp.concatenate([zh, sin, zero], axis=1)
    rep = lambda a: jnp.concatenate([a] * (LANES // DK), axis=1)
    return rep(cos_h), rep(up_h), rep(dn_h)


def _prep_in_weights(w_in_l, w_gate_l):
    o = np.cumsum((0, QW, QW, VW, VW, GLA_RANK, QW, QW, VW, VW, 2 * HEADS, 2 * QW, 2 * QW, VW))
    seg = lambda i: w_in_l[:, int(o[i]):int(o[i + 1])]
    gq, gk, gv, gr, glr, mq, mk, mv, mo, mif, dq, dk, dv = (seg(i) for i in range(13))
    main = jnp.concatenate([w_gate_l, gq, gk, gv, gr, mq, mk, mv, mo, dq, dk, dv], axis=1).astype(bf16)
    pad = jnp.zeros((D_MODEL, LANES - GLA_RANK - 2 * HEADS), f32)
    small = jnp.concatenate([glr, mif, pad], axis=1).astype(bf16)
    return main, small


def kernel(x_prompt, x_sample, cache_k, cache_v, state_gla, state_mlstm_c, state_mlstm_n, state_mlstm_m, page_table, p_prompt, p_sample, g_mix, w_in, w_gla_gk, b_gla_gk, g_gla_norm, b_mlstm_if, g_mlstm_norm, diff_lambda, g_diff_norm, w_branch, w_gate, w_out, g_ffn, w_ffn_gate, w_ffn_up, w_ffn_down, g_ple, w_ple_gate, w_ple_proj, g_final):
    bp, lp, _ = x_prompt.shape
    bs, ls, _ = x_sample.shape
    depth = g_mix.shape[0]
    n_pages = page_table.shape[1]
    page = cache_k.shape[2]
    past = n_pages * page
    tp = bp * lp
    ts = bs * ls
    assert lp % TM == 0 and ts == TM and bs == LANES and page == LANES and TM_IN % bs == 0
    npb = tp // TM

    def to_step_major(a):
        return jnp.swapaxes(a, 0, 1).reshape((ts,) + a.shape[2:])

    def from_step_major(a):
        return jnp.swapaxes(a.reshape((ls, bs) + a.shape[1:]), 0, 1)

    h_p, h_s, h_s_row0 = x_prompt.reshape(tp, D_MODEL), to_step_major(x_sample), 0

    pos_rows = jnp.concatenate([jnp.arange(lp, dtype=jnp.int32),
                                past + jnp.repeat(jnp.arange(ls, dtype=jnp.int32), bs)])
    cos_t, sup_t, sdn_t = _rope_tables(pos_rows)

    ck = jnp.transpose(cache_k, (0, 1, 3, 4, 2)).reshape(cache_k.shape[0], cache_k.shape[1], 2 * QW, page)
    cv = cache_v.reshape(cache_v.shape[0], cache_v.shape[1], page * HEADS, DV)
    state_n = state_mlstm_n.reshape(depth, bs, QW)
    p_rows = p_prompt.reshape(depth * tp, PLE_DIM)
    outs_p, outs_s = [], []
    for l in range(depth):
        lam_init = 0.8 - 0.6 * math.exp(-0.3 * l)
        w_main, w_small = _prep_in_weights(w_in[l], w_gate[l])
        w_gk = jnp.concatenate([w_gla_gk[l], jnp.zeros((LANES - GLA_RANK, QW), f32)], axis=0)
        b_if = jnp.zeros((1, LANES), f32).at[0, L_MI:L_MI + 2 * HEADS].set(b_mlstm_if[l])
        gates, proj, k_t, v_p, k_ts, la, gif = _inproj(h_p, h_s, h_s_row0 // TM_IN, g_mix[l][None], w_main, w_small, w_gk,
                                                  b_gla_gk[l][None], b_if, cos_t, sup_t, sdn_t, bp, lp, bs, ls)

        oa_p, gla_p = _gla_prompt(proj, la, bp, lp)
        om_p, c_p, n_p, m_p = _mlstm_prompt(proj, gif, bp, lp)
        oa_p, om_p = oa_p.reshape(tp, VW), om_p.reshape(tp, VW)
        od_p, od_s = _attention(proj, k_ts, page_table, diff_lambda[l], ck, cv, l, lam_init, bp, lp, npb)

        ga, gk_, gq_, ma, mk_, mq_, n_s, m_s = _sample_prep(proj, la, gif, state_n, state_mlstm_m, l, npb, bs, ls)
        oa_s, gla_s = _sample_state(ga, gk_, gq_, proj, T_GV, npb, state_gla, l, ls)
        om_s, c_s = _sample_state(ma, mk_, mq_, proj, T_MV, npb, state_mlstm_c, l, ls)
        v_s = from_step_major(proj[tp:, T_DV * TILE:(T_DV + 1) * TILE])

        wb = w_branch[l].astype(bf16)
        h = _merge_ffn(h_p, h_s, h_s_row0, (oa_p, oa_s), (om_p, om_s), (od_p, od_s), proj, gates,
                       g_gla_norm[l][None], g_mlstm_norm[l][None], g_diff_norm[l][None],
                       wb[:VW], wb[VW:2 * VW], wb[2 * VW:], w_out[l].astype(bf16), lam_init,
                       g_ffn[l][None], w_ffn_gate[l].astype(bf16), w_ffn_up[l].astype(bf16),
                       w_ffn_down[l].astype(bf16), p_rows, to_step_major(p_sample[l]), l * tp,
                       g_ple[l][None], w_ple_gate[l].astype(bf16), w_ple_proj[l].astype(bf16), g_final[None],
                       l == depth - 1, tp, ts)
        h_p, h_s, h_s_row0 = h, h, tp

        k_p = k_t.reshape(bp, 2 * HEADS, DK, lp).transpose(0, 3, 1, 2)
        k_new = k_ts.reshape(ls, 2 * HEADS, DK, bs).transpose(3, 0, 1, 2)
        outs_p.append((k_p, v_p.reshape(bp, lp, HEADS, DV),
                       gla_p.reshape(bp, HEADS, DK, DV), c_p, n_p, m_p[:, :, 0]))
        outs_s.append((k_new, v_s.reshape(bs, ls, HEADS, DV),
                       gla_s, c_s, n_s.reshape(bs, HEADS, DK), m_s[:, ::DK]))

    y_prompt = h[0].reshape(bp, lp, D_MODEL)
    y_sample = from_step_major(h[1])
    stack = lambda items: [jnp.stack(t) for t in zip(*items)]
    return tuple([y_prompt, y_sample] + stack(outs_p) + stack(outs_s))
```

```python
import functools
import math

import numpy as np
import jax
import jax.numpy as jnp
from jax import lax
from jax.experimental import pallas as pl
from jax.experimental.pallas import tpu as pltpu

f32 = jnp.float32
bf16 = jnp.bfloat16
HIGHEST = lax.Precision.HIGHEST

D_MODEL = 1024
PLE_DIM = 256
HEADS = 4
DK = 64
DV = 128
GLA_RANK = 16
GLA_GATE_NORM = 16.0
ROT_DIM = 16
ROPE_THETA = 500000.0
CHUNK = 64
D_FF = 2816
EPS = 1e-6
LOG2E = 1.4426950408889634
VW = HEADS * DV
QW = HEADS * DK

LANES = 128
SUBLANES = 8
VMEM_LIMIT = 48 * 1024 * 1024

TILE = 512
GATE_W = 3 * D_MODEL
T_GQK, T_GV, T_GR = 0, 1, 2
T_MQK, T_MV, T_MO = 3, 4, 5
T_DQ, T_DK, T_DV = 6, 7, 8
N_TILES = 9
P_MIX = N_TILES * TILE
P_MAIN = GATE_W + P_MIX
L_GLR = 0
L_MI = 16
L_MF = 20

TM = 512


def _cparams(*sem):
    return pltpu.CompilerParams(dimension_semantics=sem, vmem_limit_bytes=VMEM_LIMIT)


def _log_sigmoid(x):
    return jnp.minimum(x, 0.0) - jnp.log1p(jnp.exp(-jnp.abs(x)))


def _sigmoid(x):
    return 0.5 * jnp.tanh(0.5 * x) + 0.5


def _rms(x, g):
    return x * lax.rsqrt(jnp.mean(x * x, axis=-1, keepdims=True) + EPS) * g


def _dot(a, b):
    return jnp.dot(a.astype(bf16), b.astype(bf16), preferred_element_type=f32)


def _dot_nt(a, b):
    return lax.dot_general(a.astype(bf16), b.astype(bf16), (((1,), (1,)), ((), ())), preferred_element_type=f32)


def _dot_tn(a, b):
    return lax.dot_general(a.astype(bf16), b.astype(bf16), (((0,), (0,)), ((), ())), preferred_element_type=f32)


def _rope_tile(x, cos, sin_up, sin_dn):
    parts = []
    for c in range(x.shape[1] // LANES):
        xc = x[:, c * LANES:(c + 1) * LANES]
        parts.append(xc * cos + pltpu.roll(xc, LANES - ROT_DIM // 2, axis=1) * sin_up
                     + pltpu.roll(xc, ROT_DIM // 2, axis=1) * sin_dn)
    return jnp.concatenate(parts, axis=1)


def _row_specs(width, n_prompt_blocks, sample_block, tm=None, n_sample_blocks=1, prompt_block0=0):
    tm = TM if tm is None else tm
    return (pl.BlockSpec((tm, width), lambda i: (prompt_block0 + jnp.minimum(i, n_prompt_blocks - 1), 0)),
            pl.BlockSpec((tm, width),
                         lambda i: (sample_block + jnp.clip(i - n_prompt_blocks, 0, n_sample_blocks - 1), 0)))


def _pick_rows(n_prompt_blocks, p_ref, s_ref):
    return jnp.where(pl.program_id(0) >= n_prompt_blocks, s_ref[...], p_ref[...])


TM_IN = 256


def _inproj_kernel(xp_ref, xs_ref, g_ref, w_ref, ws_ref, wgk_ref, bgk_ref, bif_ref, cos_ref, sup_ref, sdn_ref,
                   gates_ref, proj_ref, kt_ref, vp_ref, kts_ref, la_ref, gif_ref, *, n_prompt_blocks, dec_batch):
    i = pl.program_id(0)
    xn = _rms(_pick_rows(n_prompt_blocks, xp_ref, xs_ref), g_ref[...]).astype(bf16)
    small = jnp.dot(xn, ws_ref[...], preferred_element_type=f32)
    z = jnp.dot(small, wgk_ref[...], preferred_element_type=f32, precision=HIGHEST) + bgk_ref[...]
    la_ref[...] = _log_sigmoid(z) * (1.0 / GLA_GATE_NORM)
    gi = small + bif_ref[...]
    lane = lax.broadcasted_iota(jnp.int32, gi.shape, 1)
    gif_ref[...] = jnp.where((lane >= L_MF) & (lane < L_MF + HEADS), _log_sigmoid(gi), gi)

    for c in range(GATE_W // TILE):
        cols = slice(c * TILE, (c + 1) * TILE)
        gates_ref[:, cols] = _sigmoid(jnp.dot(xn, w_ref[:, cols], preferred_element_type=f32)).astype(bf16)

    cos, sup, sdn = cos_ref[...], sup_ref[...], sdn_ref[...]
    for t in range(N_TILES):
        acc = jnp.dot(xn, w_ref[:, GATE_W + t * TILE:GATE_W + (t + 1) * TILE], preferred_element_type=f32)
        if t in (T_DQ, T_DK):
            acc = _rope_tile(acc, cos, sup, sdn)
        proj_ref[:, t * TILE:(t + 1) * TILE] = acc
        if t == T_DK:
            k_t = acc.T
        if t == T_DV:
            v_rows = acc

    @pl.when(i < n_prompt_blocks)
    def _():
        kt_ref[...] = k_t
        for h in range(HEADS):
            vp_ref[pl.ds(h, v_rows.shape[0], stride=HEADS), :] = v_rows[:, h * DV:(h + 1) * DV]

    @pl.when(i >= n_prompt_blocks)
    def _():
        for s in range(k_t.shape[1] // dec_batch):
            kts_ref[s] = k_t[:, s * dec_batch:(s + 1) * dec_batch]


def _inproj(h_p, h_s, sample_block, g_mix, w_main, w_small, w_gk, b_gk, b_if, cos_t, sup_t, sdn_t,
            batch, seq, dec_batch, dec_seq):
    tm = TM_IN
    seq_blocks = seq // tm
    npb = batch * seq_blocks
    steps_per_block = tm // dec_batch
    nsb = dec_seq // steps_per_block
    nt = npb + nsb
    T = nt * tm
    const = lambda i: (0, 0)
    resident = lambda shape: pl.BlockSpec(shape, const, pipeline_mode=pl.Buffered(1))

    def tab_map(i):
        return (jnp.where(i < npb, i % seq_blocks, seq_blocks + i - npb), 0)

    def kt_map(i):
        j = jnp.minimum(i, npb - 1)
        return (j // seq_blocks, 0, j % seq_blocks)

    return pl.pallas_call(
        functools.partial(_inproj_kernel, n_prompt_blocks=npb, dec_batch=dec_batch),
        grid=(nt,),
        in_specs=[
            *_row_specs(D_MODEL, npb, sample_block, tm, nsb),
            pl.BlockSpec((1, D_MODEL), const),
            resident((D_MODEL, P_MAIN)),
            resident((D_MODEL, LANES)),
            resident((LANES, QW)),
            pl.BlockSpec((1, QW), const),
            pl.BlockSpec((1, LANES), const),
            pl.BlockSpec((tm, LANES), tab_map),
            pl.BlockSpec((tm, LANES), tab_map),
            pl.BlockSpec((tm, LANES), tab_map),
        ],
        out_specs=[
            pl.BlockSpec((tm, GATE_W), lambda i: (i, 0)),
            pl.BlockSpec((tm, P_MIX), lambda i: (i, 0)),
            pl.BlockSpec((None, TILE, tm), kt_map),
            pl.BlockSpec((tm * HEADS, DV), lambda i: (jnp.minimum(i, npb - 1), 0)),
            pl.BlockSpec((steps_per_block, TILE, dec_batch), lambda i: (jnp.maximum(i - npb, 0), 0, 0)),
            pl.BlockSpec((tm, QW), lambda i: (i, 0)),
            pl.BlockSpec((tm, LANES), lambda i: (i, 0)),
        ],
        out_shape=[
            jax.ShapeDtypeStruct((T, GATE_W), bf16),
            jax.ShapeDtypeStruct((T, P_MIX), f32),
            jax.ShapeDtypeStruct((batch, TILE, seq), f32),
            jax.ShapeDtypeStruct((batch * seq * HEADS, DV), f32),
            jax.ShapeDtypeStruct((dec_seq, TILE, dec_batch), f32),
            jax.ShapeDtypeStruct((T, QW), f32),
            jax.ShapeDtypeStruct((T, LANES), f32),
        ],
        compiler_params=_cparams("arbitrary"),
        name="inproj",
    )(h_p, h_s, g_mix, w_main, w_small, w_gk, b_gk, b_if, cos_t, sup_t, sdn_t)


def _tril(n):
    r = lax.broadcasted_iota(jnp.int32, (n, n), 0)
    c = lax.broadcasted_iota(jnp.int32, (n, n), 1)
    return r >= c


def _gla_prompt_kernel(*refs, n_chunks, batch):
    q_refs, k_refs, v_refs, la_refs = (refs[i * batch:(i + 1) * batch] for i in range(4))
    o_ref, s_out_ref, s_scr = refs[4 * batch:]
    blk = pl.program_id(0)

    @pl.when(blk == 0)
    def _():
        s_scr[...] = jnp.zeros_like(s_scr)

    ltri = _tril(CHUNK).astype(f32)
    r2 = lax.broadcasted_iota(jnp.int32, (2 * CHUNK, CHUNK), 0)
    c2 = lax.broadcasted_iota(jnp.int32, (2 * CHUNK, CHUNK), 1)
    tril2 = jnp.where(r2 >= CHUNK, r2 - CHUNK, r2) >= c2
    lane = lax.broadcasted_iota(jnp.int32, (CHUNK, LANES), 1)
    lo = (lane < DK).astype(f32)
    hi = 1.0 - lo

    def chunk(c, carry):
        rows = pl.ds(pl.multiple_of(c * CHUNK, CHUNK), CHUNK)
        for s in range(batch):
            g = la_refs[s][rows, :]
            b = jnp.dot(ltri, g, preferred_element_type=f32, precision=HIGHEST)
            b_last = b[CHUNK - 1:CHUNK, :]
            q = q_refs[s][rows, :]
            k = k_refs[s][rows, :]
            qg = q * jnp.exp(b) * (DK ** -0.5)
            kg = k * jnp.exp(-b)
            kd = k * jnp.exp(b_last - b)
            dec = jnp.exp(b_last)
            for p in range(HEADS // 2):
                ls = slice(p * LANES, (p + 1) * LANES)
                qg_p, kg_p, kd_p = qg[:, ls], kg[:, ls], kd[:, ls]
                qs = jnp.concatenate([qg_p * lo, qg_p * hi], axis=0)
                a = jnp.where(tril2, _dot_nt(qs, kg_p), 0.0)
                s_p = s_scr[s, p]
                inter = _dot(qs, s_p)
                v0 = v_refs[s][rows, pl.ds((2 * p) * DV, DV)]
                v1 = v_refs[s][rows, pl.ds((2 * p + 1) * DV, DV)]
                o_ref[s, rows, pl.ds((2 * p) * DV, DV)] = inter[:CHUNK] + _dot(a[:CHUNK], v0)
                o_ref[s, rows, pl.ds((2 * p + 1) * DV, DV)] = inter[CHUNK:] + _dot(a[CHUNK:], v1)
                dcol = jnp.broadcast_to(dec[:, ls], (LANES, LANES)).T
                s_scr[s, p] = dcol * s_p + _dot_tn(kd_p * lo, v0) + _dot_tn(kd_p * hi, v1)
        return carry

    lax.fori_loop(0, n_chunks, chunk, 0)

    @pl.when(blk == pl.num_programs(0) - 1)
    def _():
        s_out_ref[...] = s_scr[...]


def _stream_specs(width, batch, nb, col):
    return [pl.BlockSpec((TM, width), functools.partial(lambda s, i: (s * nb + i, col), s)) for s in range(batch)]


def _gla_prompt(proj, la, batch, seq):
    nb = seq // TM
    qw_blocks = TILE // QW
    return pl.pallas_call(
        functools.partial(_gla_prompt_kernel, n_chunks=TM // CHUNK, batch=batch),
        grid=(nb,),
        in_specs=[
            *_stream_specs(QW, batch, nb, T_GQK * qw_blocks),
            *_stream_specs(QW, batch, nb, T_GQK * qw_blocks + 1),
            *_stream_specs(VW, batch, nb, T_GV),
            *_stream_specs(QW, batch, nb, 0),
        ],
        out_specs=[
            pl.BlockSpec((batch, TM, VW), lambda i: (0, i, 0)),
            pl.BlockSpec((batch, HEADS // 2, LANES, LANES), lambda i: (0, 0, 0, 0)),
        ],
        out_shape=[
            jax.ShapeDtypeStruct((batch, seq, VW), f32),
            jax.ShapeDtypeStruct((batch, HEADS // 2, LANES, LANES), f32),
        ],
        scratch_shapes=[pltpu.VMEM((batch, HEADS // 2, LANES, LANES), f32)],
        compiler_params=_cparams("arbitrary"),
        name="gla_prompt",
    )(*([proj] * (3 * batch)), *([la] * batch))


def _mlstm_prompt_kernel(*refs, n_chunks, batch):
    q_refs, k_refs, v_refs, gif_refs = (refs[i * batch:(i + 1) * batch] for i in range(4))
    h_ref, c_out_ref, n_out_ref, m_out_ref, ct_scr, n_scr, m_scr = refs[4 * batch:]
    blk = pl.program_id(0)

    @pl.when(blk == 0)
    def _():
        ct_scr[...] = jnp.zeros_like(ct_scr)
        n_scr[...] = jnp.zeros_like(n_scr)
        m_scr[...] = jnp.zeros_like(m_scr)

    ltri = _tril(CHUNK).astype(f32)
    r_i = lax.broadcasted_iota(jnp.int32, (CHUNK, CHUNK), 0)
    c_i = lax.broadcasted_iota(jnp.int32, (CHUNK, CHUNK), 1)
    upper = r_i <= c_i

    def chunk(c, carry):
        rows = pl.ds(pl.multiple_of(c * CHUNK, CHUNK), CHUNK)
        for sq in range(batch):
            g = gif_refs[sq][rows, :]
            fc = jnp.dot(ltri, g, preferred_element_type=f32, precision=HIGHEST)
            g_t = g.T
            fc_t = fc.T
            q_all = q_refs[sq][rows, :]
            k_all = k_refs[sq][rows, :] * (DK ** -0.5)
            n_all = n_scr[sq]
            for h in range(HEADS):
                i_col = g[:, L_MI + h:L_MI + h + 1]
                f_col = fc[:, L_MF + h:L_MF + h + 1]
                i_row = g_t[L_MI + h:L_MI + h + 1, :]
                f_row = fc_t[L_MF + h:L_MF + h + 1, :]
                f_last = fc[CHUNK - 1:CHUNK, L_MF + h:L_MF + h + 1]
                m_prev = m_scr[sq, h:h + 1, 0:1]
                c_t = ct_scr[sq, h]
                q = q_all[:, h * DK:(h + 1) * DK]
                k = k_all[:, h * DK:(h + 1) * DK]
                v_t = v_refs[sq][rows, pl.ds(h * DV, DV)].T

                log_d = jnp.where(upper, f_row + (i_col - f_col), -jnp.inf)
                m_inter = m_prev + f_row
                m_row = jnp.maximum(m_inter, jnp.max(log_d, axis=0, keepdims=True))
                w_inter = jnp.exp(m_inter - m_row)
                sc = _dot_nt(k, q) * jnp.exp(log_d - m_row)
                num = w_inter * _dot_nt(c_t, q) + _dot(v_t, sc)
                qn = _dot_nt(n_all, q)[h:h + 1, :]
                den = w_inter * qn + jnp.sum(sc, axis=0, keepdims=True)
                h_t = num / jnp.maximum(jnp.abs(den), jnp.exp(-m_row))
                h_ref[sq, rows, pl.ds(h * DV, DV)] = h_t.T

                log_w = f_last - f_col + i_col
                m_new = jnp.maximum(m_prev + f_last, jnp.max(log_w, axis=0, keepdims=True))
                wk = jnp.exp(log_w - m_new) * k
                decay = jnp.exp(m_prev + f_last - m_new)
                ct_scr[sq, h] = decay * c_t + _dot(v_t, wk)
                n_scr[sq, h:h + 1, :] = decay * n_all[h:h + 1, :] + jnp.sum(wk, axis=0, keepdims=True)
                m_scr[sq, h:h + 1, :] = jnp.broadcast_to(m_new, (1, LANES))
        return carry

    lax.fori_loop(0, n_chunks, chunk, 0)

    @pl.when(blk == pl.num_programs(0) - 1)
    def _():
        for sq in range(batch):
            for h in range(HEADS):
                c_out_ref[sq, h] = ct_scr[sq, h].T
        n_out_ref[...] = n_scr[:, 0:HEADS, :]
        m_out_ref[...] = m_scr[:, 0:HEADS, :]


def _mlstm_prompt(proj, gif, batch, seq):
    nb = seq // TM
    qw_blocks = TILE // QW
    return pl.pallas_call(
        functools.partial(_mlstm_prompt_kernel, n_chunks=TM // CHUNK, batch=batch),
        grid=(nb,),
        in_specs=[
            *_stream_specs(QW, batch, nb, T_MQK * qw_blocks),
            *_stream_specs(QW, batch, nb, T_MQK * qw_blocks + 1),
            *_stream_specs(VW, batch, nb, T_MV),
            *_stream_specs(LANES, batch, nb, 0),
        ],
        out_specs=[
            pl.BlockSpec((batch, TM, VW), lambda i: (0, i, 0)),
            pl.BlockSpec((batch, HEADS, DK, DV), lambda i: (0, 0, 0, 0)),
            pl.BlockSpec((batch, HEADS, DK), lambda i: (0, 0, 0)),
            pl.BlockSpec((batch, HEADS, LANES), lambda i: (0, 0, 0)),
        ],
        out_shape=[
            jax.ShapeDtypeStruct((batch, seq, VW), f32),
            jax.ShapeDtypeStruct((batch, HEADS, DK, DV), f32),
            jax.ShapeDtypeStruct((batch, HEADS, DK), f32),
            jax.ShapeDtypeStruct((batch, HEADS, LANES), f32),
        ],
        scratch_shapes=[
            pltpu.VMEM((batch, HEADS, DV, DK), f32),
            pltpu.VMEM((batch, SUBLANES, DK), f32),
            pltpu.VMEM((batch, SUBLANES, LANES), f32),
        ],
        compiler_params=_cparams("arbitrary"),
        name="mlstm_prompt",
    )(*([proj] * (3 * batch)), *([gif] * batch))


def _diff_lambda_value(lam_ref, lam_init):
    lam = lam_ref[...]
    s1 = jnp.sum(lam[0:1] * lam[1:2], axis=1, keepdims=True)
    s2 = jnp.sum(lam[2:3] * lam[3:4], axis=1, keepdims=True)
    return jnp.exp(s1) - jnp.exp(s2) + lam_init


def _attn_prompt_step(p, qi_ref, ki_ref, q_ref, k_ref, v_ref, lam_ref, o_ref, q2_scr, m_scr, l_scr, acc_scr,
                      *, tq, lam_init, tail):
    qi = qi_ref[p]
    ki = ki_ref[p]
    pair = lambda j: slice(j * LANES, (j + 1) * LANES)

    @pl.when(ki == 0)
    def _():
        for j in range(HEADS):
            q = q_ref[:, pair(j)] * (DK ** -0.5 * LOG2E)
            lane = lax.broadcasted_iota(jnp.int32, q.shape, 1)
            q2_scr[j, 0:tq, :] = jnp.where(lane < DK, q, 0.0).astype(bf16)
            q2_scr[j, tq:2 * tq, :] = jnp.where(lane >= DK, q, 0.0).astype(bf16)
        m_scr[...] = jnp.full_like(m_scr, -jnp.inf)
        l_scr[...] = jnp.zeros_like(l_scr)
        acc_scr[...] = jnp.zeros_like(acc_scr)

    def step(masked):
        sts = [_dot_nt(k_ref[:, pair(j)], q2_scr[j]) for j in range(HEADS)]
        pts, alphas = [], []
        for j in range(HEADS):
            st = sts[j]
            if masked:
                r = lax.broadcasted_iota(jnp.int32, st.shape, 0)
                c = lax.broadcasted_iota(jnp.int32, st.shape, 1)
                c = jnp.where(c >= tq, c - tq, c)
                st = jnp.where(r <= c, st, -jnp.inf)
            m_prev = m_scr[j]
            m_new = jnp.maximum(m_prev, jnp.max(st, axis=0, keepdims=True))
            alpha = jnp.exp2(m_prev - m_new)
            pt = jnp.exp2(st - m_new)
            l_scr[j] = alpha * l_scr[j] + jnp.sum(pt, axis=0, keepdims=True)
            m_scr[j] = m_new
            pts.append(pt.astype(bf16))
            alphas.append(alpha)
        for j in range(HEADS):
            acc_scr[j] = alphas[j] * acc_scr[j] + _dot_tn(v_ref[:, pair(j)], pts[j])

    @pl.when(ki < qi)
    def _():
        step(False)
        tail()

    @pl.when(ki == qi)
    def _():
        step(True)
        lam = _diff_lambda_value(lam_ref, lam_init)
        for j in range(HEADS):
            ot = acc_scr[j] / l_scr[j]
            o_ref[:, pair(j)] = (ot[:, 0:tq] - lam * ot[:, tq:2 * tq]).T
        tail()


def _sample_page_copies(seq_idx, slot, pt_ref, ck_hbm, cv_hbm, kbuf, vbuf, sem, *, layer, n_pages, page):
    out = []
    for pg in range(n_pages):
        src = pt_ref[seq_idx, pg]
        out.append(pltpu.make_async_copy(ck_hbm.at[layer, src], kbuf.at[slot, :, pl.ds(pg * page, page)],
                                         sem.at[0, slot]))
        out.append(pltpu.make_async_copy(cv_hbm.at[layer, src],
                                         vbuf.at[slot, pl.ds(pg * page * HEADS, page * HEADS), :], sem.at[1, slot]))
    return out


def _attn_sample_pages(step, n_steps, pt_ref, ck_hbm, cv_hbm, o_ref, kbuf, vbuf, sem, *, nseq, pages):
    past = pages["n_pages"] * pages["page"]
    seq = lambda st: jnp.minimum(st, nseq - 1)
    slot = step % 2
    copies = functools.partial(_sample_page_copies, pt_ref=pt_ref, ck_hbm=ck_hbm, cv_hbm=cv_hbm, kbuf=kbuf,
                               vbuf=vbuf, sem=sem, **pages)

    @pl.when(step == 0)
    def _():
        kbuf[:, :, pl.ds(past, LANES)] = jnp.zeros((2, kbuf.shape[1], LANES), f32)
        vbuf[:, pl.ds(past * HEADS, LANES * HEADS), :] = jnp.zeros((2, LANES * HEADS, DV), f32)
        o_ref[...] = jnp.zeros_like(o_ref)
        for cp in copies(0, 0):
            cp.start()

    @pl.when(step + 1 < n_steps)
    def _():
        for cp in copies(seq(step + 1), 1 - slot):
            cp.start()

    for cp in copies(seq(step), slot):
        cp.wait()


def _attn_sample_compute(step, q_ref, kts_ref, v_ref, lam_ref, o_ref, kbuf, vbuf,
                         *, n_pages, page, dec_seq, nseq, lam_init):
    b = jnp.minimum(step, nseq - 1)
    slot = step % 2
    past = n_pages * page
    n_qh = 2 * HEADS

    shift = lax.rem(nseq - b, nseq)
    b8 = (b // SUBLANES) * SUBLANES
    groups = [pl.ds(pl.multiple_of(t * nseq + b8, SUBLANES), SUBLANES) for t in range(dec_seq)]
    mine = lax.broadcasted_iota(jnp.int32, (SUBLANES, 2 * QW), 0) == b - b8
    pick = lambda ref, t: jnp.sum(jnp.where(mine, ref[groups[t], :], 0.0), axis=0, keepdims=True)
    head_of_lane = lax.broadcasted_iota(jnp.int32, (n_qh, 2 * QW), 1) // DK
    head_of_row = lax.broadcasted_iota(jnp.int32, (n_qh, 2 * QW), 0)
    q_rows = []
    for t in range(dec_seq):
        kbuf[slot, :, pl.ds(past + t, 1)] = pltpu.roll(kts_ref[t], shift, axis=1)[:, 0:1]
        v_row = pick(v_ref, t)
        for h in range(HEADS):
            vbuf[slot, pl.ds((past + t) * HEADS + h, 1), :] = v_row[:, h * DV:(h + 1) * DV]
        q = jnp.broadcast_to(pick(q_ref, t) * (DK ** -0.5), (n_qh, 2 * QW))
        q_rows.append(jnp.where(head_of_lane == head_of_row, q, 0.0))
    qbd = jnp.concatenate(q_rows, axis=0)

    s = _dot(qbd, kbuf[slot])
    r = lax.broadcasted_iota(jnp.int32, s.shape, 0)
    c = lax.broadcasted_iota(jnp.int32, s.shape, 1)
    s = jnp.where(c - past <= r // n_qh, s, -jnp.inf)
    m = jnp.max(s, axis=1, keepdims=True)
    pr = jnp.exp(s - m)
    inv_l = 1.0 / jnp.sum(pr, axis=1, keepdims=True)
    lam = _diff_lambda_value(lam_ref, lam_init)
    mine_v = mine[:, 0:DV]
    for hv in range(HEADS):
        v_h = vbuf[slot, pl.ds(hv, past + LANES, stride=HEADS), :]
        o_h = _dot(pr, v_h) * inv_l
        for t in range(dec_seq):
            r1 = t * n_qh + 2 * hv
            d = o_h[r1:r1 + 1] - lam * o_h[r1 + 1:r1 + 2]
            lanes = slice(hv * DV, (hv + 1) * DV)
            o_ref[groups[t], lanes] = jnp.where(mine_v, d, o_ref[groups[t], lanes])


def _attn_kernel(qi_ref, ki_ref, pt_ref, q_ref, k_ref, v_ref, lam_ref, qs_ref, kts_ref, vs_ref, ck_hbm, cv_hbm,
                 o_ref, os_ref, q2_scr, m_scr, l_scr, acc_scr, kbuf, vbuf, sem, *, prompt, sample, nseq, n_pairs):
    p = pl.program_id(1)
    step = pl.program_id(0) * pl.num_programs(1) + p
    n_steps = pl.num_programs(0) * pl.num_programs(1)
    pages = dict(layer=sample["layer"], n_pages=sample["n_pages"], page=sample["page"])
    _attn_sample_pages(step, n_steps, pt_ref, ck_hbm, cv_hbm, os_ref, kbuf, vbuf, sem, nseq=nseq, pages=pages)

    def sample_work():
        _attn_sample_compute(step, qs_ref, kts_ref, vs_ref, lam_ref, os_ref, kbuf, vbuf, nseq=nseq,
                             n_pages=sample["n_pages"], page=sample["page"], dec_seq=sample["dec_seq"],
                             lam_init=sample["lam_init"])

    @pl.when(p < n_pairs)
    def _():
        _attn_prompt_step(p, qi_ref, ki_ref, q_ref, k_ref, v_ref, lam_ref, o_ref,
                          q2_scr, m_scr, l_scr, acc_scr, tail=sample_work, **prompt)

    @pl.when(p >= n_pairs)
    def _():
        sample_work()


def _attention(proj, k_ts, page_table, diff_lambda, cache_kt, cache_v2, layer, lam_init, batch, seq, sample_block,
               tq=512):
    nq = seq // tq
    pairs = [(q, k) for q in range(nq) for k in range(q + 1)]
    qi_tab = jnp.asarray(np.array([a for a, _ in pairs], np.int32))
    ki_tab = jnp.asarray(np.array([b for _, b in pairs], np.int32))
    nseq, n_pages = page_table.shape
    n_pairs = len(pairs)
    steps = max(n_pairs, -(-nseq // batch))
    last = n_pairs - 1
    dec_seq = k_ts.shape[0]
    rows = dec_seq * nseq
    page = cache_kt.shape[3]
    cols = n_pages * page + LANES
    grid_spec = pltpu.PrefetchScalarGridSpec(
        num_scalar_prefetch=3,
        grid=(batch, steps),
        in_specs=[
            pl.BlockSpec((tq, TILE), lambda b, p, qt, kt, pt: (b * nq + qt[jnp.minimum(p, last)], T_DQ)),
            pl.BlockSpec((tq, TILE), lambda b, p, qt, kt, pt: (b * nq + kt[jnp.minimum(p, last)], T_DK)),
            pl.BlockSpec((tq, TILE), lambda b, p, qt, kt, pt: (b * nq + kt[jnp.minimum(p, last)], T_DV)),
            pl.BlockSpec((4, DK), lambda b, p, qt, kt, pt: (0, 0)),
            pl.BlockSpec((rows, TILE), lambda b, p, qt, kt, pt: (sample_block, T_DQ)),
            pl.BlockSpec((dec_seq, TILE, nseq), lambda b, p, qt, kt, pt: (0, 0, 0)),
            pl.BlockSpec((rows, TILE), lambda b, p, qt, kt, pt: (sample_block, T_DV)),
            pl.BlockSpec(memory_space=pl.ANY),
            pl.BlockSpec(memory_space=pl.ANY),
        ],
        out_specs=[
            pl.BlockSpec((tq, VW), lambda b, p, qt, kt, pt: (b * nq + qt[jnp.minimum(p, last)], 0)),
            pl.BlockSpec((rows, VW), lambda b, p, qt, kt, pt: (0, 0)),
        ],
        scratch_shapes=[
            pltpu.VMEM((HEADS, 2 * tq, LANES), bf16),
            pltpu.VMEM((HEADS, 1, 2 * tq), f32),
            pltpu.VMEM((HEADS, 1, 2 * tq), f32),
            pltpu.VMEM((HEADS, DV, 2 * tq), f32),
            pltpu.VMEM((2, 2 * QW, cols), f32),
            pltpu.VMEM((2, cols * HEADS, DV), f32),
            pltpu.SemaphoreType.DMA((2, 2)),
        ],
    )
    return pl.pallas_call(
        functools.partial(_attn_kernel, nseq=nseq, n_pairs=n_pairs, prompt=dict(tq=tq, lam_init=lam_init),
                          sample=dict(layer=layer, n_pages=n_pages, page=page, dec_seq=dec_seq, lam_init=lam_init)),
        grid_spec=grid_spec,
        out_shape=[jax.ShapeDtypeStruct((batch * seq, VW), f32), jax.ShapeDtypeStruct((rows, VW), f32)],
        compiler_params=_cparams("arbitrary", "arbitrary"),
        name="attention",
    )(qi_tab, ki_tab, page_table, proj, proj, proj, diff_lambda, proj, k_ts, proj, cache_kt, cache_v2)


def _head_of_lane(shape):
    return lax.broadcasted_iota(jnp.int32, shape, 1) // DK


def _expand_heads(cols, base, head):
    out = cols[:, base + HEADS - 1:base + HEADS]
    for h in range(HEADS - 2, -1, -1):
        out = jnp.where(head == h, cols[:, base + h:base + h + 1], out)
    return out


def _sample_prep_kernel(gq_ref, gk_ref, la_ref, mq_ref, mk_ref, gif_ref, n0_ref, m0_ref,
                        ga_ref, gkk_ref, gqq_ref, ma_ref, mkk_ref, mqq_ref, n_out_ref, m_out_ref, *, nseq, dec_seq):
    head = _head_of_lane((nseq, QW))
    m = _expand_heads(m0_ref[...], 0, head)
    n = n0_ref[...]
    for t in range(dec_seq):
        rows = pl.ds(t * nseq, nseq)
        ga_ref[t] = jnp.exp(la_ref[rows, :]).T
        gkk_ref[t] = gk_ref[rows, :].T
        gqq_ref[t] = (gq_ref[rows, :] * (DK ** -0.5)).T
        gates = gif_ref[rows, :]
        i_e = _expand_heads(gates, L_MI, head)
        f_e = _expand_heads(gates, L_MF, head)
        k = mk_ref[rows, :] * (DK ** -0.5)
        q = mq_ref[rows, :]
        m_new = jnp.maximum(f_e + m, i_e)
        fp = jnp.exp(f_e + m - m_new)
        ip = jnp.exp(i_e - m_new)
        n = fp * n + ip * k
        nq = n * q
        den = jnp.zeros_like(nq)
        for h in range(HEADS):
            den = jnp.where(head == h, jnp.sum(jnp.where(head == h, nq, 0.0), axis=1, keepdims=True), den)
        inv = 1.0 / jnp.maximum(jnp.abs(den), jnp.exp(-m_new))
        ma_ref[t] = fp.T
        mkk_ref[t] = (ip * k).T
        mqq_ref[t] = (q * inv).T
        m = m_new
    n_out_ref[...] = n
    m_out_ref[...] = m


def _sample_prep(proj, la, gif, state_n, state_m, layer, sample_block, nseq, dec_seq):
    rows = nseq * dec_seq
    qw_blocks = TILE // QW
    tok = jax.ShapeDtypeStruct((dec_seq, QW, nseq), f32)
    st = jax.ShapeDtypeStruct((nseq, QW), f32)
    tok_spec = pl.BlockSpec((dec_seq, QW, nseq), lambda i: (0, 0, 0))
    st_spec = pl.BlockSpec((nseq, QW), lambda i: (0, 0))
    col = lambda c: pl.BlockSpec((rows, QW), lambda i: (sample_block, c))
    return pl.pallas_call(
        functools.partial(_sample_prep_kernel, nseq=nseq, dec_seq=dec_seq),
        grid=(1,),
        in_specs=[
            col(T_GQK * qw_blocks), col(T_GQK * qw_blocks + 1), col(0),
            col(T_MQK * qw_blocks), col(T_MQK * qw_blocks + 1),
            pl.BlockSpec((rows, LANES), lambda i: (sample_block, 0)),
            pl.BlockSpec((None, nseq, QW), lambda i: (layer, 0, 0)),
            pl.BlockSpec((None, nseq, HEADS), lambda i: (layer, 0, 0)),
        ],
        out_specs=[tok_spec] * 6 + [st_spec] * 2,
        out_shape=[tok] * 6 + [st] * 2,
        compiler_params=_cparams("arbitrary"),
        name="sample_prep",
    )(proj, proj, la, proj, proj, gif, state_n, state_m)


SEQ_PER_STEP = 8


def _sample_state_kernel(a_ref, k_ref, q_ref, v_ref, s_ref, o_ref, s_out_ref, *, dec_seq, nseq):
    blk = pl.program_id(0)
    seq0 = blk * SEQ_PER_STEP
    shift = lax.rem(nseq - seq0, nseq)
    a_t = [pltpu.roll(a_ref[t], shift, axis=1) for t in range(dec_seq)]
    k_t = [pltpu.roll(k_ref[t], shift, axis=1) for t in range(dec_seq)]
    q_t = [pltpu.roll(q_ref[t], shift, axis=1) for t in range(dec_seq)]
    toks = [pl.ds(pl.multiple_of(t * nseq + seq0, SEQ_PER_STEP), SEQ_PER_STEP) for t in range(dec_seq)]
    v_t = [v_ref[toks[t], :] for t in range(dec_seq)]
    row = lax.broadcasted_iota(jnp.int32, (SEQ_PER_STEP, DV), 0)
    outs = [[jnp.zeros((SEQ_PER_STEP, DV), f32) for _ in range(HEADS)] for _ in range(dec_seq)]
    for bb in range(SEQ_PER_STEP):
        for h in range(HEADS):
            s = s_ref[bb, h]
            rows = slice(h * DK, (h + 1) * DK)
            for t in range(dec_seq):
                a = a_t[t][rows, bb:bb + 1]
                k = k_t[t][rows, bb:bb + 1]
                q = q_t[t][rows, bb:bb + 1]
                v = v_t[t][bb:bb + 1, h * DV:(h + 1) * DV]
                s = a * s + k * v
                outs[t][h] = jnp.where(row == bb, jnp.sum(q * s, axis=0, keepdims=True), outs[t][h])
            s_out_ref[bb, h] = s
    for t in range(dec_seq):
        for h in range(HEADS):
            o_ref[toks[t], h * DV:(h + 1) * DV] = outs[t][h]


def _sample_state(a_t, k_t, q_t, proj, v_tile, sample_block, state, layer, dec_seq):
    nseq = a_t.shape[2]
    rows = dec_seq * nseq
    full = pl.BlockSpec((dec_seq, QW, nseq), lambda i: (0, 0, 0))
    return pl.pallas_call(
        functools.partial(_sample_state_kernel, dec_seq=dec_seq, nseq=nseq),
        grid=(nseq // SEQ_PER_STEP,),
        in_specs=[
            full, full, full,
            pl.BlockSpec((rows, VW), lambda i: (sample_block, v_tile)),
            pl.BlockSpec((None, SEQ_PER_STEP, HEADS, DK, DV), lambda i: (layer, i, 0, 0, 0)),
        ],
        out_specs=[
            pl.BlockSpec((rows, VW), lambda i: (0, 0)),
            pl.BlockSpec((SEQ_PER_STEP, HEADS, DK, DV), lambda i: (i, 0, 0, 0)),
        ],
        out_shape=[
            jax.ShapeDtypeStruct((rows, VW), f32),
            jax.ShapeDtypeStruct(state.shape[1:], f32),
        ],
        compiler_params=_cparams("arbitrary"),
        name="sample_state",
    )(a_t, k_t, q_t, proj, state)


def _head_rms(x, g):
    parts = []
    for h in range(HEADS):
        parts.append(_rms(x[:, h * DV:(h + 1) * DV], g))
    return jnp.concatenate(parts, axis=1)


FF_SLABS = ((0, 768), (768, 768), (1536, 768), (2304, 512))
TM_CH = 256


def _merge_ffn_kernel(hp_ref, hs_ref, oap_ref, oas_ref, omp_ref, oms_ref, odp_ref, ods_ref, gr_ref, mo_ref,
                      g0_ref, g1_ref, g2_ref, gg_ref, gm_ref, gd_ref, wa_ref, wm_ref, wd_ref, wo_ref,
                      gf_ref, wg_ref, wu_ref, wdn_ref, pp_ref, ps_ref, gp_ref, wpg_ref, wpp_ref, gfin_ref,
                      *out_refs, lam_init, final, n_prompt_blocks):
    pick = functools.partial(_pick_rows, n_prompt_blocks)
    gr = gr_ref[...]
    a = _head_rms(pick(oap_ref, oas_ref), gg_ref[...]) * (gr * _sigmoid(gr))
    m = _sigmoid(mo_ref[...]) * _head_rms(pick(omp_ref, oms_ref), gm_ref[...])
    d = _head_rms(pick(odp_ref, ods_ref), gd_ref[...]) * (1.0 - lam_init)
    merged = (g0_ref[...] * _dot(a, wa_ref[...]) + g1_ref[...] * _dot(m, wm_ref[...])
              + g2_ref[...] * _dot(d, wd_ref[...]))
    h = pick(hp_ref, hs_ref) + _dot(merged, wo_ref[...])

    xf = _rms(h, gf_ref[...]).astype(bf16)
    h2 = h
    for lo, width in FF_SLABS:
        gate = jnp.dot(xf, wg_ref[:, lo:lo + width], preferred_element_type=f32)
        up = jnp.dot(xf, wu_ref[:, lo:lo + width], preferred_element_type=f32)
        h2 = h2 + _dot(gate * _sigmoid(gate) * up, wdn_ref[lo:lo + width, :])
    ple_gate = _sigmoid(_dot(_rms(h2, gp_ref[...]), wpg_ref[...]))
    h3 = h2 + ple_gate * _dot(pick(pp_ref, ps_ref), wpp_ref[...])
    if not final:
        out_refs[0][...] = h3
        return
    y = _rms(h3, gfin_ref[...])
    yp_ref, ys_ref = out_refs
    i = pl.program_id(0)

    @pl.when(i < n_prompt_blocks)
    def _():
        yp_ref[...] = y

    @pl.when(i >= n_prompt_blocks)
    def _():
        ys_ref[...] = y


def _merge_ffn(h_p, h_s, h_s_row0, o_a, o_m, o_d, proj, gates, g_gla, g_ml, g_diff, w_a, w_m, w_d, w_out, lam_init,
               g_ffn, w_g, w_u, w_dn, p_p, p_s, p_row0, g_ple, w_pg, w_pp, g_final, final, n_prompt_rows, n_sample_rows):
    tm = TM_CH
    npb, nsb = n_prompt_rows // tm, n_sample_rows // tm
    T = n_prompt_rows + n_sample_rows
    row = lambda i: (i, 0)
    const = lambda i: (0, 0)
    resident = lambda shape: pl.BlockSpec(shape, const, pipeline_mode=pl.Buffered(1))
    rows2 = lambda width, sample_block=0, prompt_block0=0: _row_specs(width, npb, sample_block, tm, nsb, prompt_block0)
    assert FF_SLABS[-1][0] + FF_SLABS[-1][1] == D_FF
    if final:
        out_specs = [pl.BlockSpec((tm, D_MODEL), lambda i: (jnp.minimum(i, npb - 1), 0)),
                     pl.BlockSpec((tm, D_MODEL), lambda i: (jnp.clip(i - npb, 0, nsb - 1), 0))]
        out_shape = [jax.ShapeDtypeStruct((n_prompt_rows, D_MODEL), f32),
                     jax.ShapeDtypeStruct((n_sample_rows, D_MODEL), f32)]
    else:
        out_specs = pl.BlockSpec((tm, D_MODEL), row)
        out_shape = jax.ShapeDtypeStruct((T, D_MODEL), f32)
    return pl.pallas_call(
        functools.partial(_merge_ffn_kernel, lam_init=lam_init, final=final, n_prompt_blocks=npb),
        grid=(npb + nsb,),
        in_specs=[
            *rows2(D_MODEL, sample_block=h_s_row0 // tm),
            *rows2(VW), *rows2(VW), *rows2(VW),
            pl.BlockSpec((tm, VW), lambda i: (i, T_GR)),
            pl.BlockSpec((tm, VW), lambda i: (i, T_MO)),
            pl.BlockSpec((tm, D_MODEL), lambda i: (i, 0)),
            pl.BlockSpec((tm, D_MODEL), lambda i: (i, 1)),
            pl.BlockSpec((tm, D_MODEL), lambda i: (i, 2)),
            pl.BlockSpec((1, DV), const),
            pl.BlockSpec((1, DV), const),
            pl.BlockSpec((1, DV), const),
            resident((VW, D_MODEL)), resident((VW, D_MODEL)), resident((VW, D_MODEL)),
            resident((D_MODEL, D_MODEL)),
            pl.BlockSpec((1, D_MODEL), const),
            resident((D_MODEL, D_FF)), resident((D_MODEL, D_FF)), resident((D_FF, D_MODEL)),
            *rows2(PLE_DIM, prompt_block0=p_row0 // tm),
            pl.BlockSpec((1, D_MODEL), const),
            resident((D_MODEL, D_MODEL)),
            resident((PLE_DIM, D_MODEL)),
            pl.BlockSpec((1, D_MODEL), const),
        ],
        out_specs=out_specs,
        out_shape=out_shape,
        compiler_params=_cparams("arbitrary"),
        name="merge_ffn",
    )(h_p, h_s, *o_a, *o_m, *o_d, proj, proj, gates, gates, gates, g_gla, g_ml, g_diff, w_a, w_m, w_d, w_out,
      g_ffn, w_g, w_u, w_dn, p_p, p_s, g_ple, w_pg, w_pp, g_final)


def _rope_tables(positions):
    half = ROT_DIM // 2
    inv_freq = ROPE_THETA ** (-jnp.arange(half, dtype=f32) * 2.0 / ROT_DIM)
    ang = positions.astype(f32)[:, None] * inv_freq[None, :]
    cos, sin = jnp.cos(ang), jnp.sin(ang)
    n = positions.shape[0]
    one = jnp.ones((n, DK - ROT_DIM), f32)
    zero = jnp.zeros((n, DK - ROT_DIM), f32)
    zh = jnp.zeros((n, half), f32)
    cos_h = jnp.concatenate([cos, cos, one], axis=1)
    up_h = jnp.concatenate([-sin, zh, zero], axis=1)
    dn_h = jnp.concatenate([zh, sin, zero], axis=1)
    rep = lambda a: jnp.concatenate([a] * (LANES // DK), axis=1)
    return rep(cos_h), rep(up_h), rep(dn_h)


def _prep_in_weights(w_in_l, w_gate_l):
    o = np.cumsum((0, QW, QW, VW, VW, GLA_RANK, QW, QW, VW, VW, 2 * HEADS, 2 * QW, 2 * QW, VW))
    seg = lambda i: w_in_l[:, int(o[i]):int(o[i + 1])]
    gq, gk, gv, gr, glr, mq, mk, mv, mo, mif, dq, dk, dv = (seg(i) for i in range(13))
    main = jnp.concatenate([w_gate_l, gq, gk, gv, gr, mq, mk, mv, mo, dq, dk, dv], axis=1).astype(bf16)
    pad = jnp.zeros((D_MODEL, LANES - GLA_RANK - 2 * HEADS), f32)
    small = jnp.concatenate([glr, mif, pad], axis=1).astype(bf16)
    return main, small


def kernel(x_prompt, x_sample, cache_k, cache_v, state_gla, state_mlstm_c, state_mlstm_n, state_mlstm_m, page_table, p_prompt, p_sample, g_mix, w_in, w_gla_gk, b_gla_gk, g_gla_norm, b_mlstm_if, g_mlstm_norm, diff_lambda, g_diff_norm, w_branch, w_gate, w_out, g_ffn, w_ffn_gate, w_ffn_up, w_ffn_down, g_ple, w_ple_gate, w_ple_proj, g_final):
    bp, lp, _ = x_prompt.shape
    bs, ls, _ = x_sample.shape
    depth = g_mix.shape[0]
    n_pages = page_table.shape[1]
    page = cache_k.shape[2]
    past = n_pages * page
    tp = bp * lp
    ts = bs * ls
    assert lp % TM == 0 and ts == TM and bs == LANES and page == LANES and TM_IN % bs == 0
    npb = tp // TM

    def to_step_major(a):
        return jnp.swapaxes(a, 0, 1).reshape((ts,) + a.shape[2:])

    def from_step_major(a):
        return jnp.swapaxes(a.reshape((ls, bs) + a.shape[1:]), 0, 1)

    h_p, h_s, h_s_row0 = x_prompt.reshape(tp, D_MODEL), to_step_major(x_sample), 0

    pos_rows = jnp.concatenate([jnp.arange(lp, dtype=jnp.int32),
                                past + jnp.repeat(jnp.arange(ls, dtype=jnp.int32), bs)])
    cos_t, sup_t, sdn_t = _rope_tables(pos_rows)

    ck = jnp.transpose(cache_k, (0, 1, 3, 4, 2)).reshape(cache_k.shape[0], cache_k.shape[1], 2 * QW, page)
    cv = cache_v.reshape(cache_v.shape[0], cache_v.shape[1], page * HEADS, DV)
    state_n = state_mlstm_n.reshape(depth, bs, QW)
    p_rows = p_prompt.reshape(depth * tp, PLE_DIM)
    outs_p, outs_s = [], []
    for l in range(depth):
        lam_init = 0.8 - 0.6 * math.exp(-0.3 * l)
        w_main, w_small = _prep_in_weights(w_in[l], w_gate[l])
        w_gk = jnp.concatenate([w_gla_gk[l], jnp.zeros((LANES - GLA_RANK, QW), f32)], axis=0)
        b_if = jnp.zeros((1, LANES), f32).at[0, L_MI:L_MI + 2 * HEADS].set(b_mlstm_if[l])
        gates, proj, k_t, v_p, k_ts, la, gif = _inproj(h_p, h_s, h_s_row0 // TM_IN, g_mix[l][None], w_main, w_small, w_gk,
                                                  b_gla_gk[l][None], b_if, cos_t, sup_t, sdn_t, bp, lp, bs, ls)

        oa_p, gla_p = _gla_prompt(proj, la, bp, lp)
        om_p, c_p, n_p, m_p = _mlstm_prompt(proj, gif, bp, lp)
        oa_p, om_p = oa_p.reshape(tp, VW), om_p.reshape(tp, VW)
        od_p, od_s = _attention(proj, k_ts, page_table, diff_lambda[l], ck, cv, l, lam_init, bp, lp, npb)

        ga, gk_, gq_, ma, mk_, mq_, n_s, m_s = _sample_prep(proj, la, gif, state_n, state_mlstm_m, l, npb, bs, ls)
        oa_s, gla_s = _sample_state(ga, gk_, gq_, proj, T_GV, npb, state_gla, l, ls)
        om_s, c_s = _sample_state(ma, mk_, mq_, proj, T_MV, npb, state_mlstm_c, l, ls)
        v_s = from_step_major(proj[tp:, T_DV * TILE:(T_DV + 1) * TILE])

        wb = w_branch[l].astype(bf16)
        h = _merge_ffn(h_p, h_s, h_s_row0, (oa_p, oa_s), (om_p, om_s), (od_p, od_s), proj, gates,
                       g_gla_norm[l][None], g_mlstm_norm[l][None], g_diff_norm[l][None],
                       wb[:VW], wb[VW:2 * VW], wb[2 * VW:], w_out[l].astype(bf16), lam_init,
                       g_ffn[l][None], w_ffn_gate[l].astype(bf16), w_ffn_up[l].astype(bf16),
                       w_ffn_down[l].astype(bf16), p_rows, to_step_major(p_sample[l]), l * tp,
                       g_ple[l][None], w_ple_gate[l].astype(bf16), w_ple_proj[l].astype(bf16), g_final[None],
                       l == depth - 1, tp, ts)
        h_p, h_s, h_s_row0 = h, h, tp

        k_p = k_t.reshape(bp, 2 * HEADS, DK, lp).transpose(0, 3, 1, 2)
        k_new = k_ts.reshape(ls, 2 * HEADS, DK, bs).transpose(3, 0, 1, 2)
        outs_p.append((k_p, v_p.reshape(bp, lp, HEADS, DV),
                       gla_p.reshape(bp, HEADS, DK, DV), c_p, n_p, m_p[:, :, 0]))
        outs_s.append((k_new, v_s.reshape(bs, ls, HEADS, DV),
                       gla_s, c_s, n_s.reshape(bs, HEADS, DK), m_s[:, ::DK]))

    y_prompt = h[0].reshape(bp, lp, D_MODEL)
    y_sample = from_step_major(h[1])
    stack = lambda items: [jnp.stack(t) for t in zip(*items)]
    return tuple([y_prompt, y_sample] + stack(outs_p) + stack(outs_s))
```

```python
import functools
import math

import numpy as np
import jax
import jax.numpy as jnp
from jax import lax
from jax.experimental import pallas as pl
from jax.experimental.pallas import tpu as pltpu

f32 = jnp.float32
bf16 = jnp.bfloat16
HIGHEST = lax.Precision.HIGHEST

D_MODEL = 1024
PLE_DIM = 256
HEADS = 4
DK = 64
DV = 128
GLA_RANK = 16
GLA_GATE_NORM = 16.0
ROT_DIM = 16
ROPE_THETA = 500000.0
CHUNK = 64
CHUNK_MLSTM = 256
D_FF = 2816
EPS = 1e-6
LOG2E = 1.4426950408889634
VW = HEADS * DV
QW = HEADS * DK

LANES = 128
SUBLANES = 8
VMEM_LIMIT = 48 * 1024 * 1024

TILE = 512
GATE_W = 3 * D_MODEL
T_GQK, T_GV, T_GR = 0, 1, 2
T_MQK, T_MV, T_MO = 3, 4, 5
T_DQ, T_DK, T_DV = 6, 7, 8
N_TILES = 9
P_MIX = N_TILES * TILE
P_MAIN = GATE_W + P_MIX
L_GLR = 0
L_MI = 16
L_MF = 20

TM = 512


def _cparams(*sem):
    return pltpu.CompilerParams(dimension_semantics=sem, vmem_limit_bytes=VMEM_LIMIT)


def _log_sigmoid(x):
    return jnp.minimum(x, 0.0) - jnp.log1p(jnp.exp(-jnp.abs(x)))


def _sigmoid(x):
    return 0.5 * jnp.tanh(0.5 * x) + 0.5


def _rms(x, g):
    return x * lax.rsqrt(jnp.mean(x * x, axis=-1, keepdims=True) + EPS) * g


def _dot(a, b):
    return jnp.dot(a.astype(bf16), b.astype(bf16), preferred_element_type=f32)


def _dot_nt(a, b):
    return lax.dot_general(a.astype(bf16), b.astype(bf16), (((1,), (1,)), ((), ())), preferred_element_type=f32)


def _dot_tn(a, b):
    return lax.dot_general(a.astype(bf16), b.astype(bf16), (((0,), (0,)), ((), ())), preferred_element_type=f32)


def _rope_tile(x, cos, sin_up, sin_dn):
    parts = []
    for c in range(x.shape[1] // LANES):
        xc = x[:, c * LANES:(c + 1) * LANES]
        parts.append(xc * cos + pltpu.roll(xc, LANES - ROT_DIM // 2, axis=1) * sin_up
                     + pltpu.roll(xc, ROT_DIM // 2, axis=1) * sin_dn)
    return jnp.concatenate(parts, axis=1)


def _row_specs(width, n_prompt_blocks, sample_block, tm=None, n_sample_blocks=1, prompt_block0=0):
    tm = TM if tm is None else tm
    return (pl.BlockSpec((tm, width), lambda i: (prompt_block0 + jnp.minimum(i, n_prompt_blocks - 1), 0)),
            pl.BlockSpec((tm, width),
                         lambda i: (sample_block + jnp.clip(i - n_prompt_blocks, 0, n_sample_blocks - 1), 0)))


def _pick_rows(n_prompt_blocks, p_ref, s_ref):
    return jnp.where(pl.program_id(0) >= n_prompt_blocks, s_ref[...], p_ref[...])


TM_IN = 256


def _inproj_kernel(xp_ref, xs_ref, g_ref, w_ref, ws_ref, wgk_ref, bgk_ref, bif_ref, cos_ref, sup_ref, sdn_ref,
                   gates_ref, proj_ref, kt_ref, vp_ref, kts_ref, la_ref, gif_ref, *, n_prompt_blocks, dec_batch):
    i = pl.program_id(0)
    xn = _rms(_pick_rows(n_prompt_blocks, xp_ref, xs_ref), g_ref[...]).astype(bf16)
    small = jnp.dot(xn, ws_ref[...], preferred_element_type=f32)
    z = jnp.dot(small, wgk_ref[...], preferred_element_type=f32, precision=HIGHEST) + bgk_ref[...]
    la_ref[...] = _log_sigmoid(z) * (1.0 / GLA_GATE_NORM)
    gi = small + bif_ref[...]
    lane = lax.broadcasted_iota(jnp.int32, gi.shape, 1)
    gif_ref[...] = jnp.where((lane >= L_MF) & (lane < L_MF + HEADS), _log_sigmoid(gi), gi)

    for c in range(GATE_W // TILE):
        cols = slice(c * TILE, (c + 1) * TILE)
        gates_ref[:, cols] = _sigmoid(jnp.dot(xn, w_ref[:, cols], preferred_element_type=f32)).astype(bf16)

    cos, sup, sdn = cos_ref[...], sup_ref[...], sdn_ref[...]
    for t in range(N_TILES):
        acc = jnp.dot(xn, w_ref[:, GATE_W + t * TILE:GATE_W + (t + 1) * TILE], preferred_element_type=f32)
        if t in (T_DQ, T_DK):
            acc = _rope_tile(acc, cos, sup, sdn)
        proj_ref[:, t * TILE:(t + 1) * TILE] = acc
        if t == T_DK:
            k_t = acc.T
        if t == T_DV:
            v_rows = acc

    @pl.when(i < n_prompt_blocks)
    def _():
        kt_ref[...] = k_t
        for h in range(HEADS):
            vp_ref[pl.ds(h, v_rows.shape[0], stride=HEADS), :] = v_rows[:, h * DV:(h + 1) * DV]

    @pl.when(i >= n_prompt_blocks)
    def _():
        for s in range(k_t.shape[1] // dec_batch):
            kts_ref[s] = k_t[:, s * dec_batch:(s + 1) * dec_batch]


def _inproj(h_p, h_s, sample_block, g_mix, w_main, w_small, w_gk, b_gk, b_if, cos_t, sup_t, sdn_t,
            batch, seq, dec_batch, dec_seq):
    tm = TM_IN
    seq_blocks = seq // tm
    npb = batch * seq_blocks
    steps_per_block = tm // dec_batch
    nsb = dec_seq // steps_per_block
    nt = npb + nsb
    T = nt * tm
    const = lambda i: (0, 0)
    resident = lambda shape: pl.BlockSpec(shape, const, pipeline_mode=pl.Buffered(1))

    def tab_map(i):
        return (jnp.where(i < npb, i % seq_blocks, seq_blocks + i - npb), 0)

    def kt_map(i):
        j = jnp.minimum(i, npb - 1)
        return (j // seq_blocks, 0, j % seq_blocks)

    return pl.pallas_call(
        functools.partial(_inproj_kernel, n_prompt_blocks=npb, dec_batch=dec_batch),
        grid=(nt,),
        in_specs=[
            *_row_specs(D_MODEL, npb, sample_block, tm, nsb),
            pl.BlockSpec((1, D_MODEL), const),
            resident((D_MODEL, P_MAIN)),
            resident((D_MODEL, LANES)),
            resident((LANES, QW)),
            pl.BlockSpec((1, QW), const),
            pl.BlockSpec((1, LANES), const),
            pl.BlockSpec((tm, LANES), tab_map),
            pl.BlockSpec((tm, LANES), tab_map),
            pl.BlockSpec((tm, LANES), tab_map),
        ],
        out_specs=[
            pl.BlockSpec((tm, GATE_W), lambda i: (i, 0)),
            pl.BlockSpec((tm, P_MIX), lambda i: (i, 0)),
            pl.BlockSpec((None, TILE, tm), kt_map),
            pl.BlockSpec((tm * HEADS, DV), lambda i: (jnp.minimum(i, npb - 1), 0)),
            pl.BlockSpec((steps_per_block, TILE, dec_batch), lambda i: (jnp.maximum(i - npb, 0), 0, 0)),
            pl.BlockSpec((tm, QW), lambda i: (i, 0)),
            pl.BlockSpec((tm, LANES), lambda i: (i, 0)),
        ],
        out_shape=[
            jax.ShapeDtypeStruct((T, GATE_W), bf16),
            jax.ShapeDtypeStruct((T, P_MIX), f32),
            jax.ShapeDtypeStruct((batch, TILE, seq), f32),
            jax.ShapeDtypeStruct((batch * seq * HEADS, DV), f32),
            jax.ShapeDtypeStruct((dec_seq, TILE, dec_batch), f32),
            jax.ShapeDtypeStruct((T, QW), f32),
            jax.ShapeDtypeStruct((T, LANES), f32),
        ],
        compiler_params=_cparams("arbitrary"),
        name="inproj",
    )(h_p, h_s, g_mix, w_main, w_small, w_gk, b_gk, b_if, cos_t, sup_t, sdn_t)


def _tril(n):
    r = lax.broadcasted_iota(jnp.int32, (n, n), 0)
    c = lax.broadcasted_iota(jnp.int32, (n, n), 1)
    return r >= c


def _gla_prompt_kernel(*refs, n_chunks, batch):
    q_refs, k_refs, v_refs, la_refs = (refs[i * batch:(i + 1) * batch] for i in range(4))
    o_ref, s_out_ref, s_scr = refs[4 * batch:]
    blk = pl.program_id(0)

    @pl.when(blk == 0)
    def _():
        s_scr[...] = jnp.zeros_like(s_scr)

    ltri = _tril(CHUNK).astype(f32)
    r2 = lax.broadcasted_iota(jnp.int32, (2 * CHUNK, CHUNK), 0)
    c2 = lax.broadcasted_iota(jnp.int32, (2 * CHUNK, CHUNK), 1)
    tril2 = jnp.where(r2 >= CHUNK, r2 - CHUNK, r2) >= c2
    lane = lax.broadcasted_iota(jnp.int32, (CHUNK, LANES), 1)
    lo = (lane < DK).astype(f32)
    hi = 1.0 - lo

    def chunk(c, carry):
        rows = pl.ds(pl.multiple_of(c * CHUNK, CHUNK), CHUNK)
        for s in range(batch):
            g = la_refs[s][rows, :]
            b = jnp.dot(ltri, g, preferred_element_type=f32, precision=HIGHEST)
            b_last = b[CHUNK - 1:CHUNK, :]
            q = q_refs[s][rows, :]
            k = k_refs[s][rows, :]
            qg = q * jnp.exp(b) * (DK ** -0.5)
            kg = k * jnp.exp(-b)
            kd = k * jnp.exp(b_last - b)
            dec = jnp.exp(b_last)
            for p in range(HEADS // 2):
                ls = slice(p * LANES, (p + 1) * LANES)
                qg_p, kg_p, kd_p = qg[:, ls], kg[:, ls], kd[:, ls]
                qs = jnp.concatenate([qg_p * lo, qg_p * hi], axis=0)
                a = jnp.where(tril2, _dot_nt(qs, kg_p), 0.0)
                s_p = s_scr[s, p]
                inter = _dot(qs, s_p)
                v0 = v_refs[s][rows, pl.ds((2 * p) * DV, DV)]
                v1 = v_refs[s][rows, pl.ds((2 * p + 1) * DV, DV)]
                o_ref[s, rows, pl.ds((2 * p) * DV, DV)] = inter[:CHUNK] + _dot(a[:CHUNK], v0)
                o_ref[s, rows, pl.ds((2 * p + 1) * DV, DV)] = inter[CHUNK:] + _dot(a[CHUNK:], v1)
                dcol = jnp.broadcast_to(dec[:, ls], (LANES, LANES)).T
                s_scr[s, p] = dcol * s_p + _dot_tn(kd_p * lo, v0) + _dot_tn(kd_p * hi, v1)
        return carry

    lax.fori_loop(0, n_chunks, chunk, 0)

    @pl.when(blk == pl.num_programs(0) - 1)
    def _():
        s_out_ref[...] = s_scr[...]


def _stream_specs(width, batch, nb, col):
    return [pl.BlockSpec((TM, width), functools.partial(lambda s, i: (s * nb + i, col), s)) for s in range(batch)]


def _gla_prompt(proj, la, batch, seq):
    nb = seq // TM
    qw_blocks = TILE // QW
    return pl.pallas_call(
        functools.partial(_gla_prompt_kernel, n_chunks=TM // CHUNK, batch=batch),
        grid=(nb,),
        in_specs=[
            *_stream_specs(QW, batch, nb, T_GQK * qw_blocks),
            *_stream_specs(QW, batch, nb, T_GQK * qw_blocks + 1),
            *_stream_specs(VW, batch, nb, T_GV),
            *_stream_specs(QW, batch, nb, 0),
        ],
        out_specs=[
            pl.BlockSpec((batch, TM, VW), lambda i: (0, i, 0)),
            pl.BlockSpec((batch, HEADS // 2, LANES, LANES), lambda i: (0, 0, 0, 0)),
        ],
        out_shape=[
            jax.ShapeDtypeStruct((batch, seq, VW), f32),
            jax.ShapeDtypeStruct((batch, HEADS // 2, LANES, LANES), f32),
        ],
        scratch_shapes=[pltpu.VMEM((batch, HEADS // 2, LANES, LANES), f32)],
        compiler_params=_cparams("arbitrary"),
        name="gla_prompt",
    )(*([proj] * (3 * batch)), *([la] * batch))


def _mlstm_prompt_kernel(*refs, n_chunks, batch):
    q_refs, k_refs, v_refs, gif_refs = (refs[i * batch:(i + 1) * batch] for i in range(4))
    h_ref, c_out_ref, n_out_ref, m_out_ref, ct_scr, n_scr, m_scr = refs[4 * batch:]
    blk = pl.program_id(0)

    @pl.when(blk == 0)
    def _():
        ct_scr[...] = jnp.zeros_like(ct_scr)
        n_scr[...] = jnp.zeros_like(n_scr)
        m_scr[...] = jnp.zeros_like(m_scr)

    CH = CHUNK_MLSTM
    ltri = _tril(CH).astype(f32)
    r_i = lax.broadcasted_iota(jnp.int32, (CH, CH), 0)
    c_i = lax.broadcasted_iota(jnp.int32, (CH, CH), 1)
    upper = r_i <= c_i

    def chunk(c, carry):
        rows = pl.ds(pl.multiple_of(c * CH, CH), CH)
        for sq in range(batch):
            g = gif_refs[sq][rows, :]
            fc = jnp.dot(ltri, g, preferred_element_type=f32, precision=HIGHEST)
            g_t = g.T
            fc_t = fc.T
            q_all = q_refs[sq][rows, :]
            k_all = k_refs[sq][rows, :] * (DK ** -0.5)
            n_all = n_scr[sq]
            for h in range(HEADS):
                i_col = g[:, L_MI + h:L_MI + h + 1]
                f_col = fc[:, L_MF + h:L_MF + h + 1]
                i_row = g_t[L_MI + h:L_MI + h + 1, :]
                f_row = fc_t[L_MF + h:L_MF + h + 1, :]
                f_last = fc[CH - 1:CH, L_MF + h:L_MF + h + 1]
                m_prev = m_scr[sq, h:h + 1, 0:1]
                c_t = ct_scr[sq, h]
                q = q_all[:, h * DK:(h + 1) * DK]
                k = k_all[:, h * DK:(h + 1) * DK]
                v_t = v_refs[sq][rows, pl.ds(h * DV, DV)].T

                log_d = jnp.where(upper, f_row + (i_col - f_col), -jnp.inf)
                m_inter = m_prev + f_row
                m_row = jnp.maximum(m_inter, jnp.max(log_d, axis=0, keepdims=True))
                w_inter = jnp.exp(m_inter - m_row)
                sc = _dot_nt(k, q) * jnp.exp(log_d - m_row)
                num = w_inter * _dot_nt(c_t, q) + _dot(v_t, sc)
                qn = _dot_nt(n_all, q)[h:h + 1, :]
                den = w_inter * qn + jnp.sum(sc, axis=0, keepdims=True)
                h_t = num / jnp.maximum(jnp.abs(den), jnp.exp(-m_row))
                h_ref[sq, rows, pl.ds(h * DV, DV)] = h_t.T

                log_w = f_last - f_col + i_col
                m_new = jnp.maximum(m_prev + f_last, jnp.max(log_w, axis=0, keepdims=True))
                wk = jnp.exp(log_w - m_new) * k
                decay = jnp.exp(m_prev + f_last - m_new)
                ct_scr[sq, h] = decay * c_t + _dot(v_t, wk)
                n_scr[sq, h:h + 1, :] = decay * n_all[h:h + 1, :] + jnp.sum(wk, axis=0, keepdims=True)
                m_scr[sq, h:h + 1, :] = jnp.broadcast_to(m_new, (1, LANES))
        return carry

    lax.fori_loop(0, n_chunks, chunk, 0)

    @pl.when(blk == pl.num_programs(0) - 1)
    def _():
        for sq in range(batch):
            for h in range(HEADS):
                c_out_ref[sq, h] = ct_scr[sq, h].T
        n_out_ref[...] = n_scr[:, 0:HEADS, :]
        m_out_ref[...] = m_scr[:, 0:HEADS, :]


def _mlstm_prompt(proj, gif, batch, seq):
    nb = seq // TM
    qw_blocks = TILE // QW
    return pl.pallas_call(
        functools.partial(_mlstm_prompt_kernel, n_chunks=TM // CHUNK_MLSTM, batch=batch),
        grid=(nb,),
        in_specs=[
            *_stream_specs(QW, batch, nb, T_MQK * qw_blocks),
            *_stream_specs(QW, batch, nb, T_MQK * qw_blocks + 1),
            *_stream_specs(VW, batch, nb, T_MV),
            *_stream_specs(LANES, batch, nb, 0),
        ],
        out_specs=[
            pl.BlockSpec((batch, TM, VW), lambda i: (0, i, 0)),
            pl.BlockSpec((batch, HEADS, DK, DV), lambda i: (0, 0, 0, 0)),
            pl.BlockSpec((batch, HEADS, DK), lambda i: (0, 0, 0)),
            pl.BlockSpec((batch, HEADS, LANES), lambda i: (0, 0, 0)),
        ],
        out_shape=[
            jax.ShapeDtypeStruct((batch, seq, VW), f32),
            jax.ShapeDtypeStruct((batch, HEADS, DK, DV), f32),
            jax.ShapeDtypeStruct((batch, HEADS, DK), f32),
            jax.ShapeDtypeStruct((batch, HEADS, LANES), f32),
        ],
        scratch_shapes=[
            pltpu.VMEM((batch, HEADS, DV, DK), f32),
            pltpu.VMEM((batch, SUBLANES, DK), f32),
            pltpu.VMEM((batch, SUBLANES, LANES), f32),
        ],
        compiler_params=_cparams("arbitrary"),
        name="mlstm_prompt",
    )(*([proj] * (3 * batch)), *([gif] * batch))


def _diff_lambda_value(lam_ref, lam_init):
    lam = lam_ref[...]
    s1 = jnp.sum(lam[0:1] * lam[1:2], axis=1, keepdims=True)
    s2 = jnp.sum(lam[2:3] * lam[3:4], axis=1, keepdims=True)
    return jnp.exp(s1) - jnp.exp(s2) + lam_init


def _attn_prompt_step(p, qi_ref, ki_ref, q_ref, k_ref, v_ref, lam_ref, o_ref, q2_scr, m_scr, l_scr, acc_scr,
                      *, tq, lam_init, tail):
    qi = qi_ref[p]
    ki = ki_ref[p]
    pair = lambda j: slice(j * LANES, (j + 1) * LANES)

    @pl.when(ki == 0)
    def _():
        for j in range(HEADS):
            q = q_ref[:, pair(j)] * (DK ** -0.5 * LOG2E)
            lane = lax.broadcasted_iota(jnp.int32, q.shape, 1)
            q2_scr[j, 0:tq, :] = jnp.where(lane < DK, q, 0.0).astype(bf16)
            q2_scr[j, tq:2 * tq, :] = jnp.where(lane >= DK, q, 0.0).astype(bf16)
        m_scr[...] = jnp.full_like(m_scr, -jnp.inf)
        l_scr[...] = jnp.zeros_like(l_scr)
        acc_scr[...] = jnp.zeros_like(acc_scr)

    def step(masked):
        sts = [_dot_nt(k_ref[:, pair(j)], q2_scr[j]) for j in range(HEADS)]
        pts, alphas = [], []
        for j in range(HEADS):
            st = sts[j]
            if masked:
                r = lax.broadcasted_iota(jnp.int32, st.shape, 0)
                c = lax.broadcasted_iota(jnp.int32, st.shape, 1)
                c = jnp.where(c >= tq, c - tq, c)
                st = jnp.where(r <= c, st, -jnp.inf)
            m_prev = m_scr[j]
            m_new = jnp.maximum(m_prev, jnp.max(st, axis=0, keepdims=True))
            alpha = jnp.exp2(m_prev - m_new)
            pt = jnp.exp2(st - m_new)
            l_scr[j] = alpha * l_scr[j] + jnp.sum(pt, axis=0, keepdims=True)
            m_scr[j] = m_new
            pts.append(pt.astype(bf16))
            alphas.append(alpha)
        for j in range(HEADS):
            acc_scr[j] = alphas[j] * acc_scr[j] + _dot_tn(v_ref[:, pair(j)], pts[j])

    @pl.when(ki < qi)
    def _():
        step(False)
        tail()

    @pl.when(ki == qi)
    def _():
        step(True)
        lam = _diff_lambda_value(lam_ref, lam_init)
        for j in range(HEADS):
            ot = acc_scr[j] / l_scr[j]
            o_ref[:, pair(j)] = (ot[:, 0:tq] - lam * ot[:, tq:2 * tq]).T
        tail()


def _sample_page_copies(seq_idx, slot, pt_ref, ck_hbm, cv_hbm, kbuf, vbuf, sem, *, layer, n_pages, page):
    out = []
    for pg in range(n_pages):
        src = pt_ref[seq_idx, pg]
        out.append(pltpu.make_async_copy(ck_hbm.at[layer, src], kbuf.at[slot, :, pl.ds(pg * page, page)],
                                         sem.at[0, slot]))
        out.append(pltpu.make_async_copy(cv_hbm.at[layer, src],
                                         vbuf.at[slot, pl.ds(pg * page * HEADS, page * HEADS), :], sem.at[1, slot]))
    return out


def _attn_sample_pages(step, n_steps, pt_ref, ck_hbm, cv_hbm, o_ref, kbuf, vbuf, sem, *, nseq, pages):
    past = pages["n_pages"] * pages["page"]
    seq = lambda st: jnp.minimum(st, nseq - 1)
    slot = step % 2
    copies = functools.partial(_sample_page_copies, pt_ref=pt_ref, ck_hbm=ck_hbm, cv_hbm=cv_hbm, kbuf=kbuf,
                               vbuf=vbuf, sem=sem, **pages)

    @pl.when(step == 0)
    def _():
        kbuf[:, :, pl.ds(past, LANES)] = jnp.zeros((2, kbuf.shape[1], LANES), f32)
        vbuf[:, pl.ds(past * HEADS, LANES * HEADS), :] = jnp.zeros((2, LANES * HEADS, DV), f32)
        o_ref[...] = jnp.zeros_like(o_ref)
        for cp in copies(0, 0):
            cp.start()

    @pl.when(step + 1 < n_steps)
    def _():
        for cp in copies(seq(step + 1), 1 - slot):
            cp.start()

    for cp in copies(seq(step), slot):
        cp.wait()


def _attn_sample_compute(step, q_ref, kts_ref, v_ref, lam_ref, o_ref, kbuf, vbuf,
                         *, n_pages, page, dec_seq, nseq, lam_init):
    b = jnp.minimum(step, nseq - 1)
    slot = step % 2
    past = n_pages * page
    n_qh = 2 * HEADS

    shift = lax.rem(nseq - b, nseq)
    b8 = (b // SUBLANES) * SUBLANES
    groups = [pl.ds(pl.multiple_of(t * nseq + b8, SUBLANES), SUBLANES) for t in range(dec_seq)]
    mine = lax.broadcasted_iota(jnp.int32, (SUBLANES, 2 * QW), 0) == b - b8
    pick = lambda ref, t: jnp.sum(jnp.where(mine, ref[groups[t], :], 0.0), axis=0, keepdims=True)
    head_of_lane = lax.broadcasted_iota(jnp.int32, (n_qh, 2 * QW), 1) // DK
    head_of_row = lax.broadcasted_iota(jnp.int32, (n_qh, 2 * QW), 0)
    q_rows = []
    for t in range(dec_seq):
        kbuf[slot, :, pl.ds(past + t, 1)] = pltpu.roll(kts_ref[t], shift, axis=1)[:, 0:1]
        v_row = pick(v_ref, t)
        for h in range(HEADS):
            vbuf[slot, pl.ds((past + t) * HEADS + h, 1), :] = v_row[:, h * DV:(h + 1) * DV]
        q = jnp.broadcast_to(pick(q_ref, t) * (DK ** -0.5), (n_qh, 2 * QW))
        q_rows.append(jnp.where(head_of_lane == head_of_row, q, 0.0))
    qbd = jnp.concatenate(q_rows, axis=0)

    s = _dot(qbd, kbuf[slot])
    r = lax.broadcasted_iota(jnp.int32, s.shape, 0)
    c = lax.broadcasted_iota(jnp.int32, s.shape, 1)
    s = jnp.where(c - past <= r // n_qh, s, -jnp.inf)
    m = jnp.max(s, axis=1, keepdims=True)
    pr = jnp.exp(s - m)
    inv_l = 1.0 / jnp.sum(pr, axis=1, keepdims=True)
    lam = _diff_lambda_value(lam_ref, lam_init)
    mine_v = mine[:, 0:DV]
    for hv in range(HEADS):
        v_h = vbuf[slot, pl.ds(hv, past + LANES, stride=HEADS), :]
        o_h = _dot(pr, v_h) * inv_l
        for t in range(dec_seq):
            r1 = t * n_qh + 2 * hv
            d = o_h[r1:r1 + 1] - lam * o_h[r1 + 1:r1 + 2]
            lanes = slice(hv * DV, (hv + 1) * DV)
            o_ref[groups[t], lanes] = jnp.where(mine_v, d, o_ref[groups[t], lanes])


def _attn_kernel(qi_ref, ki_ref, pt_ref, q_ref, k_ref, v_ref, lam_ref, qs_ref, kts_ref, vs_ref, ck_hbm, cv_hbm,
                 o_ref, os_ref, q2_scr, m_scr, l_scr, acc_scr, kbuf, vbuf, sem, *, prompt, sample, nseq, n_pairs):
    p = pl.program_id(1)
    step = pl.program_id(0) * pl.num_programs(1) + p
    n_steps = pl.num_programs(0) * pl.num_programs(1)
    pages = dict(layer=sample["layer"], n_pages=sample["n_pages"], page=sample["page"])
    _attn_sample_pages(step, n_steps, pt_ref, ck_hbm, cv_hbm, os_ref, kbuf, vbuf, sem, nseq=nseq, pages=pages)

    def sample_work():
        _attn_sample_compute(step, qs_ref, kts_ref, vs_ref, lam_ref, os_ref, kbuf, vbuf, nseq=nseq,
                             n_pages=sample["n_pages"], page=sample["page"], dec_seq=sample["dec_seq"],
                             lam_init=sample["lam_init"])

    @pl.when(p < n_pairs)
    def _():
        _attn_prompt_step(p, qi_ref, ki_ref, q_ref, k_ref, v_ref, lam_ref, o_ref,
                          q2_scr, m_scr, l_scr, acc_scr, tail=sample_work, **prompt)

    @pl.when(p >= n_pairs)
    def _():
        sample_work()


def _attention(proj, k_ts, page_table, diff_lambda, cache_kt, cache_v2, layer, lam_init, batch, seq, sample_block,
               tq=512):
    nq = seq // tq
    pairs = [(q, k) for q in range(nq) for k in range(q + 1)]
    qi_tab = jnp.asarray(np.array([a for a, _ in pairs], np.int32))
    ki_tab = jnp.asarray(np.array([b for _, b in pairs], np.int32))
    nseq, n_pages = page_table.shape
    n_pairs = len(pairs)
    steps = max(n_pairs, -(-nseq // batch))
    last = n_pairs - 1
    dec_seq = k_ts.shape[0]
    rows = dec_seq * nseq
    page = cache_kt.shape[3]
    cols = n_pages * page + LANES
    grid_spec = pltpu.PrefetchScalarGridSpec(
        num_scalar_prefetch=3,
        grid=(batch, steps),
        in_specs=[
            pl.BlockSpec((tq, TILE), lambda b, p, qt, kt, pt: (b * nq + qt[jnp.minimum(p, last)], T_DQ)),
            pl.BlockSpec((tq, TILE), lambda b, p, qt, kt, pt: (b * nq + kt[jnp.minimum(p, last)], T_DK)),
            pl.BlockSpec((tq, TILE), lambda b, p, qt, kt, pt: (b * nq + kt[jnp.minimum(p, last)], T_DV)),
            pl.BlockSpec((4, DK), lambda b, p, qt, kt, pt: (0, 0)),
            pl.BlockSpec((rows, TILE), lambda b, p, qt, kt, pt: (sample_block, T_DQ)),
            pl.BlockSpec((dec_seq, TILE, nseq), lambda b, p, qt, kt, pt: (0, 0, 0)),
            pl.BlockSpec((rows, TILE), lambda b, p, qt, kt, pt: (sample_block, T_DV)),
            pl.BlockSpec(memory_space=pl.ANY),
            pl.BlockSpec(memory_space=pl.ANY),
        ],
        out_specs=[
            pl.BlockSpec((tq, VW), lambda b, p, qt, kt, pt: (b * nq + qt[jnp.minimum(p, last)], 0)),
            pl.BlockSpec((rows, VW), lambda b, p, qt, kt, pt: (0, 0)),
        ],
        scratch_shapes=[
            pltpu.VMEM((HEADS, 2 * tq, LANES), bf16),
            pltpu.VMEM((HEADS, 1, 2 * tq), f32),
            pltpu.VMEM((HEADS, 1, 2 * tq), f32),
            pltpu.VMEM((HEADS, DV, 2 * tq), f32),
            pltpu.VMEM((2, 2 * QW, cols), f32),
            pltpu.VMEM((2, cols * HEADS, DV), f32),
            pltpu.SemaphoreType.DMA((2, 2)),
        ],
    )
    return pl.pallas_call(
        functools.partial(_attn_kernel, nseq=nseq, n_pairs=n_pairs, prompt=dict(tq=tq, lam_init=lam_init),
                          sample=dict(layer=layer, n_pages=n_pages, page=page, dec_seq=dec_seq, lam_init=lam_init)),
        grid_spec=grid_spec,
        out_shape=[jax.ShapeDtypeStruct((batch * seq, VW), f32), jax.ShapeDtypeStruct((rows, VW), f32)],
        compiler_params=_cparams("arbitrary", "arbitrary"),
        name="attention",
    )(qi_tab, ki_tab, page_table, proj, proj, proj, diff_lambda, proj, k_ts, proj, cache_kt, cache_v2)


def _head_of_lane(shape):
    return lax.broadcasted_iota(jnp.int32, shape, 1) // DK


def _expand_heads(cols, base, head):
    out = cols[:, base + HEADS - 1:base + HEADS]
    for h in range(HEADS - 2, -1, -1):
        out = jnp.where(head == h, cols[:, base + h:base + h + 1], out)
    return out


def _sample_prep_kernel(gq_ref, gk_ref, la_ref, mq_ref, mk_ref, gif_ref, n0_ref, m0_ref,
                        ga_ref, gkk_ref, gqq_ref, ma_ref, mkk_ref, mqq_ref, n_out_ref, m_out_ref, *, nseq, dec_seq):
    head = _head_of_lane((nseq, QW))
    m = _expand_heads(m0_ref[...], 0, head)
    n = n0_ref[...]
    for t in range(dec_seq):
        rows = pl.ds(t * nseq, nseq)
        ga_ref[t] = jnp.exp(la_ref[rows, :]).T
        gkk_ref[t] = gk_ref[rows, :].T
        gqq_ref[t] = (gq_ref[rows, :] * (DK ** -0.5)).T
        gates = gif_ref[rows, :]
        i_e = _expand_heads(gates, L_MI, head)
        f_e = _expand_heads(gates, L_MF, head)
        k = mk_ref[rows, :] * (DK ** -0.5)
        q = mq_ref[rows, :]
        m_new = jnp.maximum(f_e + m, i_e)
        fp = jnp.exp(f_e + m - m_new)
        ip = jnp.exp(i_e - m_new)
        n = fp * n + ip * k
        nq = n * q
        den = jnp.zeros_like(nq)
        for h in range(HEADS):
            den = jnp.where(head == h, jnp.sum(jnp.where(head == h, nq, 0.0), axis=1, keepdims=True), den)
        inv = 1.0 / jnp.maximum(jnp.abs(den), jnp.exp(-m_new))
        ma_ref[t] = fp.T
        mkk_ref[t] = (ip * k).T
        mqq_ref[t] = (q * inv).T
        m = m_new
    n_out_ref[...] = n
    m_out_ref[...] = m


def _sample_prep(proj, la, gif, state_n, state_m, layer, sample_block, nseq, dec_seq):
    rows = nseq * dec_seq
    qw_blocks = TILE // QW
    tok = jax.ShapeDtypeStruct((dec_seq, QW, nseq), f32)
    st = jax.ShapeDtypeStruct((nseq, QW), f32)
    tok_spec = pl.BlockSpec((dec_seq, QW, nseq), lambda i: (0, 0, 0))
    st_spec = pl.BlockSpec((nseq, QW), lambda i: (0, 0))
    col = lambda c: pl.BlockSpec((rows, QW), lambda i: (sample_block, c))
    return pl.pallas_call(
        functools.partial(_sample_prep_kernel, nseq=nseq, dec_seq=dec_seq),
        grid=(1,),
        in_specs=[
            col(T_GQK * qw_blocks), col(T_GQK * qw_blocks + 1), col(0),
            col(T_MQK * qw_blocks), col(T_MQK * qw_blocks + 1),
            pl.BlockSpec((rows, LANES), lambda i: (sample_block, 0)),
            pl.BlockSpec((None, nseq, QW), lambda i: (layer, 0, 0)),
            pl.BlockSpec((None, nseq, HEADS), lambda i: (layer, 0, 0)),
        ],
        out_specs=[tok_spec] * 6 + [st_spec] * 2,
        out_shape=[tok] * 6 + [st] * 2,
        compiler_params=_cparams("arbitrary"),
        name="sample_prep",
    )(proj, proj, la, proj, proj, gif, state_n, state_m)


SEQ_PER_STEP = 8


def _sample_state_kernel(a_ref, k_ref, q_ref, v_ref, s_ref, o_ref, s_out_ref, *, dec_seq, nseq):
    blk = pl.program_id(0)
    seq0 = blk * SEQ_PER_STEP
    shift = lax.rem(nseq - seq0, nseq)
    a_t = [pltpu.roll(a_ref[t], shift, axis=1) for t in range(dec_seq)]
    k_t = [pltpu.roll(k_ref[t], shift, axis=1) for t in range(dec_seq)]
    q_t = [pltpu.roll(q_ref[t], shift, axis=1) for t in range(dec_seq)]
    toks = [pl.ds(pl.multiple_of(t * nseq + seq0, SEQ_PER_STEP), SEQ_PER_STEP) for t in range(dec_seq)]
    v_t = [v_ref[toks[t], :] for t in range(dec_seq)]
    row = lax.broadcasted_iota(jnp.int32, (SEQ_PER_STEP, DV), 0)
    outs = [[jnp.zeros((SEQ_PER_STEP, DV), f32) for _ in range(HEADS)] for _ in range(dec_seq)]
    for bb in range(SEQ_PER_STEP):
        for h in range(HEADS):
            s = s_ref[bb, h]
            rows = slice(h * DK, (h + 1) * DK)
            for t in range(dec_seq):
                a = a_t[t][rows, bb:bb + 1]
                k = k_t[t][rows, bb:bb + 1]
                q = q_t[t][rows, bb:bb + 1]
                v = v_t[t][bb:bb + 1, h * DV:(h + 1) * DV]
                s = a * s + k * v
                outs[t][h] = jnp.where(row == bb, jnp.sum(q * s, axis=0, keepdims=True), outs[t][h])
            s_out_ref[bb, h] = s
    for t in range(dec_seq):
        for h in range(HEADS):
            o_ref[toks[t], h * DV:(h + 1) * DV] = outs[t][h]


def _sample_state(a_t, k_t, q_t, proj, v_tile, sample_block, state, layer, dec_seq):
    nseq = a_t.shape[2]
    rows = dec_seq * nseq
    full = pl.BlockSpec((dec_seq, QW, nseq), lambda i: (0, 0, 0))
    return pl.pallas_call(
        functools.partial(_sample_state_kernel, dec_seq=dec_seq, nseq=nseq),
        grid=(nseq // SEQ_PER_STEP,),
        in_specs=[
            full, full, full,
            pl.BlockSpec((rows, VW), lambda i: (sample_block, v_tile)),
            pl.BlockSpec((None, SEQ_PER_STEP, HEADS, DK, DV), lambda i: (layer, i, 0, 0, 0)),
        ],
        out_specs=[
            pl.BlockSpec((rows, VW), lambda i: (0, 0)),
            pl.BlockSpec((SEQ_PER_STEP, HEADS, DK, DV), lambda i: (i, 0, 0, 0)),
        ],
        out_shape=[
            jax.ShapeDtypeStruct((rows, VW), f32),
            jax.ShapeDtypeStruct(state.shape[1:], f32),
        ],
        compiler_params=_cparams("arbitrary"),
        name="sample_state",
    )(a_t, k_t, q_t, proj, state)


def _head_rms(x, g):
    parts = []
    for h in range(HEADS):
        parts.append(_rms(x[:, h * DV:(h + 1) * DV], g))
    return jnp.concatenate(parts, axis=1)


FF_SLABS = ((0, 768), (768, 768), (1536, 768), (2304, 512))
TM_CH = 256


def _merge_ffn_kernel(hp_ref, hs_ref, oap_ref, oas_ref, omp_ref, oms_ref, odp_ref, ods_ref, gr_ref, mo_ref,
                      g0_ref, g1_ref, g2_ref, gg_ref, gm_ref, gd_ref, wa_ref, wm_ref, wd_ref, wo_ref,
                      gf_ref, wg_ref, wu_ref, wdn_ref, pp_ref, ps_ref, gp_ref, wpg_ref, wpp_ref, gfin_ref,
                      *out_refs, lam_init, final, n_prompt_blocks):
    pick = functools.partial(_pick_rows, n_prompt_blocks)
    gr = gr_ref[...]
    a = _head_rms(pick(oap_ref, oas_ref), gg_ref[...]) * (gr * _sigmoid(gr))
    m = _sigmoid(mo_ref[...]) * _head_rms(pick(omp_ref, oms_ref), gm_ref[...])
    d = _head_rms(pick(odp_ref, ods_ref), gd_ref[...]) * (1.0 - lam_init)
    merged = (g0_ref[...] * _dot(a, wa_ref[...]) + g1_ref[...] * _dot(m, wm_ref[...])
              + g2_ref[...] * _dot(d, wd_ref[...]))
    h = pick(hp_ref, hs_ref) + _dot(merged, wo_ref[...])

    xf = _rms(h, gf_ref[...]).astype(bf16)
    h2 = h
    for lo, width in FF_SLABS:
        gate = jnp.dot(xf, wg_ref[:, lo:lo + width], preferred_element_type=f32)
        up = jnp.dot(xf, wu_ref[:, lo:lo + width], preferred_element_type=f32)
        h2 = h2 + _dot(gate * _sigmoid(gate) * up, wdn_ref[lo:lo + width, :])
    ple_gate = _sigmoid(_dot(_rms(h2, gp_ref[...]), wpg_ref[...]))
    h3 = h2 + ple_gate * _dot(pick(pp_ref, ps_ref), wpp_ref[...])
    if not final:
        out_refs[0][...] = h3
        return
    y = _rms(h3, gfin_ref[...])
    yp_ref, ys_ref = out_refs
    i = pl.program_id(0)

    @pl.when(i < n_prompt_blocks)
    def _():
        yp_ref[...] = y

    @pl.when(i >= n_prompt_blocks)
    def _():
        ys_ref[...] = y


def _merge_ffn(h_p, h_s, h_s_row0, o_a, o_m, o_d, proj, gates, g_gla, g_ml, g_diff, w_a, w_m, w_d, w_out, lam_init,
               g_ffn, w_g, w_u, w_dn, p_p, p_s, p_row0, g_ple, w_pg, w_pp, g_final, final, n_prompt_rows, n_sample_rows):
    tm = TM_CH
    npb, nsb = n_prompt_rows // tm, n_sample_rows // tm
    T = n_prompt_rows + n_sample_rows
    row = lambda i: (i, 0)
    const = lambda i: (0, 0)
    resident = lambda shape: pl.BlockSpec(shape, const, pipeline_mode=pl.Buffered(1))
    rows2 = lambda width, sample_block=0, prompt_block0=0: _row_specs(width, npb, sample_block, tm, nsb, prompt_block0)
    assert FF_SLABS[-1][0] + FF_SLABS[-1][1] == D_FF
    if final:
        out_specs = [pl.BlockSpec((tm, D_MODEL), lambda i: (jnp.minimum(i, npb - 1), 0)),
                     pl.BlockSpec((tm, D_MODEL), lambda i: (jnp.clip(i - npb, 0, nsb - 1), 0))]
        out_shape = [jax.ShapeDtypeStruct((n_prompt_rows, D_MODEL), f32),
                     jax.ShapeDtypeStruct((n_sample_rows, D_MODEL), f32)]
    else:
        out_specs = pl.BlockSpec((tm, D_MODEL), row)
        out_shape = jax.ShapeDtypeStruct((T, D_MODEL), f32)
    return pl.pallas_call(
        functools.partial(_merge_ffn_kernel, lam_init=lam_init, final=final, n_prompt_blocks=npb),
        grid=(npb + nsb,),
        in_specs=[
            *rows2(D_MODEL, sample_block=h_s_row0 // tm),
            *rows2(VW), *rows2(VW), *rows2(VW),
            pl.BlockSpec((tm, VW), lambda i: (i, T_GR)),
            pl.BlockSpec((tm, VW), lambda i: (i, T_MO)),
            pl.BlockSpec((tm, D_MODEL), lambda i: (i, 0)),
            pl.BlockSpec((tm, D_MODEL), lambda i: (i, 1)),
            pl.BlockSpec((tm, D_MODEL), lambda i: (i, 2)),
            pl.BlockSpec((1, DV), const),
            pl.BlockSpec((1, DV), const),
            pl.BlockSpec((1, DV), const),
            resident((VW, D_MODEL)), resident((VW, D_MODEL)), resident((VW, D_MODEL)),
            resident((D_MODEL, D_MODEL)),
            pl.BlockSpec((1, D_MODEL), const),
            resident((D_MODEL, D_FF)), resident((D_MODEL, D_FF)), resident((D_FF, D_MODEL)),
            *rows2(PLE_DIM, prompt_block0=p_row0 // tm),
            pl.BlockSpec((1, D_MODEL), const),
            resident((D_MODEL, D_MODEL)),
            resident((PLE_DIM, D_MODEL)),
            pl.BlockSpec((1, D_MODEL), const),
        ],
        out_specs=out_specs,
        out_shape=out_shape,
        compiler_params=_cparams("arbitrary"),
        name="merge_ffn",
    )(h_p, h_s, *o_a, *o_m, *o_d, proj, proj, gates, gates, gates, g_gla, g_ml, g_diff, w_a, w_m, w_d, w_out,
      g_ffn, w_g, w_u, w_dn, p_p, p_s, g_ple, w_pg, w_pp, g_final)


def _rope_tables(positions):
    half = ROT_DIM // 2
    inv_freq = ROPE_THETA ** (-jnp.arange(half, dtype=f32) * 2.0 / ROT_DIM)
    ang = positions.astype(f32)[:, None] * inv_freq[None, :]
    cos, sin = jnp.cos(ang), jnp.sin(ang)
    n = positions.shape[0]
    one = jnp.ones((n, DK - ROT_DIM), f32)
    zero = jnp.zeros((n, DK - ROT_DIM), f32)
    zh = jnp.zeros((n, half), f32)
    cos_h = jnp.concatenate([cos, cos, one], axis=1)
    up_h = jnp.concatenate([-sin, zh, zero], axis=1)
    dn_h = jnp.concatenate([zh, sin, zero], axis=1)
    rep = lambda a: jnp.concatenate([a] * (LANES // DK), axis=1)
    return rep(cos_h), rep(up_h), rep(dn_h)


def _prep_in_weights(w_in_l, w_gate_l):
    o = np.cumsum((0, QW, QW, VW, VW, GLA_RANK, QW, QW, VW, VW, 2 * HEADS, 2 * QW, 2 * QW, VW))
    seg = lambda i: w_in_l[:, int(o[i]):int(o[i + 1])]
    gq, gk, gv, gr, glr, mq, mk, mv, mo, mif, dq, dk, dv = (seg(i) for i in range(13))
    main = jnp.concatenate([w_gate_l, gq, gk, gv, gr, mq, mk, mv, mo, dq, dk, dv], axis=1).astype(bf16)
    pad = jnp.zeros((D_MODEL, LANES - GLA_RANK - 2 * HEADS), f32)
    small = jnp.concatenate([glr, mif, pad], axis=1).astype(bf16)
    return main, small


def kernel(x_prompt, x_sample, cache_k, cache_v, state_gla, state_mlstm_c, state_mlstm_n, state_mlstm_m, page_table, p_prompt, p_sample, g_mix, w_in, w_gla_gk, b_gla_gk, g_gla_norm, b_mlstm_if, g_mlstm_norm, diff_lambda, g_diff_norm, w_branch, w_gate, w_out, g_ffn, w_ffn_gate, w_ffn_up, w_ffn_down, g_ple, w_ple_gate, w_ple_proj, g_final):
    bp, lp, _ = x_prompt.shape
    bs, ls, _ = x_sample.shape
    depth = g_mix.shape[0]
    n_pages = page_table.shape[1]
    page = cache_k.shape[2]
    past = n_pages * page
    tp = bp * lp
    ts = bs * ls
    assert lp % TM == 0 and ts == TM and bs == LANES and page == LANES and TM_IN % bs == 0
    npb = tp // TM

    def to_step_major(a):
        return jnp.swapaxes(a, 0, 1).reshape((ts,) + a.shape[2:])

    def from_step_major(a):
        return jnp.swapaxes(a.reshape((ls, bs) + a.shape[1:]), 0, 1)

    h_p, h_s, h_s_row0 = x_prompt.reshape(tp, D_MODEL), to_step_major(x_sample), 0

    pos_rows = jnp.concatenate([jnp.arange(lp, dtype=jnp.int32),
                                past + jnp.repeat(jnp.arange(ls, dtype=jnp.int32), bs)])
    cos_t, sup_t, sdn_t = _rope_tables(pos_rows)

    ck = jnp.transpose(cache_k, (0, 1, 3, 4, 2)).reshape(cache_k.shape[0], cache_k.shape[1], 2 * QW, page)
    cv = cache_v.reshape(cache_v.shape[0], cache_v.shape[1], page * HEADS, DV)
    state_n = state_mlstm_n.reshape(depth, bs, QW)
    p_rows = p_prompt.reshape(depth * tp, PLE_DIM)
    outs_p, outs_s = [], []
    for l in range(depth):
        lam_init = 0.8 - 0.6 * math.exp(-0.3 * l)
        w_main, w_small = _prep_in_weights(w_in[l], w_gate[l])
        w_gk = jnp.concatenate([w_gla_gk[l], jnp.zeros((LANES - GLA_RANK, QW), f32)], axis=0)
        b_if = jnp.zeros((1, LANES), f32).at[0, L_MI:L_MI + 2 * HEADS].set(b_mlstm_if[l])
        gates, proj, k_t, v_p, k_ts, la, gif = _inproj(h_p, h_s, h_s_row0 // TM_IN, g_mix[l][None], w_main, w_small, w_gk,
                                                  b_gla_gk[l][None], b_if, cos_t, sup_t, sdn_t, bp, lp, bs, ls)

        oa_p, gla_p = _gla_prompt(proj, la, bp, lp)
        om_p, c_p, n_p, m_p = _mlstm_prompt(proj, gif, bp, lp)
        oa_p, om_p = oa_p.reshape(tp, VW), om_p.reshape(tp, VW)
        od_p, od_s = _attention(proj, k_ts, page_table, diff_lambda[l], ck, cv, l, lam_init, bp, lp, npb)

        ga, gk_, gq_, ma, mk_, mq_, n_s, m_s = _sample_prep(proj, la, gif, state_n, state_mlstm_m, l, npb, bs, ls)
        oa_s, gla_s = _sample_state(ga, gk_, gq_, proj, T_GV, npb, state_gla, l, ls)
        om_s, c_s = _sample_state(ma, mk_, mq_, proj, T_MV, npb, state_mlstm_c, l, ls)
        v_s = from_step_major(proj[tp:, T_DV * TILE:(T_DV + 1) * TILE])

        wb = w_branch[l].astype(bf16)
        h = _merge_ffn(h_p, h_s, h_s_row0, (oa_p, oa_s), (om_p, om_s), (od_p, od_s), proj, gates,
                       g_gla_norm[l][None], g_mlstm_norm[l][None], g_diff_norm[l][None],
                       wb[:VW], wb[VW:2 * VW], wb[2 * VW:], w_out[l].astype(bf16), lam_init,
                       g_ffn[l][None], w_ffn_gate[l].astype(bf16), w_ffn_up[l].astype(bf16),
                       w_ffn_down[l].astype(bf16), p_rows, to_step_major(p_sample[l]), l * tp,
                       g_ple[l][None], w_ple_gate[l].astype(bf16), w_ple_proj[l].astype(bf16), g_final[None],
                       l == depth - 1, tp, ts)
        h_p, h_s, h_s_row0 = h, h, tp

        k_p = k_t.reshape(bp, 2 * HEADS, DK, lp).transpose(0, 3, 1, 2)
        k_new = k_ts.reshape(ls, 2 * HEADS, DK, bs).transpose(3, 0, 1, 2)
        outs_p.append((k_p, v_p.reshape(bp, lp, HEADS, DV),
                       gla_p.reshape(bp, HEADS, DK, DV), c_p, n_p, m_p[:, :, 0]))
        outs_s.append((k_new, v_s.reshape(bs, ls, HEADS, DV),
                       gla_s, c_s, n_s.reshape(bs, HEADS, DK), m_s[:, ::DK]))

    y_prompt = h[0].reshape(bp, lp, D_MODEL)
    y_sample = from_step_major(h[1])
    stack = lambda items: [jnp.stack(t) for t in zip(*items)]
    return tuple([y_prompt, y_sample] + stack(outs_p) + stack(outs_s))
```

```python
import functools
import math

import numpy as np
import jax
import jax.numpy as jnp
from jax import lax
from jax.experimental import pallas as pl
from jax.experimental.pallas import tpu as pltpu

f32 = jnp.float32
bf16 = jnp.bfloat16
HIGHEST = lax.Precision.HIGHEST

D_MODEL = 1024
PLE_DIM = 256
HEADS = 4
DK = 64
DV = 128
GLA_RANK = 16
GLA_GATE_NORM = 16.0
ROT_DIM = 16
ROPE_THETA = 500000.0
CHUNK = 64
CHUNK_MLSTM = 256
MLSTM_GROUP = 1
D_FF = 2816
EPS = 1e-6
LOG2E = 1.4426950408889634
VW = HEADS * DV
QW = HEADS * DK

LANES = 128
SUBLANES = 8
VMEM_LIMIT = 48 * 1024 * 1024

TILE = 512
GATE_W = 3 * D_MODEL
T_GQK, T_GV, T_GR = 0, 1, 2
T_MQK, T_MV, T_MO = 3, 4, 5
T_DQ, T_DK, T_DV = 6, 7, 8
N_TILES = 9
P_MIX = N_TILES * TILE
P_MAIN = GATE_W + P_MIX
L_GLR = 0
L_MI = 16
L_MF = 20

TM = 512


def _cparams(*sem):
    return pltpu.CompilerParams(dimension_semantics=sem, vmem_limit_bytes=VMEM_LIMIT)


def _log_sigmoid(x):
    return jnp.minimum(x, 0.0) - jnp.log1p(jnp.exp(-jnp.abs(x)))


def _sigmoid(x):
    return 0.5 * jnp.tanh(0.5 * x) + 0.5


def _rms(x, g):
    return x * lax.rsqrt(jnp.mean(x * x, axis=-1, keepdims=True) + EPS) * g


def _dot(a, b):
    return jnp.dot(a.astype(bf16), b.astype(bf16), preferred_element_type=f32)


def _dot_nt(a, b):
    return lax.dot_general(a.astype(bf16), b.astype(bf16), (((1,), (1,)), ((), ())), preferred_element_type=f32)


def _dot_tn(a, b):
    return lax.dot_general(a.astype(bf16), b.astype(bf16), (((0,), (0,)), ((), ())), preferred_element_type=f32)


def _rope_tile(x, cos, sin_up, sin_dn):
    parts = []
    for c in range(x.shape[1] // LANES):
        xc = x[:, c * LANES:(c + 1) * LANES]
        parts.append(xc * cos + pltpu.roll(xc, LANES - ROT_DIM // 2, axis=1) * sin_up
                     + pltpu.roll(xc, ROT_DIM // 2, axis=1) * sin_dn)
    return jnp.concatenate(parts, axis=1)


def _row_specs(width, n_prompt_blocks, sample_block, tm=None, n_sample_blocks=1, prompt_block0=0):
    tm = TM if tm is None else tm
    return (pl.BlockSpec((tm, width), lambda i: (prompt_block0 + jnp.minimum(i, n_prompt_blocks - 1), 0)),
            pl.BlockSpec((tm, width),
                         lambda i: (sample_block + jnp.clip(i - n_prompt_blocks, 0, n_sample_blocks - 1), 0)))


def _pick_rows(n_prompt_blocks, p_ref, s_ref):
    return jnp.where(pl.program_id(0) >= n_prompt_blocks, s_ref[...], p_ref[...])


TM_IN = 256


def _inproj_kernel(xp_ref, xs_ref, g_ref, w_ref, ws_ref, wgk_ref, bgk_ref, bif_ref, cos_ref, sup_ref, sdn_ref,
                   gates_ref, proj_ref, kt_ref, vp_ref, kts_ref, la_ref, gif_ref, *, n_prompt_blocks, dec_batch):
    i = pl.program_id(0)
    xn = _rms(_pick_rows(n_prompt_blocks, xp_ref, xs_ref), g_ref[...]).astype(bf16)
    small = jnp.dot(xn, ws_ref[...], preferred_element_type=f32)
    z = jnp.dot(small, wgk_ref[...], preferred_element_type=f32, precision=HIGHEST) + bgk_ref[...]
    la_ref[...] = _log_sigmoid(z) * (1.0 / GLA_GATE_NORM)
    gi = small + bif_ref[...]
    lane = lax.broadcasted_iota(jnp.int32, gi.shape, 1)
    gif_ref[...] = jnp.where((lane >= L_MF) & (lane < L_MF + HEADS), _log_sigmoid(gi), gi)

    for c in range(GATE_W // TILE):
        cols = slice(c * TILE, (c + 1) * TILE)
        gates_ref[:, cols] = _sigmoid(jnp.dot(xn, w_ref[:, cols], preferred_element_type=f32)).astype(bf16)

    cos, sup, sdn = cos_ref[...], sup_ref[...], sdn_ref[...]
    for t in range(N_TILES):
        acc = jnp.dot(xn, w_ref[:, GATE_W + t * TILE:GATE_W + (t + 1) * TILE], preferred_element_type=f32)
        if t in (T_DQ, T_DK):
            acc = _rope_tile(acc, cos, sup, sdn)
        proj_ref[:, t * TILE:(t + 1) * TILE] = acc
        if t == T_DK:
            k_t = acc.T
        if t == T_DV:
            v_rows = acc

    @pl.when(i < n_prompt_blocks)
    def _():
        kt_ref[...] = k_t
        for h in range(HEADS):
            vp_ref[pl.ds(h, v_rows.shape[0], stride=HEADS), :] = v_rows[:, h * DV:(h + 1) * DV]

    @pl.when(i >= n_prompt_blocks)
    def _():
        for s in range(k_t.shape[1] // dec_batch):
            kts_ref[s] = k_t[:, s * dec_batch:(s + 1) * dec_batch]


def _inproj(h_p, h_s, sample_block, g_mix, w_main, w_small, w_gk, b_gk, b_if, cos_t, sup_t, sdn_t,
            batch, seq, dec_batch, dec_seq):
    tm = TM_IN
    seq_blocks = seq // tm
    npb = batch * seq_blocks
    steps_per_block = tm // dec_batch
    nsb = dec_seq // steps_per_block
    nt = npb + nsb
    T = nt * tm
    const = lambda i: (0, 0)
    resident = lambda shape: pl.BlockSpec(shape, const, pipeline_mode=pl.Buffered(1))

    def tab_map(i):
        return (jnp.where(i < npb, i % seq_blocks, seq_blocks + i - npb), 0)

    def kt_map(i):
        j = jnp.minimum(i, npb - 1)
        return (j // seq_blocks, 0, j % seq_blocks)

    return pl.pallas_call(
        functools.partial(_inproj_kernel, n_prompt_blocks=npb, dec_batch=dec_batch),
        grid=(nt,),
        in_specs=[
            *_row_specs(D_MODEL, npb, sample_block, tm, nsb),
            pl.BlockSpec((1, D_MODEL), const),
            resident((D_MODEL, P_MAIN)),
            resident((D_MODEL, LANES)),
            resident((LANES, QW)),
            pl.BlockSpec((1, QW), const),
            pl.BlockSpec((1, LANES), const),
            pl.BlockSpec((tm, LANES), tab_map),
            pl.BlockSpec((tm, LANES), tab_map),
            pl.BlockSpec((tm, LANES), tab_map),
        ],
        out_specs=[
            pl.BlockSpec((tm, GATE_W), lambda i: (i, 0)),
            pl.BlockSpec((tm, P_MIX), lambda i: (i, 0)),
            pl.BlockSpec((None, TILE, tm), kt_map),
            pl.BlockSpec((tm * HEADS, DV), lambda i: (jnp.minimum(i, npb - 1), 0)),
            pl.BlockSpec((steps_per_block, TILE, dec_batch), lambda i: (jnp.maximum(i - npb, 0), 0, 0)),
            pl.BlockSpec((tm, QW), lambda i: (i, 0)),
            pl.BlockSpec((tm, LANES), lambda i: (i, 0)),
        ],
        out_shape=[
            jax.ShapeDtypeStruct((T, GATE_W), bf16),
            jax.ShapeDtypeStruct((T, P_MIX), f32),
            jax.ShapeDtypeStruct((batch, TILE, seq), f32),
            jax.ShapeDtypeStruct((batch * seq * HEADS, DV), f32),
            jax.ShapeDtypeStruct((dec_seq, TILE, dec_batch), f32),
            jax.ShapeDtypeStruct((T, QW), f32),
            jax.ShapeDtypeStruct((T, LANES), f32),
        ],
        compiler_params=_cparams("arbitrary"),
        name="inproj",
    )(h_p, h_s, g_mix, w_main, w_small, w_gk, b_gk, b_if, cos_t, sup_t, sdn_t)


def _tril(n):
    r = lax.broadcasted_iota(jnp.int32, (n, n), 0)
    c = lax.broadcasted_iota(jnp.int32, (n, n), 1)
    return r >= c


def _gla_prompt_kernel(*refs, n_chunks, batch):
    q_refs, k_refs, v_refs, la_refs = (refs[i * batch:(i + 1) * batch] for i in range(4))
    o_ref, s_out_ref, s_scr = refs[4 * batch:]
    blk = pl.program_id(0)

    @pl.when(blk == 0)
    def _():
        s_scr[...] = jnp.zeros_like(s_scr)

    ltri = _tril(CHUNK).astype(f32)
    r2 = lax.broadcasted_iota(jnp.int32, (2 * CHUNK, CHUNK), 0)
    c2 = lax.broadcasted_iota(jnp.int32, (2 * CHUNK, CHUNK), 1)
    tril2 = jnp.where(r2 >= CHUNK, r2 - CHUNK, r2) >= c2
    lane = lax.broadcasted_iota(jnp.int32, (CHUNK, LANES), 1)
    lo = (lane < DK).astype(f32)
    hi = 1.0 - lo

    def chunk(c, carry):
        rows = pl.ds(pl.multiple_of(c * CHUNK, CHUNK), CHUNK)
        units = [(s, p) for s in range(batch) for p in range(HEADS // 2)]
        bs_ = [jnp.dot(ltri, la_refs[s][rows, :], preferred_element_type=f32, precision=HIGHEST)
               for s in range(batch)]
        qs, kgs, kd0, kd1, dcol, v0, v1, s_old = {}, {}, {}, {}, {}, {}, {}, {}
        for s in range(batch):
            b = bs_[s]
            b_last = b[CHUNK - 1:CHUNK, :]
            q = q_refs[s][rows, :]
            k = k_refs[s][rows, :]
            qg = q * jnp.exp(b) * (DK ** -0.5)
            kg = k * jnp.exp(-b)
            kd = k * jnp.exp(b_last - b)
            dec = jnp.exp(b_last)
            for p in range(HEADS // 2):
                u = (s, p)
                ls = slice(p * LANES, (p + 1) * LANES)
                qs[u] = jnp.concatenate([qg[:, ls] * lo, qg[:, ls] * hi], axis=0)
                kgs[u] = kg[:, ls]
                kd0[u], kd1[u] = kd[:, ls] * lo, kd[:, ls] * hi
                dcol[u] = jnp.broadcast_to(dec[:, ls], (LANES, LANES)).T
                v0[u] = v_refs[s][rows, pl.ds((2 * p) * DV, DV)]
                v1[u] = v_refs[s][rows, pl.ds((2 * p + 1) * DV, DV)]
                s_old[u] = s_scr[s, p]
        a = {u: _dot_nt(qs[u], kgs[u]) for u in units}
        inter = {u: _dot(qs[u], s_old[u]) for u in units}
        kv = {u: _dot_tn(kd0[u], v0[u]) + _dot_tn(kd1[u], v1[u]) for u in units}
        a = {u: jnp.where(tril2, a[u], 0.0) for u in units}
        intra0 = {u: _dot(a[u][:CHUNK], v0[u]) for u in units}
        intra1 = {u: _dot(a[u][CHUNK:], v1[u]) for u in units}
        for u in units:
            s, p = u
            o_ref[s, rows, pl.ds((2 * p) * DV, DV)] = inter[u][:CHUNK] + intra0[u]
            o_ref[s, rows, pl.ds((2 * p + 1) * DV, DV)] = inter[u][CHUNK:] + intra1[u]
            s_scr[s, p] = dcol[u] * s_old[u] + kv[u]
        return carry

    lax.fori_loop(0, n_chunks, chunk, 0)

    @pl.when(blk == pl.num_programs(0) - 1)
    def _():
        s_out_ref[...] = s_scr[...]


def _stream_specs(width, batch, nb, col):
    return [pl.BlockSpec((TM, width), functools.partial(lambda s, i: (s * nb + i, col), s)) for s in range(batch)]


def _gla_prompt(proj, la, batch, seq):
    nb = seq // TM
    qw_blocks = TILE // QW
    return pl.pallas_call(
        functools.partial(_gla_prompt_kernel, n_chunks=TM // CHUNK, batch=batch),
        grid=(nb,),
        in_specs=[
            *_stream_specs(QW, batch, nb, T_GQK * qw_blocks),
            *_stream_specs(QW, batch, nb, T_GQK * qw_blocks + 1),
            *_stream_specs(VW, batch, nb, T_GV),
            *_stream_specs(QW, batch, nb, 0),
        ],
        out_specs=[
            pl.BlockSpec((batch, TM, VW), lambda i: (0, i, 0)),
            pl.BlockSpec((batch, HEADS // 2, LANES, LANES), lambda i: (0, 0, 0, 0)),
        ],
        out_shape=[
            jax.ShapeDtypeStruct((batch, seq, VW), f32),
            jax.ShapeDtypeStruct((batch, HEADS // 2, LANES, LANES), f32),
        ],
        scratch_shapes=[pltpu.VMEM((batch, HEADS // 2, LANES, LANES), f32)],
        compiler_params=_cparams("arbitrary"),
        name="gla_prompt",
    )(*([proj] * (3 * batch)), *([la] * batch))


def _mlstm_prompt_kernel(*refs, n_chunks, batch):
    q_refs, k_refs, v_refs, gif_refs = (refs[i * batch:(i + 1) * batch] for i in range(4))
    h_ref, c_out_ref, n_out_ref, m_out_ref, ct_scr, n_scr, m_scr = refs[4 * batch:]
    blk = pl.program_id(0)

    @pl.when(blk == 0)
    def _():
        ct_scr[...] = jnp.zeros_like(ct_scr)
        n_scr[...] = jnp.zeros_like(n_scr)
        m_scr[...] = jnp.zeros_like(m_scr)

    CH = CHUNK_MLSTM
    ltri = _tril(CH).astype(f32)
    r_i = lax.broadcasted_iota(jnp.int32, (CH, CH), 0)
    c_i = lax.broadcasted_iota(jnp.int32, (CH, CH), 1)
    upper = r_i <= c_i

    def chunk(c, carry):
        rows = pl.ds(pl.multiple_of(c * CH, CH), CH)
        for sq0 in range(0, batch, MLSTM_GROUP):
            seqs = range(sq0, min(sq0 + MLSTM_GROUP, batch))
            units = [(sq, h) for sq in seqs for h in range(HEADS)]
            gate = {}
            for sq in seqs:
                g = gif_refs[sq][rows, :]
                fc = jnp.dot(ltri, g, preferred_element_type=f32, precision=HIGHEST)
                gate[sq] = (g, fc, g.T, fc.T)
            q, k, v_t, c_t, n_all = {}, {}, {}, {}, {}
            for sq in seqs:
                q_all = q_refs[sq][rows, :]
                k_all = k_refs[sq][rows, :] * (DK ** -0.5)
                n_all[sq] = n_scr[sq]
                for h in range(HEADS):
                    u = (sq, h)
                    q[u] = q_all[:, h * DK:(h + 1) * DK]
                    k[u] = k_all[:, h * DK:(h + 1) * DK]
                    v_t[u] = v_refs[sq][rows, pl.ds(h * DV, DV)].T
                    c_t[u] = ct_scr[sq, h]
            kq = {u: _dot_nt(k[u], q[u]) for u in units}
            cq = {u: _dot_nt(c_t[u], q[u]) for u in units}
            qn = {u: _dot_nt(n_all[u[0]], q[u])[u[1]:u[1] + 1, :] for u in units}
            sc, wk, w_inter, floor, decay, m_new = {}, {}, {}, {}, {}, {}
            for u in units:
                sq, h = u
                g, fc, g_t, fc_t = gate[sq]
                i_col = g[:, L_MI + h:L_MI + h + 1]
                f_col = fc[:, L_MF + h:L_MF + h + 1]
                f_row = fc_t[L_MF + h:L_MF + h + 1, :]
                f_last = fc[CH - 1:CH, L_MF + h:L_MF + h + 1]
                m_prev = m_scr[sq, h:h + 1, 0:1]
                log_d = jnp.where(upper, f_row + (i_col - f_col), -jnp.inf)
                m_inter = m_prev + f_row
                m_row = jnp.maximum(m_inter, jnp.max(log_d, axis=0, keepdims=True))
                w_inter[u] = jnp.exp(m_inter - m_row)
                floor[u] = jnp.exp(-m_row)
                sc[u] = kq[u] * jnp.exp(log_d - m_row)
                log_w = f_last - f_col + i_col
                m_new[u] = jnp.maximum(m_prev + f_last, jnp.max(log_w, axis=0, keepdims=True))
                wk[u] = jnp.exp(log_w - m_new[u]) * k[u]
                decay[u] = jnp.exp(m_prev + f_last - m_new[u])
            vs = {u: _dot(v_t[u], sc[u]) for u in units}
            vk = {u: _dot(v_t[u], wk[u]) for u in units}
            for u in units:
                sq, h = u
                num = w_inter[u] * cq[u] + vs[u]
                den = w_inter[u] * qn[u] + jnp.sum(sc[u], axis=0, keepdims=True)
                h_ref[sq, rows, pl.ds(h * DV, DV)] = (num / jnp.maximum(jnp.abs(den), floor[u])).T
                ct_scr[sq, h] = decay[u] * c_t[u] + vk[u]
                n_scr[sq, h:h + 1, :] = decay[u] * n_all[sq][h:h + 1, :] + jnp.sum(wk[u], axis=0, keepdims=True)
                m_scr[sq, h:h + 1, :] = jnp.broadcast_to(m_new[u], (1, LANES))
        return carry

    lax.fori_loop(0, n_chunks, chunk, 0)

    @pl.when(blk == pl.num_programs(0) - 1)
    def _():
        for sq in range(batch):
            for h in range(HEADS):
                c_out_ref[sq, h] = ct_scr[sq, h].T
        n_out_ref[...] = n_scr[:, 0:HEADS, :]
        m_out_ref[...] = m_scr[:, 0:HEADS, :]


def _mlstm_prompt(proj, gif, batch, seq):
    nb = seq // TM
    qw_blocks = TILE // QW
    return pl.pallas_call(
        functools.partial(_mlstm_prompt_kernel, n_chunks=TM // CHUNK_MLSTM, batch=batch),
        grid=(nb,),
        in_specs=[
            *_stream_specs(QW, batch, nb, T_MQK * qw_blocks),
            *_stream_specs(QW, batch, nb, T_MQK * qw_blocks + 1),
            *_stream_specs(VW, batch, nb, T_MV),
            *_stream_specs(LANES, batch, nb, 0),
        ],
        out_specs=[
            pl.BlockSpec((batch, TM, VW), lambda i: (0, i, 0)),
            pl.BlockSpec((batch, HEADS, DK, DV), lambda i: (0, 0, 0, 0)),
            pl.BlockSpec((batch, HEADS, DK), lambda i: (0, 0, 0)),
            pl.BlockSpec((batch, HEADS, LANES), lambda i: (0, 0, 0)),
        ],
        out_shape=[
            jax.ShapeDtypeStruct((batch, seq, VW), f32),
            jax.ShapeDtypeStruct((batch, HEADS, DK, DV), f32),
            jax.ShapeDtypeStruct((batch, HEADS, DK), f32),
            jax.ShapeDtypeStruct((batch, HEADS, LANES), f32),
        ],
        scratch_shapes=[
            pltpu.VMEM((batch, HEADS, DV, DK), f32),
            pltpu.VMEM((batch, SUBLANES, DK), f32),
            pltpu.VMEM((batch, SUBLANES, LANES), f32),
        ],
        compiler_params=_cparams("arbitrary"),
        name="mlstm_prompt",
    )(*([proj] * (3 * batch)), *([gif] * batch))


def _diff_lambda_value(lam_ref, lam_init):
    lam = lam_ref[...]
    s1 = jnp.sum(lam[0:1] * lam[1:2], axis=1, keepdims=True)
    s2 = jnp.sum(lam[2:3] * lam[3:4], axis=1, keepdims=True)
    return jnp.exp(s1) - jnp.exp(s2) + lam_init


def _attn_prompt_step(p, qi_ref, ki_ref, q_ref, k_ref, v_ref, lam_ref, o_ref, q2_scr, m_scr, l_scr, acc_scr,
                      *, tq, lam_init, tail):
    qi = qi_ref[p]
    ki = ki_ref[p]
    pair = lambda j: slice(j * LANES, (j + 1) * LANES)

    @pl.when(ki == 0)
    def _():
        for j in range(HEADS):
            q = q_ref[:, pair(j)] * (DK ** -0.5 * LOG2E)
            lane = lax.broadcasted_iota(jnp.int32, q.shape, 1)
            q2_scr[j, 0:tq, :] = jnp.where(lane < DK, q, 0.0).astype(bf16)
            q2_scr[j, tq:2 * tq, :] = jnp.where(lane >= DK, q, 0.0).astype(bf16)
        m_scr[...] = jnp.full_like(m_scr, -jnp.inf)
        l_scr[...] = jnp.zeros_like(l_scr)
        acc_scr[...] = jnp.zeros_like(acc_scr)

    def step(masked):
        sts = [_dot_nt(k_ref[:, pair(j)], q2_scr[j]) for j in range(HEADS)]
        pts, alphas = [], []
        for j in range(HEADS):
            st = sts[j]
            if masked:
                r = lax.broadcasted_iota(jnp.int32, st.shape, 0)
                c = lax.broadcasted_iota(jnp.int32, st.shape, 1)
                c = jnp.where(c >= tq, c - tq, c)
                st = jnp.where(r <= c, st, -jnp.inf)
            m_prev = m_scr[j]
            m_new = jnp.maximum(m_prev, jnp.max(st, axis=0, keepdims=True))
            alpha = jnp.exp2(m_prev - m_new)
            pt = jnp.exp2(st - m_new)
            l_scr[j] = alpha * l_scr[j] + jnp.sum(pt, axis=0, keepdims=True)
            m_scr[j] = m_new
            pts.append(pt.astype(bf16))
            alphas.append(alpha)
        for j in range(HEADS):
            acc_scr[j] = alphas[j] * acc_scr[j] + _dot_tn(v_ref[:, pair(j)], pts[j])

    @pl.when(ki < qi)
    def _():
        step(False)
        tail()

    @pl.when(ki == qi)
    def _():
        step(True)
        lam = _diff_lambda_value(lam_ref, lam_init)
        for j in range(HEADS):
            ot = acc_scr[j] / l_scr[j]
            o_ref[:, pair(j)] = (ot[:, 0:tq] - lam * ot[:, tq:2 * tq]).T
        tail()


def _sample_page_copies(seq_idx, slot, pt_ref, ck_hbm, cv_hbm, kbuf, vbuf, sem, *, layer, n_pages, page):
    out = []
    for pg in range(n_pages):
        src = pt_ref[seq_idx, pg]
        out.append(pltpu.make_async_copy(ck_hbm.at[layer, src], kbuf.at[slot, :, pl.ds(pg * page, page)],
                                         sem.at[0, slot]))
        out.append(pltpu.make_async_copy(cv_hbm.at[layer, src],
                                         vbuf.at[slot, pl.ds(pg * page * HEADS, page * HEADS), :], sem.at[1, slot]))
    return out


def _attn_sample_pages(step, n_steps, pt_ref, ck_hbm, cv_hbm, o_ref, kbuf, vbuf, sem, *, nseq, pages):
    past = pages["n_pages"] * pages["page"]
    seq = lambda st: jnp.minimum(st, nseq - 1)
    slot = step % 2
    copies = functools.partial(_sample_page_copies, pt_ref=pt_ref, ck_hbm=ck_hbm, cv_hbm=cv_hbm, kbuf=kbuf,
                               vbuf=vbuf, sem=sem, **pages)

    @pl.when(step == 0)
    def _():
        kbuf[:, :, pl.ds(past, LANES)] = jnp.zeros((2, kbuf.shape[1], LANES), f32)
        vbuf[:, pl.ds(past * HEADS, LANES * HEADS), :] = jnp.zeros((2, LANES * HEADS, DV), f32)
        o_ref[...] = jnp.zeros_like(o_ref)
        for cp in copies(0, 0):
            cp.start()

    @pl.when(step + 1 < n_steps)
    def _():
        for cp in copies(seq(step + 1), 1 - slot):
            cp.start()

    for cp in copies(seq(step), slot):
        cp.wait()


def _attn_sample_compute(step, q_ref, kts_ref, v_ref, lam_ref, o_ref, kbuf, vbuf,
                         *, n_pages, page, dec_seq, nseq, lam_init):
    b = jnp.minimum(step, nseq - 1)
    slot = step % 2
    past = n_pages * page
    n_qh = 2 * HEADS

    shift = lax.rem(nseq - b, nseq)
    b8 = (b // SUBLANES) * SUBLANES
    groups = [pl.ds(pl.multiple_of(t * nseq + b8, SUBLANES), SUBLANES) for t in range(dec_seq)]
    mine = lax.broadcasted_iota(jnp.int32, (SUBLANES, 2 * QW), 0) == b - b8
    pick = lambda ref, t: jnp.sum(jnp.where(mine, ref[groups[t], :], 0.0), axis=0, keepdims=True)
    head_of_lane = lax.broadcasted_iota(jnp.int32, (n_qh, 2 * QW), 1) // DK
    head_of_row = lax.broadcasted_iota(jnp.int32, (n_qh, 2 * QW), 0)
    q_rows = []
    for t in range(dec_seq):
        kbuf[slot, :, pl.ds(past + t, 1)] = pltpu.roll(kts_ref[t], shift, axis=1)[:, 0:1]
        v_row = pick(v_ref, t)
        for h in range(HEADS):
            vbuf[slot, pl.ds((past + t) * HEADS + h, 1), :] = v_row[:, h * DV:(h + 1) * DV]
        q = jnp.broadcast_to(pick(q_ref, t) * (DK ** -0.5), (n_qh, 2 * QW))
        q_rows.append(jnp.where(head_of_lane == head_of_row, q, 0.0))
    qbd = jnp.concatenate(q_rows, axis=0)

    s = _dot(qbd, kbuf[slot])
    r = lax.broadcasted_iota(jnp.int32, s.shape, 0)
    c = lax.broadcasted_iota(jnp.int32, s.shape, 1)
    s = jnp.where(c - past <= r // n_qh, s, -jnp.inf)
    m = jnp.max(s, axis=1, keepdims=True)
    pr = jnp.exp(s - m)
    inv_l = 1.0 / jnp.sum(pr, axis=1, keepdims=True)
    lam = _diff_lambda_value(lam_ref, lam_init)
    mine_v = mine[:, 0:DV]
    for hv in range(HEADS):
        v_h = vbuf[slot, pl.ds(hv, past + LANES, stride=HEADS), :]
        o_h = _dot(pr, v_h) * inv_l
        for t in range(dec_seq):
            r1 = t * n_qh + 2 * hv
            d = o_h[r1:r1 + 1] - lam * o_h[r1 + 1:r1 + 2]
            lanes = slice(hv * DV, (hv + 1) * DV)
            o_ref[groups[t], lanes] = jnp.where(mine_v, d, o_ref[groups[t], lanes])


def _attn_kernel(qi_ref, ki_ref, pt_ref, q_ref, k_ref, v_ref, lam_ref, qs_ref, kts_ref, vs_ref, ck_hbm, cv_hbm,
                 o_ref, os_ref, q2_scr, m_scr, l_scr, acc_scr, kbuf, vbuf, sem, *, prompt, sample, nseq, n_pairs):
    p = pl.program_id(1)
    step = pl.program_id(0) * pl.num_programs(1) + p
    n_steps = pl.num_programs(0) * pl.num_programs(1)
    pages = dict(layer=sample["layer"], n_pages=sample["n_pages"], page=sample["page"])
    _attn_sample_pages(step, n_steps, pt_ref, ck_hbm, cv_hbm, os_ref, kbuf, vbuf, sem, nseq=nseq, pages=pages)

    def sample_work():
        _attn_sample_compute(step, qs_ref, kts_ref, vs_ref, lam_ref, os_ref, kbuf, vbuf, nseq=nseq,
                             n_pages=sample["n_pages"], page=sample["page"], dec_seq=sample["dec_seq"],
                             lam_init=sample["lam_init"])

    @pl.when(p < n_pairs)
    def _():
        _attn_prompt_step(p, qi_ref, ki_ref, q_ref, k_ref, v_ref, lam_ref, o_ref,
                          q2_scr, m_scr, l_scr, acc_scr, tail=sample_work, **prompt)

    @pl.when(p >= n_pairs)
    def _():
        sample_work()


def _attention(proj, k_ts, page_table, diff_lambda, cache_kt, cache_v2, layer, lam_init, batch, seq, sample_block,
               tq=512):
    nq = seq // tq
    pairs = [(q, k) for q in range(nq) for k in range(q + 1)]
    qi_tab = jnp.asarray(np.array([a for a, _ in pairs], np.int32))
    ki_tab = jnp.asarray(np.array([b for _, b in pairs], np.int32))
    nseq, n_pages = page_table.shape
    n_pairs = len(pairs)
    steps = max(n_pairs, -(-nseq // batch))
    last = n_pairs - 1
    dec_seq = k_ts.shape[0]
    rows = dec_seq * nseq
    page = cache_kt.shape[3]
    cols = n_pages * page + LANES
    grid_spec = pltpu.PrefetchScalarGridSpec(
        num_scalar_prefetch=3,
        grid=(batch, steps),
        in_specs=[
            pl.BlockSpec((tq, TILE), lambda b, p, qt, kt, pt: (b * nq + qt[jnp.minimum(p, last)], T_DQ)),
            pl.BlockSpec((tq, TILE), lambda b, p, qt, kt, pt: (b * nq + kt[jnp.minimum(p, last)], T_DK)),
            pl.BlockSpec((tq, TILE), lambda b, p, qt, kt, pt: (b * nq + kt[jnp.minimum(p, last)], T_DV)),
            pl.BlockSpec((4, DK), lambda b, p, qt, kt, pt: (0, 0)),
            pl.BlockSpec((rows, TILE), lambda b, p, qt, kt, pt: (sample_block, T_DQ)),
            pl.BlockSpec((dec_seq, TILE, nseq), lambda b, p, qt, kt, pt: (0, 0, 0)),
            pl.BlockSpec((rows, TILE), lambda b, p, qt, kt, pt: (sample_block, T_DV)),
            pl.BlockSpec(memory_space=pl.ANY),
            pl.BlockSpec(memory_space=pl.ANY),
        ],
        out_specs=[
            pl.BlockSpec((tq, VW), lambda b, p, qt, kt, pt: (b * nq + qt[jnp.minimum(p, last)], 0)),
            pl.BlockSpec((rows, VW), lambda b, p, qt, kt, pt: (0, 0)),
        ],
        scratch_shapes=[
            pltpu.VMEM((HEADS, 2 * tq, LANES), bf16),
            pltpu.VMEM((HEADS, 1, 2 * tq), f32),
            pltpu.VMEM((HEADS, 1, 2 * tq), f32),
            pltpu.VMEM((HEADS, DV, 2 * tq), f32),
            pltpu.VMEM((2, 2 * QW, cols), f32),
            pltpu.VMEM((2, cols * HEADS, DV), f32),
            pltpu.SemaphoreType.DMA((2, 2)),
        ],
    )
    return pl.pallas_call(
        functools.partial(_attn_kernel, nseq=nseq, n_pairs=n_pairs, prompt=dict(tq=tq, lam_init=lam_init),
                          sample=dict(layer=layer, n_pages=n_pages, page=page, dec_seq=dec_seq, lam_init=lam_init)),
        grid_spec=grid_spec,
        out_shape=[jax.ShapeDtypeStruct((batch * seq, VW), f32), jax.ShapeDtypeStruct((rows, VW), f32)],
        compiler_params=_cparams("arbitrary", "arbitrary"),
        name="attention",
    )(qi_tab, ki_tab, page_table, proj, proj, proj, diff_lambda, proj, k_ts, proj, cache_kt, cache_v2)


def _head_of_lane(shape):
    return lax.broadcasted_iota(jnp.int32, shape, 1) // DK


def _expand_heads(cols, base, head):
    out = cols[:, base + HEADS - 1:base + HEADS]
    for h in range(HEADS - 2, -1, -1):
        out = jnp.where(head == h, cols[:, base + h:base + h + 1], out)
    return out


def _sample_prep_kernel(gq_ref, gk_ref, la_ref, mq_ref, mk_ref, gif_ref, n0_ref, m0_ref,
                        ga_ref, gkk_ref, gqq_ref, ma_ref, mkk_ref, mqq_ref, n_out_ref, m_out_ref, *, nseq, dec_seq):
    head = _head_of_lane((nseq, QW))
    m = _expand_heads(m0_ref[...], 0, head)
    n = n0_ref[...]
    for t in range(dec_seq):
        rows = pl.ds(t * nseq, nseq)
        ga_ref[t] = jnp.exp(la_ref[rows, :]).T
        gkk_ref[t] = gk_ref[rows, :].T
        gqq_ref[t] = (gq_ref[rows, :] * (DK ** -0.5)).T
        gates = gif_ref[rows, :]
        i_e = _expand_heads(gates, L_MI, head)
        f_e = _expand_heads(gates, L_MF, head)
        k = mk_ref[rows, :] * (DK ** -0.5)
        q = mq_ref[rows, :]
        m_new = jnp.maximum(f_e + m, i_e)
        fp = jnp.exp(f_e + m - m_new)
        ip = jnp.exp(i_e - m_new)
        n = fp * n + ip * k
        nq = n * q
        den = jnp.zeros_like(nq)
        for h in range(HEADS):
            den = jnp.where(head == h, jnp.sum(jnp.where(head == h, nq, 0.0), axis=1, keepdims=True), den)
        inv = 1.0 / jnp.maximum(jnp.abs(den), jnp.exp(-m_new))
        ma_ref[t] = fp.T
        mkk_ref[t] = (ip * k).T
        mqq_ref[t] = (q * inv).T
        m = m_new
    n_out_ref[...] = n
    m_out_ref[...] = m


def _sample_prep(proj, la, gif, state_n, state_m, layer, sample_block, nseq, dec_seq):
    rows = nseq * dec_seq
    qw_blocks = TILE // QW
    tok = jax.ShapeDtypeStruct((dec_seq, QW, nseq), f32)
    st = jax.ShapeDtypeStruct((nseq, QW), f32)
    tok_spec = pl.BlockSpec((dec_seq, QW, nseq), lambda i: (0, 0, 0))
    st_spec = pl.BlockSpec((nseq, QW), lambda i: (0, 0))
    col = lambda c: pl.BlockSpec((rows, QW), lambda i: (sample_block, c))
    return pl.pallas_call(
        functools.partial(_sample_prep_kernel, nseq=nseq, dec_seq=dec_seq),
        grid=(1,),
        in_specs=[
            col(T_GQK * qw_blocks), col(T_GQK * qw_blocks + 1), col(0),
            col(T_MQK * qw_blocks), col(T_MQK * qw_blocks + 1),
            pl.BlockSpec((rows, LANES), lambda i: (sample_block, 0)),
            pl.BlockSpec((None, nseq, QW), lambda i: (layer, 0, 0)),
            pl.BlockSpec((None, nseq, HEADS), lambda i: (layer, 0, 0)),
        ],
        out_specs=[tok_spec] * 6 + [st_spec] * 2,
        out_shape=[tok] * 6 + [st] * 2,
        compiler_params=_cparams("arbitrary"),
        name="sample_prep",
    )(proj, proj, la, proj, proj, gif, state_n, state_m)


SEQ_PER_STEP = 8


def _sample_state_kernel(a_ref, k_ref, q_ref, v_ref, s_ref, o_ref, s_out_ref, *, dec_seq, nseq):
    blk = pl.program_id(0)
    seq0 = blk * SEQ_PER_STEP
    shift = lax.rem(nseq - seq0, nseq)
    a_t = [pltpu.roll(a_ref[t], shift, axis=1) for t in range(dec_seq)]
    k_t = [pltpu.roll(k_ref[t], shift, axis=1) for t in range(dec_seq)]
    q_t = [pltpu.roll(q_ref[t], shift, axis=1) for t in range(dec_seq)]
    toks = [pl.ds(pl.multiple_of(t * nseq + seq0, SEQ_PER_STEP), SEQ_PER_STEP) for t in range(dec_seq)]
    v_t = [v_ref[toks[t], :] for t in range(dec_seq)]
    row = lax.broadcasted_iota(jnp.int32, (SEQ_PER_STEP, DV), 0)
    outs = [[jnp.zeros((SEQ_PER_STEP, DV), f32) for _ in range(HEADS)] for _ in range(dec_seq)]
    for bb in range(SEQ_PER_STEP):
        for h in range(HEADS):
            s = s_ref[bb, h]
            rows = slice(h * DK, (h + 1) * DK)
            for t in range(dec_seq):
                a = a_t[t][rows, bb:bb + 1]
                k = k_t[t][rows, bb:bb + 1]
                q = q_t[t][rows, bb:bb + 1]
                v = v_t[t][bb:bb + 1, h * DV:(h + 1) * DV]
                s = a * s + k * v
                outs[t][h] = jnp.where(row == bb, jnp.sum(q * s, axis=0, keepdims=True), outs[t][h])
            s_out_ref[bb, h] = s
    for t in range(dec_seq):
        for h in range(HEADS):
            o_ref[toks[t], h * DV:(h + 1) * DV] = outs[t][h]


def _sample_state(a_t, k_t, q_t, proj, v_tile, sample_block, state, layer, dec_seq):
    nseq = a_t.shape[2]
    rows = dec_seq * nseq
    full = pl.BlockSpec((dec_seq, QW, nseq), lambda i: (0, 0, 0))
    return pl.pallas_call(
        functools.partial(_sample_state_kernel, dec_seq=dec_seq, nseq=nseq),
        grid=(nseq // SEQ_PER_STEP,),
        in_specs=[
            full, full, full,
            pl.BlockSpec((rows, VW), lambda i: (sample_block, v_tile)),
            pl.BlockSpec((None, SEQ_PER_STEP, HEADS, DK, DV), lambda i: (layer, i, 0, 0, 0)),
        ],
        out_specs=[
            pl.BlockSpec((rows, VW), lambda i: (0, 0)),
            pl.BlockSpec((SEQ_PER_STEP, HEADS, DK, DV), lambda i: (i, 0, 0, 0)),
        ],
        out_shape=[
            jax.ShapeDtypeStruct((rows, VW), f32),
            jax.ShapeDtypeStruct(state.shape[1:], f32),
        ],
        compiler_params=_cparams("arbitrary"),
        name="sample_state",
    )(a_t, k_t, q_t, proj, state)


def _head_rms(x, g):
    parts = []
    for h in range(HEADS):
        parts.append(_rms(x[:, h * DV:(h + 1) * DV], g))
    return jnp.concatenate(parts, axis=1)


FF_SLABS = ((0, 768), (768, 768), (1536, 768), (2304, 512))
TM_CH = 256


def _merge_ffn_kernel(hp_ref, hs_ref, oap_ref, oas_ref, omp_ref, oms_ref, odp_ref, ods_ref, gr_ref, mo_ref,
                      g0_ref, g1_ref, g2_ref, gg_ref, gm_ref, gd_ref, wa_ref, wm_ref, wd_ref, wo_ref,
                      gf_ref, wg_ref, wu_ref, wdn_ref, pp_ref, ps_ref, gp_ref, wpg_ref, wpp_ref, gfin_ref,
                      *out_refs, lam_init, final, n_prompt_blocks):
    pick = functools.partial(_pick_rows, n_prompt_blocks)
    gr = gr_ref[...]
    a = _head_rms(pick(oap_ref, oas_ref), gg_ref[...]) * (gr * _sigmoid(gr))
    m = _sigmoid(mo_ref[...]) * _head_rms(pick(omp_ref, oms_ref), gm_ref[...])
    d = _head_rms(pick(odp_ref, ods_ref), gd_ref[...]) * (1.0 - lam_init)
    merged = (g0_ref[...] * _dot(a, wa_ref[...]) + g1_ref[...] * _dot(m, wm_ref[...])
              + g2_ref[...] * _dot(d, wd_ref[...]))
    h = pick(hp_ref, hs_ref) + _dot(merged, wo_ref[...])

    xf = _rms(h, gf_ref[...]).astype(bf16)
    h2 = h
    for lo, width in FF_SLABS:
        gate = jnp.dot(xf, wg_ref[:, lo:lo + width], preferred_element_type=f32)
        up = jnp.dot(xf, wu_ref[:, lo:lo + width], preferred_element_type=f32)
        h2 = h2 + _dot(gate * _sigmoid(gate) * up, wdn_ref[lo:lo + width, :])
    ple_gate = _sigmoid(_dot(_rms(h2, gp_ref[...]), wpg_ref[...]))
    h3 = h2 + ple_gate * _dot(pick(pp_ref, ps_ref), wpp_ref[...])
    if not final:
        out_refs[0][...] = h3
        return
    y = _rms(h3, gfin_ref[...])
    yp_ref, ys_ref = out_refs
    i = pl.program_id(0)

    @pl.when(i < n_prompt_blocks)
    def _():
        yp_ref[...] = y

    @pl.when(i >= n_prompt_blocks)
    def _():
        ys_ref[...] = y


def _merge_ffn(h_p, h_s, h_s_row0, o_a, o_m, o_d, proj, gates, g_gla, g_ml, g_diff, w_a, w_m, w_d, w_out, lam_init,
               g_ffn, w_g, w_u, w_dn, p_p, p_s, p_row0, g_ple, w_pg, w_pp, g_final, final, n_prompt_rows, n_sample_rows):
    tm = TM_CH
    npb, nsb = n_prompt_rows // tm, n_sample_rows // tm
    T = n_prompt_rows + n_sample_rows
    row = lambda i: (i, 0)
    const = lambda i: (0, 0)
    resident = lambda shape: pl.BlockSpec(shape, const, pipeline_mode=pl.Buffered(1))
    rows2 = lambda width, sample_block=0, prompt_block0=0: _row_specs(width, npb, sample_block, tm, nsb, prompt_block0)
    assert FF_SLABS[-1][0] + FF_SLABS[-1][1] == D_FF
    if final:
        out_specs = [pl.BlockSpec((tm, D_MODEL), lambda i: (jnp.minimum(i, npb - 1), 0)),
                     pl.BlockSpec((tm, D_MODEL), lambda i: (jnp.clip(i - npb, 0, nsb - 1), 0))]
        out_shape = [jax.ShapeDtypeStruct((n_prompt_rows, D_MODEL), f32),
                     jax.ShapeDtypeStruct((n_sample_rows, D_MODEL), f32)]
    else:
        out_specs = pl.BlockSpec((tm, D_MODEL), row)
        out_shape = jax.ShapeDtypeStruct((T, D_MODEL), f32)
    return pl.pallas_call(
        functools.partial(_merge_ffn_kernel, lam_init=lam_init, final=final, n_prompt_blocks=npb),
        grid=(npb + nsb,),
        in_specs=[
            *rows2(D_MODEL, sample_block=h_s_row0 // tm),
            *rows2(VW), *rows2(VW), *rows2(VW),
            pl.BlockSpec((tm, VW), lambda i: (i, T_GR)),
            pl.BlockSpec((tm, VW), lambda i: (i, T_MO)),
            pl.BlockSpec((tm, D_MODEL), lambda i: (i, 0)),
            pl.BlockSpec((tm, D_MODEL), lambda i: (i, 1)),
            pl.BlockSpec((tm, D_MODEL), lambda i: (i, 2)),
            pl.BlockSpec((1, DV), const),
            pl.BlockSpec((1, DV), const),
            pl.BlockSpec((1, DV), const),
            resident((VW, D_MODEL)), resident((VW, D_MODEL)), resident((VW, D_MODEL)),
            resident((D_MODEL, D_MODEL)),
            pl.BlockSpec((1, D_MODEL), const),
            resident((D_MODEL, D_FF)), resident((D_MODEL, D_FF)), resident((D_FF, D_MODEL)),
            *rows2(PLE_DIM, prompt_block0=p_row0 // tm),
            pl.BlockSpec((1, D_MODEL), const),
            resident((D_MODEL, D_MODEL)),
            resident((PLE_DIM, D_MODEL)),
            pl.BlockSpec((1, D_MODEL), const),
        ],
        out_specs=out_specs,
        out_shape=out_shape,
        compiler_params=_cparams("arbitrary"),
        name="merge_ffn",
    )(h_p, h_s, *o_a, *o_m, *o_d, proj, proj, gates, gates, gates, g_gla, g_ml, g_diff, w_a, w_m, w_d, w_out,
      g_ffn, w_g, w_u, w_dn, p_p, p_s, g_ple, w_pg, w_pp, g_final)


def _rope_tables(positions):
    half = ROT_DIM // 2
    inv_freq = ROPE_THETA ** (-jnp.arange(half, dtype=f32) * 2.0 / ROT_DIM)
    ang = positions.astype(f32)[:, None] * inv_freq[None, :]
    cos, sin = jnp.cos(ang), jnp.sin(ang)
    n = positions.shape[0]
    one = jnp.ones((n, DK - ROT_DIM), f32)
    zero = jnp.zeros((n, DK - ROT_DIM), f32)
    zh = jnp.zeros((n, half), f32)
    cos_h = jnp.concatenate([cos, cos, one], axis=1)
    up_h = jnp.concatenate([-sin, zh, zero], axis=1)
    dn_h = jnp.concatenate([zh, sin, zero], axis=1)
    rep = lambda a: jnp.concatenate([a] * (LANES // DK), axis=1)
    return rep(cos_h), rep(up_h), rep(dn_h)


def _prep_in_weights(w_in_l, w_gate_l):
    o = np.cumsum((0, QW, QW, VW, VW, GLA_RANK, QW, QW, VW, VW, 2 * HEADS, 2 * QW, 2 * QW, VW))
    seg = lambda i: w_in_l[:, int(o[i]):int(o[i + 1])]
    gq, gk, gv, gr, glr, mq, mk, mv, mo, mif, dq, dk, dv = (seg(i) for i in range(13))
    main = jnp.concatenate([w_gate_l, gq, gk, gv, gr, mq, mk, mv, mo, dq, dk, dv], axis=1).astype(bf16)
    pad = jnp.zeros((D_MODEL, LANES - GLA_RANK - 2 * HEADS), f32)
    small = jnp.concatenate([glr, mif, pad], axis=1).astype(bf16)
    return main, small


def kernel(x_prompt, x_sample, cache_k, cache_v, state_gla, state_mlstm_c, state_mlstm_n, state_mlstm_m, page_table, p_prompt, p_sample, g_mix, w_in, w_gla_gk, b_gla_gk, g_gla_norm, b_mlstm_if, g_mlstm_norm, diff_lambda, g_diff_norm, w_branch, w_gate, w_out, g_ffn, w_ffn_gate, w_ffn_up, w_ffn_down, g_ple, w_ple_gate, w_ple_proj, g_final):
    bp, lp, _ = x_prompt.shape
    bs, ls, _ = x_sample.shape
    depth = g_mix.shape[0]
    n_pages = page_table.shape[1]
    page = cache_k.shape[2]
    past = n_pages * page
    tp = bp * lp
    ts = bs * ls
    assert lp % TM == 0 and ts == TM and bs == LANES and page == LANES and TM_IN % bs == 0
    npb = tp // TM

    def to_step_major(a):
        return jnp.swapaxes(a, 0, 1).reshape((ts,) + a.shape[2:])

    def from_step_major(a):
        return jnp.swapaxes(a.reshape((ls, bs) + a.shape[1:]), 0, 1)

    h_p, h_s, h_s_row0 = x_prompt.reshape(tp, D_MODEL), to_step_major(x_sample), 0

    pos_rows = jnp.concatenate([jnp.arange(lp, dtype=jnp.int32),
                                past + jnp.repeat(jnp.arange(ls, dtype=jnp.int32), bs)])
    cos_t, sup_t, sdn_t = _rope_tables(pos_rows)

    ck = jnp.transpose(cache_k, (0, 1, 3, 4, 2)).reshape(cache_k.shape[0], cache_k.shape[1], 2 * QW, page)
    cv = cache_v.reshape(cache_v.shape[0], cache_v.shape[1], page * HEADS, DV)
    state_n = state_mlstm_n.reshape(depth, bs, QW)
    p_rows = p_prompt.reshape(depth * tp, PLE_DIM)
    outs_p, outs_s = [], []
    for l in range(depth):
        lam_init = 0.8 - 0.6 * math.exp(-0.3 * l)
        w_main, w_small = _prep_in_weights(w_in[l], w_gate[l])
        w_gk = jnp.concatenate([w_gla_gk[l], jnp.zeros((LANES - GLA_RANK, QW), f32)], axis=0)
        b_if = jnp.zeros((1, LANES), f32).at[0, L_MI:L_MI + 2 * HEADS].set(b_mlstm_if[l])
        gates, proj, k_t, v_p, k_ts, la, gif = _inproj(h_p, h_s, h_s_row0 // TM_IN, g_mix[l][None], w_main, w_small, w_gk,
                                                  b_gla_gk[l][None], b_if, cos_t, sup_t, sdn_t, bp, lp, bs, ls)

        oa_p, gla_p = _gla_prompt(proj, la, bp, lp)
        om_p, c_p, n_p, m_p = _mlstm_prompt(proj, gif, bp, lp)
        oa_p, om_p = oa_p.reshape(tp, VW), om_p.reshape(tp, VW)
        od_p, od_s = _attention(proj, k_ts, page_table, diff_lambda[l], ck, cv, l, lam_init, bp, lp, npb)

        ga, gk_, gq_, ma, mk_, mq_, n_s, m_s = _sample_prep(proj, la, gif, state_n, state_mlstm_m, l, npb, bs, ls)
        oa_s, gla_s = _sample_state(ga, gk_, gq_, proj, T_GV, npb, state_gla, l, ls)
        om_s, c_s = _sample_state(ma, mk_, mq_, proj, T_MV, npb, state_mlstm_c, l, ls)
        v_s = from_step_major(proj[tp:, T_DV * TILE:(T_DV + 1) * TILE])

        wb = w_branch[l].astype(bf16)
        h = _merge_ffn(h_p, h_s, h_s_row0, (oa_p, oa_s), (om_p, om_s), (od_p, od_s), proj, gates,
                       g_gla_norm[l][None], g_mlstm_norm[l][None], g_diff_norm[l][None],
                       wb[:VW], wb[VW:2 * VW], wb[2 * VW:], w_out[l].astype(bf16), lam_init,
                       g_ffn[l][None], w_ffn_gate[l].astype(bf16), w_ffn_up[l].astype(bf16),
                       w_ffn_down[l].astype(bf16), p_rows, to_step_major(p_sample[l]), l * tp,
                       g_ple[l][None], w_ple_gate[l].astype(bf16), w_ple_proj[l].astype(bf16), g_final[None],
                       l == depth - 1, tp, ts)
        h_p, h_s, h_s_row0 = h, h, tp

        k_p = k_t.reshape(bp, 2 * HEADS, DK, lp).transpose(0, 3, 1, 2)
        k_new = k_ts.reshape(ls, 2 * HEADS, DK, bs).transpose(3, 0, 1, 2)
        outs_p.append((k_p, v_p.reshape(bp, lp, HEADS, DV),
                       gla_p.reshape(bp, HEADS, DK, DV), c_p, n_p, m_p[:, :, 0]))
        outs_s.append((k_new, v_s.reshape(bs, ls, HEADS, DV),
                       gla_s, c_s, n_s.reshape(bs, HEADS, DK), m_s[:, ::DK]))

    y_prompt = h[0].reshape(bp, lp, D_MODEL)
    y_sample = from_step_major(h[1])
    stack = lambda items: [jnp.stack(t) for t in zip(*items)]
    return tuple([y_prompt, y_sample] + stack(outs_p) + stack(outs_s))
```

```python
import functools
import math

import numpy as np
import jax
import jax.numpy as jnp
from jax import lax
from jax.experimental import pallas as pl
from jax.experimental.pallas import tpu as pltpu

f32 = jnp.float32
bf16 = jnp.bfloat16
HIGHEST = lax.Precision.HIGHEST

D_MODEL = 1024
PLE_DIM = 256
HEADS = 4
DK = 64
DV = 128
GLA_RANK = 16
GLA_GATE_NORM = 16.0
ROT_DIM = 16
ROPE_THETA = 500000.0
CHUNK = 64
CHUNK_MLSTM = 256
MLSTM_GROUP = 1
D_FF = 2816
EPS = 1e-6
LOG2E = 1.4426950408889634
VW = HEADS * DV
QW = HEADS * DK

LANES = 128
SUBLANES = 8
VMEM_LIMIT = 48 * 1024 * 1024

TILE = 512
GATE_W = 3 * D_MODEL
T_GQK, T_GV, T_GR = 0, 1, 2
T_MQK, T_MV, T_MO = 3, 4, 5
T_DQ, T_DK, T_DV = 6, 7, 8
N_TILES = 9
P_MIX = N_TILES * TILE
P_MAIN = GATE_W + P_MIX
L_GLR = 0
L_MI = 16
L_MF = 20

TM = 512


def _cparams(*sem):
    return pltpu.CompilerParams(dimension_semantics=sem, vmem_limit_bytes=VMEM_LIMIT)


def _log_sigmoid(x):
    return jnp.minimum(x, 0.0) - jnp.log1p(jnp.exp(-jnp.abs(x)))


def _sigmoid(x):
    return 0.5 * jnp.tanh(0.5 * x) + 0.5


def _rms(x, g):
    return x * lax.rsqrt(jnp.mean(x * x, axis=-1, keepdims=True) + EPS) * g


def _dot(a, b):
    return jnp.dot(a.astype(bf16), b.astype(bf16), preferred_element_type=f32)


def _dot_nt(a, b):
    return lax.dot_general(a.astype(bf16), b.astype(bf16), (((1,), (1,)), ((), ())), preferred_element_type=f32)


def _dot_tn(a, b):
    return lax.dot_general(a.astype(bf16), b.astype(bf16), (((0,), (0,)), ((), ())), preferred_element_type=f32)


def _rope_tile(x, cos, sin_up, sin_dn):
    parts = []
    for c in range(x.shape[1] // LANES):
        xc = x[:, c * LANES:(c + 1) * LANES]
        parts.append(xc * cos + pltpu.roll(xc, LANES - ROT_DIM // 2, axis=1) * sin_up
                     + pltpu.roll(xc, ROT_DIM // 2, axis=1) * sin_dn)
    return jnp.concatenate(parts, axis=1)


def _row_specs(width, n_prompt_blocks, sample_block, tm=None, n_sample_blocks=1, prompt_block0=0):
    tm = TM if tm is None else tm
    return (pl.BlockSpec((tm, width), lambda i: (prompt_block0 + jnp.minimum(i, n_prompt_blocks - 1), 0)),
            pl.BlockSpec((tm, width),
                         lambda i: (sample_block + jnp.clip(i - n_prompt_blocks, 0, n_sample_blocks - 1), 0)))


def _pick_rows(n_prompt_blocks, p_ref, s_ref):
    return jnp.where(pl.program_id(0) >= n_prompt_blocks, s_ref[...], p_ref[...])


TM_IN = 256


def _inproj_kernel(xp_ref, xs_ref, g_ref, w_ref, ws_ref, wgk_ref, bgk_ref, bif_ref, cos_ref, sup_ref, sdn_ref,
                   kt_all_ref, vp_all_ref, gates_ref, proj_ref, kt_ref, vp_ref, kts_ref, la_ref, gif_ref,
                   *, n_prompt_blocks, dec_batch):
    del kt_all_ref, vp_all_ref
    i = pl.program_id(0)
    xn = _rms(_pick_rows(n_prompt_blocks, xp_ref, xs_ref), g_ref[...]).astype(bf16)
    small = jnp.dot(xn, ws_ref[...], preferred_element_type=f32)
    z = jnp.dot(small, wgk_ref[...], preferred_element_type=f32, precision=HIGHEST) + bgk_ref[...]
    la_ref[...] = _log_sigmoid(z) * (1.0 / GLA_GATE_NORM)
    gi = small + bif_ref[...]
    lane = lax.broadcasted_iota(jnp.int32, gi.shape, 1)
    gif_ref[...] = jnp.where((lane >= L_MF) & (lane < L_MF + HEADS), _log_sigmoid(gi), gi)

    for c in range(GATE_W // TILE):
        cols = slice(c * TILE, (c + 1) * TILE)
        gates_ref[:, cols] = _sigmoid(jnp.dot(xn, w_ref[:, cols], preferred_element_type=f32)).astype(bf16)

    cos, sup, sdn = cos_ref[...], sup_ref[...], sdn_ref[...]
    for t in range(N_TILES):
        acc = jnp.dot(xn, w_ref[:, GATE_W + t * TILE:GATE_W + (t + 1) * TILE], preferred_element_type=f32)
        if t in (T_DQ, T_DK):
            acc = _rope_tile(acc, cos, sup, sdn)
        proj_ref[:, t * TILE:(t + 1) * TILE] = acc
        if t == T_DK:
            k_t = acc.T
        if t == T_DV:
            v_rows = acc

    @pl.when(i < n_prompt_blocks)
    def _():
        kt_ref[...] = k_t
        for h in range(HEADS):
            vp_ref[pl.ds(h, v_rows.shape[0], stride=HEADS), :] = v_rows[:, h * DV:(h + 1) * DV]

    @pl.when(i >= n_prompt_blocks)
    def _():
        for s in range(k_t.shape[1] // dec_batch):
            kts_ref[s] = k_t[:, s * dec_batch:(s + 1) * dec_batch]


def _inproj(h_p, h_s, sample_block, g_mix, w_main, w_small, w_gk, b_gk, b_if, cos_t, sup_t, sdn_t,
            batch, seq, dec_batch, dec_seq, layer, depth, carried):
    tm = TM_IN
    seq_blocks = seq // tm
    npb = batch * seq_blocks
    steps_per_block = tm // dec_batch
    nsb = dec_seq // steps_per_block
    nt = npb + nsb
    T = nt * tm
    const = lambda i: (0, 0)
    resident = lambda shape: pl.BlockSpec(shape, const, pipeline_mode=pl.Buffered(1))

    def tab_map(i):
        return (jnp.where(i < npb, i % seq_blocks, seq_blocks + i - npb), 0)

    def kt_map(i):
        j = jnp.minimum(i, npb - 1)
        return (layer, j // seq_blocks, 0, j % seq_blocks)

    return pl.pallas_call(
        functools.partial(_inproj_kernel, n_prompt_blocks=npb, dec_batch=dec_batch),
        grid=(nt,),
        input_output_aliases={11: 2, 12: 3},
        in_specs=[
            *_row_specs(D_MODEL, npb, sample_block, tm, nsb),
            pl.BlockSpec((1, D_MODEL), const),
            resident((D_MODEL, P_MAIN)),
            resident((D_MODEL, LANES)),
            resident((LANES, QW)),
            pl.BlockSpec((1, QW), const),
            pl.BlockSpec((1, LANES), const),
            pl.BlockSpec((tm, LANES), tab_map),
            pl.BlockSpec((tm, LANES), tab_map),
            pl.BlockSpec((tm, LANES), tab_map),
            pl.BlockSpec(memory_space=pl.ANY),
            pl.BlockSpec(memory_space=pl.ANY),
        ],
        out_specs=[
            pl.BlockSpec((tm, GATE_W), lambda i: (i, 0)),
            pl.BlockSpec((tm, P_MIX), lambda i: (i, 0)),
            pl.BlockSpec((None, None, TILE, tm), kt_map),
            pl.BlockSpec((tm * HEADS, DV), lambda i: (layer * npb + jnp.minimum(i, npb - 1), 0)),
            pl.BlockSpec((steps_per_block, TILE, dec_batch), lambda i: (jnp.maximum(i - npb, 0), 0, 0)),
            pl.BlockSpec((tm, QW), lambda i: (i, 0)),
            pl.BlockSpec((tm, LANES), lambda i: (i, 0)),
        ],
        out_shape=[
            jax.ShapeDtypeStruct((T, GATE_W), bf16),
            jax.ShapeDtypeStruct((T, P_MIX), f32),
            jax.ShapeDtypeStruct((depth, batch, TILE, seq), f32),
            jax.ShapeDtypeStruct((depth * batch * seq * HEADS, DV), f32),
            jax.ShapeDtypeStruct((dec_seq, TILE, dec_batch), f32),
            jax.ShapeDtypeStruct((T, QW), f32),
            jax.ShapeDtypeStruct((T, LANES), f32),
        ],
        compiler_params=_cparams("arbitrary"),
        name="inproj",
    )(h_p, h_s, g_mix, w_main, w_small, w_gk, b_gk, b_if, cos_t, sup_t, sdn_t, *carried)


def _tril(n):
    r = lax.broadcasted_iota(jnp.int32, (n, n), 0)
    c = lax.broadcasted_iota(jnp.int32, (n, n), 1)
    return r >= c


def _gla_prompt_kernel(*refs, n_chunks, batch):
    q_refs, k_refs, v_refs, la_refs = (refs[i * batch:(i + 1) * batch] for i in range(4))
    o_ref, s_out_ref, s_scr = refs[4 * batch:]
    blk = pl.program_id(0)

    @pl.when(blk == 0)
    def _():
        s_scr[...] = jnp.zeros_like(s_scr)

    ltri = _tril(CHUNK).astype(f32)
    r2 = lax.broadcasted_iota(jnp.int32, (2 * CHUNK, CHUNK), 0)
    c2 = lax.broadcasted_iota(jnp.int32, (2 * CHUNK, CHUNK), 1)
    tril2 = jnp.where(r2 >= CHUNK, r2 - CHUNK, r2) >= c2
    lane = lax.broadcasted_iota(jnp.int32, (CHUNK, LANES), 1)
    lo = (lane < DK).astype(f32)
    hi = 1.0 - lo

    def chunk(c, carry):
        rows = pl.ds(pl.multiple_of(c * CHUNK, CHUNK), CHUNK)
        units = [(s, p) for s in range(batch) for p in range(HEADS // 2)]
        bs_ = [jnp.dot(ltri, la_refs[s][rows, :], preferred_element_type=f32, precision=HIGHEST)
               for s in range(batch)]
        qs, kgs, kd0, kd1, dcol, v0, v1, s_old = {}, {}, {}, {}, {}, {}, {}, {}
        for s in range(batch):
            b = bs_[s]
            b_last = b[CHUNK - 1:CHUNK, :]
            q = q_refs[s][rows, :]
            k = k_refs[s][rows, :]
            qg = q * jnp.exp(b) * (DK ** -0.5)
            kg = k * jnp.exp(-b)
            kd = k * jnp.exp(b_last - b)
            dec = jnp.exp(b_last)
            for p in range(HEADS // 2):
                u = (s, p)
                ls = slice(p * LANES, (p + 1) * LANES)
                qs[u] = jnp.concatenate([qg[:, ls] * lo, qg[:, ls] * hi], axis=0)
                kgs[u] = kg[:, ls]
                kd0[u], kd1[u] = kd[:, ls] * lo, kd[:, ls] * hi
                dcol[u] = jnp.broadcast_to(dec[:, ls], (LANES, LANES)).T
                v0[u] = v_refs[s][rows, pl.ds((2 * p) * DV, DV)]
                v1[u] = v_refs[s][rows, pl.ds((2 * p + 1) * DV, DV)]
                s_old[u] = s_scr[s, p]
        a = {u: _dot_nt(qs[u], kgs[u]) for u in units}
        inter = {u: _dot(qs[u], s_old[u]) for u in units}
        kv = {u: _dot_tn(kd0[u], v0[u]) + _dot_tn(kd1[u], v1[u]) for u in units}
        a = {u: jnp.where(tril2, a[u], 0.0) for u in units}
        intra0 = {u: _dot(a[u][:CHUNK], v0[u]) for u in units}
        intra1 = {u: _dot(a[u][CHUNK:], v1[u]) for u in units}
        for u in units:
            s, p = u
            o_ref[s, rows, pl.ds((2 * p) * DV, DV)] = inter[u][:CHUNK] + intra0[u]
            o_ref[s, rows, pl.ds((2 * p + 1) * DV, DV)] = inter[u][CHUNK:] + intra1[u]
            s_scr[s, p] = dcol[u] * s_old[u] + kv[u]
        return carry

    lax.fori_loop(0, n_chunks, chunk, 0)

    @pl.when(blk == pl.num_programs(0) - 1)
    def _():
        s_out_ref[...] = s_scr[...]


def _stream_specs(width, batch, nb, col):
    return [pl.BlockSpec((TM, width), functools.partial(lambda s, i: (s * nb + i, col), s)) for s in range(batch)]


def _gla_prompt(proj, la, batch, seq):
    nb = seq // TM
    qw_blocks = TILE // QW
    return pl.pallas_call(
        functools.partial(_gla_prompt_kernel, n_chunks=TM // CHUNK, batch=batch),
        grid=(nb,),
        in_specs=[
            *_stream_specs(QW, batch, nb, T_GQK * qw_blocks),
            *_stream_specs(QW, batch, nb, T_GQK * qw_blocks + 1),
            *_stream_specs(VW, batch, nb, T_GV),
            *_stream_specs(QW, batch, nb, 0),
        ],
        out_specs=[
            pl.BlockSpec((batch, TM, VW), lambda i: (0, i, 0)),
            pl.BlockSpec((batch, HEADS // 2, LANES, LANES), lambda i: (0, 0, 0, 0)),
        ],
        out_shape=[
            jax.ShapeDtypeStruct((batch, seq, VW), f32),
            jax.ShapeDtypeStruct((batch, HEADS // 2, LANES, LANES), f32),
        ],
        scratch_shapes=[pltpu.VMEM((batch, HEADS // 2, LANES, LANES), f32)],
        compiler_params=_cparams("arbitrary"),
        name="gla_prompt",
    )(*([proj] * (3 * batch)), *([la] * batch))


def _mlstm_prompt_kernel(*refs, n_chunks, batch):
    q_refs, k_refs, v_refs, gif_refs = (refs[i * batch:(i + 1) * batch] for i in range(4))
    h_ref, c_out_ref, n_out_ref, m_out_ref, ct_scr, n_scr, m_scr = refs[4 * batch:]
    blk = pl.program_id(0)

    @pl.when(blk == 0)
    def _():
        ct_scr[...] = jnp.zeros_like(ct_scr)
        n_scr[...] = jnp.zeros_like(n_scr)
        m_scr[...] = jnp.zeros_like(m_scr)

    CH = CHUNK_MLSTM
    ltri = _tril(CH).astype(f32)
    r_i = lax.broadcasted_iota(jnp.int32, (CH, CH), 0)
    c_i = lax.broadcasted_iota(jnp.int32, (CH, CH), 1)
    upper = r_i <= c_i

    def chunk(c, carry):
        rows = pl.ds(pl.multiple_of(c * CH, CH), CH)
        for sq0 in range(0, batch, MLSTM_GROUP):
            seqs = range(sq0, min(sq0 + MLSTM_GROUP, batch))
            units = [(sq, h) for sq in seqs for h in range(HEADS)]
            gate = {}
            for sq in seqs:
                g = gif_refs[sq][rows, :]
                fc = jnp.dot(ltri, g, preferred_element_type=f32, precision=HIGHEST)
                gate[sq] = (g, fc, g.T, fc.T)
            q, k, v_t, c_t, n_all = {}, {}, {}, {}, {}
            for sq in seqs:
                q_all = q_refs[sq][rows, :]
                k_all = k_refs[sq][rows, :] * (DK ** -0.5)
                n_all[sq] = n_scr[sq]
                for h in range(HEADS):
                    u = (sq, h)
                    q[u] = q_all[:, h * DK:(h + 1) * DK]
                    k[u] = k_all[:, h * DK:(h + 1) * DK]
                    v_t[u] = v_refs[sq][rows, pl.ds(h * DV, DV)].T
                    c_t[u] = ct_scr[sq, h]
            kq = {u: _dot_nt(k[u], q[u]) for u in units}
            cq = {u: _dot_nt(c_t[u], q[u]) for u in units}
            qn = {u: _dot_nt(n_all[u[0]], q[u])[u[1]:u[1] + 1, :] for u in units}
            sc, wk, w_inter, floor, decay, m_new = {}, {}, {}, {}, {}, {}
            for u in units:
                sq, h = u
                g, fc, g_t, fc_t = gate[sq]
                i_col = g[:, L_MI + h:L_MI + h + 1]
                f_col = fc[:, L_MF + h:L_MF + h + 1]
                f_row = fc_t[L_MF + h:L_MF + h + 1, :]
                f_last = fc[CH - 1:CH, L_MF + h:L_MF + h + 1]
                m_prev = m_scr[sq, h:h + 1, 0:1]
                log_d = jnp.where(upper, f_row + (i_col - f_col), -jnp.inf)
                m_inter = m_prev + f_row
                m_row = jnp.maximum(m_inter, jnp.max(log_d, axis=0, keepdims=True))
                w_inter[u] = jnp.exp(m_inter - m_row)
                floor[u] = jnp.exp(-m_row)
                sc[u] = kq[u] * jnp.exp(log_d - m_row)
                log_w = f_last - f_col + i_col
                m_new[u] = jnp.maximum(m_prev + f_last, jnp.max(log_w, axis=0, keepdims=True))
                wk[u] = jnp.exp(log_w - m_new[u]) * k[u]
                decay[u] = jnp.exp(m_prev + f_last - m_new[u])
            vs = {u: _dot(v_t[u], sc[u]) for u in units}
            vk = {u: _dot(v_t[u], wk[u]) for u in units}
            for u in units:
                sq, h = u
                num = w_inter[u] * cq[u] + vs[u]
                den = w_inter[u] * qn[u] + jnp.sum(sc[u], axis=0, keepdims=True)
                h_ref[sq, rows, pl.ds(h * DV, DV)] = (num / jnp.maximum(jnp.abs(den), floor[u])).T
                ct_scr[sq, h] = decay[u] * c_t[u] + vk[u]
                n_scr[sq, h:h + 1, :] = decay[u] * n_all[sq][h:h + 1, :] + jnp.sum(wk[u], axis=0, keepdims=True)
                m_scr[sq, h:h + 1, :] = jnp.broadcast_to(m_new[u], (1, LANES))
        return carry

    lax.fori_loop(0, n_chunks, chunk, 0)

    @pl.when(blk == pl.num_programs(0) - 1)
    def _():
        for sq in range(batch):
            for h in range(HEADS):
                c_out_ref[sq, h] = ct_scr[sq, h].T
        n_out_ref[...] = n_scr[:, 0:HEADS, :]
        m_out_ref[...] = m_scr[:, 0:HEADS, :]


def _mlstm_prompt(proj, gif, batch, seq):
    nb = seq // TM
    qw_blocks = TILE // QW
    return pl.pallas_call(
        functools.partial(_mlstm_prompt_kernel, n_chunks=TM // CHUNK_MLSTM, batch=batch),
        grid=(nb,),
        in_specs=[
            *_stream_specs(QW, batch, nb, T_MQK * qw_blocks),
            *_stream_specs(QW, batch, nb, T_MQK * qw_blocks + 1),
            *_stream_specs(VW, batch, nb, T_MV),
            *_stream_specs(LANES, batch, nb, 0),
        ],
        out_specs=[
            pl.BlockSpec((batch, TM, VW), lambda i: (0, i, 0)),
            pl.BlockSpec((batch, HEADS, DK, DV), lambda i: (0, 0, 0, 0)),
            pl.BlockSpec((batch, HEADS, DK), lambda i: (0, 0, 0)),
            pl.BlockSpec((batch, HEADS, LANES), lambda i: (0, 0, 0)),
        ],
        out_shape=[
            jax.ShapeDtypeStruct((batch, seq, VW), f32),
            jax.ShapeDtypeStruct((batch, HEADS, DK, DV), f32),
            jax.ShapeDtypeStruct((batch, HEADS, DK), f32),
            jax.ShapeDtypeStruct((batch, HEADS, LANES), f32),
        ],
        scratch_shapes=[
            pltpu.VMEM((batch, HEADS, DV, DK), f32),
            pltpu.VMEM((batch, SUBLANES, DK), f32),
            pltpu.VMEM((batch, SUBLANES, LANES), f32),
        ],
        compiler_params=_cparams("arbitrary"),
        name="mlstm_prompt",
    )(*([proj] * (3 * batch)), *([gif] * batch))


def _diff_lambda_value(lam_ref, lam_init):
    lam = lam_ref[...]
    s1 = jnp.sum(lam[0:1] * lam[1:2], axis=1, keepdims=True)
    s2 = jnp.sum(lam[2:3] * lam[3:4], axis=1, keepdims=True)
    return jnp.exp(s1) - jnp.exp(s2) + lam_init


def _attn_prompt_step(p, qi_ref, ki_ref, q_ref, k_ref, v_ref, lam_ref, o_ref, q2_scr, m_scr, l_scr, acc_scr,
                      *, tq, lam_init, other_stages):
    qi = qi_ref[p]
    ki = ki_ref[p]
    pair = lambda j: slice(j * LANES, (j + 1) * LANES)

    @pl.when(ki == 0)
    def _():
        for j in range(HEADS):
            q = q_ref[:, pair(j)] * (DK ** -0.5 * LOG2E)
            lane = lax.broadcasted_iota(jnp.int32, q.shape, 1)
            q2_scr[j, 0:tq, :] = jnp.where(lane < DK, q, 0.0).astype(bf16)
            q2_scr[j, tq:2 * tq, :] = jnp.where(lane >= DK, q, 0.0).astype(bf16)
        m_scr[...] = jnp.full_like(m_scr, -jnp.inf)
        l_scr[...] = jnp.zeros_like(l_scr)
        acc_scr[...] = jnp.zeros_like(acc_scr)

    def step(masked, other):
        sts = [_dot_nt(k_ref[:, pair(j)], q2_scr[j]) for j in range(HEADS)]
        other[0]()
        pts, alphas = [], []
        for j in range(HEADS):
            st = sts[j]
            if masked:
                r = lax.broadcasted_iota(jnp.int32, st.shape, 0)
                c = lax.broadcasted_iota(jnp.int32, st.shape, 1)
                c = jnp.where(c >= tq, c - tq, c)
                st = jnp.where(r <= c, st, -jnp.inf)
            m_prev = m_scr[j]
            m_new = jnp.maximum(m_prev, jnp.max(st, axis=0, keepdims=True))
            alpha = jnp.exp2(m_prev - m_new)
            pt = jnp.exp2(st - m_new)
            l_scr[j] = alpha * l_scr[j] + jnp.sum(pt, axis=0, keepdims=True)
            m_scr[j] = m_new
            pts.append(pt.astype(bf16))
            alphas.append(alpha)
            if j == 0:
                other[1]()
        other[2]()
        for j in range(HEADS):
            acc_scr[j] = alphas[j] * acc_scr[j] + _dot_tn(v_ref[:, pair(j)], pts[j])
        other[3]()

    @pl.when(ki < qi)
    def _():
        step(False, other_stages())

    @pl.when(ki == qi)
    def _():
        step(True, other_stages())
        lam = _diff_lambda_value(lam_ref, lam_init)
        for j in range(HEADS):
            ot = acc_scr[j] / l_scr[j]
            o_ref[:, pair(j)] = (ot[:, 0:tq] - lam * ot[:, tq:2 * tq]).T


def _sample_page_copies(seq_idx, slot, pt_ref, ck_hbm, cv_hbm, kbuf, vbuf, sem, *, layer, n_pages, page):
    out = []
    for pg in range(n_pages):
        src = pt_ref[seq_idx, pg]
        out.append(pltpu.make_async_copy(ck_hbm.at[layer, src], kbuf.at[slot, :, pl.ds(pg * page, page)],
                                         sem.at[0, slot]))
        out.append(pltpu.make_async_copy(cv_hbm.at[layer, src],
                                         vbuf.at[slot, pl.ds(pg * page * HEADS, page * HEADS), :], sem.at[1, slot]))
    return out


def _attn_sample_pages(step, n_steps, pt_ref, ck_hbm, cv_hbm, o_ref, kbuf, vbuf, sem, *, nseq, pages):
    past = pages["n_pages"] * pages["page"]
    seq = lambda st: jnp.minimum(st, nseq - 1)
    slot = step % 2
    copies = functools.partial(_sample_page_copies, pt_ref=pt_ref, ck_hbm=ck_hbm, cv_hbm=cv_hbm, kbuf=kbuf,
                               vbuf=vbuf, sem=sem, **pages)

    @pl.when(step == 0)
    def _():
        kbuf[:, :, pl.ds(past, LANES)] = jnp.zeros((2, kbuf.shape[1], LANES), f32)
        vbuf[:, pl.ds(past * HEADS, LANES * HEADS), :] = jnp.zeros((2, LANES * HEADS, DV), f32)
        o_ref[...] = jnp.zeros_like(o_ref)
        for cp in copies(0, 0):
            cp.start()

    @pl.when(step + 1 < n_steps)
    def _():
        for cp in copies(seq(step + 1), 1 - slot):
            cp.start()

    for cp in copies(seq(step), slot):
        cp.wait()


def _attn_sample_stages(step, q_ref, kts_ref, v_ref, lam_ref, o_ref, kbuf, vbuf,
                        *, n_pages, page, dec_seq, nseq, lam_init):
    st = {}
    b = jnp.minimum(step, nseq - 1)
    slot = step % 2
    past = n_pages * page
    n_qh = 2 * HEADS

    shift = lax.rem(nseq - b, nseq)
    b8 = (b // SUBLANES) * SUBLANES
    groups = [pl.ds(pl.multiple_of(t * nseq + b8, SUBLANES), SUBLANES) for t in range(dec_seq)]
    mine = lax.broadcasted_iota(jnp.int32, (SUBLANES, 2 * QW), 0) == b - b8
    pick = lambda ref, t: jnp.sum(jnp.where(mine, ref[groups[t], :], 0.0), axis=0, keepdims=True)
    def scores():
        head_of_lane = lax.broadcasted_iota(jnp.int32, (n_qh, 2 * QW), 1) // DK
        head_of_row = lax.broadcasted_iota(jnp.int32, (n_qh, 2 * QW), 0)
        q_rows = []
        for t in range(dec_seq):
            kbuf[slot, :, pl.ds(past + t, 1)] = pltpu.roll(kts_ref[t], shift, axis=1)[:, 0:1]
            v_row = pick(v_ref, t)
            for h in range(HEADS):
                vbuf[slot, pl.ds((past + t) * HEADS + h, 1), :] = v_row[:, h * DV:(h + 1) * DV]
            q = jnp.broadcast_to(pick(q_ref, t) * (DK ** -0.5), (n_qh, 2 * QW))
            q_rows.append(jnp.where(head_of_lane == head_of_row, q, 0.0))
        qbd = jnp.concatenate(q_rows, axis=0)
        st["s"] = _dot(qbd, kbuf[slot])

    def softmax():
        s = st["s"]
        r = lax.broadcasted_iota(jnp.int32, s.shape, 0)
        c = lax.broadcasted_iota(jnp.int32, s.shape, 1)
        s = jnp.where(c - past <= r // n_qh, s, -jnp.inf)
        pr = jnp.exp(s - jnp.max(s, axis=1, keepdims=True))
        st["inv_l"] = 1.0 / jnp.sum(pr, axis=1, keepdims=True)
        st["pr"] = pr.astype(bf16)

    def values():
        v_hs = [vbuf[slot, pl.ds(hv, past + LANES, stride=HEADS), :] for hv in range(HEADS)]
        st["o"] = [_dot(st["pr"], v_h) for v_h in v_hs]

    def write():
        lam = _diff_lambda_value(lam_ref, lam_init)
        mine_v = mine[:, 0:DV]
        for hv in range(HEADS):
            o_h = st["o"][hv] * st["inv_l"]
            for t in range(dec_seq):
                r1 = t * n_qh + 2 * hv
                d = o_h[r1:r1 + 1] - lam * o_h[r1 + 1:r1 + 2]
                lanes = slice(hv * DV, (hv + 1) * DV)
                o_ref[groups[t], lanes] = jnp.where(mine_v, d, o_ref[groups[t], lanes])

    return scores, softmax, values, write


def _attn_kernel(qi_ref, ki_ref, pt_ref, q_ref, k_ref, v_ref, lam_ref, qs_ref, kts_ref, vs_ref, ck_hbm, cv_hbm,
                 o_ref, os_ref, q2_scr, m_scr, l_scr, acc_scr, kbuf, vbuf, sem, *, prompt, sample, nseq, n_pairs):
    p = pl.program_id(1)
    step = pl.program_id(0) * pl.num_programs(1) + p
    n_steps = pl.num_programs(0) * pl.num_programs(1)
    pages = dict(layer=sample["layer"], n_pages=sample["n_pages"], page=sample["page"])
    _attn_sample_pages(step, n_steps, pt_ref, ck_hbm, cv_hbm, os_ref, kbuf, vbuf, sem, nseq=nseq, pages=pages)

    def sample_stages():
        return _attn_sample_stages(step, qs_ref, kts_ref, vs_ref, lam_ref, os_ref, kbuf, vbuf, nseq=nseq,
                                   n_pages=sample["n_pages"], page=sample["page"], dec_seq=sample["dec_seq"],
                                   lam_init=sample["lam_init"])

    @pl.when(p < n_pairs)
    def _():
        _attn_prompt_step(p, qi_ref, ki_ref, q_ref, k_ref, v_ref, lam_ref, o_ref,
                          q2_scr, m_scr, l_scr, acc_scr, other_stages=sample_stages, **prompt)

    @pl.when(p >= n_pairs)
    def _():
        for stage in sample_stages():
            stage()


def _attention(proj, k_ts, page_table, diff_lambda, cache_kt, cache_v2, layer, lam_init, batch, seq, sample_block,
               tq=512):
    nq = seq // tq
    pairs = [(q, k) for q in range(nq) for k in range(q + 1)]
    qi_tab = jnp.asarray(np.array([a for a, _ in pairs], np.int32))
    ki_tab = jnp.asarray(np.array([b for _, b in pairs], np.int32))
    nseq, n_pages = page_table.shape
    n_pairs = len(pairs)
    steps = max(n_pairs, -(-nseq // batch))
    last = n_pairs - 1
    dec_seq = k_ts.shape[0]
    rows = dec_seq * nseq
    page = cache_kt.shape[3]
    cols = n_pages * page + LANES
    grid_spec = pltpu.PrefetchScalarGridSpec(
        num_scalar_prefetch=3,
        grid=(batch, steps),
        in_specs=[
            pl.BlockSpec((tq, TILE), lambda b, p, qt, kt, pt: (b * nq + qt[jnp.minimum(p, last)], T_DQ)),
            pl.BlockSpec((tq, TILE), lambda b, p, qt, kt, pt: (b * nq + kt[jnp.minimum(p, last)], T_DK)),
            pl.BlockSpec((tq, TILE), lambda b, p, qt, kt, pt: (b * nq + kt[jnp.minimum(p, last)], T_DV)),
            pl.BlockSpec((4, DK), lambda b, p, qt, kt, pt: (0, 0)),
            pl.BlockSpec((rows, TILE), lambda b, p, qt, kt, pt: (sample_block, T_DQ)),
            pl.BlockSpec((dec_seq, TILE, nseq), lambda b, p, qt, kt, pt: (0, 0, 0)),
            pl.BlockSpec((rows, TILE), lambda b, p, qt, kt, pt: (sample_block, T_DV)),
            pl.BlockSpec(memory_space=pl.ANY),
            pl.BlockSpec(memory_space=pl.ANY),
        ],
        out_specs=[
            pl.BlockSpec((tq, VW), lambda b, p, qt, kt, pt: (b * nq + qt[jnp.minimum(p, last)], 0)),
            pl.BlockSpec((rows, VW), lambda b, p, qt, kt, pt: (0, 0)),
        ],
        scratch_shapes=[
            pltpu.VMEM((HEADS, 2 * tq, LANES), bf16),
            pltpu.VMEM((HEADS, 1, 2 * tq), f32),
            pltpu.VMEM((HEADS, 1, 2 * tq), f32),
            pltpu.VMEM((HEADS, DV, 2 * tq), f32),
            pltpu.VMEM((2, 2 * QW, cols), f32),
            pltpu.VMEM((2, cols * HEADS, DV), f32),
            pltpu.SemaphoreType.DMA((2, 2)),
        ],
    )
    return pl.pallas_call(
        functools.partial(_attn_kernel, nseq=nseq, n_pairs=n_pairs, prompt=dict(tq=tq, lam_init=lam_init),
                          sample=dict(layer=layer, n_pages=n_pages, page=page, dec_seq=dec_seq, lam_init=lam_init)),
        grid_spec=grid_spec,
        out_shape=[jax.ShapeDtypeStruct((batch * seq, VW), f32), jax.ShapeDtypeStruct((rows, VW), f32)],
        compiler_params=_cparams("arbitrary", "arbitrary"),
        name="attention",
    )(qi_tab, ki_tab, page_table, proj, proj, proj, diff_lambda, proj, k_ts, proj, cache_kt, cache_v2)


def _head_of_lane(shape):
    return lax.broadcasted_iota(jnp.int32, shape, 1) // DK


def _expand_heads(cols, base, head):
    out = cols[:, base + HEADS - 1:base + HEADS]
    for h in range(HEADS - 2, -1, -1):
        out = jnp.where(head == h, cols[:, base + h:base + h + 1], out)
    return out


def _sample_prep_kernel(gq_ref, gk_ref, la_ref, mq_ref, mk_ref, gif_ref, n0_ref, m0_ref,
                        ga_ref, gkk_ref, gqq_ref, ma_ref, mkk_ref, mqq_ref, n_out_ref, m_out_ref, *, nseq, dec_seq):
    head = _head_of_lane((nseq, QW))
    m = _expand_heads(m0_ref[...], 0, head)
    n = n0_ref[...]
    for t in range(dec_seq):
        rows = pl.ds(t * nseq, nseq)
        ga_ref[t] = jnp.exp(la_ref[rows, :]).T
        gkk_ref[t] = gk_ref[rows, :].T
        gqq_ref[t] = (gq_ref[rows, :] * (DK ** -0.5)).T
        gates = gif_ref[rows, :]
        i_e = _expand_heads(gates, L_MI, head)
        f_e = _expand_heads(gates, L_MF, head)
        k = mk_ref[rows, :] * (DK ** -0.5)
        q = mq_ref[rows, :]
        m_new = jnp.maximum(f_e + m, i_e)
        fp = jnp.exp(f_e + m - m_new)
        ip = jnp.exp(i_e - m_new)
        n = fp * n + ip * k
        nq = n * q
        den = jnp.zeros_like(nq)
        for h in range(HEADS):
            den = jnp.where(head == h, jnp.sum(jnp.where(head == h, nq, 0.0), axis=1, keepdims=True), den)
        inv = 1.0 / jnp.maximum(jnp.abs(den), jnp.exp(-m_new))
        ma_ref[t] = fp.T
        mkk_ref[t] = (ip * k).T
        mqq_ref[t] = (q * inv).T
        m = m_new
    n_out_ref[...] = n
    m_out_ref[...] = m


def _sample_prep(proj, la, gif, state_n, state_m, layer, sample_block, nseq, dec_seq):
    rows = nseq * dec_seq
    qw_blocks = TILE // QW
    tok = jax.ShapeDtypeStruct((dec_seq, QW, nseq), f32)
    st = jax.ShapeDtypeStruct((nseq, QW), f32)
    tok_spec = pl.BlockSpec((dec_seq, QW, nseq), lambda i: (0, 0, 0))
    st_spec = pl.BlockSpec((nseq, QW), lambda i: (0, 0))
    col = lambda c: pl.BlockSpec((rows, QW), lambda i: (sample_block, c))
    return pl.pallas_call(
        functools.partial(_sample_prep_kernel, nseq=nseq, dec_seq=dec_seq),
        grid=(1,),
        in_specs=[
            col(T_GQK * qw_blocks), col(T_GQK * qw_blocks + 1), col(0),
            col(T_MQK * qw_blocks), col(T_MQK * qw_blocks + 1),
            pl.BlockSpec((rows, LANES), lambda i: (sample_block, 0)),
            pl.BlockSpec((None, nseq, QW), lambda i: (layer, 0, 0)),
            pl.BlockSpec((None, nseq, HEADS), lambda i: (layer, 0, 0)),
        ],
        out_specs=[tok_spec] * 6 + [st_spec] * 2,
        out_shape=[tok] * 6 + [st] * 2,
        compiler_params=_cparams("arbitrary"),
        name="sample_prep",
    )(proj, proj, la, proj, proj, gif, state_n, state_m)


SEQ_PER_STEP = 8


def _sample_state_kernel(a_ref, k_ref, q_ref, v_ref, s_ref, s_all_ref, o_ref, s_out_ref, *, dec_seq, nseq):
    del s_all_ref
    blk = pl.program_id(0)
    seq0 = blk * SEQ_PER_STEP
    shift = lax.rem(nseq - seq0, nseq)
    a_t = [pltpu.roll(a_ref[t], shift, axis=1) for t in range(dec_seq)]
    k_t = [pltpu.roll(k_ref[t], shift, axis=1) for t in range(dec_seq)]
    q_t = [pltpu.roll(q_ref[t], shift, axis=1) for t in range(dec_seq)]
    toks = [pl.ds(pl.multiple_of(t * nseq + seq0, SEQ_PER_STEP), SEQ_PER_STEP) for t in range(dec_seq)]
    v_t = [v_ref[toks[t], :] for t in range(dec_seq)]
    row = lax.broadcasted_iota(jnp.int32, (SEQ_PER_STEP, DV), 0)
    outs = [[jnp.zeros((SEQ_PER_STEP, DV), f32) for _ in range(HEADS)] for _ in range(dec_seq)]
    for bb in range(SEQ_PER_STEP):
        for h in range(HEADS):
            s = s_ref[bb, h]
            rows = slice(h * DK, (h + 1) * DK)
            for t in range(dec_seq):
                a = a_t[t][rows, bb:bb + 1]
                k = k_t[t][rows, bb:bb + 1]
                q = q_t[t][rows, bb:bb + 1]
                v = v_t[t][bb:bb + 1, h * DV:(h + 1) * DV]
                s = a * s + k * v
                outs[t][h] = jnp.where(row == bb, jnp.sum(q * s, axis=0, keepdims=True), outs[t][h])
            s_out_ref[bb, h] = s
    for t in range(dec_seq):
        for h in range(HEADS):
            o_ref[toks[t], h * DV:(h + 1) * DV] = outs[t][h]


def _sample_state(a_t, k_t, q_t, proj, v_tile, sample_block, state, layer, dec_seq, carried):
    nseq = a_t.shape[2]
    rows = dec_seq * nseq
    full = pl.BlockSpec((dec_seq, QW, nseq), lambda i: (0, 0, 0))
    state_spec = pl.BlockSpec((None, SEQ_PER_STEP, HEADS, DK, DV), lambda i: (layer, i, 0, 0, 0))
    return pl.pallas_call(
        functools.partial(_sample_state_kernel, dec_seq=dec_seq, nseq=nseq),
        grid=(nseq // SEQ_PER_STEP,),
        input_output_aliases={5: 1},
        in_specs=[
            full, full, full,
            pl.BlockSpec((rows, VW), lambda i: (sample_block, v_tile)),
            state_spec,
            pl.BlockSpec(memory_space=pl.ANY),
        ],
        out_specs=[pl.BlockSpec((rows, VW), lambda i: (0, 0)), state_spec],
        out_shape=[jax.ShapeDtypeStruct((rows, VW), f32), jax.ShapeDtypeStruct(state.shape, f32)],
        compiler_params=_cparams("arbitrary"),
        name="sample_state",
    )(a_t, k_t, q_t, proj, state, carried)


def _head_rms(x, g):
    parts = []
    for h in range(HEADS):
        parts.append(_rms(x[:, h * DV:(h + 1) * DV], g))
    return jnp.concatenate(parts, axis=1)


FF_SLABS = ((0, 768), (768, 768), (1536, 768), (2304, 512))
TM_CH = 256


def _merge_ffn_kernel(hp_ref, hs_ref, oap_ref, oas_ref, omp_ref, oms_ref, odp_ref, ods_ref, gr_ref, mo_ref,
                      g0_ref, g1_ref, g2_ref, gg_ref, gm_ref, gd_ref, wa_ref, wm_ref, wd_ref, wo_ref,
                      gf_ref, wg_ref, wu_ref, wdn_ref, pp_ref, ps_ref, gp_ref, wpg_ref, wpp_ref, gfin_ref,
                      *out_refs, lam_init, final, n_prompt_blocks):
    pick = functools.partial(_pick_rows, n_prompt_blocks)
    gr = gr_ref[...]
    a = _head_rms(pick(oap_ref, oas_ref), gg_ref[...]) * (gr * _sigmoid(gr))
    m = _sigmoid(mo_ref[...]) * _head_rms(pick(omp_ref, oms_ref), gm_ref[...])
    d = _head_rms(pick(odp_ref, ods_ref), gd_ref[...]) * (1.0 - lam_init)
    merged = (g0_ref[...] * _dot(a, wa_ref[...]) + g1_ref[...] * _dot(m, wm_ref[...])
              + g2_ref[...] * _dot(d, wd_ref[...]))
    h = pick(hp_ref, hs_ref) + _dot(merged, wo_ref[...])

    xf = _rms(h, gf_ref[...]).astype(bf16)
    h2 = h
    for lo, width in FF_SLABS:
        gate = jnp.dot(xf, wg_ref[:, lo:lo + width], preferred_element_type=f32)
        up = jnp.dot(xf, wu_ref[:, lo:lo + width], preferred_element_type=f32)
        h2 = h2 + _dot(gate * _sigmoid(gate) * up, wdn_ref[lo:lo + width, :])
    ple_gate = _sigmoid(_dot(_rms(h2, gp_ref[...]), wpg_ref[...]))
    h3 = h2 + ple_gate * _dot(pick(pp_ref, ps_ref), wpp_ref[...])
    if not final:
        out_refs[0][...] = h3
        return
    y = _rms(h3, gfin_ref[...])
    yp_ref, ys_ref = out_refs
    i = pl.program_id(0)

    @pl.when(i < n_prompt_blocks)
    def _():
        yp_ref[...] = y

    @pl.when(i >= n_prompt_blocks)
    def _():
        ys_ref[...] = y


def _merge_ffn(h_p, h_s, h_s_row0, o_a, o_m, o_d, proj, gates, g_gla, g_ml, g_diff, w_a, w_m, w_d, w_out, lam_init,
               g_ffn, w_g, w_u, w_dn, p_p, p_s, p_row0, g_ple, w_pg, w_pp, g_final, final, n_prompt_rows, n_sample_rows):
    tm = TM_CH
    npb, nsb = n_prompt_rows // tm, n_sample_rows // tm
    T = n_prompt_rows + n_sample_rows
    row = lambda i: (i, 0)
    const = lambda i: (0, 0)
    resident = lambda shape: pl.BlockSpec(shape, const, pipeline_mode=pl.Buffered(1))
    rows2 = lambda width, sample_block=0, prompt_block0=0: _row_specs(width, npb, sample_block, tm, nsb, prompt_block0)
    assert FF_SLABS[-1][0] + FF_SLABS[-1][1] == D_FF
    if final:
        out_specs = [pl.BlockSpec((tm, D_MODEL), lambda i: (jnp.minimum(i, npb - 1), 0)),
                     pl.BlockSpec((tm, D_MODEL), lambda i: (jnp.clip(i - npb, 0, nsb - 1), 0))]
        out_shape = [jax.ShapeDtypeStruct((n_prompt_rows, D_MODEL), f32),
                     jax.ShapeDtypeStruct((n_sample_rows, D_MODEL), f32)]
    else:
        out_specs = pl.BlockSpec((tm, D_MODEL), row)
        out_shape = jax.ShapeDtypeStruct((T, D_MODEL), f32)
    return pl.pallas_call(
        functools.partial(_merge_ffn_kernel, lam_init=lam_init, final=final, n_prompt_blocks=npb),
        grid=(npb + nsb,),
        in_specs=[
            *rows2(D_MODEL, sample_block=h_s_row0 // tm),
            *rows2(VW), *rows2(VW), *rows2(VW),
            pl.BlockSpec((tm, VW), lambda i: (i, T_GR)),
            pl.BlockSpec((tm, VW), lambda i: (i, T_MO)),
            pl.BlockSpec((tm, D_MODEL), lambda i: (i, 0)),
            pl.BlockSpec((tm, D_MODEL), lambda i: (i, 1)),
            pl.BlockSpec((tm, D_MODEL), lambda i: (i, 2)),
            pl.BlockSpec((1, DV), const),
            pl.BlockSpec((1, DV), const),
            pl.BlockSpec((1, DV), const),
            resident((VW, D_MODEL)), resident((VW, D_MODEL)), resident((VW, D_MODEL)),
            resident((D_MODEL, D_MODEL)),
            pl.BlockSpec((1, D_MODEL), const),
            resident((D_MODEL, D_FF)), resident((D_MODEL, D_FF)), resident((D_FF, D_MODEL)),
            *rows2(PLE_DIM, prompt_block0=p_row0 // tm),
            pl.BlockSpec((1, D_MODEL), const),
            resident((D_MODEL, D_MODEL)),
            resident((PLE_DIM, D_MODEL)),
            pl.BlockSpec((1, D_MODEL), const),
        ],
        out_specs=out_specs,
        out_shape=out_shape,
        compiler_params=_cparams("arbitrary"),
        name="merge_ffn",
    )(h_p, h_s, *o_a, *o_m, *o_d, proj, proj, gates, gates, gates, g_gla, g_ml, g_diff, w_a, w_m, w_d, w_out,
      g_ffn, w_g, w_u, w_dn, p_p, p_s, g_ple, w_pg, w_pp, g_final)


def _rope_tables(positions):
    half = ROT_DIM // 2
    inv_freq = ROPE_THETA ** (-jnp.arange(half, dtype=f32) * 2.0 / ROT_DIM)
    ang = positions.astype(f32)[:, None] * inv_freq[None, :]
    cos, sin = jnp.cos(ang), jnp.sin(ang)
    n = positions.shape[0]
    one = jnp.ones((n, DK - ROT_DIM), f32)
    zero = jnp.zeros((n, DK - ROT_DIM), f32)
    zh = jnp.zeros((n, half), f32)
    cos_h = jnp.concatenate([cos, cos, one], axis=1)
    up_h = jnp.concatenate([-sin, zh, zero], axis=1)
    dn_h = jnp.concatenate([zh, sin, zero], axis=1)
    rep = lambda a: jnp.concatenate([a] * (LANES // DK), axis=1)
    return rep(cos_h), rep(up_h), rep(dn_h)


def _prep_in_weights(w_in_l, w_gate_l):
    o = np.cumsum((0, QW, QW, VW, VW, GLA_RANK, QW, QW, VW, VW, 2 * HEADS, 2 * QW, 2 * QW, VW))
    seg = lambda i: w_in_l[:, int(o[i]):int(o[i + 1])]
    gq, gk, gv, gr, glr, mq, mk, mv, mo, mif, dq, dk, dv = (seg(i) for i in range(13))
    main = jnp.concatenate([w_gate_l, gq, gk, gv, gr, mq, mk, mv, mo, dq, dk, dv], axis=1).astype(bf16)
    pad = jnp.zeros((D_MODEL, LANES - GLA_RANK - 2 * HEADS), f32)
    small = jnp.concatenate([glr, mif, pad], axis=1).astype(bf16)
    return main, small


def kernel(x_prompt, x_sample, cache_k, cache_v, state_gla, state_mlstm_c, state_mlstm_n, state_mlstm_m, page_table, p_prompt, p_sample, g_mix, w_in, w_gla_gk, b_gla_gk, g_gla_norm, b_mlstm_if, g_mlstm_norm, diff_lambda, g_diff_norm, w_branch, w_gate, w_out, g_ffn, w_ffn_gate, w_ffn_up, w_ffn_down, g_ple, w_ple_gate, w_ple_proj, g_final):
    bp, lp, _ = x_prompt.shape
    bs, ls, _ = x_sample.shape
    depth = g_mix.shape[0]
    n_pages = page_table.shape[1]
    page = cache_k.shape[2]
    past = n_pages * page
    tp = bp * lp
    ts = bs * ls
    assert lp % TM == 0 and ts == TM and bs == LANES and page == LANES and TM_IN % bs == 0
    npb = tp // TM

    def to_step_major(a):
        return jnp.swapaxes(a, 0, 1).reshape((ts,) + a.shape[2:])

    def from_step_major(a):
        return jnp.swapaxes(a.reshape((ls, bs) + a.shape[1:]), 0, 1)

    h_p, h_s, h_s_row0 = x_prompt.reshape(tp, D_MODEL), to_step_major(x_sample), 0

    pos_rows = jnp.concatenate([jnp.arange(lp, dtype=jnp.int32),
                                past + jnp.repeat(jnp.arange(ls, dtype=jnp.int32), bs)])
    cos_t, sup_t, sdn_t = _rope_tables(pos_rows)

    ck = jnp.transpose(cache_k, (0, 1, 3, 4, 2)).reshape(cache_k.shape[0], cache_k.shape[1], 2 * QW, page)
    cv = cache_v.reshape(cache_v.shape[0], cache_v.shape[1], page * HEADS, DV)
    state_n = state_mlstm_n.reshape(depth, bs, QW)
    p_rows = p_prompt.reshape(depth * tp, PLE_DIM)
    k_t = jnp.zeros((depth, bp, 2 * QW, lp), f32)
    v_p = jnp.zeros((depth * tp * HEADS, DV), f32)
    gla_s = jnp.zeros(state_gla.shape, f32)
    c_s = jnp.zeros(state_mlstm_c.shape, f32)
    outs_p, outs_s = [], []
    for l in range(depth):
        lam_init = 0.8 - 0.6 * math.exp(-0.3 * l)
        w_main, w_small = _prep_in_weights(w_in[l], w_gate[l])
        w_gk = jnp.concatenate([w_gla_gk[l], jnp.zeros((LANES - GLA_RANK, QW), f32)], axis=0)
        b_if = jnp.zeros((1, LANES), f32).at[0, L_MI:L_MI + 2 * HEADS].set(b_mlstm_if[l])
        gates, proj, k_t, v_p, k_ts, la, gif = _inproj(
            h_p, h_s, h_s_row0 // TM_IN, g_mix[l][None], w_main, w_small, w_gk, b_gla_gk[l][None], b_if,
            cos_t, sup_t, sdn_t, bp, lp, bs, ls, l, depth, (k_t, v_p))

        oa_p, gla_p = _gla_prompt(proj, la, bp, lp)
        om_p, c_p, n_p, m_p = _mlstm_prompt(proj, gif, bp, lp)
        oa_p, om_p = oa_p.reshape(tp, VW), om_p.reshape(tp, VW)
        od_p, od_s = _attention(proj, k_ts, page_table, diff_lambda[l], ck, cv, l, lam_init, bp, lp, npb)

        ga, gk_, gq_, ma, mk_, mq_, n_s, m_s = _sample_prep(proj, la, gif, state_n, state_mlstm_m, l, npb, bs, ls)
        oa_s, gla_s = _sample_state(ga, gk_, gq_, proj, T_GV, npb, state_gla, l, ls, gla_s)
        om_s, c_s = _sample_state(ma, mk_, mq_, proj, T_MV, npb, state_mlstm_c, l, ls, c_s)
        v_s = from_step_major(proj[tp:, T_DV * TILE:(T_DV + 1) * TILE])

        wb = w_branch[l].astype(bf16)
        h = _merge_ffn(h_p, h_s, h_s_row0, (oa_p, oa_s), (om_p, om_s), (od_p, od_s), proj, gates,
                       g_gla_norm[l][None], g_mlstm_norm[l][None], g_diff_norm[l][None],
                       wb[:VW], wb[VW:2 * VW], wb[2 * VW:], w_out[l].astype(bf16), lam_init,
                       g_ffn[l][None], w_ffn_gate[l].astype(bf16), w_ffn_up[l].astype(bf16),
                       w_ffn_down[l].astype(bf16), p_rows, to_step_major(p_sample[l]), l * tp,
                       g_ple[l][None], w_ple_gate[l].astype(bf16), w_ple_proj[l].astype(bf16), g_final[None],
                       l == depth - 1, tp, ts)
        h_p, h_s, h_s_row0 = h, h, tp

        k_new = k_ts.reshape(ls, 2 * HEADS, DK, bs).transpose(3, 0, 1, 2)
        outs_p.append((gla_p.reshape(bp, HEADS, DK, DV), c_p, n_p, m_p[:, :, 0]))
        outs_s.append((k_new, v_s.reshape(bs, ls, HEADS, DV), n_s.reshape(bs, HEADS, DK), m_s[:, ::DK]))

    y_prompt = h[0].reshape(bp, lp, D_MODEL)
    y_sample = from_step_major(h[1])
    k_prompt = k_t.reshape(depth, bp, 2 * HEADS, DK, lp).transpose(0, 1, 4, 2, 3)
    v_prompt = v_p.reshape(depth, bp, lp, HEADS, DV)
    gla_prompt, c_prompt, n_prompt, m_prompt = [jnp.stack(t) for t in zip(*outs_p)]
    k_sample, v_sample, n_sample, m_sample = [jnp.stack(t) for t in zip(*outs_s)]
    return (y_prompt, y_sample, k_prompt, v_prompt, gla_prompt, c_prompt, n_prompt, m_prompt,
            k_sample, v_sample, gla_s, c_s, n_sample, m_sample)
```

```python
import functools
import math

import numpy as np
import jax
import jax.numpy as jnp
from jax import lax
from jax.experimental import pallas as pl
from jax.experimental.pallas import tpu as pltpu

f32 = jnp.float32
bf16 = jnp.bfloat16
HIGHEST = lax.Precision.HIGHEST

D_MODEL = 1024
PLE_DIM = 256
HEADS = 4
DK = 64
DV = 128
GLA_RANK = 16
GLA_GATE_NORM = 16.0
ROT_DIM = 16
ROPE_THETA = 500000.0
CHUNK = 64
CHUNK_MLSTM = 256
MLSTM_GROUP = 1
D_FF = 2816
EPS = 1e-6
LOG2E = 1.4426950408889634
VW = HEADS * DV
QW = HEADS * DK

LANES = 128
SUBLANES = 8
VMEM_LIMIT = 48 * 1024 * 1024

TILE = 512
GATE_W = 3 * D_MODEL
T_GQK, T_GV, T_GR = 0, 1, 2
T_MQK, T_MV, T_MO = 3, 4, 5
T_DQ, T_DK, T_DV = 6, 7, 8
N_TILES = 9
P_MIX = N_TILES * TILE
P_MAIN = GATE_W + P_MIX
L_MI = 16
L_MF = 20

TM = 512


def _cparams(*sem):
    return pltpu.CompilerParams(dimension_semantics=sem, vmem_limit_bytes=VMEM_LIMIT)


def _log_sigmoid(x):
    return jnp.minimum(x, 0.0) - jnp.log1p(jnp.exp(-jnp.abs(x)))


def _sigmoid(x):
    return 0.5 * jnp.tanh(0.5 * x) + 0.5


def _rms(x, g):
    return x * lax.rsqrt(jnp.mean(x * x, axis=-1, keepdims=True) + EPS) * g


def _dot(a, b):
    return jnp.dot(a.astype(bf16), b.astype(bf16), preferred_element_type=f32)


def _dot_nt(a, b):
    return lax.dot_general(a.astype(bf16), b.astype(bf16), (((1,), (1,)), ((), ())), preferred_element_type=f32)


def _dot_tn(a, b):
    return lax.dot_general(a.astype(bf16), b.astype(bf16), (((0,), (0,)), ((), ())), preferred_element_type=f32)


def _rope_tile(x, cos, sin_up, sin_dn):
    parts = []
    for c in range(x.shape[1] // LANES):
        xc = x[:, c * LANES:(c + 1) * LANES]
        parts.append(xc * cos + pltpu.roll(xc, LANES - ROT_DIM // 2, axis=1) * sin_up
                     + pltpu.roll(xc, ROT_DIM // 2, axis=1) * sin_dn)
    return jnp.concatenate(parts, axis=1)


def _row_specs(width, n_prompt_blocks, sample_block, tm=None, n_sample_blocks=1, prompt_block0=0):
    tm = TM if tm is None else tm
    return (pl.BlockSpec((tm, width), lambda i: (prompt_block0 + jnp.minimum(i, n_prompt_blocks - 1), 0)),
            pl.BlockSpec((tm, width),
                         lambda i: (sample_block + jnp.clip(i - n_prompt_blocks, 0, n_sample_blocks - 1), 0)))


def _pick_rows(n_prompt_blocks, p_ref, s_ref):
    return jnp.where(pl.program_id(0) >= n_prompt_blocks, s_ref[...], p_ref[...])


TM_IN = 256


def _inproj_kernel(xp_ref, xs_ref, g_ref, w_ref, ws_ref, wgk_ref, bgk_ref, bif_ref, cos_ref, sup_ref, sdn_ref,
                   kt_all_ref, vp_all_ref, gates_ref, proj_ref, kt_ref, vp_ref, kts_ref, la_ref, gif_ref,
                   *, n_prompt_blocks, dec_batch):
    del kt_all_ref, vp_all_ref
    i = pl.program_id(0)
    xn = _rms(_pick_rows(n_prompt_blocks, xp_ref, xs_ref), g_ref[...]).astype(bf16)
    small = jnp.dot(xn, ws_ref[...], preferred_element_type=f32)
    z = jnp.dot(small, wgk_ref[...], preferred_element_type=f32, precision=HIGHEST) + bgk_ref[...]
    la_ref[...] = _log_sigmoid(z) * (1.0 / GLA_GATE_NORM)
    gi = small + bif_ref[...]
    lane = lax.broadcasted_iota(jnp.int32, gi.shape, 1)
    gif_ref[...] = jnp.where((lane >= L_MF) & (lane < L_MF + HEADS), _log_sigmoid(gi), gi)

    for c in range(GATE_W // TILE):
        cols = slice(c * TILE, (c + 1) * TILE)
        gates_ref[:, cols] = _sigmoid(jnp.dot(xn, w_ref[:, cols], preferred_element_type=f32)).astype(bf16)

    cos, sup, sdn = cos_ref[...], sup_ref[...], sdn_ref[...]
    for t in range(N_TILES):
        acc = jnp.dot(xn, w_ref[:, GATE_W + t * TILE:GATE_W + (t + 1) * TILE], preferred_element_type=f32)
        if t in (T_DQ, T_DK):
            acc = _rope_tile(acc, cos, sup, sdn)
        proj_ref[:, t * TILE:(t + 1) * TILE] = acc
        if t == T_DK:
            k_t = acc.T
        if t == T_DV:
            v_rows = acc

    @pl.when(i < n_prompt_blocks)
    def _():
        kt_ref[...] = k_t
        for h in range(HEADS):
            vp_ref[pl.ds(h, v_rows.shape[0], stride=HEADS), :] = v_rows[:, h * DV:(h + 1) * DV]

    @pl.when(i >= n_prompt_blocks)
    def _():
        for s in range(k_t.shape[1] // dec_batch):
            kts_ref[s] = k_t[:, s * dec_batch:(s + 1) * dec_batch]


def _inproj(h_p, h_s, sample_block, g_mix, w_main, w_small, w_gk, b_gk, b_if, cos_t, sup_t, sdn_t,
            batch, seq, dec_batch, dec_seq, layer, depth, carried):
    tm = TM_IN
    seq_blocks = seq // tm
    npb = batch * seq_blocks
    steps_per_block = tm // dec_batch
    nsb = dec_seq // steps_per_block
    nt = npb + nsb
    T = nt * tm
    const = lambda i: (0, 0)
    resident = lambda shape: pl.BlockSpec(shape, const, pipeline_mode=pl.Buffered(1))

    def tab_map(i):
        return (jnp.where(i < npb, i % seq_blocks, seq_blocks + i - npb), 0)

    def kt_map(i):
        j = jnp.minimum(i, npb - 1)
        return (layer, j // seq_blocks, 0, j % seq_blocks)

    return pl.pallas_call(
        functools.partial(_inproj_kernel, n_prompt_blocks=npb, dec_batch=dec_batch),
        grid=(nt,),
        input_output_aliases={11: 2, 12: 3},
        in_specs=[
            *_row_specs(D_MODEL, npb, sample_block, tm, nsb),
            pl.BlockSpec((1, D_MODEL), const),
            resident((D_MODEL, P_MAIN)),
            resident((D_MODEL, LANES)),
            resident((LANES, QW)),
            pl.BlockSpec((1, QW), const),
            pl.BlockSpec((1, LANES), const),
            pl.BlockSpec((tm, LANES), tab_map),
            pl.BlockSpec((tm, LANES), tab_map),
            pl.BlockSpec((tm, LANES), tab_map),
            pl.BlockSpec(memory_space=pl.ANY),
            pl.BlockSpec(memory_space=pl.ANY),
        ],
        out_specs=[
            pl.BlockSpec((tm, GATE_W), lambda i: (i, 0)),
            pl.BlockSpec((tm, P_MIX), lambda i: (i, 0)),
            pl.BlockSpec((None, None, TILE, tm), kt_map),
            pl.BlockSpec((tm * HEADS, DV), lambda i: (layer * npb + jnp.minimum(i, npb - 1), 0)),
            pl.BlockSpec((steps_per_block, TILE, dec_batch), lambda i: (jnp.maximum(i - npb, 0), 0, 0)),
            pl.BlockSpec((tm, QW), lambda i: (i, 0)),
            pl.BlockSpec((tm, LANES), lambda i: (i, 0)),
        ],
        out_shape=[
            jax.ShapeDtypeStruct((T, GATE_W), bf16),
            jax.ShapeDtypeStruct((T, P_MIX), f32),
            jax.ShapeDtypeStruct((depth, batch, TILE, seq), f32),
            jax.ShapeDtypeStruct((depth * batch * seq * HEADS, DV), f32),
            jax.ShapeDtypeStruct((dec_seq, TILE, dec_batch), f32),
            jax.ShapeDtypeStruct((T, QW), f32),
            jax.ShapeDtypeStruct((T, LANES), f32),
        ],
        compiler_params=_cparams("arbitrary"),
        name="inproj",
    )(h_p, h_s, g_mix, w_main, w_small, w_gk, b_gk, b_if, cos_t, sup_t, sdn_t, *carried)


def _tril(n):
    r = lax.broadcasted_iota(jnp.int32, (n, n), 0)
    c = lax.broadcasted_iota(jnp.int32, (n, n), 1)
    return r >= c


def _gla_prompt_kernel(*refs, n_chunks, batch):
    q_refs, k_refs, v_refs, la_refs = (refs[i * batch:(i + 1) * batch] for i in range(4))
    o_ref, s_out_ref, s_scr = refs[4 * batch:]
    blk = pl.program_id(0)

    @pl.when(blk == 0)
    def _():
        s_scr[...] = jnp.zeros_like(s_scr)

    ltri = _tril(CHUNK).astype(f32)
    r2 = lax.broadcasted_iota(jnp.int32, (2 * CHUNK, CHUNK), 0)
    c2 = lax.broadcasted_iota(jnp.int32, (2 * CHUNK, CHUNK), 1)
    tril2 = jnp.where(r2 >= CHUNK, r2 - CHUNK, r2) >= c2
    lane = lax.broadcasted_iota(jnp.int32, (CHUNK, LANES), 1)
    lo = (lane < DK).astype(f32)
    hi = 1.0 - lo

    def chunk(c, carry):
        rows = pl.ds(pl.multiple_of(c * CHUNK, CHUNK), CHUNK)
        units = [(s, p) for s in range(batch) for p in range(HEADS // 2)]
        bs_ = [jnp.dot(ltri, la_refs[s][rows, :], preferred_element_type=f32, precision=HIGHEST)
               for s in range(batch)]
        qs, kgs, kd0, kd1, dcol, v0, v1, s_old = {}, {}, {}, {}, {}, {}, {}, {}
        for s in range(batch):
            b = bs_[s]
            b_last = b[CHUNK - 1:CHUNK, :]
            q = q_refs[s][rows, :]
            k = k_refs[s][rows, :]
            qg = q * jnp.exp(b) * (DK ** -0.5)
            kg = k * jnp.exp(-b)
            kd = k * jnp.exp(b_last - b)
            dec = jnp.exp(b_last)
            for p in range(HEADS // 2):
                u = (s, p)
                ls = slice(p * LANES, (p + 1) * LANES)
                qs[u] = jnp.concatenate([qg[:, ls] * lo, qg[:, ls] * hi], axis=0)
                kgs[u] = kg[:, ls]
                kd0[u], kd1[u] = kd[:, ls] * lo, kd[:, ls] * hi
                dcol[u] = jnp.broadcast_to(dec[:, ls], (LANES, LANES)).T
                v0[u] = v_refs[s][rows, pl.ds((2 * p) * DV, DV)]
                v1[u] = v_refs[s][rows, pl.ds((2 * p + 1) * DV, DV)]
                s_old[u] = s_scr[s, p]
        a = {u: _dot_nt(qs[u], kgs[u]) for u in units}
        inter = {u: _dot(qs[u], s_old[u]) for u in units}
        kv = {u: _dot_tn(kd0[u], v0[u]) + _dot_tn(kd1[u], v1[u]) for u in units}
        a = {u: jnp.where(tril2, a[u], 0.0) for u in units}
        intra0 = {u: _dot(a[u][:CHUNK], v0[u]) for u in units}
        intra1 = {u: _dot(a[u][CHUNK:], v1[u]) for u in units}
        for u in units:
            s, p = u
            o_ref[s, rows, pl.ds((2 * p) * DV, DV)] = inter[u][:CHUNK] + intra0[u]
            o_ref[s, rows, pl.ds((2 * p + 1) * DV, DV)] = inter[u][CHUNK:] + intra1[u]
            s_scr[s, p] = dcol[u] * s_old[u] + kv[u]
        return carry

    lax.fori_loop(0, n_chunks, chunk, 0)

    @pl.when(blk == pl.num_programs(0) - 1)
    def _():
        s_out_ref[...] = s_scr[...]


def _stream_specs(width, batch, nb, col):
    return [pl.BlockSpec((TM, width), functools.partial(lambda s, i: (s * nb + i, col), s)) for s in range(batch)]


def _gla_prompt(proj, la, batch, seq):
    nb = seq // TM
    qw_blocks = TILE // QW
    return pl.pallas_call(
        functools.partial(_gla_prompt_kernel, n_chunks=TM // CHUNK, batch=batch),
        grid=(nb,),
        in_specs=[
            *_stream_specs(QW, batch, nb, T_GQK * qw_blocks),
            *_stream_specs(QW, batch, nb, T_GQK * qw_blocks + 1),
            *_stream_specs(VW, batch, nb, T_GV),
            *_stream_specs(QW, batch, nb, 0),
        ],
        out_specs=[
            pl.BlockSpec((batch, TM, VW), lambda i: (0, i, 0)),
            pl.BlockSpec((batch, HEADS // 2, LANES, LANES), lambda i: (0, 0, 0, 0)),
        ],
        out_shape=[
            jax.ShapeDtypeStruct((batch, seq, VW), f32),
            jax.ShapeDtypeStruct((batch, HEADS // 2, LANES, LANES), f32),
        ],
        scratch_shapes=[pltpu.VMEM((batch, HEADS // 2, LANES, LANES), f32)],
        compiler_params=_cparams("arbitrary"),
        name="gla_prompt",
    )(*([proj] * (3 * batch)), *([la] * batch))


def _mlstm_prompt_kernel(*refs, n_chunks, batch):
    q_refs, k_refs, v_refs, gif_refs = (refs[i * batch:(i + 1) * batch] for i in range(4))
    h_ref, c_out_ref, n_out_ref, m_out_ref, ct_scr, n_scr, m_scr = refs[4 * batch:]
    blk = pl.program_id(0)

    @pl.when(blk == 0)
    def _():
        ct_scr[...] = jnp.zeros_like(ct_scr)
        n_scr[...] = jnp.zeros_like(n_scr)
        m_scr[...] = jnp.zeros_like(m_scr)

    CH = CHUNK_MLSTM
    ltri = _tril(CH).astype(f32)
    r_i = lax.broadcasted_iota(jnp.int32, (CH, CH), 0)
    c_i = lax.broadcasted_iota(jnp.int32, (CH, CH), 1)
    upper = r_i <= c_i

    def chunk(c, carry):
        rows = pl.ds(pl.multiple_of(c * CH, CH), CH)
        for sq0 in range(0, batch, MLSTM_GROUP):
            seqs = range(sq0, min(sq0 + MLSTM_GROUP, batch))
            units = [(sq, h) for sq in seqs for h in range(HEADS)]
            gate = {}
            for sq in seqs:
                g = gif_refs[sq][rows, :]
                fc = jnp.dot(ltri, g, preferred_element_type=f32, precision=HIGHEST)
                gate[sq] = (g, fc, g.T, fc.T)
            q, k, v_t, c_t, n_all = {}, {}, {}, {}, {}
            for sq in seqs:
                q_all = q_refs[sq][rows, :]
                k_all = k_refs[sq][rows, :] * (DK ** -0.5)
                n_all[sq] = n_scr[sq]
                for h in range(HEADS):
                    u = (sq, h)
                    q[u] = q_all[:, h * DK:(h + 1) * DK]
                    k[u] = k_all[:, h * DK:(h + 1) * DK]
                    v_t[u] = v_refs[sq][rows, pl.ds(h * DV, DV)].T
                    c_t[u] = ct_scr[sq, h]
            kq = {u: _dot_nt(k[u], q[u]) for u in units}
            cq = {u: _dot_nt(c_t[u], q[u]) for u in units}
            qn = {u: _dot_nt(n_all[u[0]], q[u])[u[1]:u[1] + 1, :] for u in units}
            sc, wk, w_inter, floor, decay, m_new = {}, {}, {}, {}, {}, {}
            for u in units:
                sq, h = u
                g, fc, g_t, fc_t = gate[sq]
                i_col = g[:, L_MI + h:L_MI + h + 1]
                f_col = fc[:, L_MF + h:L_MF + h + 1]
                f_row = fc_t[L_MF + h:L_MF + h + 1, :]
                f_last = fc[CH - 1:CH, L_MF + h:L_MF + h + 1]
                m_prev = m_scr[sq, h:h + 1, 0:1]
                log_d = jnp.where(upper, f_row + (i_col - f_col), -jnp.inf)
                m_inter = m_prev + f_row
                m_row = jnp.maximum(m_inter, jnp.max(log_d, axis=0, keepdims=True))
                w_inter[u] = jnp.exp(m_inter - m_row)
                floor[u] = jnp.exp(-m_row)
                sc[u] = kq[u] * jnp.exp(log_d - m_row)
                log_w = f_last - f_col + i_col
                m_new[u] = jnp.maximum(m_prev + f_last, jnp.max(log_w, axis=0, keepdims=True))
                wk[u] = jnp.exp(log_w - m_new[u]) * k[u]
                decay[u] = jnp.exp(m_prev + f_last - m_new[u])
            vs = {u: _dot(v_t[u], sc[u]) for u in units}
            vk = {u: _dot(v_t[u], wk[u]) for u in units}
            for u in units:
                sq, h = u
                num = w_inter[u] * cq[u] + vs[u]
                den = w_inter[u] * qn[u] + jnp.sum(sc[u], axis=0, keepdims=True)
                h_ref[sq, rows, pl.ds(h * DV, DV)] = (num / jnp.maximum(jnp.abs(den), floor[u])).T
                ct_scr[sq, h] = decay[u] * c_t[u] + vk[u]
                n_scr[sq, h:h + 1, :] = decay[u] * n_all[sq][h:h + 1, :] + jnp.sum(wk[u], axis=0, keepdims=True)
                m_scr[sq, h:h + 1, :] = jnp.broadcast_to(m_new[u], (1, LANES))
        return carry

    lax.fori_loop(0, n_chunks, chunk, 0)

    @pl.when(blk == pl.num_programs(0) - 1)
    def _():
        for sq in range(batch):
            for h in range(HEADS):
                c_out_ref[sq, h] = ct_scr[sq, h].T
        n_out_ref[...] = n_scr[:, 0:HEADS, :]
        m_out_ref[...] = m_scr[:, 0:HEADS, :]


def _mlstm_prompt(proj, gif, batch, seq):
    nb = seq // TM
    qw_blocks = TILE // QW
    return pl.pallas_call(
        functools.partial(_mlstm_prompt_kernel, n_chunks=TM // CHUNK_MLSTM, batch=batch),
        grid=(nb,),
        in_specs=[
            *_stream_specs(QW, batch, nb, T_MQK * qw_blocks),
            *_stream_specs(QW, batch, nb, T_MQK * qw_blocks + 1),
            *_stream_specs(VW, batch, nb, T_MV),
            *_stream_specs(LANES, batch, nb, 0),
        ],
        out_specs=[
            pl.BlockSpec((batch, TM, VW), lambda i: (0, i, 0)),
            pl.BlockSpec((batch, HEADS, DK, DV), lambda i: (0, 0, 0, 0)),
            pl.BlockSpec((batch, HEADS, DK), lambda i: (0, 0, 0)),
            pl.BlockSpec((batch, HEADS, LANES), lambda i: (0, 0, 0)),
        ],
        out_shape=[
            jax.ShapeDtypeStruct((batch, seq, VW), f32),
            jax.ShapeDtypeStruct((batch, HEADS, DK, DV), f32),
            jax.ShapeDtypeStruct((batch, HEADS, DK), f32),
            jax.ShapeDtypeStruct((batch, HEADS, LANES), f32),
        ],
        scratch_shapes=[
            pltpu.VMEM((batch, HEADS, DV, DK), f32),
            pltpu.VMEM((batch, SUBLANES, DK), f32),
            pltpu.VMEM((batch, SUBLANES, LANES), f32),
        ],
        compiler_params=_cparams("arbitrary"),
        name="mlstm_prompt",
    )(*([proj] * (3 * batch)), *([gif] * batch))


def _diff_lambda_value(lam_ref, lam_init):
    lam = lam_ref[...]
    s1 = jnp.sum(lam[0:1] * lam[1:2], axis=1, keepdims=True)
    s2 = jnp.sum(lam[2:3] * lam[3:4], axis=1, keepdims=True)
    return jnp.exp(s1) - jnp.exp(s2) + lam_init


def _attn_prompt_step(p, qi_ref, ki_ref, q_ref, k_ref, v_ref, lam_ref, o_ref, q2_scr, m_scr, l_scr, acc_scr,
                      *, tq, lam_init, other_stages):
    qi = qi_ref[p]
    ki = ki_ref[p]
    pair = lambda j: slice(j * LANES, (j + 1) * LANES)

    @pl.when(ki == 0)
    def _():
        for j in range(HEADS):
            q = q_ref[:, pair(j)] * (DK ** -0.5 * LOG2E)
            lane = lax.broadcasted_iota(jnp.int32, q.shape, 1)
            q2_scr[j, 0:tq, :] = jnp.where(lane < DK, q, 0.0).astype(bf16)
            q2_scr[j, tq:2 * tq, :] = jnp.where(lane >= DK, q, 0.0).astype(bf16)
        m_scr[...] = jnp.full_like(m_scr, -jnp.inf)
        l_scr[...] = jnp.zeros_like(l_scr)
        acc_scr[...] = jnp.zeros_like(acc_scr)

    def step(masked, other):
        sts = [_dot_nt(k_ref[:, pair(j)], q2_scr[j]) for j in range(HEADS)]
        other[0]()
        pts, alphas = [], []
        for j in range(HEADS):
            st = sts[j]
            if masked:
                r = lax.broadcasted_iota(jnp.int32, st.shape, 0)
                c = lax.broadcasted_iota(jnp.int32, st.shape, 1)
                c = jnp.where(c >= tq, c - tq, c)
                st = jnp.where(r <= c, st, -jnp.inf)
            m_prev = m_scr[j]
            m_new = jnp.maximum(m_prev, jnp.max(st, axis=0, keepdims=True))
            alpha = jnp.exp2(m_prev - m_new)
            pt = jnp.exp2(st - m_new)
            l_scr[j] = alpha * l_scr[j] + jnp.sum(pt, axis=0, keepdims=True)
            m_scr[j] = m_new
            pts.append(pt.astype(bf16))
            alphas.append(alpha)
        for j in range(HEADS):
            acc_scr[j] = alphas[j] * acc_scr[j] + _dot_tn(v_ref[:, pair(j)], pts[j])
        for stage in other[1:]:
            stage()

    @pl.when(ki < qi)
    def _():
        step(False, other_stages())

    @pl.when(ki == qi)
    def _():
        step(True, other_stages())
        lam = _diff_lambda_value(lam_ref, lam_init)
        for j in range(HEADS):
            ot = acc_scr[j] / l_scr[j]
            o_ref[:, pair(j)] = (ot[:, 0:tq] - lam * ot[:, tq:2 * tq]).T


def _sample_page_copies(seq_idx, slot, pt_ref, ck_hbm, cv_hbm, kbuf, vbuf, sem, *, layer, n_pages, page):
    out = []
    for pg in range(n_pages):
        src = pt_ref[seq_idx, pg]
        out.append(pltpu.make_async_copy(ck_hbm.at[layer, src], kbuf.at[slot, :, pl.ds(pg * page, page)],
                                         sem.at[0, slot]))
        out.append(pltpu.make_async_copy(cv_hbm.at[layer, src],
                                         vbuf.at[slot, pl.ds(pg * page * HEADS, page * HEADS), :], sem.at[1, slot]))
    return out


def _attn_sample_pages(step, n_steps, pt_ref, ck_hbm, cv_hbm, o_ref, kbuf, vbuf, sem, *, nseq, pages):
    past = pages["n_pages"] * pages["page"]
    seq = lambda st: jnp.minimum(st, nseq - 1)
    slot = step % 2
    copies = functools.partial(_sample_page_copies, pt_ref=pt_ref, ck_hbm=ck_hbm, cv_hbm=cv_hbm, kbuf=kbuf,
                               vbuf=vbuf, sem=sem, **pages)

    @pl.when(step == 0)
    def _():
        kbuf[:, :, pl.ds(past, LANES)] = jnp.zeros((2, kbuf.shape[1], LANES), f32)
        vbuf[:, pl.ds(past * HEADS, LANES * HEADS), :] = jnp.zeros((2, LANES * HEADS, DV), f32)
        o_ref[...] = jnp.zeros_like(o_ref)
        for cp in copies(0, 0):
            cp.start()

    @pl.when(step + 1 < n_steps)
    def _():
        for cp in copies(seq(step + 1), 1 - slot):
            cp.start()

    for cp in copies(seq(step), slot):
        cp.wait()


def _attn_sample_stages(step, q_ref, kts_ref, v_ref, lam_ref, o_ref, kbuf, vbuf,
                        *, n_pages, page, dec_seq, nseq, lam_init):
    st = {}
    b = jnp.minimum(step, nseq - 1)
    slot = step % 2
    past = n_pages * page
    n_qh = 2 * HEADS

    shift = lax.rem(nseq - b, nseq)
    b8 = (b // SUBLANES) * SUBLANES
    groups = [pl.ds(pl.multiple_of(t * nseq + b8, SUBLANES), SUBLANES) for t in range(dec_seq)]
    mine = lax.broadcasted_iota(jnp.int32, (SUBLANES, 2 * QW), 0) == b - b8
    pick = lambda ref, t: jnp.sum(jnp.where(mine, ref[groups[t], :], 0.0), axis=0, keepdims=True)
    def scores():
        head_of_lane = lax.broadcasted_iota(jnp.int32, (n_qh, 2 * QW), 1) // DK
        head_of_row = lax.broadcasted_iota(jnp.int32, (n_qh, 2 * QW), 0)
        q_rows = []
        for t in range(dec_seq):
            kbuf[slot, :, pl.ds(past + t, 1)] = pltpu.roll(kts_ref[t], shift, axis=1)[:, 0:1]
            v_row = pick(v_ref, t)
            for h in range(HEADS):
                vbuf[slot, pl.ds((past + t) * HEADS + h, 1), :] = v_row[:, h * DV:(h + 1) * DV]
            q = jnp.broadcast_to(pick(q_ref, t) * (DK ** -0.5), (n_qh, 2 * QW))
            q_rows.append(jnp.where(head_of_lane == head_of_row, q, 0.0))
        qbd = jnp.concatenate(q_rows, axis=0)
        st["s"] = _dot(qbd, kbuf[slot])

    def softmax():
        s = st["s"]
        r = lax.broadcasted_iota(jnp.int32, s.shape, 0)
        c = lax.broadcasted_iota(jnp.int32, s.shape, 1)
        s = jnp.where(c - past <= r // n_qh, s, -jnp.inf)
        pr = jnp.exp(s - jnp.max(s, axis=1, keepdims=True))
        st["inv_l"] = 1.0 / jnp.sum(pr, axis=1, keepdims=True)
        st["pr"] = pr.astype(bf16)

    def values():
        v_hs = [vbuf[slot, pl.ds(hv, past + LANES, stride=HEADS), :] for hv in range(HEADS)]
        st["o"] = [_dot(st["pr"], v_h) for v_h in v_hs]

    def write():
        lam = _diff_lambda_value(lam_ref, lam_init)
        mine_v = mine[:, 0:DV]
        for hv in range(HEADS):
            o_h = st["o"][hv] * st["inv_l"]
            for t in range(dec_seq):
                r1 = t * n_qh + 2 * hv
                d = o_h[r1:r1 + 1] - lam * o_h[r1 + 1:r1 + 2]
                lanes = slice(hv * DV, (hv + 1) * DV)
                o_ref[groups[t], lanes] = jnp.where(mine_v, d, o_ref[groups[t], lanes])

    return scores, softmax, values, write


def _attn_kernel(qi_ref, ki_ref, pt_ref, q_ref, k_ref, v_ref, lam_ref, qs_ref, kts_ref, vs_ref, ck_hbm, cv_hbm,
                 o_ref, os_ref, q2_scr, m_scr, l_scr, acc_scr, kbuf, vbuf, sem, *, prompt, sample, nseq, n_pairs):
    p = pl.program_id(1)
    step = pl.program_id(0) * pl.num_programs(1) + p
    n_steps = pl.num_programs(0) * pl.num_programs(1)
    pages = dict(layer=sample["layer"], n_pages=sample["n_pages"], page=sample["page"])
    _attn_sample_pages(step, n_steps, pt_ref, ck_hbm, cv_hbm, os_ref, kbuf, vbuf, sem, nseq=nseq, pages=pages)

    def sample_stages():
        return _attn_sample_stages(step, qs_ref, kts_ref, vs_ref, lam_ref, os_ref, kbuf, vbuf, nseq=nseq,
                                   n_pages=sample["n_pages"], page=sample["page"], dec_seq=sample["dec_seq"],
                                   lam_init=sample["lam_init"])

    @pl.when(p < n_pairs)
    def _():
        _attn_prompt_step(p, qi_ref, ki_ref, q_ref, k_ref, v_ref, lam_ref, o_ref,
                          q2_scr, m_scr, l_scr, acc_scr, other_stages=sample_stages, **prompt)

    @pl.when(p >= n_pairs)
    def _():
        for stage in sample_stages():
            stage()


def _attention(proj, k_ts, page_table, diff_lambda, cache_kt, cache_v2, layer, lam_init, batch, seq, sample_block,
               tq=512):
    nq = seq // tq
    pairs = [(q, k) for q in range(nq) for k in range(q + 1)]
    qi_tab = jnp.asarray(np.array([a for a, _ in pairs], np.int32))
    ki_tab = jnp.asarray(np.array([b for _, b in pairs], np.int32))
    nseq, n_pages = page_table.shape
    n_pairs = len(pairs)
    steps = max(n_pairs, -(-nseq // batch))
    last = n_pairs - 1
    dec_seq = k_ts.shape[0]
    rows = dec_seq * nseq
    page = cache_kt.shape[3]
    cols = n_pages * page + LANES
    grid_spec = pltpu.PrefetchScalarGridSpec(
        num_scalar_prefetch=3,
        grid=(batch, steps),
        in_specs=[
            pl.BlockSpec((tq, TILE), lambda b, p, qt, kt, pt: (b * nq + qt[jnp.minimum(p, last)], T_DQ)),
            pl.BlockSpec((tq, TILE), lambda b, p, qt, kt, pt: (b * nq + kt[jnp.minimum(p, last)], T_DK)),
            pl.BlockSpec((tq, TILE), lambda b, p, qt, kt, pt: (b * nq + kt[jnp.minimum(p, last)], T_DV)),
            pl.BlockSpec((4, DK), lambda b, p, qt, kt, pt: (0, 0)),
            pl.BlockSpec((rows, TILE), lambda b, p, qt, kt, pt: (sample_block, T_DQ)),
            pl.BlockSpec((dec_seq, TILE, nseq), lambda b, p, qt, kt, pt: (0, 0, 0)),
            pl.BlockSpec((rows, TILE), lambda b, p, qt, kt, pt: (sample_block, T_DV)),
            pl.BlockSpec(memory_space=pl.ANY),
            pl.BlockSpec(memory_space=pl.ANY),
        ],
        out_specs=[
            pl.BlockSpec((tq, VW), lambda b, p, qt, kt, pt: (b * nq + qt[jnp.minimum(p, last)], 0)),
            pl.BlockSpec((rows, VW), lambda b, p, qt, kt, pt: (0, 0)),
        ],
        scratch_shapes=[
            pltpu.VMEM((HEADS, 2 * tq, LANES), bf16),
            pltpu.VMEM((HEADS, 1, 2 * tq), f32),
            pltpu.VMEM((HEADS, 1, 2 * tq), f32),
            pltpu.VMEM((HEADS, DV, 2 * tq), f32),
            pltpu.VMEM((2, 2 * QW, cols), f32),
            pltpu.VMEM((2, cols * HEADS, DV), f32),
            pltpu.SemaphoreType.DMA((2, 2)),
        ],
    )
    return pl.pallas_call(
        functools.partial(_attn_kernel, nseq=nseq, n_pairs=n_pairs, prompt=dict(tq=tq, lam_init=lam_init),
                          sample=dict(layer=layer, n_pages=n_pages, page=page, dec_seq=dec_seq, lam_init=lam_init)),
        grid_spec=grid_spec,
        out_shape=[jax.ShapeDtypeStruct((batch * seq, VW), f32), jax.ShapeDtypeStruct((rows, VW), f32)],
        compiler_params=_cparams("arbitrary", "arbitrary"),
        name="attention",
    )(qi_tab, ki_tab, page_table, proj, proj, proj, diff_lambda, proj, k_ts, proj, cache_kt, cache_v2)


def _head_of_lane(shape):
    return lax.broadcasted_iota(jnp.int32, shape, 1) // DK


def _expand_heads(cols, base, head):
    out = cols[:, base + HEADS - 1:base + HEADS]
    for h in range(HEADS - 2, -1, -1):
        out = jnp.where(head == h, cols[:, base + h:base + h + 1], out)
    return out


def _sample_prep_kernel(gq_ref, gk_ref, la_ref, mq_ref, mk_ref, gif_ref, n0_ref, m0_ref,
                        ga_ref, gkk_ref, gqq_ref, ma_ref, mkk_ref, mqq_ref, n_out_ref, m_out_ref, *, nseq, dec_seq):
    head = _head_of_lane((nseq, QW))
    m = _expand_heads(m0_ref[...], 0, head)
    n = n0_ref[...]
    for t in range(dec_seq):
        rows = pl.ds(t * nseq, nseq)
        ga_ref[t] = jnp.exp(la_ref[rows, :]).T
        gkk_ref[t] = gk_ref[rows, :].T
        gqq_ref[t] = (gq_ref[rows, :] * (DK ** -0.5)).T
        gates = gif_ref[rows, :]
        i_e = _expand_heads(gates, L_MI, head)
        f_e = _expand_heads(gates, L_MF, head)
        k = mk_ref[rows, :] * (DK ** -0.5)
        q = mq_ref[rows, :]
        m_new = jnp.maximum(f_e + m, i_e)
        fp = jnp.exp(f_e + m - m_new)
        ip = jnp.exp(i_e - m_new)
        n = fp * n + ip * k
        nq = n * q
        den = jnp.zeros_like(nq)
        for h in range(HEADS):
            den = jnp.where(head == h, jnp.sum(jnp.where(head == h, nq, 0.0), axis=1, keepdims=True), den)
        inv = 1.0 / jnp.maximum(jnp.abs(den), jnp.exp(-m_new))
        ma_ref[t] = fp.T
        mkk_ref[t] = (ip * k).T
        mqq_ref[t] = (q * inv).T
        m = m_new
    n_out_ref[...] = n
    m_out_ref[...] = m


def _sample_prep(proj, la, gif, state_n, state_m, layer, sample_block, nseq, dec_seq):
    rows = nseq * dec_seq
    qw_blocks = TILE // QW
    tok = jax.ShapeDtypeStruct((dec_seq, QW, nseq), f32)
    st = jax.ShapeDtypeStruct((nseq, QW), f32)
    tok_spec = pl.BlockSpec((dec_seq, QW, nseq), lambda i: (0, 0, 0))
    st_spec = pl.BlockSpec((nseq, QW), lambda i: (0, 0))
    col = lambda c: pl.BlockSpec((rows, QW), lambda i: (sample_block, c))
    return pl.pallas_call(
        functools.partial(_sample_prep_kernel, nseq=nseq, dec_seq=dec_seq),
        grid=(1,),
        in_specs=[
            col(T_GQK * qw_blocks), col(T_GQK * qw_blocks + 1), col(0),
            col(T_MQK * qw_blocks), col(T_MQK * qw_blocks + 1),
            pl.BlockSpec((rows, LANES), lambda i: (sample_block, 0)),
            pl.BlockSpec((None, nseq, QW), lambda i: (layer, 0, 0)),
            pl.BlockSpec((None, nseq, HEADS), lambda i: (layer, 0, 0)),
        ],
        out_specs=[tok_spec] * 6 + [st_spec] * 2,
        out_shape=[tok] * 6 + [st] * 2,
        compiler_params=_cparams("arbitrary"),
        name="sample_prep",
    )(proj, proj, la, proj, proj, gif, state_n, state_m)


SEQ_PER_STEP = 8


def _sample_state_kernel(a_ref, k_ref, q_ref, v_ref, s_ref, s_all_ref, o_ref, s_out_ref, *, dec_seq, nseq):
    del s_all_ref
    blk = pl.program_id(0)
    seq0 = blk * SEQ_PER_STEP
    shift = lax.rem(nseq - seq0, nseq)
    a_t = [pltpu.roll(a_ref[t], shift, axis=1) for t in range(dec_seq)]
    k_t = [pltpu.roll(k_ref[t], shift, axis=1) for t in range(dec_seq)]
    q_t = [pltpu.roll(q_ref[t], shift, axis=1) for t in range(dec_seq)]
    toks = [pl.ds(pl.multiple_of(t * nseq + seq0, SEQ_PER_STEP), SEQ_PER_STEP) for t in range(dec_seq)]
    v_t = [v_ref[toks[t], :] for t in range(dec_seq)]
    row = lax.broadcasted_iota(jnp.int32, (SEQ_PER_STEP, DV), 0)
    outs = [[jnp.zeros((SEQ_PER_STEP, DV), f32) for _ in range(HEADS)] for _ in range(dec_seq)]
    for bb in range(SEQ_PER_STEP):
        for h in range(HEADS):
            s = s_ref[bb, h]
            rows = slice(h * DK, (h + 1) * DK)
            for t in range(dec_seq):
                a = a_t[t][rows, bb:bb + 1]
                k = k_t[t][rows, bb:bb + 1]
                q = q_t[t][rows, bb:bb + 1]
                v = v_t[t][bb:bb + 1, h * DV:(h + 1) * DV]
                s = a * s + k * v
                outs[t][h] = jnp.where(row == bb, jnp.sum(q * s, axis=0, keepdims=True), outs[t][h])
            s_out_ref[bb, h] = s
    for t in range(dec_seq):
        for h in range(HEADS):
            o_ref[toks[t], h * DV:(h + 1) * DV] = outs[t][h]


def _sample_state(a_t, k_t, q_t, proj, v_tile, sample_block, state, layer, dec_seq, carried):
    nseq = a_t.shape[2]
    rows = dec_seq * nseq
    full = pl.BlockSpec((dec_seq, QW, nseq), lambda i: (0, 0, 0))
    state_spec = pl.BlockSpec((None, SEQ_PER_STEP, HEADS, DK, DV), lambda i: (layer, i, 0, 0, 0))
    return pl.pallas_call(
        functools.partial(_sample_state_kernel, dec_seq=dec_seq, nseq=nseq),
        grid=(nseq // SEQ_PER_STEP,),
        input_output_aliases={5: 1},
        in_specs=[
            full, full, full,
            pl.BlockSpec((rows, VW), lambda i: (sample_block, v_tile)),
            state_spec,
            pl.BlockSpec(memory_space=pl.ANY),
        ],
        out_specs=[pl.BlockSpec((rows, VW), lambda i: (0, 0)), state_spec],
        out_shape=[jax.ShapeDtypeStruct((rows, VW), f32), jax.ShapeDtypeStruct(state.shape, f32)],
        compiler_params=_cparams("arbitrary"),
        name="sample_state",
    )(a_t, k_t, q_t, proj, state, carried)


def _head_rms(x, g):
    parts = []
    for h in range(HEADS):
        parts.append(_rms(x[:, h * DV:(h + 1) * DV], g))
    return jnp.concatenate(parts, axis=1)


FF_SLABS = ((0, 768), (768, 768), (1536, 768), (2304, 512))
TM_CH = 256


def _merge_ffn_kernel(hp_ref, hs_ref, oap_ref, oas_ref, omp_ref, oms_ref, odp_ref, ods_ref, gr_ref, mo_ref,
                      g0_ref, g1_ref, g2_ref, gg_ref, gm_ref, gd_ref, wa_ref, wm_ref, wd_ref, wo_ref,
                      gf_ref, wg_ref, wu_ref, wdn_ref, pp_ref, ps_ref, gp_ref, wpg_ref, wpp_ref, gfin_ref,
                      *out_refs, lam_init, final, n_prompt_blocks):
    pick = functools.partial(_pick_rows, n_prompt_blocks)
    gr = gr_ref[...]
    a = _head_rms(pick(oap_ref, oas_ref), gg_ref[...]) * (gr * _sigmoid(gr))
    m = _sigmoid(mo_ref[...]) * _head_rms(pick(omp_ref, oms_ref), gm_ref[...])
    d = _head_rms(pick(odp_ref, ods_ref), gd_ref[...]) * (1.0 - lam_init)
    merged = (g0_ref[...] * _dot(a, wa_ref[...]) + g1_ref[...] * _dot(m, wm_ref[...])
              + g2_ref[...] * _dot(d, wd_ref[...]))
    h = pick(hp_ref, hs_ref) + _dot(merged, wo_ref[...])

    xf = _rms(h, gf_ref[...]).astype(bf16)
    h2 = h
    for lo, width in FF_SLABS:
        gate = jnp.dot(xf, wg_ref[:, lo:lo + width], preferred_element_type=f32)
        up = jnp.dot(xf, wu_ref[:, lo:lo + width], preferred_element_type=f32)
        h2 = h2 + _dot(gate * _sigmoid(gate) * up, wdn_ref[lo:lo + width, :])
    ple_gate = _sigmoid(_dot(_rms(h2, gp_ref[...]), wpg_ref[...]))
    h3 = h2 + ple_gate * _dot(pick(pp_ref, ps_ref), wpp_ref[...])
    if not final:
        out_refs[0][...] = h3
        return
    y = _rms(h3, gfin_ref[...])
    yp_ref, ys_ref = out_refs
    i = pl.program_id(0)

    @pl.when(i < n_prompt_blocks)
    def _():
        yp_ref[...] = y

    @pl.when(i >= n_prompt_blocks)
    def _():
        ys_ref[...] = y


def _merge_ffn(h_p, h_s, h_s_row0, o_a, o_m, o_d, proj, gates, g_gla, g_ml, g_diff, w_a, w_m, w_d, w_out, lam_init,
               g_ffn, w_g, w_u, w_dn, p_p, p_s, p_row0, g_ple, w_pg, w_pp, g_final, final, n_prompt_rows, n_sample_rows):
    tm = TM_CH
    npb, nsb = n_prompt_rows // tm, n_sample_rows // tm
    T = n_prompt_rows + n_sample_rows
    row = lambda i: (i, 0)
    const = lambda i: (0, 0)
    resident = lambda shape: pl.BlockSpec(shape, const, pipeline_mode=pl.Buffered(1))
    rows2 = lambda width, sample_block=0, prompt_block0=0: _row_specs(width, npb, sample_block, tm, nsb, prompt_block0)
    assert FF_SLABS[-1][0] + FF_SLABS[-1][1] == D_FF
    if final:
        out_specs = [pl.BlockSpec((tm, D_MODEL), lambda i: (jnp.minimum(i, npb - 1), 0)),
                     pl.BlockSpec((tm, D_MODEL), lambda i: (jnp.clip(i - npb, 0, nsb - 1), 0))]
        out_shape = [jax.ShapeDtypeStruct((n_prompt_rows, D_MODEL), f32),
                     jax.ShapeDtypeStruct((n_sample_rows, D_MODEL), f32)]
    else:
        out_specs = pl.BlockSpec((tm, D_MODEL), row)
        out_shape = jax.ShapeDtypeStruct((T, D_MODEL), f32)
    return pl.pallas_call(
        functools.partial(_merge_ffn_kernel, lam_init=lam_init, final=final, n_prompt_blocks=npb),
        grid=(npb + nsb,),
        in_specs=[
            *rows2(D_MODEL, sample_block=h_s_row0 // tm),
            *rows2(VW), *rows2(VW), *rows2(VW),
            pl.BlockSpec((tm, VW), lambda i: (i, T_GR)),
            pl.BlockSpec((tm, VW), lambda i: (i, T_MO)),
            pl.BlockSpec((tm, D_MODEL), lambda i: (i, 0)),
            pl.BlockSpec((tm, D_MODEL), lambda i: (i, 1)),
            pl.BlockSpec((tm, D_MODEL), lambda i: (i, 2)),
            pl.BlockSpec((1, DV), const),
            pl.BlockSpec((1, DV), const),
            pl.BlockSpec((1, DV), const),
            resident((VW, D_MODEL)), resident((VW, D_MODEL)), resident((VW, D_MODEL)),
            resident((D_MODEL, D_MODEL)),
            pl.BlockSpec((1, D_MODEL), const),
            resident((D_MODEL, D_FF)), resident((D_MODEL, D_FF)), resident((D_FF, D_MODEL)),
            *rows2(PLE_DIM, prompt_block0=p_row0 // tm),
            pl.BlockSpec((1, D_MODEL), const),
            resident((D_MODEL, D_MODEL)),
            resident((PLE_DIM, D_MODEL)),
            pl.BlockSpec((1, D_MODEL), const),
        ],
        out_specs=out_specs,
        out_shape=out_shape,
        compiler_params=_cparams("arbitrary"),
        name="merge_ffn",
    )(h_p, h_s, *o_a, *o_m, *o_d, proj, proj, gates, gates, gates, g_gla, g_ml, g_diff, w_a, w_m, w_d, w_out,
      g_ffn, w_g, w_u, w_dn, p_p, p_s, g_ple, w_pg, w_pp, g_final)


def _rope_tables(positions):
    half = ROT_DIM // 2
    inv_freq = ROPE_THETA ** (-jnp.arange(half, dtype=f32) * 2.0 / ROT_DIM)
    ang = positions.astype(f32)[:, None] * inv_freq[None, :]
    cos, sin = jnp.cos(ang), jnp.sin(ang)
    n = positions.shape[0]
    one = jnp.ones((n, DK - ROT_DIM), f32)
    zero = jnp.zeros((n, DK - ROT_DIM), f32)
    zh = jnp.zeros((n, half), f32)
    cos_h = jnp.concatenate([cos, cos, one], axis=1)
    up_h = jnp.concatenate([-sin, zh, zero], axis=1)
    dn_h = jnp.concatenate([zh, sin, zero], axis=1)
    rep = lambda a: jnp.concatenate([a] * (LANES // DK), axis=1)
    return rep(cos_h), rep(up_h), rep(dn_h)


def _prep_in_weights(w_in_l, w_gate_l):
    o = np.cumsum((0, QW, QW, VW, VW, GLA_RANK, QW, QW, VW, VW, 2 * HEADS, 2 * QW, 2 * QW, VW))
    seg = lambda i: w_in_l[:, int(o[i]):int(o[i + 1])]
    gq, gk, gv, gr, glr, mq, mk, mv, mo, mif, dq, dk, dv = (seg(i) for i in range(13))
    main = jnp.concatenate([w_gate_l, gq, gk, gv, gr, mq, mk, mv, mo, dq, dk, dv], axis=1).astype(bf16)
    pad = jnp.zeros((D_MODEL, LANES - GLA_RANK - 2 * HEADS), f32)
    small = jnp.concatenate([glr, mif, pad], axis=1).astype(bf16)
    return main, small


def kernel(x_prompt, x_sample, cache_k, cache_v, state_gla, state_mlstm_c, state_mlstm_n, state_mlstm_m, page_table, p_prompt, p_sample, g_mix, w_in, w_gla_gk, b_gla_gk, g_gla_norm, b_mlstm_if, g_mlstm_norm, diff_lambda, g_diff_norm, w_branch, w_gate, w_out, g_ffn, w_ffn_gate, w_ffn_up, w_ffn_down, g_ple, w_ple_gate, w_ple_proj, g_final):
    bp, lp, _ = x_prompt.shape
    bs, ls, _ = x_sample.shape
    depth = g_mix.shape[0]
    n_pages = page_table.shape[1]
    page = cache_k.shape[2]
    past = n_pages * page
    tp = bp * lp
    ts = bs * ls
    assert lp % TM == 0 and ts == TM and bs == LANES and page == LANES and TM_IN % bs == 0
    npb = tp // TM

    def to_step_major(a):
        return jnp.swapaxes(a, 0, 1).reshape((ts,) + a.shape[2:])

    def from_step_major(a):
        return jnp.swapaxes(a.reshape((ls, bs) + a.shape[1:]), 0, 1)

    h_p, h_s, h_s_row0 = x_prompt.reshape(tp, D_MODEL), to_step_major(x_sample), 0

    pos_rows = jnp.concatenate([jnp.arange(lp, dtype=jnp.int32),
                                past + jnp.repeat(jnp.arange(ls, dtype=jnp.int32), bs)])
    cos_t, sup_t, sdn_t = _rope_tables(pos_rows)

    ck = jnp.transpose(cache_k, (0, 1, 3, 4, 2)).reshape(cache_k.shape[0], cache_k.shape[1], 2 * QW, page)
    cv = cache_v.reshape(cache_v.shape[0], cache_v.shape[1], page * HEADS, DV)
    state_n = state_mlstm_n.reshape(depth, bs, QW)
    p_rows = p_prompt.reshape(depth * tp, PLE_DIM)
    k_t = jnp.zeros((depth, bp, 2 * QW, lp), f32)
    v_p = jnp.zeros((depth * tp * HEADS, DV), f32)
    gla_s = jnp.zeros(state_gla.shape, f32)
    c_s = jnp.zeros(state_mlstm_c.shape, f32)
    outs_p, outs_s = [], []
    for l in range(depth):
        lam_init = 0.8 - 0.6 * math.exp(-0.3 * l)
        w_main, w_small = _prep_in_weights(w_in[l], w_gate[l])
        w_gk = jnp.concatenate([w_gla_gk[l], jnp.zeros((LANES - GLA_RANK, QW), f32)], axis=0)
        b_if = jnp.zeros((1, LANES), f32).at[0, L_MI:L_MI + 2 * HEADS].set(b_mlstm_if[l])
        gates, proj, k_t, v_p, k_ts, la, gif = _inproj(
            h_p, h_s, h_s_row0 // TM_IN, g_mix[l][None], w_main, w_small, w_gk, b_gla_gk[l][None], b_if,
            cos_t, sup_t, sdn_t, bp, lp, bs, ls, l, depth, (k_t, v_p))

        oa_p, gla_p = _gla_prompt(proj, la, bp, lp)
        om_p, c_p, n_p, m_p = _mlstm_prompt(proj, gif, bp, lp)
        oa_p, om_p = oa_p.reshape(tp, VW), om_p.reshape(tp, VW)
        od_p, od_s = _attention(proj, k_ts, page_table, diff_lambda[l], ck, cv, l, lam_init, bp, lp, npb)

        ga, gk_, gq_, ma, mk_, mq_, n_s, m_s = _sample_prep(proj, la, gif, state_n, state_mlstm_m, l, npb, bs, ls)
        oa_s, gla_s = _sample_state(ga, gk_, gq_, proj, T_GV, npb, state_gla, l, ls, gla_s)
        om_s, c_s = _sample_state(ma, mk_, mq_, proj, T_MV, npb, state_mlstm_c, l, ls, c_s)
        v_s = from_step_major(proj[tp:, T_DV * TILE:(T_DV + 1) * TILE])

        wb = w_branch[l].astype(bf16)
        h = _merge_ffn(h_p, h_s, h_s_row0, (oa_p, oa_s), (om_p, om_s), (od_p, od_s), proj, gates,
                       g_gla_norm[l][None], g_mlstm_norm[l][None], g_diff_norm[l][None],
                       wb[:VW], wb[VW:2 * VW], wb[2 * VW:], w_out[l].astype(bf16), lam_init,
                       g_ffn[l][None], w_ffn_gate[l].astype(bf16), w_ffn_up[l].astype(bf16),
                       w_ffn_down[l].astype(bf16), p_rows, to_step_major(p_sample[l]), l * tp,
                       g_ple[l][None], w_ple_gate[l].astype(bf16), w_ple_proj[l].astype(bf16), g_final[None],
                       l == depth - 1, tp, ts)
        h_p, h_s, h_s_row0 = h, h, tp

        k_new = k_ts.reshape(ls, 2 * HEADS, DK, bs).transpose(3, 0, 1, 2)
        outs_p.append((gla_p.reshape(bp, HEADS, DK, DV), c_p, n_p, m_p[:, :, 0]))
        outs_s.append((k_new, v_s.reshape(bs, ls, HEADS, DV), n_s.reshape(bs, HEADS, DK), m_s[:, ::DK]))

    y_prompt = h[0].reshape(bp, lp, D_MODEL)
    y_sample = from_step_major(h[1])
    k_prompt = k_t.reshape(depth, bp, 2 * HEADS, DK, lp).transpose(0, 1, 4, 2, 3)
    v_prompt = v_p.reshape(depth, bp, lp, HEADS, DV)
    gla_prompt, c_prompt, n_prompt, m_prompt = [jnp.stack(t) for t in zip(*outs_p)]
    k_sample, v_sample, n_sample, m_sample = [jnp.stack(t) for t in zip(*outs_s)]
    return (y_prompt, y_sample, k_prompt, v_prompt, gla_prompt, c_prompt, n_prompt, m_prompt,
            k_sample, v_sample, gla_s, c_s, n_sample, m_sample)
```

```python
import functools
import math

import numpy as np
import jax
import jax.numpy as jnp
from jax import lax
from jax.experimental import pallas as pl
from jax.experimental.pallas import tpu as pltpu

f32 = jnp.float32
bf16 = jnp.bfloat16
HIGHEST = lax.Precision.HIGHEST

D_MODEL = 1024
PLE_DIM = 256
HEADS = 4
DK = 64
DV = 128
GLA_RANK = 16
GLA_GATE_NORM = 16.0
ROT_DIM = 16
ROPE_THETA = 500000.0
CHUNK = 64
CHUNK_MLSTM = 256
MLSTM_GROUP = 1
D_FF = 2816
EPS = 1e-6
LOG2E = 1.4426950408889634
VW = HEADS * DV
QW = HEADS * DK

LANES = 128
SUBLANES = 8
VMEM_LIMIT = 48 * 1024 * 1024

TILE = 512
GATE_W = 3 * D_MODEL
T_GQK, T_GV, T_GR = 0, 1, 2
T_MQK, T_MV, T_MO = 3, 4, 5
T_DQ, T_DK, T_DV = 6, 7, 8
N_TILES = 9
P_MIX = N_TILES * TILE
P_MAIN = GATE_W + P_MIX
L_MI = 16
L_MF = 20

TM = 512


def _cparams(*sem):
    return pltpu.CompilerParams(dimension_semantics=sem, vmem_limit_bytes=VMEM_LIMIT)


def _log_sigmoid(x):
    return jnp.minimum(x, 0.0) - jnp.log1p(jnp.exp(-jnp.abs(x)))


def _sigmoid(x):
    return 0.5 * jnp.tanh(0.5 * x) + 0.5


def _rms(x, g):
    return x * lax.rsqrt(jnp.mean(x * x, axis=-1, keepdims=True) + EPS) * g


def _dot(a, b):
    return jnp.dot(a.astype(bf16), b.astype(bf16), preferred_element_type=f32)


def _dot_nt(a, b):
    return lax.dot_general(a.astype(bf16), b.astype(bf16), (((1,), (1,)), ((), ())), preferred_element_type=f32)


def _dot_tn(a, b):
    return lax.dot_general(a.astype(bf16), b.astype(bf16), (((0,), (0,)), ((), ())), preferred_element_type=f32)


def _rope_tile(x, cos, sin_up, sin_dn):
    parts = []
    for c in range(x.shape[1] // LANES):
        xc = x[:, c * LANES:(c + 1) * LANES]
        parts.append(xc * cos + pltpu.roll(xc, LANES - ROT_DIM // 2, axis=1) * sin_up
                     + pltpu.roll(xc, ROT_DIM // 2, axis=1) * sin_dn)
    return jnp.concatenate(parts, axis=1)


def _row_specs(width, n_prompt_blocks, sample_block, tm=None, n_sample_blocks=1, prompt_block0=0):
    tm = TM if tm is None else tm
    return (pl.BlockSpec((tm, width), lambda i: (prompt_block0 + jnp.minimum(i, n_prompt_blocks - 1), 0)),
            pl.BlockSpec((tm, width),
                         lambda i: (sample_block + jnp.clip(i - n_prompt_blocks, 0, n_sample_blocks - 1), 0)))


def _pick_rows(n_prompt_blocks, p_ref, s_ref):
    return jnp.where(pl.program_id(0) >= n_prompt_blocks, s_ref[...], p_ref[...])


TM_IN = 256


def _inproj_kernel(xp_ref, xs_ref, g_ref, w_ref, ws_ref, wgk_ref, bgk_ref, bif_ref, cos_ref, sup_ref, sdn_ref,
                   kt_all_ref, vp_all_ref, gates_ref, proj_ref, kt_ref, vp_ref, kts_ref, la_ref, gif_ref,
                   *, n_prompt_blocks, dec_batch):
    del kt_all_ref, vp_all_ref
    i = pl.program_id(0)
    xn = _rms(_pick_rows(n_prompt_blocks, xp_ref, xs_ref), g_ref[...]).astype(bf16)
    small = jnp.dot(xn, ws_ref[...], preferred_element_type=f32)
    z = jnp.dot(small, wgk_ref[...], preferred_element_type=f32, precision=HIGHEST) + bgk_ref[...]
    la_ref[...] = _log_sigmoid(z) * (1.0 / GLA_GATE_NORM)
    gi = small + bif_ref[...]
    lane = lax.broadcasted_iota(jnp.int32, gi.shape, 1)
    gif_ref[...] = jnp.where((lane >= L_MF) & (lane < L_MF + HEADS), _log_sigmoid(gi), gi)

    for c in range(GATE_W // TILE):
        cols = slice(c * TILE, (c + 1) * TILE)
        gates_ref[:, cols] = _sigmoid(jnp.dot(xn, w_ref[:, cols], preferred_element_type=f32)).astype(bf16)

    cos, sup, sdn = cos_ref[...], sup_ref[...], sdn_ref[...]
    for t in range(N_TILES):
        acc = jnp.dot(xn, w_ref[:, GATE_W + t * TILE:GATE_W + (t + 1) * TILE], preferred_element_type=f32)
        if t in (T_DQ, T_DK):
            acc = _rope_tile(acc, cos, sup, sdn)
        proj_ref[:, t * TILE:(t + 1) * TILE] = acc
        if t == T_DK:
            k_t = acc.T
        if t == T_DV:
            v_rows = acc

    @pl.when(i < n_prompt_blocks)
    def _():
        kt_ref[...] = k_t
        for h in range(HEADS):
            vp_ref[pl.ds(h, v_rows.shape[0], stride=HEADS), :] = v_rows[:, h * DV:(h + 1) * DV]

    @pl.when(i >= n_prompt_blocks)
    def _():
        for s in range(k_t.shape[1] // dec_batch):
            kts_ref[s] = k_t[:, s * dec_batch:(s + 1) * dec_batch]


def _inproj(h_p, h_s, sample_block, g_mix, w_main, w_small, w_gk, b_gk, b_if, cos_t, sup_t, sdn_t,
            batch, seq, dec_batch, dec_seq, layer, depth, carried):
    tm = TM_IN
    seq_blocks = seq // tm
    npb = batch * seq_blocks
    steps_per_block = tm // dec_batch
    nsb = dec_seq // steps_per_block
    nt = npb + nsb
    T = nt * tm
    const = lambda i: (0, 0)
    resident = lambda shape: pl.BlockSpec(shape, const, pipeline_mode=pl.Buffered(1))

    def tab_map(i):
        return (jnp.where(i < npb, i % seq_blocks, seq_blocks + i - npb), 0)

    def kt_map(i):
        j = jnp.minimum(i, npb - 1)
        return (layer, j // seq_blocks, 0, j % seq_blocks)

    return pl.pallas_call(
        functools.partial(_inproj_kernel, n_prompt_blocks=npb, dec_batch=dec_batch),
        grid=(nt,),
        input_output_aliases={11: 2, 12: 3},
        in_specs=[
            *_row_specs(D_MODEL, npb, sample_block, tm, nsb),
            pl.BlockSpec((1, D_MODEL), const),
            resident((D_MODEL, P_MAIN)),
            resident((D_MODEL, LANES)),
            resident((LANES, QW)),
            pl.BlockSpec((1, QW), const),
            pl.BlockSpec((1, LANES), const),
            pl.BlockSpec((tm, LANES), tab_map),
            pl.BlockSpec((tm, LANES), tab_map),
            pl.BlockSpec((tm, LANES), tab_map),
            pl.BlockSpec(memory_space=pl.ANY),
            pl.BlockSpec(memory_space=pl.ANY),
        ],
        out_specs=[
            pl.BlockSpec((tm, GATE_W), lambda i: (i, 0)),
            pl.BlockSpec((tm, P_MIX), lambda i: (i, 0)),
            pl.BlockSpec((None, None, TILE, tm), kt_map),
            pl.BlockSpec((tm * HEADS, DV), lambda i: (layer * npb + jnp.minimum(i, npb - 1), 0)),
            pl.BlockSpec((steps_per_block, TILE, dec_batch), lambda i: (jnp.maximum(i - npb, 0), 0, 0)),
            pl.BlockSpec((tm, QW), lambda i: (i, 0)),
            pl.BlockSpec((tm, LANES), lambda i: (i, 0)),
        ],
        out_shape=[
            jax.ShapeDtypeStruct((T, GATE_W), bf16),
            jax.ShapeDtypeStruct((T, P_MIX), f32),
            jax.ShapeDtypeStruct((depth, batch, TILE, seq), f32),
            jax.ShapeDtypeStruct((depth * batch * seq * HEADS, DV), f32),
            jax.ShapeDtypeStruct((dec_seq, TILE, dec_batch), f32),
            jax.ShapeDtypeStruct((T, QW), f32),
            jax.ShapeDtypeStruct((T, LANES), f32),
        ],
        compiler_params=_cparams("arbitrary"),
        name="inproj",
    )(h_p, h_s, g_mix, w_main, w_small, w_gk, b_gk, b_if, cos_t, sup_t, sdn_t, *carried)


def _tril(n):
    r = lax.broadcasted_iota(jnp.int32, (n, n), 0)
    c = lax.broadcasted_iota(jnp.int32, (n, n), 1)
    return r >= c


def _gla_prompt_kernel(*refs, n_chunks, batch):
    q_refs, k_refs, v_refs, la_refs = (refs[i * batch:(i + 1) * batch] for i in range(4))
    o_ref, s_out_ref, s_scr = refs[4 * batch:]
    blk = pl.program_id(0)

    @pl.when(blk == 0)
    def _():
        s_scr[...] = jnp.zeros_like(s_scr)

    ltri = _tril(CHUNK).astype(f32)
    r2 = lax.broadcasted_iota(jnp.int32, (2 * CHUNK, CHUNK), 0)
    c2 = lax.broadcasted_iota(jnp.int32, (2 * CHUNK, CHUNK), 1)
    tril2 = jnp.where(r2 >= CHUNK, r2 - CHUNK, r2) >= c2
    lane = lax.broadcasted_iota(jnp.int32, (CHUNK, LANES), 1)
    lo = (lane < DK).astype(f32)
    hi = 1.0 - lo

    def chunk(c, carry):
        rows = pl.ds(pl.multiple_of(c * CHUNK, CHUNK), CHUNK)
        units = [(s, p) for s in range(batch) for p in range(HEADS // 2)]
        bs_ = [jnp.dot(ltri, la_refs[s][rows, :], preferred_element_type=f32, precision=HIGHEST)
               for s in range(batch)]
        qs, kgs, kd0, kd1, dcol, v0, v1, s_old = {}, {}, {}, {}, {}, {}, {}, {}
        for s in range(batch):
            b = bs_[s]
            b_last = b[CHUNK - 1:CHUNK, :]
            q = q_refs[s][rows, :]
            k = k_refs[s][rows, :]
            qg = q * jnp.exp(b) * (DK ** -0.5)
            kg = k * jnp.exp(-b)
            kd = k * jnp.exp(b_last - b)
            dec = jnp.exp(b_last)
            for p in range(HEADS // 2):
                u = (s, p)
                ls = slice(p * LANES, (p + 1) * LANES)
                qs[u] = jnp.concatenate([qg[:, ls] * lo, qg[:, ls] * hi], axis=0)
                kgs[u] = kg[:, ls]
                kd0[u], kd1[u] = kd[:, ls] * lo, kd[:, ls] * hi
                dcol[u] = jnp.broadcast_to(dec[:, ls], (LANES, LANES)).T
                v0[u] = v_refs[s][rows, pl.ds((2 * p) * DV, DV)]
                v1[u] = v_refs[s][rows, pl.ds((2 * p + 1) * DV, DV)]
                s_old[u] = s_scr[s, p]
        a = {u: _dot_nt(qs[u], kgs[u]) for u in units}
        inter = {u: _dot(qs[u], s_old[u]) for u in units}
        kv = {u: _dot_tn(kd0[u], v0[u]) + _dot_tn(kd1[u], v1[u]) for u in units}
        a = {u: jnp.where(tril2, a[u], 0.0) for u in units}
        intra0 = {u: _dot(a[u][:CHUNK], v0[u]) for u in units}
        intra1 = {u: _dot(a[u][CHUNK:], v1[u]) for u in units}
        for u in units:
            s, p = u
            o_ref[s, rows, pl.ds((2 * p) * DV, DV)] = inter[u][:CHUNK] + intra0[u]
            o_ref[s, rows, pl.ds((2 * p + 1) * DV, DV)] = inter[u][CHUNK:] + intra1[u]
            s_scr[s, p] = dcol[u] * s_old[u] + kv[u]
        return carry

    lax.fori_loop(0, n_chunks, chunk, 0)

    @pl.when(blk == pl.num_programs(0) - 1)
    def _():
        s_out_ref[...] = s_scr[...]


def _stream_specs(width, batch, nb, col):
    return [pl.BlockSpec((TM, width), functools.partial(lambda s, i: (s * nb + i, col), s)) for s in range(batch)]


def _gla_prompt(proj, la, batch, seq):
    nb = seq // TM
    qw_blocks = TILE // QW
    return pl.pallas_call(
        functools.partial(_gla_prompt_kernel, n_chunks=TM // CHUNK, batch=batch),
        grid=(nb,),
        in_specs=[
            *_stream_specs(QW, batch, nb, T_GQK * qw_blocks),
            *_stream_specs(QW, batch, nb, T_GQK * qw_blocks + 1),
            *_stream_specs(VW, batch, nb, T_GV),
            *_stream_specs(QW, batch, nb, 0),
        ],
        out_specs=[
            pl.BlockSpec((batch, TM, VW), lambda i: (0, i, 0)),
            pl.BlockSpec((batch, HEADS // 2, LANES, LANES), lambda i: (0, 0, 0, 0)),
        ],
        out_shape=[
            jax.ShapeDtypeStruct((batch, seq, VW), f32),
            jax.ShapeDtypeStruct((batch, HEADS // 2, LANES, LANES), f32),
        ],
        scratch_shapes=[pltpu.VMEM((batch, HEADS // 2, LANES, LANES), f32)],
        compiler_params=_cparams("arbitrary"),
        name="gla_prompt",
    )(*([proj] * (3 * batch)), *([la] * batch))


def _mlstm_prompt_kernel(*refs, n_chunks, batch):
    q_refs, k_refs, v_refs, gif_refs = (refs[i * batch:(i + 1) * batch] for i in range(4))
    h_ref, c_out_ref, n_out_ref, m_out_ref, ct_scr, n_scr, m_scr = refs[4 * batch:]
    blk = pl.program_id(0)

    @pl.when(blk == 0)
    def _():
        ct_scr[...] = jnp.zeros_like(ct_scr)
        n_scr[...] = jnp.zeros_like(n_scr)
        m_scr[...] = jnp.zeros_like(m_scr)

    CH = CHUNK_MLSTM
    ltri = _tril(CH).astype(f32)
    r_i = lax.broadcasted_iota(jnp.int32, (CH, CH), 0)
    c_i = lax.broadcasted_iota(jnp.int32, (CH, CH), 1)
    upper = r_i <= c_i

    def chunk(c, carry):
        rows = pl.ds(pl.multiple_of(c * CH, CH), CH)
        for sq0 in range(0, batch, MLSTM_GROUP):
            seqs = range(sq0, min(sq0 + MLSTM_GROUP, batch))
            units = [(sq, h) for sq in seqs for h in range(HEADS)]
            gate = {}
            for sq in seqs:
                g = gif_refs[sq][rows, :]
                fc = jnp.dot(ltri, g, preferred_element_type=f32, precision=HIGHEST)
                gate[sq] = (g, fc, g.T, fc.T)
            q, k, v_t, c_t, n_all = {}, {}, {}, {}, {}
            for sq in seqs:
                q_all = q_refs[sq][rows, :]
                k_all = k_refs[sq][rows, :] * (DK ** -0.5)
                n_all[sq] = n_scr[sq]
                for h in range(HEADS):
                    u = (sq, h)
                    q[u] = q_all[:, h * DK:(h + 1) * DK]
                    k[u] = k_all[:, h * DK:(h + 1) * DK]
                    v_t[u] = v_refs[sq][rows, pl.ds(h * DV, DV)].T
                    c_t[u] = ct_scr[sq, h]
            kq = {u: _dot_nt(k[u], q[u]) for u in units}
            cq = {u: _dot_nt(c_t[u], q[u]) for u in units}
            qn = {u: _dot_nt(n_all[u[0]], q[u])[u[1]:u[1] + 1, :] for u in units}
            sc, wk, w_inter, floor, decay, m_new = {}, {}, {}, {}, {}, {}
            for u in units:
                sq, h = u
                g, fc, g_t, fc_t = gate[sq]
                i_col = g[:, L_MI + h:L_MI + h + 1]
                f_col = fc[:, L_MF + h:L_MF + h + 1]
                f_row = fc_t[L_MF + h:L_MF + h + 1, :]
                f_last = fc[CH - 1:CH, L_MF + h:L_MF + h + 1]
                m_prev = m_scr[sq, h:h + 1, 0:1]
                log_d = jnp.where(upper, f_row + (i_col - f_col), -jnp.inf)
                m_inter = m_prev + f_row
                m_row = jnp.maximum(m_inter, jnp.max(log_d, axis=0, keepdims=True))
                w_inter[u] = jnp.exp(m_inter - m_row)
                floor[u] = jnp.exp(-m_row)
                sc[u] = kq[u] * jnp.exp(log_d - m_row)
                log_w = f_last - f_col + i_col
                m_new[u] = jnp.maximum(m_prev + f_last, jnp.max(log_w, axis=0, keepdims=True))
                wk[u] = jnp.exp(log_w - m_new[u]) * k[u]
                decay[u] = jnp.exp(m_prev + f_last - m_new[u])
            vs = {u: _dot(v_t[u], sc[u]) for u in units}
            vk = {u: _dot(v_t[u], wk[u]) for u in units}
            for u in units:
                sq, h = u
                num = w_inter[u] * cq[u] + vs[u]
                den = w_inter[u] * qn[u] + jnp.sum(sc[u], axis=0, keepdims=True)
                h_ref[sq, rows, pl.ds(h * DV, DV)] = (num / jnp.maximum(jnp.abs(den), floor[u])).T
                ct_scr[sq, h] = decay[u] * c_t[u] + vk[u]
                n_scr[sq, h:h + 1, :] = decay[u] * n_all[sq][h:h + 1, :] + jnp.sum(wk[u], axis=0, keepdims=True)
                m_scr[sq, h:h + 1, :] = jnp.broadcast_to(m_new[u], (1, LANES))
        return carry

    lax.fori_loop(0, n_chunks, chunk, 0)

    @pl.when(blk == pl.num_programs(0) - 1)
    def _():
        for sq in range(batch):
            for h in range(HEADS):
                c_out_ref[sq, h] = ct_scr[sq, h].T
        n_out_ref[...] = n_scr[:, 0:HEADS, :]
        m_out_ref[...] = m_scr[:, 0:HEADS, :]


def _mlstm_prompt(proj, gif, batch, seq):
    nb = seq // TM
    qw_blocks = TILE // QW
    return pl.pallas_call(
        functools.partial(_mlstm_prompt_kernel, n_chunks=TM // CHUNK_MLSTM, batch=batch),
        grid=(nb,),
        in_specs=[
            *_stream_specs(QW, batch, nb, T_MQK * qw_blocks),
            *_stream_specs(QW, batch, nb, T_MQK * qw_blocks + 1),
            *_stream_specs(VW, batch, nb, T_MV),
            *_stream_specs(LANES, batch, nb, 0),
        ],
        out_specs=[
            pl.BlockSpec((batch, TM, VW), lambda i: (0, i, 0)),
            pl.BlockSpec((batch, HEADS, DK, DV), lambda i: (0, 0, 0, 0)),
            pl.BlockSpec((batch, HEADS, DK), lambda i: (0, 0, 0)),
            pl.BlockSpec((batch, HEADS, LANES), lambda i: (0, 0, 0)),
        ],
        out_shape=[
            jax.ShapeDtypeStruct((batch, seq, VW), f32),
            jax.ShapeDtypeStruct((batch, HEADS, DK, DV), f32),
            jax.ShapeDtypeStruct((batch, HEADS, DK), f32),
            jax.ShapeDtypeStruct((batch, HEADS, LANES), f32),
        ],
        scratch_shapes=[
            pltpu.VMEM((batch, HEADS, DV, DK), f32),
            pltpu.VMEM((batch, SUBLANES, DK), f32),
            pltpu.VMEM((batch, SUBLANES, LANES), f32),
        ],
        compiler_params=_cparams("arbitrary"),
        name="mlstm_prompt",
    )(*([proj] * (3 * batch)), *([gif] * batch))


def _diff_lambda_value(lam_ref, lam_init):
    lam = lam_ref[...]
    s1 = jnp.sum(lam[0:1] * lam[1:2], axis=1, keepdims=True)
    s2 = jnp.sum(lam[2:3] * lam[3:4], axis=1, keepdims=True)
    return jnp.exp(s1) - jnp.exp(s2) + lam_init


def _attn_prompt_step(p, qi_ref, ki_ref, q_ref, k_ref, v_ref, lam_ref, o_ref, q2_scr, m_scr, l_scr, acc_scr,
                      *, tq, lam_init, other_stages):
    qi = qi_ref[p]
    ki = ki_ref[p]
    pair = lambda j: slice(j * LANES, (j + 1) * LANES)

    @pl.when(ki == 0)
    def _():
        for j in range(HEADS):
            q = q_ref[:, pair(j)] * (DK ** -0.5 * LOG2E)
            lane = lax.broadcasted_iota(jnp.int32, q.shape, 1)
            q2_scr[j, 0:tq, :] = jnp.where(lane < DK, q, 0.0).astype(bf16)
            q2_scr[j, tq:2 * tq, :] = jnp.where(lane >= DK, q, 0.0).astype(bf16)
        m_scr[...] = jnp.full_like(m_scr, -jnp.inf)
        l_scr[...] = jnp.zeros_like(l_scr)
        acc_scr[...] = jnp.zeros_like(acc_scr)

    def step(masked, other):
        sts = [_dot_nt(k_ref[:, pair(j)], q2_scr[j]) for j in range(HEADS)]
        other[0]()
        pts, alphas = [], []
        for j in range(HEADS):
            st = sts[j]
            if masked:
                r = lax.broadcasted_iota(jnp.int32, st.shape, 0)
                c = lax.broadcasted_iota(jnp.int32, st.shape, 1)
                c = jnp.where(c >= tq, c - tq, c)
                st = jnp.where(r <= c, st, -jnp.inf)
            m_prev = m_scr[j]
            m_new = jnp.maximum(m_prev, jnp.max(st, axis=0, keepdims=True))
            alpha = jnp.exp2(m_prev - m_new)
            pt = jnp.exp2(st - m_new)
            l_scr[j] = alpha * l_scr[j] + jnp.sum(pt, axis=0, keepdims=True)
            m_scr[j] = m_new
            pts.append(pt.astype(bf16))
            alphas.append(alpha)
        for j in range(HEADS):
            acc_scr[j] = alphas[j] * acc_scr[j] + _dot_tn(v_ref[:, pair(j)], pts[j])
        for stage in other[1:]:
            stage()

    @pl.when(ki < qi)
    def _():
        step(False, other_stages())

    @pl.when(ki == qi)
    def _():
        step(True, other_stages())
        lam = _diff_lambda_value(lam_ref, lam_init)
        for j in range(HEADS):
            ot = acc_scr[j] / l_scr[j]
            o_ref[:, pair(j)] = (ot[:, 0:tq] - lam * ot[:, tq:2 * tq]).T


def _sample_page_copies(seq_idx, slot, pt_ref, ck_hbm, cv_hbm, kbuf, vbuf, sem, *, layer, n_pages, page):
    out = []
    for pg in range(n_pages):
        src = pt_ref[seq_idx, pg]
        out.append(pltpu.make_async_copy(ck_hbm.at[layer, src], kbuf.at[slot, :, pl.ds(pg * page, page)],
                                         sem.at[0, slot]))
        out.append(pltpu.make_async_copy(cv_hbm.at[layer, src],
                                         vbuf.at[slot, pl.ds(pg * page * HEADS, page * HEADS), :], sem.at[1, slot]))
    return out


def _attn_sample_pages(step, n_steps, pt_ref, ck_hbm, cv_hbm, o_ref, kbuf, vbuf, sem, *, nseq, pages):
    past = pages["n_pages"] * pages["page"]
    seq = lambda st: jnp.minimum(st, nseq - 1)
    slot = step % 2
    copies = functools.partial(_sample_page_copies, pt_ref=pt_ref, ck_hbm=ck_hbm, cv_hbm=cv_hbm, kbuf=kbuf,
                               vbuf=vbuf, sem=sem, **pages)

    @pl.when(step == 0)
    def _():
        kbuf[:, :, pl.ds(past, LANES)] = jnp.zeros((2, kbuf.shape[1], LANES), f32)
        vbuf[:, pl.ds(past * HEADS, LANES * HEADS), :] = jnp.zeros((2, LANES * HEADS, DV), f32)
        o_ref[...] = jnp.zeros_like(o_ref)
        for cp in copies(0, 0):
            cp.start()

    @pl.when(step + 1 < n_steps)
    def _():
        for cp in copies(seq(step + 1), 1 - slot):
            cp.start()

    for cp in copies(seq(step), slot):
        cp.wait()


def _attn_sample_stages(step, q_ref, kts_ref, v_ref, lam_ref, o_ref, kbuf, vbuf,
                        *, n_pages, page, dec_seq, nseq, lam_init):
    st = {}
    b = jnp.minimum(step, nseq - 1)
    slot = step % 2
    past = n_pages * page
    n_qh = 2 * HEADS

    shift = lax.rem(nseq - b, nseq)
    b8 = (b // SUBLANES) * SUBLANES
    groups = [pl.ds(pl.multiple_of(t * nseq + b8, SUBLANES), SUBLANES) for t in range(dec_seq)]
    mine = lax.broadcasted_iota(jnp.int32, (SUBLANES, 2 * QW), 0) == b - b8
    pick = lambda ref, t: jnp.sum(jnp.where(mine, ref[groups[t], :], 0.0), axis=0, keepdims=True)
    def scores():
        head_of_lane = lax.broadcasted_iota(jnp.int32, (n_qh, 2 * QW), 1) // DK
        head_of_row = lax.broadcasted_iota(jnp.int32, (n_qh, 2 * QW), 0)
        q_rows = []
        for t in range(dec_seq):
            kbuf[slot, :, pl.ds(past + t, 1)] = pltpu.roll(kts_ref[t], shift, axis=1)[:, 0:1]
            v_row = pick(v_ref, t)
            for h in range(HEADS):
                vbuf[slot, pl.ds((past + t) * HEADS + h, 1), :] = v_row[:, h * DV:(h + 1) * DV]
            q = jnp.broadcast_to(pick(q_ref, t) * (DK ** -0.5), (n_qh, 2 * QW))
            q_rows.append(jnp.where(head_of_lane == head_of_row, q, 0.0))
        qbd = jnp.concatenate(q_rows, axis=0)
        st["s"] = _dot(qbd, kbuf[slot])

    def softmax():
        s = st["s"]
        r = lax.broadcasted_iota(jnp.int32, s.shape, 0)
        c = lax.broadcasted_iota(jnp.int32, s.shape, 1)
        s = jnp.where(c - past <= r // n_qh, s, -jnp.inf)
        pr = jnp.exp(s - jnp.max(s, axis=1, keepdims=True))
        st["inv_l"] = 1.0 / jnp.sum(pr, axis=1, keepdims=True)
        st["pr"] = pr.astype(bf16)

    def values():
        v_hs = [vbuf[slot, pl.ds(hv, past + LANES, stride=HEADS), :] for hv in range(HEADS)]
        st["o"] = [_dot(st["pr"], v_h) for v_h in v_hs]

    def write():
        lam = _diff_lambda_value(lam_ref, lam_init)
        mine_v = mine[:, 0:DV]
        for hv in range(HEADS):
            o_h = st["o"][hv] * st["inv_l"]
            for t in range(dec_seq):
                r1 = t * n_qh + 2 * hv
                d = o_h[r1:r1 + 1] - lam * o_h[r1 + 1:r1 + 2]
                lanes = slice(hv * DV, (hv + 1) * DV)
                o_ref[groups[t], lanes] = jnp.where(mine_v, d, o_ref[groups[t], lanes])

    return scores, softmax, values, write


def _attn_kernel(qi_ref, ki_ref, pt_ref, q_ref, k_ref, v_ref, lam_ref, qs_ref, kts_ref, vs_ref, ck_hbm, cv_hbm,
                 o_ref, os_ref, q2_scr, m_scr, l_scr, acc_scr, kbuf, vbuf, sem, *, prompt, sample, nseq, n_pairs):
    p = pl.program_id(1)
    step = pl.program_id(0) * pl.num_programs(1) + p
    n_steps = pl.num_programs(0) * pl.num_programs(1)
    pages = dict(layer=sample["layer"], n_pages=sample["n_pages"], page=sample["page"])
    _attn_sample_pages(step, n_steps, pt_ref, ck_hbm, cv_hbm, os_ref, kbuf, vbuf, sem, nseq=nseq, pages=pages)

    def sample_stages():
        return _attn_sample_stages(step, qs_ref, kts_ref, vs_ref, lam_ref, os_ref, kbuf, vbuf, nseq=nseq,
                                   n_pages=sample["n_pages"], page=sample["page"], dec_seq=sample["dec_seq"],
                                   lam_init=sample["lam_init"])

    @pl.when(p < n_pairs)
    def _():
        _attn_prompt_step(p, qi_ref, ki_ref, q_ref, k_ref, v_ref, lam_ref, o_ref,
                          q2_scr, m_scr, l_scr, acc_scr, other_stages=sample_stages, **prompt)

    @pl.when(p >= n_pairs)
    def _():
        for stage in sample_stages():
            stage()


def _attention(proj, k_ts, page_table, diff_lambda, cache_kt, cache_v2, layer, lam_init, batch, seq, sample_block,
               tq=512):
    nq = seq // tq
    pairs = [(q, k) for q in range(nq) for k in range(q + 1)]
    qi_tab = jnp.asarray(np.array([a for a, _ in pairs], np.int32))
    ki_tab = jnp.asarray(np.array([b for _, b in pairs], np.int32))
    nseq, n_pages = page_table.shape
    n_pairs = len(pairs)
    steps = max(n_pairs, -(-nseq // batch))
    last = n_pairs - 1
    dec_seq = k_ts.shape[0]
    rows = dec_seq * nseq
    page = cache_kt.shape[3]
    cols = n_pages * page + LANES
    grid_spec = pltpu.PrefetchScalarGridSpec(
        num_scalar_prefetch=3,
        grid=(batch, steps),
        in_specs=[
            pl.BlockSpec((tq, TILE), lambda b, p, qt, kt, pt: (b * nq + qt[jnp.minimum(p, last)], T_DQ)),
            pl.BlockSpec((tq, TILE), lambda b, p, qt, kt, pt: (b * nq + kt[jnp.minimum(p, last)], T_DK)),
            pl.BlockSpec((tq, TILE), lambda b, p, qt, kt, pt: (b * nq + kt[jnp.minimum(p, last)], T_DV)),
            pl.BlockSpec((4, DK), lambda b, p, qt, kt, pt: (0, 0)),
            pl.BlockSpec((rows, TILE), lambda b, p, qt, kt, pt: (sample_block, T_DQ)),
            pl.BlockSpec((dec_seq, TILE, nseq), lambda b, p, qt, kt, pt: (0, 0, 0)),
            pl.BlockSpec((rows, TILE), lambda b, p, qt, kt, pt: (sample_block, T_DV)),
            pl.BlockSpec(memory_space=pl.ANY),
            pl.BlockSpec(memory_space=pl.ANY),
        ],
        out_specs=[
            pl.BlockSpec((tq, VW), lambda b, p, qt, kt, pt: (b * nq + qt[jnp.minimum(p, last)], 0)),
            pl.BlockSpec((rows, VW), lambda b, p, qt, kt, pt: (0, 0)),
        ],
        scratch_shapes=[
            pltpu.VMEM((HEADS, 2 * tq, LANES), bf16),
            pltpu.VMEM((HEADS, 1, 2 * tq), f32),
            pltpu.VMEM((HEADS, 1, 2 * tq), f32),
            pltpu.VMEM((HEADS, DV, 2 * tq), f32),
            pltpu.VMEM((2, 2 * QW, cols), f32),
            pltpu.VMEM((2, cols * HEADS, DV), f32),
            pltpu.SemaphoreType.DMA((2, 2)),
        ],
    )
    return pl.pallas_call(
        functools.partial(_attn_kernel, nseq=nseq, n_pairs=n_pairs, prompt=dict(tq=tq, lam_init=lam_init),
                          sample=dict(layer=layer, n_pages=n_pages, page=page, dec_seq=dec_seq, lam_init=lam_init)),
        grid_spec=grid_spec,
        out_shape=[jax.ShapeDtypeStruct((batch * seq, VW), f32), jax.ShapeDtypeStruct((rows, VW), f32)],
        compiler_params=_cparams("arbitrary", "arbitrary"),
        name="attention",
    )(qi_tab, ki_tab, page_table, proj, proj, proj, diff_lambda, proj, k_ts, proj, cache_kt, cache_v2)


def _head_of_lane(shape):
    return lax.broadcasted_iota(jnp.int32, shape, 1) // DK


def _expand_heads(cols, base, head):
    out = cols[:, base + HEADS - 1:base + HEADS]
    for h in range(HEADS - 2, -1, -1):
        out = jnp.where(head == h, cols[:, base + h:base + h + 1], out)
    return out


def _sample_prep_kernel(gq_ref, gk_ref, la_ref, mq_ref, mk_ref, gif_ref, n0_ref, m0_ref,
                        ga_ref, gkk_ref, gqq_ref, ma_ref, mkk_ref, mqq_ref, n_out_ref, m_out_ref, *, nseq, dec_seq):
    head = _head_of_lane((nseq, QW))
    m = _expand_heads(m0_ref[...], 0, head)
    n = n0_ref[...]
    for t in range(dec_seq):
        rows = pl.ds(t * nseq, nseq)
        ga_ref[t] = jnp.exp(la_ref[rows, :]).T
        gkk_ref[t] = gk_ref[rows, :].T
        gqq_ref[t] = (gq_ref[rows, :] * (DK ** -0.5)).T
        gates = gif_ref[rows, :]
        i_e = _expand_heads(gates, L_MI, head)
        f_e = _expand_heads(gates, L_MF, head)
        k = mk_ref[rows, :] * (DK ** -0.5)
        q = mq_ref[rows, :]
        m_new = jnp.maximum(f_e + m, i_e)
        fp = jnp.exp(f_e + m - m_new)
        ip = jnp.exp(i_e - m_new)
        n = fp * n + ip * k
        nq = n * q
        den = jnp.zeros_like(nq)
        for h in range(HEADS):
            den = jnp.where(head == h, jnp.sum(jnp.where(head == h, nq, 0.0), axis=1, keepdims=True), den)
        inv = 1.0 / jnp.maximum(jnp.abs(den), jnp.exp(-m_new))
        ma_ref[t] = fp.T
        mkk_ref[t] = (ip * k).T
        mqq_ref[t] = (q * inv).T
        m = m_new
    n_out_ref[...] = n
    m_out_ref[...] = m


def _sample_prep(proj, la, gif, state_n, state_m, layer, sample_block, nseq, dec_seq):
    rows = nseq * dec_seq
    qw_blocks = TILE // QW
    tok = jax.ShapeDtypeStruct((dec_seq, QW, nseq), f32)
    st = jax.ShapeDtypeStruct((nseq, QW), f32)
    tok_spec = pl.BlockSpec((dec_seq, QW, nseq), lambda i: (0, 0, 0))
    st_spec = pl.BlockSpec((nseq, QW), lambda i: (0, 0))
    col = lambda c: pl.BlockSpec((rows, QW), lambda i: (sample_block, c))
    return pl.pallas_call(
        functools.partial(_sample_prep_kernel, nseq=nseq, dec_seq=dec_seq),
        grid=(1,),
        in_specs=[
            col(T_GQK * qw_blocks), col(T_GQK * qw_blocks + 1), col(0),
            col(T_MQK * qw_blocks), col(T_MQK * qw_blocks + 1),
            pl.BlockSpec((rows, LANES), lambda i: (sample_block, 0)),
            pl.BlockSpec((None, nseq, QW), lambda i: (layer, 0, 0)),
            pl.BlockSpec((None, nseq, HEADS), lambda i: (layer, 0, 0)),
        ],
        out_specs=[tok_spec] * 6 + [st_spec] * 2,
        out_shape=[tok] * 6 + [st] * 2,
        compiler_params=_cparams("arbitrary"),
        name="sample_prep",
    )(proj, proj, la, proj, proj, gif, state_n, state_m)


DK_PER_STEP = SUBLANES


def _sample_state_kernel(a_ref, k_ref, q_ref, v_ref, s_ref, s_all_ref, o_ref, s_out_ref, vt_scr, o_scr,
                         *, dec_seq, nseq):
    del s_all_ref
    h = pl.program_id(0)
    g = pl.program_id(1)

    @pl.when(g == 0)
    def _():
        for t in range(dec_seq):
            vt_scr[t] = v_ref[t * nseq:(t + 1) * nseq, :].T
        o_scr[...] = jnp.zeros_like(o_scr)

    rows = pl.ds(pl.multiple_of(h * DK + g * DK_PER_STEP, DK_PER_STEP), DK_PER_STEP)
    a = [a_ref[t, rows, :] for t in range(dec_seq)]
    k = [k_ref[t, rows, :] for t in range(dec_seq)]
    q = [q_ref[t, rows, :] for t in range(dec_seq)]
    v_t = [vt_scr[t] for t in range(dec_seq)]
    acc = [o_scr[t] for t in range(dec_seq)]
    for dd in range(DK_PER_STEP):
        s = s_ref[:, dd, :].T
        for t in range(dec_seq):
            s = a[t][dd:dd + 1, :] * s + k[t][dd:dd + 1, :] * v_t[t]
            acc[t] = acc[t] + q[t][dd:dd + 1, :] * s
        s_out_ref[:, dd, :] = s.T
    for t in range(dec_seq):
        o_scr[t] = acc[t]

    @pl.when(g == pl.num_programs(1) - 1)
    def _():
        for t in range(dec_seq):
            o_ref[t * nseq:(t + 1) * nseq, :] = acc[t].T


def _sample_state(a_t, k_t, q_t, proj, v_tile, sample_block, state, layer, dec_seq, carried):
    nseq = a_t.shape[2]
    rows = dec_seq * nseq
    full = pl.BlockSpec((dec_seq, QW, nseq), lambda h, g: (0, 0, 0))
    state_spec = pl.BlockSpec((None, nseq, None, DK_PER_STEP, DV), lambda h, g: (layer, 0, h, g, 0))
    dv_blocks = TILE // DV
    return pl.pallas_call(
        functools.partial(_sample_state_kernel, dec_seq=dec_seq, nseq=nseq),
        grid=(HEADS, DK // DK_PER_STEP),
        input_output_aliases={5: 1},
        in_specs=[
            full, full, full,
            pl.BlockSpec((rows, DV), lambda h, g: (sample_block, v_tile * dv_blocks + h)),
            state_spec,
            pl.BlockSpec(memory_space=pl.ANY),
        ],
        out_specs=[pl.BlockSpec((rows, DV), lambda h, g: (0, h)), state_spec],
        out_shape=[jax.ShapeDtypeStruct((rows, VW), f32), jax.ShapeDtypeStruct(state.shape, f32)],
        scratch_shapes=[pltpu.VMEM((dec_seq, DV, nseq), f32), pltpu.VMEM((dec_seq, DV, nseq), f32)],
        compiler_params=_cparams("arbitrary", "arbitrary"),
        name="sample_state",
    )(a_t, k_t, q_t, proj, state, carried)


def _head_rms(x, g):
    parts = []
    for h in range(HEADS):
        parts.append(_rms(x[:, h * DV:(h + 1) * DV], g))
    return jnp.concatenate(parts, axis=1)


FF_SLABS = ((0, 768), (768, 768), (1536, 768), (2304, 512))
TM_CH = 256


def _merge_ffn_kernel(hp_ref, hs_ref, oap_ref, oas_ref, omp_ref, oms_ref, odp_ref, ods_ref, gr_ref, mo_ref,
                      g0_ref, g1_ref, g2_ref, gg_ref, gm_ref, gd_ref, wa_ref, wm_ref, wd_ref, wo_ref,
                      gf_ref, wg_ref, wu_ref, wdn_ref, pp_ref, ps_ref, gp_ref, wpg_ref, wpp_ref, gfin_ref,
                      *out_refs, lam_init, final, n_prompt_blocks):
    pick = functools.partial(_pick_rows, n_prompt_blocks)
    gr = gr_ref[...]
    a = _head_rms(pick(oap_ref, oas_ref), gg_ref[...]) * (gr * _sigmoid(gr))
    m = _sigmoid(mo_ref[...]) * _head_rms(pick(omp_ref, oms_ref), gm_ref[...])
    d = _head_rms(pick(odp_ref, ods_ref), gd_ref[...]) * (1.0 - lam_init)
    merged = (g0_ref[...] * _dot(a, wa_ref[...]) + g1_ref[...] * _dot(m, wm_ref[...])
              + g2_ref[...] * _dot(d, wd_ref[...]))
    h = pick(hp_ref, hs_ref) + _dot(merged, wo_ref[...])

    xf = _rms(h, gf_ref[...]).astype(bf16)
    h2 = h
    for lo, width in FF_SLABS:
        gate = jnp.dot(xf, wg_ref[:, lo:lo + width], preferred_element_type=f32)
        up = jnp.dot(xf, wu_ref[:, lo:lo + width], preferred_element_type=f32)
        h2 = h2 + _dot(gate * _sigmoid(gate) * up, wdn_ref[lo:lo + width, :])
    ple_gate = _sigmoid(_dot(_rms(h2, gp_ref[...]), wpg_ref[...]))
    h3 = h2 + ple_gate * _dot(pick(pp_ref, ps_ref), wpp_ref[...])
    if not final:
        out_refs[0][...] = h3
        return
    y = _rms(h3, gfin_ref[...])
    yp_ref, ys_ref = out_refs
    i = pl.program_id(0)

    @pl.when(i < n_prompt_blocks)
    def _():
        yp_ref[...] = y

    @pl.when(i >= n_prompt_blocks)
    def _():
        ys_ref[...] = y


def _merge_ffn(h_p, h_s, h_s_row0, o_a, o_m, o_d, proj, gates, g_gla, g_ml, g_diff, w_a, w_m, w_d, w_out, lam_init,
               g_ffn, w_g, w_u, w_dn, p_p, p_s, p_row0, g_ple, w_pg, w_pp, g_final, final, n_prompt_rows, n_sample_rows):
    tm = TM_CH
    npb, nsb = n_prompt_rows // tm, n_sample_rows // tm
    T = n_prompt_rows + n_sample_rows
    row = lambda i: (i, 0)
    const = lambda i: (0, 0)
    resident = lambda shape: pl.BlockSpec(shape, const, pipeline_mode=pl.Buffered(1))
    rows2 = lambda width, sample_block=0, prompt_block0=0: _row_specs(width, npb, sample_block, tm, nsb, prompt_block0)
    assert FF_SLABS[-1][0] + FF_SLABS[-1][1] == D_FF
    if final:
        out_specs = [pl.BlockSpec((tm, D_MODEL), lambda i: (jnp.minimum(i, npb - 1), 0)),
                     pl.BlockSpec((tm, D_MODEL), lambda i: (jnp.clip(i - npb, 0, nsb - 1), 0))]
        out_shape = [jax.ShapeDtypeStruct((n_prompt_rows, D_MODEL), f32),
                     jax.ShapeDtypeStruct((n_sample_rows, D_MODEL), f32)]
    else:
        out_specs = pl.BlockSpec((tm, D_MODEL), row)
        out_shape = jax.ShapeDtypeStruct((T, D_MODEL), f32)
    return pl.pallas_call(
        functools.partial(_merge_ffn_kernel, lam_init=lam_init, final=final, n_prompt_blocks=npb),
        grid=(npb + nsb,),
        in_specs=[
            *rows2(D_MODEL, sample_block=h_s_row0 // tm),
            *rows2(VW), *rows2(VW), *rows2(VW),
            pl.BlockSpec((tm, VW), lambda i: (i, T_GR)),
            pl.BlockSpec((tm, VW), lambda i: (i, T_MO)),
            pl.BlockSpec((tm, D_MODEL), lambda i: (i, 0)),
            pl.BlockSpec((tm, D_MODEL), lambda i: (i, 1)),
            pl.BlockSpec((tm, D_MODEL), lambda i: (i, 2)),
            pl.BlockSpec((1, DV), const),
            pl.BlockSpec((1, DV), const),
            pl.BlockSpec((1, DV), const),
            resident((VW, D_MODEL)), resident((VW, D_MODEL)), resident((VW, D_MODEL)),
            resident((D_MODEL, D_MODEL)),
            pl.BlockSpec((1, D_MODEL), const),
            resident((D_MODEL, D_FF)), resident((D_MODEL, D_FF)), resident((D_FF, D_MODEL)),
            *rows2(PLE_DIM, prompt_block0=p_row0 // tm),
            pl.BlockSpec((1, D_MODEL), const),
            resident((D_MODEL, D_MODEL)),
            resident((PLE_DIM, D_MODEL)),
            pl.BlockSpec((1, D_MODEL), const),
        ],
        out_specs=out_specs,
        out_shape=out_shape,
        compiler_params=_cparams("arbitrary"),
        name="merge_ffn",
    )(h_p, h_s, *o_a, *o_m, *o_d, proj, proj, gates, gates, gates, g_gla, g_ml, g_diff, w_a, w_m, w_d, w_out,
      g_ffn, w_g, w_u, w_dn, p_p, p_s, g_ple, w_pg, w_pp, g_final)


def _rope_tables(positions):
    half = ROT_DIM // 2
    inv_freq = ROPE_THETA ** (-jnp.arange(half, dtype=f32) * 2.0 / ROT_DIM)
    ang = positions.astype(f32)[:, None] * inv_freq[None, :]
    cos, sin = jnp.cos(ang), jnp.sin(ang)
    n = positions.shape[0]
    one = jnp.ones((n, DK - ROT_DIM), f32)
    zero = jnp.zeros((n, DK - ROT_DIM), f32)
    zh = jnp.zeros((n, half), f32)
    cos_h = jnp.concatenate([cos, cos, one], axis=1)
    up_h = jnp.concatenate([-sin, zh, zero], axis=1)
    dn_h = jnp.concatenate([zh, sin, zero], axis=1)
    rep = lambda a: jnp.concatenate([a] * (LANES // DK), axis=1)
    return rep(cos_h), rep(up_h), rep(dn_h)


def _prep_in_weights(w_in_l, w_gate_l):
    o = np.cumsum((0, QW, QW, VW, VW, GLA_RANK, QW, QW, VW, VW, 2 * HEADS, 2 * QW, 2 * QW, VW))
    seg = lambda i: w_in_l[:, int(o[i]):int(o[i + 1])]
    gq, gk, gv, gr, glr, mq, mk, mv, mo, mif, dq, dk, dv = (seg(i) for i in range(13))
    main = jnp.concatenate([w_gate_l, gq, gk, gv, gr, mq, mk, mv, mo, dq, dk, dv], axis=1).astype(bf16)
    pad = jnp.zeros((D_MODEL, LANES - GLA_RANK - 2 * HEADS), f32)
    small = jnp.concatenate([glr, mif, pad], axis=1).astype(bf16)
    return main, small


def kernel(x_prompt, x_sample, cache_k, cache_v, state_gla, state_mlstm_c, state_mlstm_n, state_mlstm_m, page_table, p_prompt, p_sample, g_mix, w_in, w_gla_gk, b_gla_gk, g_gla_norm, b_mlstm_if, g_mlstm_norm, diff_lambda, g_diff_norm, w_branch, w_gate, w_out, g_ffn, w_ffn_gate, w_ffn_up, w_ffn_down, g_ple, w_ple_gate, w_ple_proj, g_final):
    bp, lp, _ = x_prompt.shape
    bs, ls, _ = x_sample.shape
    depth = g_mix.shape[0]
    n_pages = page_table.shape[1]
    page = cache_k.shape[2]
    past = n_pages * page
    tp = bp * lp
    ts = bs * ls
    assert lp % TM == 0 and ts == TM and bs == LANES and page == LANES and TM_IN % bs == 0
    npb = tp // TM

    def to_step_major(a):
        return jnp.swapaxes(a, 0, 1).reshape((ts,) + a.shape[2:])

    def from_step_major(a):
        return jnp.swapaxes(a.reshape((ls, bs) + a.shape[1:]), 0, 1)

    h_p, h_s, h_s_row0 = x_prompt.reshape(tp, D_MODEL), to_step_major(x_sample), 0

    pos_rows = jnp.concatenate([jnp.arange(lp, dtype=jnp.int32),
                                past + jnp.repeat(jnp.arange(ls, dtype=jnp.int32), bs)])
    cos_t, sup_t, sdn_t = _rope_tables(pos_rows)

    ck = jnp.transpose(cache_k, (0, 1, 3, 4, 2)).reshape(cache_k.shape[0], cache_k.shape[1], 2 * QW, page)
    cv = cache_v.reshape(cache_v.shape[0], cache_v.shape[1], page * HEADS, DV)
    state_n = state_mlstm_n.reshape(depth, bs, QW)
    p_rows = p_prompt.reshape(depth * tp, PLE_DIM)
    k_t = jnp.zeros((depth, bp, 2 * QW, lp), f32)
    v_p = jnp.zeros((depth * tp * HEADS, DV), f32)
    gla_s = jnp.zeros(state_gla.shape, f32)
    c_s = jnp.zeros(state_mlstm_c.shape, f32)
    outs_p, outs_s = [], []
    for l in range(depth):
        lam_init = 0.8 - 0.6 * math.exp(-0.3 * l)
        w_main, w_small = _prep_in_weights(w_in[l], w_gate[l])
        w_gk = jnp.concatenate([w_gla_gk[l], jnp.zeros((LANES - GLA_RANK, QW), f32)], axis=0)
        b_if = jnp.zeros((1, LANES), f32).at[0, L_MI:L_MI + 2 * HEADS].set(b_mlstm_if[l])
        gates, proj, k_t, v_p, k_ts, la, gif = _inproj(
            h_p, h_s, h_s_row0 // TM_IN, g_mix[l][None], w_main, w_small, w_gk, b_gla_gk[l][None], b_if,
            cos_t, sup_t, sdn_t, bp, lp, bs, ls, l, depth, (k_t, v_p))

        oa_p, gla_p = _gla_prompt(proj, la, bp, lp)
        om_p, c_p, n_p, m_p = _mlstm_prompt(proj, gif, bp, lp)
        oa_p, om_p = oa_p.reshape(tp, VW), om_p.reshape(tp, VW)
        od_p, od_s = _attention(proj, k_ts, page_table, diff_lambda[l], ck, cv, l, lam_init, bp, lp, npb)

        ga, gk_, gq_, ma, mk_, mq_, n_s, m_s = _sample_prep(proj, la, gif, state_n, state_mlstm_m, l, npb, bs, ls)
        oa_s, gla_s = _sample_state(ga, gk_, gq_, proj, T_GV, npb, state_gla, l, ls, gla_s)
        om_s, c_s = _sample_state(ma, mk_, mq_, proj, T_MV, npb, state_mlstm_c, l, ls, c_s)
        v_s = from_step_major(proj[tp:, T_DV * TILE:(T_DV + 1) * TILE])

        wb = w_branch[l].astype(bf16)
        h = _merge_ffn(h_p, h_s, h_s_row0, (oa_p, oa_s), (om_p, om_s), (od_p, od_s), proj, gates,
                       g_gla_norm[l][None], g_mlstm_norm[l][None], g_diff_norm[l][None],
                       wb[:VW], wb[VW:2 * VW], wb[2 * VW:], w_out[l].astype(bf16), lam_init,
                       g_ffn[l][None], w_ffn_gate[l].astype(bf16), w_ffn_up[l].astype(bf16),
                       w_ffn_down[l].astype(bf16), p_rows, to_step_major(p_sample[l]), l * tp,
                       g_ple[l][None], w_ple_gate[l].astype(bf16), w_ple_proj[l].astype(bf16), g_final[None],
                       l == depth - 1, tp, ts)
        h_p, h_s, h_s_row0 = h, h, tp

        k_new = k_ts.reshape(ls, 2 * HEADS, DK, bs).transpose(3, 0, 1, 2)
        outs_p.append((gla_p.reshape(bp, HEADS, DK, DV), c_p, n_p, m_p[:, :, 0]))
        outs_s.append((k_new, v_s.reshape(bs, ls, HEADS, DV), n_s.reshape(bs, HEADS, DK), m_s[:, ::DK]))

    y_prompt = h[0].reshape(bp, lp, D_MODEL)
    y_sample = from_step_major(h[1])
    k_prompt = k_t.reshape(depth, bp, 2 * HEADS, DK, lp).transpose(0, 1, 4, 2, 3)
    v_prompt = v_p.reshape(depth, bp, lp, HEADS, DV)
    gla_prompt, c_prompt, n_prompt, m_prompt = [jnp.stack(t) for t in zip(*outs_p)]
    k_sample, v_sample, n_sample, m_sample = [jnp.stack(t) for t in zip(*outs_s)]
    return (y_prompt, y_sample, k_prompt, v_prompt, gla_prompt, c_prompt, n_prompt, m_prompt,
            k_sample, v_sample, gla_s, c_s, n_sample, m_sample)
```

```python
import functools
import math

import numpy as np
import jax
import jax.numpy as jnp
from jax import lax
from jax.experimental import pallas as pl
from jax.experimental.pallas import tpu as pltpu

f32 = jnp.float32
bf16 = jnp.bfloat16
HIGHEST = lax.Precision.HIGHEST

D_MODEL = 1024
PLE_DIM = 256
HEADS = 4
DK = 64
DV = 128
GLA_RANK = 16
GLA_GATE_NORM = 16.0
ROT_DIM = 16
ROPE_THETA = 500000.0
CHUNK = 64
CHUNK_MLSTM = 256
MLSTM_GROUP = 1
D_FF = 2816
EPS = 1e-6
LOG2E = 1.4426950408889634
VW = HEADS * DV
QW = HEADS * DK

LANES = 128
SUBLANES = 8
VMEM_LIMIT = 48 * 1024 * 1024

TILE = 512
GATE_W = 3 * D_MODEL
T_GQK, T_GV, T_GR = 0, 1, 2
T_MQK, T_MV, T_MO = 3, 4, 5
T_DQ, T_DK, T_DV = 6, 7, 8
N_TILES = 9
P_MIX = N_TILES * TILE
P_MAIN = GATE_W + P_MIX
L_MI = 16
L_MF = 20

TM = 512


def _cparams(*sem):
    return pltpu.CompilerParams(dimension_semantics=sem, vmem_limit_bytes=VMEM_LIMIT)


def _log_sigmoid(x):
    return jnp.minimum(x, 0.0) - jnp.log1p(jnp.exp(-jnp.abs(x)))


def _sigmoid(x):
    return 0.5 * jnp.tanh(0.5 * x) + 0.5


def _rms(x, g):
    return x * lax.rsqrt(jnp.mean(x * x, axis=-1, keepdims=True) + EPS) * g


def _dot(a, b):
    return jnp.dot(a.astype(bf16), b.astype(bf16), preferred_element_type=f32)


def _dot_nt(a, b):
    return lax.dot_general(a.astype(bf16), b.astype(bf16), (((1,), (1,)), ((), ())), preferred_element_type=f32)


def _dot_tn(a, b):
    return lax.dot_general(a.astype(bf16), b.astype(bf16), (((0,), (0,)), ((), ())), preferred_element_type=f32)


def _rope_tile(x, cos, sin_up, sin_dn):
    parts = []
    for c in range(x.shape[1] // LANES):
        xc = x[:, c * LANES:(c + 1) * LANES]
        parts.append(xc * cos + pltpu.roll(xc, LANES - ROT_DIM // 2, axis=1) * sin_up
                     + pltpu.roll(xc, ROT_DIM // 2, axis=1) * sin_dn)
    return jnp.concatenate(parts, axis=1)


def _row_specs(width, n_prompt_blocks, sample_block, tm=None, n_sample_blocks=1, prompt_block0=0):
    tm = TM if tm is None else tm
    return (pl.BlockSpec((tm, width), lambda i: (prompt_block0 + jnp.minimum(i, n_prompt_blocks - 1), 0)),
            pl.BlockSpec((tm, width),
                         lambda i: (sample_block + jnp.clip(i - n_prompt_blocks, 0, n_sample_blocks - 1), 0)))


def _pick_rows(n_prompt_blocks, p_ref, s_ref):
    return jnp.where(pl.program_id(0) >= n_prompt_blocks, s_ref[...], p_ref[...])


TM_IN = 256


def _inproj_kernel(xp_ref, xs_ref, g_ref, w_ref, ws_ref, wgk_ref, bgk_ref, bif_ref, cos_ref, sup_ref, sdn_ref,
                   kt_all_ref, vp_all_ref, gates_ref, proj_ref, kt_ref, vp_ref, kts_ref, la_ref, gif_ref,
                   *, n_prompt_blocks, dec_batch):
    del kt_all_ref, vp_all_ref
    i = pl.program_id(0)
    xn = _rms(_pick_rows(n_prompt_blocks, xp_ref, xs_ref), g_ref[...]).astype(bf16)
    small = jnp.dot(xn, ws_ref[...], preferred_element_type=f32)
    z = jnp.dot(small, wgk_ref[...], preferred_element_type=f32, precision=HIGHEST) + bgk_ref[...]
    la_ref[...] = _log_sigmoid(z) * (1.0 / GLA_GATE_NORM)
    gi = small + bif_ref[...]
    lane = lax.broadcasted_iota(jnp.int32, gi.shape, 1)
    gif_ref[...] = jnp.where((lane >= L_MF) & (lane < L_MF + HEADS), _log_sigmoid(gi), gi)

    for c in range(GATE_W // TILE):
        cols = slice(c * TILE, (c + 1) * TILE)
        gates_ref[:, cols] = _sigmoid(jnp.dot(xn, w_ref[:, cols], preferred_element_type=f32)).astype(bf16)

    cos, sup, sdn = cos_ref[...], sup_ref[...], sdn_ref[...]
    for t in range(N_TILES):
        acc = jnp.dot(xn, w_ref[:, GATE_W + t * TILE:GATE_W + (t + 1) * TILE], preferred_element_type=f32)
        if t in (T_DQ, T_DK):
            acc = _rope_tile(acc, cos, sup, sdn)
        proj_ref[:, t * TILE:(t + 1) * TILE] = acc
        if t == T_DK:
            k_t = acc.T
        if t == T_DV:
            v_rows = acc

    @pl.when(i < n_prompt_blocks)
    def _():
        kt_ref[...] = k_t
        for h in range(HEADS):
            vp_ref[pl.ds(h, v_rows.shape[0], stride=HEADS), :] = v_rows[:, h * DV:(h + 1) * DV]

    @pl.when(i >= n_prompt_blocks)
    def _():
        for s in range(k_t.shape[1] // dec_batch):
            kts_ref[s] = k_t[:, s * dec_batch:(s + 1) * dec_batch]


def _inproj(h_p, h_s, sample_block, g_mix, w_main, w_small, w_gk, b_gk, b_if, cos_t, sup_t, sdn_t,
            batch, seq, dec_batch, dec_seq, layer, depth, carried):
    tm = TM_IN
    seq_blocks = seq // tm
    npb = batch * seq_blocks
    steps_per_block = tm // dec_batch
    nsb = dec_seq // steps_per_block
    nt = npb + nsb
    T = nt * tm
    const = lambda i: (0, 0)
    resident = lambda shape: pl.BlockSpec(shape, const, pipeline_mode=pl.Buffered(1))

    def tab_map(i):
        return (jnp.where(i < npb, i % seq_blocks, seq_blocks + i - npb), 0)

    def kt_map(i):
        j = jnp.minimum(i, npb - 1)
        return (layer, j // seq_blocks, 0, j % seq_blocks)

    return pl.pallas_call(
        functools.partial(_inproj_kernel, n_prompt_blocks=npb, dec_batch=dec_batch),
        grid=(nt,),
        input_output_aliases={11: 2, 12: 3},
        in_specs=[
            *_row_specs(D_MODEL, npb, sample_block, tm, nsb),
            pl.BlockSpec((1, D_MODEL), const),
            resident((D_MODEL, P_MAIN)),
            resident((D_MODEL, LANES)),
            resident((LANES, QW)),
            pl.BlockSpec((1, QW), const),
            pl.BlockSpec((1, LANES), const),
            pl.BlockSpec((tm, LANES), tab_map),
            pl.BlockSpec((tm, LANES), tab_map),
            pl.BlockSpec((tm, LANES), tab_map),
            pl.BlockSpec(memory_space=pl.ANY),
            pl.BlockSpec(memory_space=pl.ANY),
        ],
        out_specs=[
            pl.BlockSpec((tm, GATE_W), lambda i: (i, 0)),
            pl.BlockSpec((tm, P_MIX), lambda i: (i, 0)),
            pl.BlockSpec((None, None, TILE, tm), kt_map),
            pl.BlockSpec((tm * HEADS, DV), lambda i: (layer * npb + jnp.minimum(i, npb - 1), 0)),
            pl.BlockSpec((steps_per_block, TILE, dec_batch), lambda i: (jnp.maximum(i - npb, 0), 0, 0)),
            pl.BlockSpec((tm, QW), lambda i: (i, 0)),
            pl.BlockSpec((tm, LANES), lambda i: (i, 0)),
        ],
        out_shape=[
            jax.ShapeDtypeStruct((T, GATE_W), bf16),
            jax.ShapeDtypeStruct((T, P_MIX), f32),
            jax.ShapeDtypeStruct((depth, batch, TILE, seq), f32),
            jax.ShapeDtypeStruct((depth * batch * seq * HEADS, DV), f32),
            jax.ShapeDtypeStruct((dec_seq, TILE, dec_batch), f32),
            jax.ShapeDtypeStruct((T, QW), f32),
            jax.ShapeDtypeStruct((T, LANES), f32),
        ],
        compiler_params=_cparams("arbitrary"),
        name="inproj",
    )(h_p, h_s, g_mix, w_main, w_small, w_gk, b_gk, b_if, cos_t, sup_t, sdn_t, *carried)


def _tril(n):
    r = lax.broadcasted_iota(jnp.int32, (n, n), 0)
    c = lax.broadcasted_iota(jnp.int32, (n, n), 1)
    return r >= c


def _gla_prompt_kernel(*refs, n_chunks, batch):
    q_refs, k_refs, v_refs, la_refs = (refs[i * batch:(i + 1) * batch] for i in range(4))
    o_ref, s_out_ref, s_scr = refs[4 * batch:]
    blk = pl.program_id(0)

    @pl.when(blk == 0)
    def _():
        s_scr[...] = jnp.zeros_like(s_scr)

    ltri = _tril(CHUNK).astype(f32)
    r2 = lax.broadcasted_iota(jnp.int32, (2 * CHUNK, CHUNK), 0)
    c2 = lax.broadcasted_iota(jnp.int32, (2 * CHUNK, CHUNK), 1)
    tril2 = jnp.where(r2 >= CHUNK, r2 - CHUNK, r2) >= c2
    lane = lax.broadcasted_iota(jnp.int32, (CHUNK, LANES), 1)
    lo = (lane < DK).astype(f32)
    hi = 1.0 - lo

    def chunk(c, carry):
        rows = pl.ds(pl.multiple_of(c * CHUNK, CHUNK), CHUNK)
        units = [(s, p) for s in range(batch) for p in range(HEADS // 2)]
        bs_ = [jnp.dot(ltri, la_refs[s][rows, :], preferred_element_type=f32, precision=HIGHEST)
               for s in range(batch)]
        qs, kgs, kd0, kd1, dcol, v0, v1, s_old = {}, {}, {}, {}, {}, {}, {}, {}
        for s in range(batch):
            b = bs_[s]
            b_last = b[CHUNK - 1:CHUNK, :]
            q = q_refs[s][rows, :]
            k = k_refs[s][rows, :]
            qg = q * jnp.exp(b) * (DK ** -0.5)
            kg = k * jnp.exp(-b)
            kd = k * jnp.exp(b_last - b)
            dec = jnp.exp(b_last)
            for p in range(HEADS // 2):
                u = (s, p)
                ls = slice(p * LANES, (p + 1) * LANES)
                qs[u] = jnp.concatenate([qg[:, ls] * lo, qg[:, ls] * hi], axis=0)
                kgs[u] = kg[:, ls]
                kd0[u], kd1[u] = kd[:, ls] * lo, kd[:, ls] * hi
                dcol[u] = jnp.broadcast_to(dec[:, ls], (LANES, LANES)).T
                v0[u] = v_refs[s][rows, pl.ds((2 * p) * DV, DV)]
                v1[u] = v_refs[s][rows, pl.ds((2 * p + 1) * DV, DV)]
                s_old[u] = s_scr[s, p]
        a = {u: _dot_nt(qs[u], kgs[u]) for u in units}
        inter = {u: _dot(qs[u], s_old[u]) for u in units}
        kv = {u: _dot_tn(kd0[u], v0[u]) + _dot_tn(kd1[u], v1[u]) for u in units}
        a = {u: jnp.where(tril2, a[u], 0.0) for u in units}
        intra0 = {u: _dot(a[u][:CHUNK], v0[u]) for u in units}
        intra1 = {u: _dot(a[u][CHUNK:], v1[u]) for u in units}
        for u in units:
            s, p = u
            o_ref[s, rows, pl.ds((2 * p) * DV, DV)] = inter[u][:CHUNK] + intra0[u]
            o_ref[s, rows, pl.ds((2 * p + 1) * DV, DV)] = inter[u][CHUNK:] + intra1[u]
            s_scr[s, p] = dcol[u] * s_old[u] + kv[u]
        return carry

    lax.fori_loop(0, n_chunks, chunk, 0)

    @pl.when(blk == pl.num_programs(0) - 1)
    def _():
        s_out_ref[...] = s_scr[...]


def _stream_specs(width, batch, nb, col):
    return [pl.BlockSpec((TM, width), functools.partial(lambda s, i: (s * nb + i, col), s)) for s in range(batch)]


def _gla_prompt(proj, la, batch, seq):
    nb = seq // TM
    qw_blocks = TILE // QW
    return pl.pallas_call(
        functools.partial(_gla_prompt_kernel, n_chunks=TM // CHUNK, batch=batch),
        grid=(nb,),
        in_specs=[
            *_stream_specs(QW, batch, nb, T_GQK * qw_blocks),
            *_stream_specs(QW, batch, nb, T_GQK * qw_blocks + 1),
            *_stream_specs(VW, batch, nb, T_GV),
            *_stream_specs(QW, batch, nb, 0),
        ],
        out_specs=[
            pl.BlockSpec((batch, TM, VW), lambda i: (0, i, 0)),
            pl.BlockSpec((batch, HEADS // 2, LANES, LANES), lambda i: (0, 0, 0, 0)),
        ],
        out_shape=[
            jax.ShapeDtypeStruct((batch, seq, VW), f32),
            jax.ShapeDtypeStruct((batch, HEADS // 2, LANES, LANES), f32),
        ],
        scratch_shapes=[pltpu.VMEM((batch, HEADS // 2, LANES, LANES), f32)],
        compiler_params=_cparams("arbitrary"),
        name="gla_prompt",
    )(*([proj] * (3 * batch)), *([la] * batch))


def _mlstm_prompt_kernel(*refs, n_chunks, batch):
    q_refs, k_refs, v_refs, gif_refs = (refs[i * batch:(i + 1) * batch] for i in range(4))
    h_ref, c_out_ref, n_out_ref, m_out_ref, ct_scr, n_scr, m_scr = refs[4 * batch:]
    blk = pl.program_id(0)

    @pl.when(blk == 0)
    def _():
        ct_scr[...] = jnp.zeros_like(ct_scr)
        n_scr[...] = jnp.zeros_like(n_scr)
        m_scr[...] = jnp.zeros_like(m_scr)

    CH = CHUNK_MLSTM
    ltri = _tril(CH).astype(f32)
    r_i = lax.broadcasted_iota(jnp.int32, (CH, CH), 0)
    c_i = lax.broadcasted_iota(jnp.int32, (CH, CH), 1)
    upper = r_i <= c_i

    def chunk(c, carry):
        rows = pl.ds(pl.multiple_of(c * CH, CH), CH)
        for sq0 in range(0, batch, MLSTM_GROUP):
            seqs = range(sq0, min(sq0 + MLSTM_GROUP, batch))
            units = [(sq, h) for sq in seqs for h in range(HEADS)]
            gate = {}
            for sq in seqs:
                g = gif_refs[sq][rows, :]
                fc = jnp.dot(ltri, g, preferred_element_type=f32, precision=HIGHEST)
                gate[sq] = (g, fc, g.T, fc.T)
            q, k, v_t, c_t, n_all = {}, {}, {}, {}, {}
            for sq in seqs:
                q_all = q_refs[sq][rows, :]
                k_all = k_refs[sq][rows, :] * (DK ** -0.5)
                n_all[sq] = n_scr[sq]
                for h in range(HEADS):
                    u = (sq, h)
                    q[u] = q_all[:, h * DK:(h + 1) * DK]
                    k[u] = k_all[:, h * DK:(h + 1) * DK]
                    v_t[u] = v_refs[sq][rows, pl.ds(h * DV, DV)].T
                    c_t[u] = ct_scr[sq, h]
            kq = {u: _dot_nt(k[u], q[u]) for u in units}
            cq = {u: _dot_nt(c_t[u], q[u]) for u in units}
            qn = {u: _dot_nt(n_all[u[0]], q[u])[u[1]:u[1] + 1, :] for u in units}
            sc, wk, w_inter, floor, decay, m_new = {}, {}, {}, {}, {}, {}
            for u in units:
                sq, h = u
                g, fc, g_t, fc_t = gate[sq]
                i_col = g[:, L_MI + h:L_MI + h + 1]
                f_col = fc[:, L_MF + h:L_MF + h + 1]
                f_row = fc_t[L_MF + h:L_MF + h + 1, :]
                f_last = fc[CH - 1:CH, L_MF + h:L_MF + h + 1]
                m_prev = m_scr[sq, h:h + 1, 0:1]
                log_d = jnp.where(upper, f_row + (i_col - f_col), -jnp.inf)
                m_inter = m_prev + f_row
                m_row = jnp.maximum(m_inter, jnp.max(log_d, axis=0, keepdims=True))
                w_inter[u] = jnp.exp(m_inter - m_row)
                floor[u] = jnp.exp(-m_row)
                sc[u] = kq[u] * jnp.exp(log_d - m_row)
                log_w = f_last - f_col + i_col
                m_new[u] = jnp.maximum(m_prev + f_last, jnp.max(log_w, axis=0, keepdims=True))
                wk[u] = jnp.exp(log_w - m_new[u]) * k[u]
                decay[u] = jnp.exp(m_prev + f_last - m_new[u])
            vs = {u: _dot(v_t[u], sc[u]) for u in units}
            vk = {u: _dot(v_t[u], wk[u]) for u in units}
            for u in units:
                sq, h = u
                num = w_inter[u] * cq[u] + vs[u]
                den = w_inter[u] * qn[u] + jnp.sum(sc[u], axis=0, keepdims=True)
                h_ref[sq, rows, pl.ds(h * DV, DV)] = (num / jnp.maximum(jnp.abs(den), floor[u])).T
                ct_scr[sq, h] = decay[u] * c_t[u] + vk[u]
                n_scr[sq, h:h + 1, :] = decay[u] * n_all[sq][h:h + 1, :] + jnp.sum(wk[u], axis=0, keepdims=True)
                m_scr[sq, h:h + 1, :] = jnp.broadcast_to(m_new[u], (1, LANES))
        return carry

    lax.fori_loop(0, n_chunks, chunk, 0)

    @pl.when(blk == pl.num_programs(0) - 1)
    def _():
        for sq in range(batch):
            for h in range(HEADS):
                c_out_ref[sq, h] = ct_scr[sq, h].T
        n_out_ref[...] = n_scr[:, 0:HEADS, :]
        m_out_ref[...] = m_scr[:, 0:HEADS, :]


def _mlstm_prompt(proj, gif, batch, seq):
    nb = seq // TM
    qw_blocks = TILE // QW
    return pl.pallas_call(
        functools.partial(_mlstm_prompt_kernel, n_chunks=TM // CHUNK_MLSTM, batch=batch),
        grid=(nb,),
        in_specs=[
            *_stream_specs(QW, batch, nb, T_MQK * qw_blocks),
            *_stream_specs(QW, batch, nb, T_MQK * qw_blocks + 1),
            *_stream_specs(VW, batch, nb, T_MV),
            *_stream_specs(LANES, batch, nb, 0),
        ],
        out_specs=[
            pl.BlockSpec((batch, TM, VW), lambda i: (0, i, 0)),
            pl.BlockSpec((batch, HEADS, DK, DV), lambda i: (0, 0, 0, 0)),
            pl.BlockSpec((batch, HEADS, DK), lambda i: (0, 0, 0)),
            pl.BlockSpec((batch, HEADS, LANES), lambda i: (0, 0, 0)),
        ],
        out_shape=[
            jax.ShapeDtypeStruct((batch, seq, VW), f32),
            jax.ShapeDtypeStruct((batch, HEADS, DK, DV), f32),
            jax.ShapeDtypeStruct((batch, HEADS, DK), f32),
            jax.ShapeDtypeStruct((batch, HEADS, LANES), f32),
        ],
        scratch_shapes=[
            pltpu.VMEM((batch, HEADS, DV, DK), f32),
            pltpu.VMEM((batch, SUBLANES, DK), f32),
            pltpu.VMEM((batch, SUBLANES, LANES), f32),
        ],
        compiler_params=_cparams("arbitrary"),
        name="mlstm_prompt",
    )(*([proj] * (3 * batch)), *([gif] * batch))


def _diff_lambda_value(lam_ref, lam_init):
    lam = lam_ref[...]
    s1 = jnp.sum(lam[0:1] * lam[1:2], axis=1, keepdims=True)
    s2 = jnp.sum(lam[2:3] * lam[3:4], axis=1, keepdims=True)
    return jnp.exp(s1) - jnp.exp(s2) + lam_init


def _attn_prompt_step(p, qi_ref, ki_ref, q_ref, k_ref, v_ref, lam_ref, o_ref, q2_scr, m_scr, l_scr, acc_scr,
                      *, tq, lam_init, other_stages):
    qi = qi_ref[p]
    ki = ki_ref[p]
    pair = lambda j: slice(j * LANES, (j + 1) * LANES)

    @pl.when(ki == 0)
    def _():
        for j in range(HEADS):
            q = q_ref[:, pair(j)] * (DK ** -0.5 * LOG2E)
            lane = lax.broadcasted_iota(jnp.int32, q.shape, 1)
            q2_scr[j, 0:tq, :] = jnp.where(lane < DK, q, 0.0).astype(bf16)
            q2_scr[j, tq:2 * tq, :] = jnp.where(lane >= DK, q, 0.0).astype(bf16)
        m_scr[...] = jnp.full_like(m_scr, -jnp.inf)
        l_scr[...] = jnp.zeros_like(l_scr)
        acc_scr[...] = jnp.zeros_like(acc_scr)

    def step(masked, other):
        sts = [_dot_nt(k_ref[:, pair(j)], q2_scr[j]) for j in range(HEADS)]
        other[0]()
        pts, alphas = [], []
        for j in range(HEADS):
            st = sts[j]
            if masked:
                r = lax.broadcasted_iota(jnp.int32, st.shape, 0)
                c = lax.broadcasted_iota(jnp.int32, st.shape, 1)
                c = jnp.where(c >= tq, c - tq, c)
                st = jnp.where(r <= c, st, -jnp.inf)
            m_prev = m_scr[j]
            m_new = jnp.maximum(m_prev, jnp.max(st, axis=0, keepdims=True))
            alpha = jnp.exp2(m_prev - m_new)
            pt = jnp.exp2(st - m_new)
            l_scr[j] = alpha * l_scr[j] + jnp.sum(pt, axis=0, keepdims=True)
            m_scr[j] = m_new
            pts.append(pt.astype(bf16))
            alphas.append(alpha)
        for j in range(HEADS):
            acc_scr[j] = alphas[j] * acc_scr[j] + _dot_tn(v_ref[:, pair(j)], pts[j])
        for stage in other[1:]:
            stage()

    @pl.when(ki < qi)
    def _():
        step(False, other_stages())

    @pl.when(ki == qi)
    def _():
        step(True, other_stages())
        lam = _diff_lambda_value(lam_ref, lam_init)
        for j in range(HEADS):
            ot = acc_scr[j] / l_scr[j]
            o_ref[:, pair(j)] = (ot[:, 0:tq] - lam * ot[:, tq:2 * tq]).T


def _sample_page_copies(seq_idx, slot, pt_ref, ck_hbm, cv_hbm, kbuf, vbuf, sem, *, layer, n_pages, page):
    out = []
    for pg in range(n_pages):
        src = pt_ref[seq_idx, pg]
        out.append(pltpu.make_async_copy(ck_hbm.at[layer, src], kbuf.at[slot, :, pl.ds(pg * page, page)],
                                         sem.at[0, slot]))
        out.append(pltpu.make_async_copy(cv_hbm.at[layer, src],
                                         vbuf.at[slot, pl.ds(pg * page * HEADS, page * HEADS), :], sem.at[1, slot]))
    return out


def _attn_sample_pages(step, n_steps, pt_ref, ck_hbm, cv_hbm, o_ref, kbuf, vbuf, sem, *, nseq, pages):
    past = pages["n_pages"] * pages["page"]
    seq = lambda st: jnp.minimum(st, nseq - 1)
    slot = step % 2
    copies = functools.partial(_sample_page_copies, pt_ref=pt_ref, ck_hbm=ck_hbm, cv_hbm=cv_hbm, kbuf=kbuf,
                               vbuf=vbuf, sem=sem, **pages)

    @pl.when(step == 0)
    def _():
        kbuf[:, :, pl.ds(past, LANES)] = jnp.zeros((2, kbuf.shape[1], LANES), f32)
        vbuf[:, pl.ds(past * HEADS, LANES * HEADS), :] = jnp.zeros((2, LANES * HEADS, DV), f32)
        o_ref[...] = jnp.zeros_like(o_ref)
        for cp in copies(0, 0):
            cp.start()

    @pl.when(step + 1 < n_steps)
    def _():
        for cp in copies(seq(step + 1), 1 - slot):
            cp.start()

    for cp in copies(seq(step), slot):
        cp.wait()


def _attn_sample_stages(step, q_ref, kts_ref, v_ref, lam_ref, o_ref, kbuf, vbuf,
                        *, n_pages, page, dec_seq, nseq, lam_init):
    st = {}
    b = jnp.minimum(step, nseq - 1)
    slot = step % 2
    past = n_pages * page
    n_qh = 2 * HEADS

    shift = lax.rem(nseq - b, nseq)
    b8 = (b // SUBLANES) * SUBLANES
    groups = [pl.ds(pl.multiple_of(t * nseq + b8, SUBLANES), SUBLANES) for t in range(dec_seq)]
    mine = lax.broadcasted_iota(jnp.int32, (SUBLANES, 2 * QW), 0) == b - b8
    pick = lambda ref, t: jnp.sum(jnp.where(mine, ref[groups[t], :], 0.0), axis=0, keepdims=True)
    def scores():
        head_of_lane = lax.broadcasted_iota(jnp.int32, (n_qh, 2 * QW), 1) // DK
        head_of_row = lax.broadcasted_iota(jnp.int32, (n_qh, 2 * QW), 0)
        q_rows = []
        for t in range(dec_seq):
            kbuf[slot, :, pl.ds(past + t, 1)] = pltpu.roll(kts_ref[t], shift, axis=1)[:, 0:1]
            v_row = pick(v_ref, t)
            for h in range(HEADS):
                vbuf[slot, pl.ds((past + t) * HEADS + h, 1), :] = v_row[:, h * DV:(h + 1) * DV]
            q = jnp.broadcast_to(pick(q_ref, t) * (DK ** -0.5), (n_qh, 2 * QW))
            q_rows.append(jnp.where(head_of_lane == head_of_row, q, 0.0))
        qbd = jnp.concatenate(q_rows, axis=0)
        st["s"] = _dot(qbd, kbuf[slot])

    def softmax():
        s = st["s"]
        r = lax.broadcasted_iota(jnp.int32, s.shape, 0)
        c = lax.broadcasted_iota(jnp.int32, s.shape, 1)
        s = jnp.where(c - past <= r // n_qh, s, -jnp.inf)
        pr = jnp.exp(s - jnp.max(s, axis=1, keepdims=True))
        st["inv_l"] = 1.0 / jnp.sum(pr, axis=1, keepdims=True)
        st["pr"] = pr.astype(bf16)

    def values():
        v_hs = [vbuf[slot, pl.ds(hv, past + LANES, stride=HEADS), :] for hv in range(HEADS)]
        st["o"] = [_dot(st["pr"], v_h) for v_h in v_hs]

    def write():
        lam = _diff_lambda_value(lam_ref, lam_init)
        mine_v = mine[:, 0:DV]
        for hv in range(HEADS):
            o_h = st["o"][hv] * st["inv_l"]
            for t in range(dec_seq):
                r1 = t * n_qh + 2 * hv
                d = o_h[r1:r1 + 1] - lam * o_h[r1 + 1:r1 + 2]
                lanes = slice(hv * DV, (hv + 1) * DV)
                o_ref[groups[t], lanes] = jnp.where(mine_v, d, o_ref[groups[t], lanes])

    return scores, softmax, values, write


def _attn_kernel(qi_ref, ki_ref, pt_ref, q_ref, k_ref, v_ref, lam_ref, qs_ref, kts_ref, vs_ref, ck_hbm, cv_hbm,
                 o_ref, os_ref, q2_scr, m_scr, l_scr, acc_scr, kbuf, vbuf, sem, *, prompt, sample, nseq, n_pairs):
    p = pl.program_id(1)
    step = pl.program_id(0) * pl.num_programs(1) + p
    n_steps = pl.num_programs(0) * pl.num_programs(1)
    pages = dict(layer=sample["layer"], n_pages=sample["n_pages"], page=sample["page"])
    _attn_sample_pages(step, n_steps, pt_ref, ck_hbm, cv_hbm, os_ref, kbuf, vbuf, sem, nseq=nseq, pages=pages)

    def sample_stages():
        return _attn_sample_stages(step, qs_ref, kts_ref, vs_ref, lam_ref, os_ref, kbuf, vbuf, nseq=nseq,
                                   n_pages=sample["n_pages"], page=sample["page"], dec_seq=sample["dec_seq"],
                                   lam_init=sample["lam_init"])

    @pl.when(p < n_pairs)
    def _():
        _attn_prompt_step(p, qi_ref, ki_ref, q_ref, k_ref, v_ref, lam_ref, o_ref,
                          q2_scr, m_scr, l_scr, acc_scr, other_stages=sample_stages, **prompt)

    @pl.when(p >= n_pairs)
    def _():
        for stage in sample_stages():
            stage()


def _attention(proj, k_ts, page_table, diff_lambda, cache_kt, cache_v2, layer, lam_init, batch, seq, sample_block,
               tq=512):
    nq = seq // tq
    pairs = [(q, k) for q in range(nq) for k in range(q + 1)]
    qi_tab = jnp.asarray(np.array([a for a, _ in pairs], np.int32))
    ki_tab = jnp.asarray(np.array([b for _, b in pairs], np.int32))
    nseq, n_pages = page_table.shape
    n_pairs = len(pairs)
    steps = max(n_pairs, -(-nseq // batch))
    last = n_pairs - 1
    dec_seq = k_ts.shape[0]
    rows = dec_seq * nseq
    page = cache_kt.shape[3]
    cols = n_pages * page + LANES
    grid_spec = pltpu.PrefetchScalarGridSpec(
        num_scalar_prefetch=3,
        grid=(batch, steps),
        in_specs=[
            pl.BlockSpec((tq, TILE), lambda b, p, qt, kt, pt: (b * nq + qt[jnp.minimum(p, last)], T_DQ)),
            pl.BlockSpec((tq, TILE), lambda b, p, qt, kt, pt: (b * nq + kt[jnp.minimum(p, last)], T_DK)),
            pl.BlockSpec((tq, TILE), lambda b, p, qt, kt, pt: (b * nq + kt[jnp.minimum(p, last)], T_DV)),
            pl.BlockSpec((4, DK), lambda b, p, qt, kt, pt: (0, 0)),
            pl.BlockSpec((rows, TILE), lambda b, p, qt, kt, pt: (sample_block, T_DQ)),
            pl.BlockSpec((dec_seq, TILE, nseq), lambda b, p, qt, kt, pt: (0, 0, 0)),
            pl.BlockSpec((rows, TILE), lambda b, p, qt, kt, pt: (sample_block, T_DV)),
            pl.BlockSpec(memory_space=pl.ANY),
            pl.BlockSpec(memory_space=pl.ANY),
        ],
        out_specs=[
            pl.BlockSpec((tq, VW), lambda b, p, qt, kt, pt: (b * nq + qt[jnp.minimum(p, last)], 0)),
            pl.BlockSpec((rows, VW), lambda b, p, qt, kt, pt: (0, 0)),
        ],
        scratch_shapes=[
            pltpu.VMEM((HEADS, 2 * tq, LANES), bf16),
            pltpu.VMEM((HEADS, 1, 2 * tq), f32),
            pltpu.VMEM((HEADS, 1, 2 * tq), f32),
            pltpu.VMEM((HEADS, DV, 2 * tq), f32),
            pltpu.VMEM((2, 2 * QW, cols), f32),
            pltpu.VMEM((2, cols * HEADS, DV), f32),
            pltpu.SemaphoreType.DMA((2, 2)),
        ],
    )
    return pl.pallas_call(
        functools.partial(_attn_kernel, nseq=nseq, n_pairs=n_pairs, prompt=dict(tq=tq, lam_init=lam_init),
                          sample=dict(layer=layer, n_pages=n_pages, page=page, dec_seq=dec_seq, lam_init=lam_init)),
        grid_spec=grid_spec,
        out_shape=[jax.ShapeDtypeStruct((batch * seq, VW), f32), jax.ShapeDtypeStruct((rows, VW), f32)],
        compiler_params=_cparams("arbitrary", "arbitrary"),
        name="attention",
    )(qi_tab, ki_tab, page_table, proj, proj, proj, diff_lambda, proj, k_ts, proj, cache_kt, cache_v2)


def _head_of_lane(shape):
    return lax.broadcasted_iota(jnp.int32, shape, 1) // DK


def _expand_heads(cols, base, head):
    out = cols[:, base + HEADS - 1:base + HEADS]
    for h in range(HEADS - 2, -1, -1):
        out = jnp.where(head == h, cols[:, base + h:base + h + 1], out)
    return out


def _sample_prep_kernel(gq_ref, gk_ref, la_ref, mq_ref, mk_ref, gif_ref, n0_ref, m0_ref,
                        ga_ref, gkk_ref, gqq_ref, ma_ref, mkk_ref, mqq_ref, n_out_ref, m_out_ref, *, nseq, dec_seq):
    head = _head_of_lane((nseq, QW))
    m = _expand_heads(m0_ref[...], 0, head)
    n = n0_ref[...]
    for t in range(dec_seq):
        rows = pl.ds(t * nseq, nseq)
        ga_ref[t] = jnp.exp(la_ref[rows, :]).T
        gkk_ref[t] = gk_ref[rows, :].T
        gqq_ref[t] = (gq_ref[rows, :] * (DK ** -0.5)).T
        gates = gif_ref[rows, :]
        i_e = _expand_heads(gates, L_MI, head)
        f_e = _expand_heads(gates, L_MF, head)
        k = mk_ref[rows, :] * (DK ** -0.5)
        q = mq_ref[rows, :]
        m_new = jnp.maximum(f_e + m, i_e)
        fp = jnp.exp(f_e + m - m_new)
        ip = jnp.exp(i_e - m_new)
        n = fp * n + ip * k
        nq = n * q
        den = jnp.zeros_like(nq)
        for h in range(HEADS):
            den = jnp.where(head == h, jnp.sum(jnp.where(head == h, nq, 0.0), axis=1, keepdims=True), den)
        inv = 1.0 / jnp.maximum(jnp.abs(den), jnp.exp(-m_new))
        ma_ref[t] = fp.T
        mkk_ref[t] = (ip * k).T
        mqq_ref[t] = (q * inv).T
        m = m_new
    n_out_ref[...] = n
    m_out_ref[...] = m


def _sample_prep(proj, la, gif, state_n, state_m, layer, sample_block, nseq, dec_seq):
    rows = nseq * dec_seq
    qw_blocks = TILE // QW
    tok = jax.ShapeDtypeStruct((dec_seq, QW, nseq), f32)
    st = jax.ShapeDtypeStruct((nseq, QW), f32)
    tok_spec = pl.BlockSpec((dec_seq, QW, nseq), lambda i: (0, 0, 0))
    st_spec = pl.BlockSpec((nseq, QW), lambda i: (0, 0))
    col = lambda c: pl.BlockSpec((rows, QW), lambda i: (sample_block, c))
    return pl.pallas_call(
        functools.partial(_sample_prep_kernel, nseq=nseq, dec_seq=dec_seq),
        grid=(1,),
        in_specs=[
            col(T_GQK * qw_blocks), col(T_GQK * qw_blocks + 1), col(0),
            col(T_MQK * qw_blocks), col(T_MQK * qw_blocks + 1),
            pl.BlockSpec((rows, LANES), lambda i: (sample_block, 0)),
            pl.BlockSpec((None, nseq, QW), lambda i: (layer, 0, 0)),
            pl.BlockSpec((None, nseq, HEADS), lambda i: (layer, 0, 0)),
        ],
        out_specs=[tok_spec] * 6 + [st_spec] * 2,
        out_shape=[tok] * 6 + [st] * 2,
        compiler_params=_cparams("arbitrary"),
        name="sample_prep",
    )(proj, proj, la, proj, proj, gif, state_n, state_m)


DK_PER_STEP = 2 * SUBLANES


def _sample_state_kernel(a_ref, k_ref, q_ref, v_ref, s_ref, s_all_ref, o_ref, s_out_ref, vt_scr, o_scr,
                         *, dec_seq, nseq):
    del s_all_ref
    h = pl.program_id(0)
    g = pl.program_id(1)

    @pl.when(g == 0)
    def _():
        for t in range(dec_seq):
            vt_scr[t] = v_ref[t * nseq:(t + 1) * nseq, :].T
        o_scr[...] = jnp.zeros_like(o_scr)

    rows = pl.ds(pl.multiple_of(h * DK + g * DK_PER_STEP, DK_PER_STEP), DK_PER_STEP)
    a = [a_ref[t, rows, :] for t in range(dec_seq)]
    k = [k_ref[t, rows, :] for t in range(dec_seq)]
    q = [q_ref[t, rows, :] for t in range(dec_seq)]
    v_t = [vt_scr[t] for t in range(dec_seq)]
    acc = [o_scr[t] for t in range(dec_seq)]
    for dd in range(DK_PER_STEP):
        s = s_ref[:, dd, :].T
        for t in range(dec_seq):
            s = a[t][dd:dd + 1, :] * s + k[t][dd:dd + 1, :] * v_t[t]
            acc[t] = acc[t] + q[t][dd:dd + 1, :] * s
        s_out_ref[:, dd, :] = s.T
    for t in range(dec_seq):
        o_scr[t] = acc[t]

    @pl.when(g == pl.num_programs(1) - 1)
    def _():
        for t in range(dec_seq):
            o_ref[t * nseq:(t + 1) * nseq, :] = acc[t].T


def _sample_state(a_t, k_t, q_t, proj, v_tile, sample_block, state, layer, dec_seq, carried):
    nseq = a_t.shape[2]
    rows = dec_seq * nseq
    full = pl.BlockSpec((dec_seq, QW, nseq), lambda h, g: (0, 0, 0))
    state_spec = pl.BlockSpec((None, nseq, None, DK_PER_STEP, DV), lambda h, g: (layer, 0, h, g, 0))
    dv_blocks = TILE // DV
    return pl.pallas_call(
        functools.partial(_sample_state_kernel, dec_seq=dec_seq, nseq=nseq),
        grid=(HEADS, DK // DK_PER_STEP),
        input_output_aliases={5: 1},
        in_specs=[
            full, full, full,
            pl.BlockSpec((rows, DV), lambda h, g: (sample_block, v_tile * dv_blocks + h)),
            state_spec,
            pl.BlockSpec(memory_space=pl.ANY),
        ],
        out_specs=[pl.BlockSpec((rows, DV), lambda h, g: (0, h)), state_spec],
        out_shape=[jax.ShapeDtypeStruct((rows, VW), f32), jax.ShapeDtypeStruct(state.shape, f32)],
        scratch_shapes=[pltpu.VMEM((dec_seq, DV, nseq), f32), pltpu.VMEM((dec_seq, DV, nseq), f32)],
        compiler_params=_cparams("arbitrary", "arbitrary"),
        name="sample_state",
    )(a_t, k_t, q_t, proj, state, carried)


def _head_rms(x, g):
    parts = []
    for h in range(HEADS):
        parts.append(_rms(x[:, h * DV:(h + 1) * DV], g))
    return jnp.concatenate(parts, axis=1)


FF_SLABS = ((0, 768), (768, 768), (1536, 768), (2304, 512))
TM_CH = 256


def _merge_ffn_kernel(hp_ref, hs_ref, oap_ref, oas_ref, omp_ref, oms_ref, odp_ref, ods_ref, gr_ref, mo_ref,
                      g0_ref, g1_ref, g2_ref, gg_ref, gm_ref, gd_ref, wa_ref, wm_ref, wd_ref, wo_ref,
                      gf_ref, wg_ref, wu_ref, wdn_ref, pp_ref, ps_ref, gp_ref, wpg_ref, wpp_ref, gfin_ref,
                      *out_refs, lam_init, final, n_prompt_blocks):
    pick = functools.partial(_pick_rows, n_prompt_blocks)
    gr = gr_ref[...]
    a = _head_rms(pick(oap_ref, oas_ref), gg_ref[...]) * (gr * _sigmoid(gr))
    m = _sigmoid(mo_ref[...]) * _head_rms(pick(omp_ref, oms_ref), gm_ref[...])
    d = _head_rms(pick(odp_ref, ods_ref), gd_ref[...]) * (1.0 - lam_init)
    merged = (g0_ref[...] * _dot(a, wa_ref[...]) + g1_ref[...] * _dot(m, wm_ref[...])
              + g2_ref[...] * _dot(d, wd_ref[...]))
    h = pick(hp_ref, hs_ref) + _dot(merged, wo_ref[...])

    xf = _rms(h, gf_ref[...]).astype(bf16)
    h2 = h
    for lo, width in FF_SLABS:
        gate = jnp.dot(xf, wg_ref[:, lo:lo + width], preferred_element_type=f32)
        up = jnp.dot(xf, wu_ref[:, lo:lo + width], preferred_element_type=f32)
        h2 = h2 + _dot(gate * _sigmoid(gate) * up, wdn_ref[lo:lo + width, :])
    ple_gate = _sigmoid(_dot(_rms(h2, gp_ref[...]), wpg_ref[...]))
    h3 = h2 + ple_gate * _dot(pick(pp_ref, ps_ref), wpp_ref[...])
    if not final:
        out_refs[0][...] = h3
        return
    y = _rms(h3, gfin_ref[...])
    yp_ref, ys_ref = out_refs
    i = pl.program_id(0)

    @pl.when(i < n_prompt_blocks)
    def _():
        yp_ref[...] = y

    @pl.when(i >= n_prompt_blocks)
    def _():
        ys_ref[...] = y


def _merge_ffn(h_p, h_s, h_s_row0, o_a, o_m, o_d, proj, gates, g_gla, g_ml, g_diff, w_a, w_m, w_d, w_out, lam_init,
               g_ffn, w_g, w_u, w_dn, p_p, p_s, p_row0, g_ple, w_pg, w_pp, g_final, final, n_prompt_rows, n_sample_rows):
    tm = TM_CH
    npb, nsb = n_prompt_rows // tm, n_sample_rows // tm
    T = n_prompt_rows + n_sample_rows
    row = lambda i: (i, 0)
    const = lambda i: (0, 0)
    resident = lambda shape: pl.BlockSpec(shape, const, pipeline_mode=pl.Buffered(1))
    rows2 = lambda width, sample_block=0, prompt_block0=0: _row_specs(width, npb, sample_block, tm, nsb, prompt_block0)
    assert FF_SLABS[-1][0] + FF_SLABS[-1][1] == D_FF
    if final:
        out_specs = [pl.BlockSpec((tm, D_MODEL), lambda i: (jnp.minimum(i, npb - 1), 0)),
                     pl.BlockSpec((tm, D_MODEL), lambda i: (jnp.clip(i - npb, 0, nsb - 1), 0))]
        out_shape = [jax.ShapeDtypeStruct((n_prompt_rows, D_MODEL), f32),
                     jax.ShapeDtypeStruct((n_sample_rows, D_MODEL), f32)]
    else:
        out_specs = pl.BlockSpec((tm, D_MODEL), row)
        out_shape = jax.ShapeDtypeStruct((T, D_MODEL), f32)
    return pl.pallas_call(
        functools.partial(_merge_ffn_kernel, lam_init=lam_init, final=final, n_prompt_blocks=npb),
        grid=(npb + nsb,),
        in_specs=[
            *rows2(D_MODEL, sample_block=h_s_row0 // tm),
            *rows2(VW), *rows2(VW), *rows2(VW),
            pl.BlockSpec((tm, VW), lambda i: (i, T_GR)),
            pl.BlockSpec((tm, VW), lambda i: (i, T_MO)),
            pl.BlockSpec((tm, D_MODEL), lambda i: (i, 0)),
            pl.BlockSpec((tm, D_MODEL), lambda i: (i, 1)),
            pl.BlockSpec((tm, D_MODEL), lambda i: (i, 2)),
            pl.BlockSpec((1, DV), const),
            pl.BlockSpec((1, DV), const),
            pl.BlockSpec((1, DV), const),
            resident((VW, D_MODEL)), resident((VW, D_MODEL)), resident((VW, D_MODEL)),
            resident((D_MODEL, D_MODEL)),
            pl.BlockSpec((1, D_MODEL), const),
            resident((D_MODEL, D_FF)), resident((D_MODEL, D_FF)), resident((D_FF, D_MODEL)),
            *rows2(PLE_DIM, prompt_block0=p_row0 // tm),
            pl.BlockSpec((1, D_MODEL), const),
            resident((D_MODEL, D_MODEL)),
            resident((PLE_DIM, D_MODEL)),
            pl.BlockSpec((1, D_MODEL), const),
        ],
        out_specs=out_specs,
        out_shape=out_shape,
        compiler_params=_cparams("arbitrary"),
        name="merge_ffn",
    )(h_p, h_s, *o_a, *o_m, *o_d, proj, proj, gates, gates, gates, g_gla, g_ml, g_diff, w_a, w_m, w_d, w_out,
      g_ffn, w_g, w_u, w_dn, p_p, p_s, g_ple, w_pg, w_pp, g_final)


def _rope_tables(positions):
    half = ROT_DIM // 2
    inv_freq = ROPE_THETA ** (-jnp.arange(half, dtype=f32) * 2.0 / ROT_DIM)
    ang = positions.astype(f32)[:, None] * inv_freq[None, :]
    cos, sin = jnp.cos(ang), jnp.sin(ang)
    n = positions.shape[0]
    one = jnp.ones((n, DK - ROT_DIM), f32)
    zero = jnp.zeros((n, DK - ROT_DIM), f32)
    zh = jnp.zeros((n, half), f32)
    cos_h = jnp.concatenate([cos, cos, one], axis=1)
    up_h = jnp.concatenate([-sin, zh, zero], axis=1)
    dn_h = jnp.concatenate([zh, sin, zero], axis=1)
    rep = lambda a: jnp.concatenate([a] * (LANES // DK), axis=1)
    return rep(cos_h), rep(up_h), rep(dn_h)


def _prep_in_weights(w_in_l, w_gate_l):
    o = np.cumsum((0, QW, QW, VW, VW, GLA_RANK, QW, QW, VW, VW, 2 * HEADS, 2 * QW, 2 * QW, VW))
    seg = lambda i: w_in_l[:, int(o[i]):int(o[i + 1])]
    gq, gk, gv, gr, glr, mq, mk, mv, mo, mif, dq, dk, dv = (seg(i) for i in range(13))
    main = jnp.concatenate([w_gate_l, gq, gk, gv, gr, mq, mk, mv, mo, dq, dk, dv], axis=1).astype(bf16)
    pad = jnp.zeros((D_MODEL, LANES - GLA_RANK - 2 * HEADS), f32)
    small = jnp.concatenate([glr, mif, pad], axis=1).astype(bf16)
    return main, small


def kernel(x_prompt, x_sample, cache_k, cache_v, state_gla, state_mlstm_c, state_mlstm_n, state_mlstm_m, page_table, p_prompt, p_sample, g_mix, w_in, w_gla_gk, b_gla_gk, g_gla_norm, b_mlstm_if, g_mlstm_norm, diff_lambda, g_diff_norm, w_branch, w_gate, w_out, g_ffn, w_ffn_gate, w_ffn_up, w_ffn_down, g_ple, w_ple_gate, w_ple_proj, g_final):
    bp, lp, _ = x_prompt.shape
    bs, ls, _ = x_sample.shape
    depth = g_mix.shape[0]
    n_pages = page_table.shape[1]
    page = cache_k.shape[2]
    past = n_pages * page
    tp = bp * lp
    ts = bs * ls
    assert lp % TM == 0 and ts == TM and bs == LANES and page == LANES and TM_IN % bs == 0
    npb = tp // TM

    def to_step_major(a):
        return jnp.swapaxes(a, 0, 1).reshape((ts,) + a.shape[2:])

    def from_step_major(a):
        return jnp.swapaxes(a.reshape((ls, bs) + a.shape[1:]), 0, 1)

    h_p, h_s, h_s_row0 = x_prompt.reshape(tp, D_MODEL), to_step_major(x_sample), 0

    pos_rows = jnp.concatenate([jnp.arange(lp, dtype=jnp.int32),
                                past + jnp.repeat(jnp.arange(ls, dtype=jnp.int32), bs)])
    cos_t, sup_t, sdn_t = _rope_tables(pos_rows)

    ck = jnp.transpose(cache_k, (0, 1, 3, 4, 2)).reshape(cache_k.shape[0], cache_k.shape[1], 2 * QW, page)
    cv = cache_v.reshape(cache_v.shape[0], cache_v.shape[1], page * HEADS, DV)
    state_n = state_mlstm_n.reshape(depth, bs, QW)
    p_rows = p_prompt.reshape(depth * tp, PLE_DIM)
    k_t = jnp.zeros((depth, bp, 2 * QW, lp), f32)
    v_p = jnp.zeros((depth * tp * HEADS, DV), f32)
    gla_s = jnp.zeros(state_gla.shape, f32)
    c_s = jnp.zeros(state_mlstm_c.shape, f32)
    outs_p, outs_s = [], []
    for l in range(depth):
        lam_init = 0.8 - 0.6 * math.exp(-0.3 * l)
        w_main, w_small = _prep_in_weights(w_in[l], w_gate[l])
        w_gk = jnp.concatenate([w_gla_gk[l], jnp.zeros((LANES - GLA_RANK, QW), f32)], axis=0)
        b_if = jnp.zeros((1, LANES), f32).at[0, L_MI:L_MI + 2 * HEADS].set(b_mlstm_if[l])
        gates, proj, k_t, v_p, k_ts, la, gif = _inproj(
            h_p, h_s, h_s_row0 // TM_IN, g_mix[l][None], w_main, w_small, w_gk, b_gla_gk[l][None], b_if,
            cos_t, sup_t, sdn_t, bp, lp, bs, ls, l, depth, (k_t, v_p))

        oa_p, gla_p = _gla_prompt(proj, la, bp, lp)
        om_p, c_p, n_p, m_p = _mlstm_prompt(proj, gif, bp, lp)
        oa_p, om_p = oa_p.reshape(tp, VW), om_p.reshape(tp, VW)
        od_p, od_s = _attention(proj, k_ts, page_table, diff_lambda[l], ck, cv, l, lam_init, bp, lp, npb)

        ga, gk_, gq_, ma, mk_, mq_, n_s, m_s = _sample_prep(proj, la, gif, state_n, state_mlstm_m, l, npb, bs, ls)
        oa_s, gla_s = _sample_state(ga, gk_, gq_, proj, T_GV, npb, state_gla, l, ls, gla_s)
        om_s, c_s = _sample_state(ma, mk_, mq_, proj, T_MV, npb, state_mlstm_c, l, ls, c_s)
        v_s = from_step_major(proj[tp:, T_DV * TILE:(T_DV + 1) * TILE])

        wb = w_branch[l].astype(bf16)
        h = _merge_ffn(h_p, h_s, h_s_row0, (oa_p, oa_s), (om_p, om_s), (od_p, od_s), proj, gates,
                       g_gla_norm[l][None], g_mlstm_norm[l][None], g_diff_norm[l][None],
                       wb[:VW], wb[VW:2 * VW], wb[2 * VW:], w_out[l].astype(bf16), lam_init,
                       g_ffn[l][None], w_ffn_gate[l].astype(bf16), w_ffn_up[l].astype(bf16),
                       w_ffn_down[l].astype(bf16), p_rows, to_step_major(p_sample[l]), l * tp,
                       g_ple[l][None], w_ple_gate[l].astype(bf16), w_ple_proj[l].astype(bf16), g_final[None],
                       l == depth - 1, tp, ts)
        h_p, h_s, h_s_row0 = h, h, tp

        k_new = k_ts.reshape(ls, 2 * HEADS, DK, bs).transpose(3, 0, 1, 2)
        outs_p.append((gla_p.reshape(bp, HEADS, DK, DV), c_p, n_p, m_p[:, :, 0]))
        outs_s.append((k_new, v_s.reshape(bs, ls, HEADS, DV), n_s.reshape(bs, HEADS, DK), m_s[:, ::DK]))

    y_prompt = h[0].reshape(bp, lp, D_MODEL)
    y_sample = from_step_major(h[1])
    k_prompt = k_t.reshape(depth, bp, 2 * HEADS, DK, lp).transpose(0, 1, 4, 2, 3)
    v_prompt = v_p.reshape(depth, bp, lp, HEADS, DV)
    gla_prompt, c_prompt, n_prompt, m_prompt = [jnp.stack(t) for t in zip(*outs_p)]
    k_sample, v_sample, n_sample, m_sample = [jnp.stack(t) for t in zip(*outs_s)]
    return (y_prompt, y_sample, k_prompt, v_prompt, gla_prompt, c_prompt, n_prompt, m_prompt,
            k_sample, v_sample, gla_s, c_s, n_sample, m_sample)
```
